```python
import jax, jax.numpy as jnp
from jax import lax
import numpy as np

D_MODEL = 1024
BATCH = 2
SEQ = 8192
DEPTH = 4
DEC_BATCH = 32
DEC_SEQ = 4
PAST_LEN = 8192
PAGE_SIZE = 128

N_MIXERS = 4
N_A = (DEPTH + 3) // 4
N_B = (DEPTH + 2) // 4
N_C = (DEPTH + 1) // 4
N_D = DEPTH // 4

D_FF = 2816
EPS = 1e-6

RET_HEADS = 4
RET_DK = D_MODEL // RET_HEADS
RET_DV = 2 * RET_DK
RET_CHUNK = 128
RET_THETA = 10000.0

D_RNN = D_MODEL
LRU_BLOCKS = 4
LRU_BW = D_RNN // LRU_BLOCKS
CONV_W = 4
LRU_C = 8.0

N_HEADS = 8
HEAD_DIM = D_MODEL // N_HEADS
N_KV = 2
GROUP = N_HEADS // N_KV
ROPE_DIM = HEAD_DIM // 4
ROPE_THETA = 500000.0
ATT_QB = 128

IDX_HEADS = 16
IDX_DIM = 64
IDX_ROPE = IDX_DIM // 4
TOPK_MAX = 256

CMP_BLK = 64
SLC_BLK = 64
N_SLC = 16
WINDOW = 512
NSA_QB = 64
BIG = 1e4

kernel_name = 'hybrid_retention_rglru_dsa_nsa_macaron_step'

F32 = jnp.float32


def rms_norm(x, g=None):
    xf = x.astype(F32)
    y = xf * lax.rsqrt(jnp.mean(xf * xf, axis=-1, keepdims=True) + EPS)
    if g is not None:
        y = y * g.astype(F32)
    return y.astype(x.dtype)


def rotary(x, pos, rot_dim, theta):
    half = rot_dim // 2
    inv = theta ** (-jnp.arange(half, dtype=F32) / half)
    ang = pos.astype(F32)[:, None] * inv[None, :]
    cos = jnp.cos(ang)[:, None, :]
    sin = jnp.sin(ang)[:, None, :]
    xf = x.astype(F32)
    x1 = xf[..., :half]
    x2 = xf[..., half:rot_dim]
    out = jnp.concatenate([x1 * cos - x2 * sin, x2 * cos + x1 * sin, xf[..., rot_dim:]], axis=-1)
    return out.astype(x.dtype)


def masked_softmax(logits, mask):
    lg = jnp.where(mask, logits.astype(F32), -jnp.inf)
    m = jnp.max(lg, axis=-1, keepdims=True)
    m = jnp.where(jnp.isfinite(m), m, 0.0)
    p = jnp.where(mask, jnp.exp(lg - m), 0.0)
    return p / jnp.maximum(jnp.sum(p, axis=-1, keepdims=True), 1e-30)


def macaron_half(x, g, w_gate, w_up, w_down):
    h = rms_norm(x, g)
    return x + 0.5 * ((jax.nn.silu(h @ w_gate) * (h @ w_up)) @ w_down)


def to_blocks(a, nb, qb):
    return a.reshape((a.shape[0], nb, qb) + a.shape[2:]).swapaxes(0, 1)


def from_blocks(o):
    o = o.swapaxes(0, 1)
    return o.reshape((o.shape[0], o.shape[1] * o.shape[2]) + o.shape[3:])


def gather_pages(cache, page_table):
    rows = cache[page_table]
    return rows.reshape((page_table.shape[0], page_table.shape[1] * PAGE_SIZE) + cache.shape[2:])


def paged_gather(cache, page_table, new_rows, pos, head=None):
    past_len = page_table.shape[1] * PAGE_SIZE
    b = jnp.arange(pos.shape[0]).reshape((-1,) + (1,) * (pos.ndim - 1))
    pp = jnp.minimum(pos, past_len - 1)
    phys = page_table[b, pp // PAGE_SIZE]
    off = pp % PAGE_SIZE
    pn = jnp.clip(pos - past_len, 0, new_rows.shape[1] - 1)
    if head is None:
        old = cache[phys, off]
        new = new_rows[b, pn]
        is_past = (pos < past_len)[..., None, None]
    else:
        old = cache[phys, off, head]
        new = new_rows[b, pn, head]
        is_past = (pos < past_len)[..., None]
    return jnp.where(is_past, old, new)


def retention_mixer(h, pos, s0, w_q, w_k, w_v, w_g, w_o):
    B, T, _ = h.shape
    q = rotary((h @ w_q).reshape(B, T, RET_HEADS, RET_DK), pos, RET_DK, RET_THETA)
    k = rotary((h @ w_k).reshape(B, T, RET_HEADS, RET_DK), pos, RET_DK, RET_THETA) * (RET_DK ** -0.5)
    v = (h @ w_v).reshape(B, T, RET_HEADS, RET_DV)
    C = RET_CHUNK if T % RET_CHUNK == 0 else T
    nC = T // C
    lg = jnp.log1p(-jnp.exp2(-5.0 - jnp.arange(RET_HEADS, dtype=F32)))
    idx = jnp.arange(C, dtype=F32)
    diff = idx[:, None] - idx[None, :]
    dmask = jnp.where(diff >= 0, jnp.exp(lg[:, None, None] * jnp.maximum(diff, 0.0)), 0.0)
    xi = jnp.exp(lg[:, None] * (idx + 1.0))[None, :, :, None]
    zeta = jnp.exp(lg[:, None] * (C - 1.0 - idx))[None, :, :, None]
    g_c = jnp.exp(lg * C)[None, :, None, None]

    def chunks(a):
        return a.reshape(B, nC, C, RET_HEADS, a.shape[-1]).transpose(1, 0, 3, 2, 4)

    def step(S, inp):
        qc, kc, vc = (t.astype(F32) for t in inp)
        att = jnp.einsum('bhqd,bhkd->bhqk', qc, kc) * dmask
        o = jnp.einsum('bhqk,bhkv->bhqv', att, vc) + jnp.einsum('bhqd,bhdv->bhqv', qc * xi, S)
        S = S * g_c + jnp.einsum('bhkd,bhkv->bhdv', kc * zeta, vc)
        return S, o

    S, o = lax.scan(step, s0.astype(F32), (chunks(q), chunks(k), chunks(v)))
    o = o.transpose(1, 0, 3, 2, 4).reshape(B, T, RET_HEADS, RET_DV)
    o = rms_norm(o).reshape(B, T, RET_HEADS * RET_DV).astype(h.dtype)
    y = (jax.nn.silu(h @ w_g) * o) @ w_o
    return y, S.astype(s0.dtype)


def causal_conv(x, buf, w, b):
    T = x.shape[1]
    xp = jnp.concatenate([buf.astype(x.dtype), x], axis=1)
    y = b + sum(xp[:, j:j + T] * w[j] for j in range(CONV_W))
    return y, xp[:, -(CONV_W - 1):]


def rglru_mixer(h, h0, conv0, w_y, b_y, w_x, b_x, conv_w, conv_b, w_a, b_a, w_i, b_i, lam, w_o):
    B, T, _ = h.shape
    gate = jax.nn.gelu(h @ w_y + b_y)
    xb, conv_new = causal_conv(h @ w_x + b_x, conv0, conv_w, conv_b)
    xblk = xb.reshape(B, T, LRU_BLOCKS, LRU_BW)
    r = jax.nn.sigmoid(jnp.einsum('btnc,ncd->btnd', xblk, w_a) + b_a).reshape(B, T, D_RNN)
    i = jax.nn.sigmoid(jnp.einsum('btnc,ncd->btnd', xblk, w_i) + b_i).reshape(B, T, D_RNN)
    log_a = -LRU_C * r.astype(F32) * jax.nn.softplus(-lam.astype(F32))
    a = jnp.exp(log_a)
    bt = jnp.sqrt(-jnp.expm1(2.0 * log_a)) * (i * xb).astype(F32)

    def combine(l, rr):
        a1, b1 = l
        a2, b2 = rr
        return a1 * a2, a2 * b1 + b2

    A, Hs = lax.associative_scan(combine, (a, bt), axis=1)
    hs = Hs + A * h0.astype(F32)[:, None, :]
    y = (gate * hs.astype(h.dtype)) @ w_o
    return y, hs[:, -1].astype(h0.dtype), conv_new


def dsa_project(h, pos, w_q, g_q, w_k, g_k, w_v, w_iq, w_ik, g_ik, w_iw):
    B, T, _ = h.shape
    q = rotary(rms_norm((h @ w_q).reshape(B, T, N_HEADS, HEAD_DIM), g_q), pos, ROPE_DIM, ROPE_THETA)
    k = rotary(rms_norm((h @ w_k).reshape(B, T, N_KV, HEAD_DIM), g_k), pos, ROPE_DIM, ROPE_THETA)
    v = (h @ w_v).reshape(B, T, N_KV, HEAD_DIM)
    qi = rotary((h @ w_iq).reshape(B, T, IDX_HEADS, IDX_DIM), pos, IDX_ROPE, ROPE_THETA)
    ki = rotary(rms_norm(h @ w_ik, g_ik)[:, :, None, :], pos, IDX_ROPE, ROPE_THETA)[:, :, 0, :]
    wi = (h @ w_iw) * (IDX_HEADS ** -0.5 * IDX_DIM ** -0.5)
    return q, k, v, qi, ki, wi


def dsa_attend(q, qi, wi, qpos, ki_all, kv_gather):
    B, T = q.shape[:2]
    L = ki_all.shape[1]
    top = min(TOPK_MAX, L // 4)
    qb = min(ATT_QB, T)
    nb = T // qb
    kpos = jnp.arange(L)

    def block(args):
        q_b, qi_b, wi_b, p_b = args
        sc = jax.nn.relu(jnp.einsum('bqhd,bsd->bqhs', qi_b, ki_all).astype(F32))
        score = jnp.einsum('bqhs,bqh->bqs', sc, wi_b.astype(F32))
        score = jnp.where((kpos[None, :] <= p_b[:, None])[None], score, -jnp.inf)
        _, sel = lax.top_k(score, top)
        ks, vs = kv_gather(sel)
        qg = q_b.reshape(B, qb, N_KV, GROUP, HEAD_DIM)
        lo = jnp.einsum('bqhgd,bqkhd->bqhgk', qg, ks) * (HEAD_DIM ** -0.5)
        p = masked_softmax(lo, (sel <= p_b[None, :, None])[:, :, None, None, :])
        o = jnp.einsum('bqhgk,bqkhd->bqhgd', p.astype(vs.dtype), vs)
        return o.reshape(B, qb, N_HEADS, HEAD_DIM).astype(q.dtype)

    o = lax.map(block, (to_blocks(q, nb, qb), to_blocks(qi, nb, qb), to_blocks(wi, nb, qb), qpos.reshape(nb, qb)))
    return from_blocks(o)


def nsa_project(h, pos, w_q, g_q, w_ck, w_cv, w_sk, w_sv, g_sk, w_wk, w_wv, g_wk, w_gate):
    B, T, _ = h.shape
    qc = rms_norm((h @ w_q).reshape(B, T, N_HEADS, HEAD_DIM), g_q)
    q = rotary(qc, pos, ROPE_DIM, ROPE_THETA)
    ck = (h @ w_ck).reshape(B, T, N_KV, HEAD_DIM)
    cv = (h @ w_cv).reshape(B, T, N_KV, HEAD_DIM)
    sk = rotary(rms_norm((h @ w_sk).reshape(B, T, N_KV, HEAD_DIM), g_sk), pos, ROPE_DIM, ROPE_THETA)
    sv = (h @ w_sv).reshape(B, T, N_KV, HEAD_DIM)
    wk = rotary(rms_norm((h @ w_wk).reshape(B, T, N_KV, HEAD_DIM), g_wk), pos, ROPE_DIM, ROPE_THETA)
    wv = (h @ w_wv).reshape(B, T, N_KV, HEAD_DIM)
    gate = jax.nn.sigmoid(h @ w_gate).reshape(B, T, N_HEADS, 3)
    return q, qc, ck, cv, sk, sv, wk, wv, gate


def nsa_compress(ck, cv, pw_k, pw_v, phi_k, phi_v, g_ck):
    B, L = ck.shape[:2]
    n = L // CMP_BLK
    kb = ck[:, :n * CMP_BLK].reshape(B, n, CMP_BLK, N_KV, HEAD_DIM)
    vb = cv[:, :n * CMP_BLK].reshape(B, n, CMP_BLK, N_KV, HEAD_DIM)
    kc = jnp.einsum('bnhd,hde->bnhe', jnp.einsum('bnphd,ph->bnhd', kb, pw_k), phi_k)
    vc = jnp.einsum('bnhd,hde->bnhe', jnp.einsum('bnphd,ph->bnhd', vb, pw_v), phi_v)
    return rms_norm(kc, g_ck), vc


def prompt_window_fetch(wk, wv, qb):
    kp = jnp.pad(wk, ((0, 0), (WINDOW, 0), (0, 0), (0, 0)))
    vp = jnp.pad(wv, ((0, 0), (WINDOW, 0), (0, 0), (0, 0)))
    n = WINDOW + qb

    def fetch(bi):
        s = bi * qb
        return (lax.dynamic_slice_in_dim(kp, s, n, 1), lax.dynamic_slice_in_dim(vp, s, n, 1),
                s - WINDOW + jnp.arange(n))
    return fetch


def nsa_attend(q, qc, gate, qpos, kc, vc, slc_gather, win_fetch, n_keys):
    B, T = q.shape[:2]
    nc = kc.shape[1]
    ns = -(-n_keys // SLC_BLK)
    nsel = min(N_SLC, ns)
    qb = min(NSA_QB, T)
    nb = T // qb
    scale = HEAD_DIM ** -0.5
    cmp_end = jnp.arange(nc) * CMP_BLK + (CMP_BLK - 1)
    blk_ids = jnp.arange(ns)
    offs = jnp.arange(SLC_BLK)
    hidx = jnp.arange(N_KV)[None, None, :, None]

    def block(args):
        q_b, qc_b, g_b, p_b, bi = args
        qg = q_b.reshape(B, qb, N_KV, GROUP, HEAD_DIM)
        qcg = qc_b.reshape(B, qb, N_KV, GROUP, HEAD_DIM)
        lc = jnp.einsum('bqhgd,bnhd->bqhgn', qcg, kc) * scale
        pc = masked_softmax(lc, (cmp_end[None, :] <= p_b[:, None])[None, :, None, None, :])
        o_c = jnp.einsum('bqhgn,bnhd->bqhgd', pc.astype(vc.dtype), vc)
        imp = jnp.pad(jnp.sum(pc, axis=3), ((0, 0), (0, 0), (0, 0), (0, ns - nc)))
        cur = (p_b // SLC_BLK)[:, None]
        forced = (blk_ids[None] == 0) | (blk_ids[None] == cur) | (blk_ids[None] == cur - 1)
        imp = jnp.where(forced[None, :, None, :], BIG, imp)
        imp = jnp.where((blk_ids[None] > cur)[None, :, None, :], -jnp.inf, imp)
        _, sel = lax.top_k(imp, nsel)
        spos = (sel[..., None] * SLC_BLK + offs).reshape(B, qb, N_KV, nsel * SLC_BLK)
        ks, vs = slc_gather(spos, hidx)
        ls = jnp.einsum('bqhgd,bqhkd->bqhgk', qg, ks) * scale
        ps = masked_softmax(ls, (spos <= p_b[None, :, None, None])[:, :, :, None, :])
        o_s = jnp.einsum('bqhgk,bqhkd->bqhgd', ps.astype(vs.dtype), vs)
        kw, vw, wpos = win_fetch(bi)
        dist = p_b[:, None] - wpos[None, :]
        mw = (dist >= 0) & (dist < WINDOW) & (wpos[None, :] >= 0)
        lw = jnp.einsum('bqhgd,bkhd->bqhgk', qg, kw) * scale
        pw = masked_softmax(lw, mw[None, :, None, None, :])
        o_w = jnp.einsum('bqhgk,bkhd->bqhgd', pw.astype(vw.dtype), vw)
        g = g_b.reshape(B, qb, N_KV, GROUP, 3)
        o = g[..., 0:1] * o_c + g[..., 1:2] * o_s + g[..., 2:3] * o_w
        return o.reshape(B, qb, N_HEADS, HEAD_DIM).astype(q.dtype)

    o = lax.map(block, (to_blocks(q, nb, qb), to_blocks(qc, nb, qb), to_blocks(gate, nb, qb),
                        qpos.reshape(nb, qb), jnp.arange(nb)))
    return from_blocks(o)


def setup_inputs(seed: int = 0) -> dict:
    key = jax.random.key(seed)
    keys = iter(jax.random.split(key, 128))
    n_pages = PAST_LEN // PAGE_SIZE
    n_used = DEC_BATCH * n_pages
    n_pool = n_used + n_used // 4
    wbuf = min(WINDOW, PAST_LEN)

    def nrm(shape, scale=1.0):
        return scale * jax.random.normal(next(keys), shape, F32)

    def wt(shape):
        return nrm(shape, shape[-2] ** -0.5)

    def gain(shape):
        return 1.0 + 0.05 * nrm(shape)

    d = {}
    d['x_prompt'] = nrm((BATCH, SEQ, D_MODEL))
    d['x_sample'] = nrm((DEC_BATCH, DEC_SEQ, D_MODEL))
    d['state_a_ret'] = nrm((N_A, DEC_BATCH, RET_HEADS, RET_DK, RET_DV), 0.1)
    d['state_b_h'] = nrm((N_B, DEC_BATCH, D_RNN), 0.5)
    d['state_b_conv'] = nrm((N_B, DEC_BATCH, CONV_W - 1, D_RNN))
    d['cache_c_k'] = nrm((N_C, n_pool, PAGE_SIZE, N_KV, HEAD_DIM))
    d['cache_c_v'] = nrm((N_C, n_pool, PAGE_SIZE, N_KV, HEAD_DIM))
    d['cache_c_kidx'] = nrm((N_C, n_pool, PAGE_SIZE, IDX_DIM))
    d['cache_d_ck'] = nrm((N_D, n_pool, PAGE_SIZE, N_KV, HEAD_DIM))
    d['cache_d_cv'] = nrm((N_D, n_pool, PAGE_SIZE, N_KV, HEAD_DIM))
    d['cache_d_sk'] = nrm((N_D, n_pool, PAGE_SIZE, N_KV, HEAD_DIM))
    d['cache_d_sv'] = nrm((N_D, n_pool, PAGE_SIZE, N_KV, HEAD_DIM))
    d['state_d_wk'] = nrm((N_D, DEC_BATCH, wbuf, N_KV, HEAD_DIM))
    d['state_d_wv'] = nrm((N_D, DEC_BATCH, wbuf, N_KV, HEAD_DIM))
    perm = jax.random.permutation(next(keys), n_pool)[:n_used]
    d['page_table'] = perm.reshape(DEC_BATCH, n_pages).astype(jnp.int32)
    d['ffn1_norm'] = gain((DEPTH, D_MODEL))
    d['ffn1_wg'] = wt((DEPTH, D_MODEL, D_FF))
    d['ffn1_wu'] = wt((DEPTH, D_MODEL, D_FF))
    d['ffn1_wd'] = wt((DEPTH, D_FF, D_MODEL))
    d['mix_norm'] = gain((DEPTH, D_MODEL))
    d['ffn2_norm'] = gain((DEPTH, D_MODEL))
    d['ffn2_wg'] = wt((DEPTH, D_MODEL, D_FF))
    d['ffn2_wu'] = wt((DEPTH, D_MODEL, D_FF))
    d['ffn2_wd'] = wt((DEPTH, D_FF, D_MODEL))
    d['a_wq'] = wt((N_A, D_MODEL, RET_HEADS * RET_DK))
    d['a_wk'] = wt((N_A, D_MODEL, RET_HEADS * RET_DK))
    d['a_wv'] = wt((N_A, D_MODEL, RET_HEADS * RET_DV))
    d['a_wg'] = wt((N_A, D_MODEL, RET_HEADS * RET_DV))
    d['a_wo'] = wt((N_A, RET_HEADS * RET_DV, D_MODEL))
    d['b_wy'] = wt((N_B, D_MODEL, D_RNN))
    d['b_by'] = nrm((N_B, D_RNN), 0.01)
    d['b_wx'] = wt((N_B, D_MODEL, D_RNN))
    d['b_bx'] = nrm((N_B, D_RNN), 0.01)
    d['b_conv_w'] = wt((N_B, CONV_W, D_RNN))
    d['b_conv_b'] = nrm((N_B, D_RNN), 0.01)
    d['b_wa'] = wt((N_B, LRU_BLOCKS, LRU_BW, LRU_BW))
    d['b_ba'] = nrm((N_B, LRU_BLOCKS, LRU_BW), 0.01)
    d['b_wi'] = wt((N_B, LRU_BLOCKS, LRU_BW, LRU_BW))
    d['b_bi'] = nrm((N_B, LRU_BLOCKS, LRU_BW), 0.01)
    u = jax.random.uniform(next(keys), (N_B, D_RNN), F32, minval=0.9, maxval=0.999)
    s = u ** (1.0 / LRU_C)
    d['b_lam'] = jnp.log(s) - jnp.log1p(-s)
    d['b_wo'] = wt((N_B, D_RNN, D_MODEL))
    d['c_wq'] = wt((N_C, D_MODEL, N_HEADS * HEAD_DIM))
    d['c_gq'] = gain((N_C, HEAD_DIM))
    d['c_wk'] = wt((N_C, D_MODEL, N_KV * HEAD_DIM))
    d['c_gk'] = gain((N_C, HEAD_DIM))
    d['c_wv'] = wt((N_C, D_MODEL, N_KV * HEAD_DIM))
    d['c_wo'] = wt((N_C, N_HEADS * HEAD_DIM, D_MODEL))
    d['c_wiq'] = wt((N_C, D_MODEL, IDX_HEADS * IDX_DIM))
    d['c_wik'] = wt((N_C, D_MODEL, IDX_DIM))
    d['c_gik'] = gain((N_C, IDX_DIM))
    d['c_wiw'] = wt((N_C, D_MODEL, IDX_HEADS))
    d['d_wq'] = wt((N_D, D_MODEL, N_HEADS * HEAD_DIM))
    d['d_gq'] = gain((N_D, HEAD_DIM))
    d['d_wck'] = wt((N_D, D_MODEL, N_KV * HEAD_DIM))
    d['d_wcv'] = wt((N_D, D_MODEL, N_KV * HEAD_DIM))
    d['d_pwk'] = (1.0 + 0.5 * nrm((N_D, CMP_BLK, N_KV))) / CMP_BLK
    d['d_pwv'] = (1.0 + 0.5 * nrm((N_D, CMP_BLK, N_KV))) / CMP_BLK
    d['d_phik'] = wt((N_D, N_KV, HEAD_DIM, HEAD_DIM))
    d['d_phiv'] = wt((N_D, N_KV, HEAD_DIM, HEAD_DIM))
    d['d_gck'] = gain((N_D, HEAD_DIM))
    d['d_wsk'] = wt((N_D, D_MODEL, N_KV * HEAD_DIM))
    d['d_wsv'] = wt((N_D, D_MODEL, N_KV * HEAD_DIM))
    d['d_gsk'] = gain((N_D, HEAD_DIM))
    d['d_wwk'] = wt((N_D, D_MODEL, N_KV * HEAD_DIM))
    d['d_wwv'] = wt((N_D, D_MODEL, N_KV * HEAD_DIM))
    d['d_gwk'] = gain((N_D, HEAD_DIM))
    d['d_wgate'] = wt((N_D, D_MODEL, N_HEADS * 3))
    d['d_wo'] = wt((N_D, N_HEADS * HEAD_DIM, D_MODEL))
    return d


def reference(x_prompt, x_sample, state_a_ret, state_b_h, state_b_conv, cache_c_k, cache_c_v, cache_c_kidx,
              cache_d_ck, cache_d_cv, cache_d_sk, cache_d_sv, state_d_wk, state_d_wv, page_table,
              ffn1_norm, ffn1_wg, ffn1_wu, ffn1_wd, mix_norm, ffn2_norm, ffn2_wg, ffn2_wu, ffn2_wd,
              a_wq, a_wk, a_wv, a_wg, a_wo,
              b_wy, b_by, b_wx, b_bx, b_conv_w, b_conv_b, b_wa, b_ba, b_wi, b_bi, b_lam, b_wo,
              c_wq, c_gq, c_wk, c_gk, c_wv, c_wo, c_wiq, c_wik, c_gik, c_wiw,
              d_wq, d_gq, d_wck, d_wcv, d_pwk, d_pwv, d_phik, d_phiv, d_gck, d_wsk, d_wsv, d_gsk,
              d_wwk, d_wwv, d_gwk, d_wgate, d_wo):
    Bp, Tp, _ = x_prompt.shape
    Bs, Ts, _ = x_sample.shape
    past_len = page_table.shape[1] * PAGE_SIZE
    n_keys_s = past_len + Ts
    pos_p = jnp.arange(Tp, dtype=jnp.int32)
    pos_s = past_len + jnp.arange(Ts, dtype=jnp.int32)
    bp3 = jnp.arange(Bp)[:, None, None]
    bp4 = jnp.arange(Bp)[:, None, None, None]

    ret_p, ret_s, lh_p, lh_s, lc_p, lc_s = [], [], [], [], [], []
    ck_p, ck_s, cv_p, cv_s, ci_p, ci_s = [], [], [], [], [], []
    dck_p, dck_s, dcv_p, dcv_s, dsk_p, dsk_s, dsv_p, dsv_s = [], [], [], [], [], [], [], []
    dwk_p, dwk_s, dwv_p, dwv_s = [], [], [], []

    xp, xs = x_prompt, x_sample
    for i in range(DEPTH):
        m, j = i % N_MIXERS, i // N_MIXERS
        xp = macaron_half(xp, ffn1_norm[i], ffn1_wg[i], ffn1_wu[i], ffn1_wd[i])
        xs = macaron_half(xs, ffn1_norm[i], ffn1_wg[i], ffn1_wu[i], ffn1_wd[i])
        hp = rms_norm(xp, mix_norm[i])
        hs = rms_norm(xs, mix_norm[i])
        if m == 0:
            aw = (a_wq[j], a_wk[j], a_wv[j], a_wg[j], a_wo[j])
            yp, sP = retention_mixer(hp, pos_p, jnp.zeros((Bp, RET_HEADS, RET_DK, RET_DV), hp.dtype), *aw)
            ys, sS = retention_mixer(hs, pos_s, state_a_ret[j], *aw)
            ret_p.append(sP)
            ret_s.append(sS)
        elif m == 1:
            bw = (b_wy[j], b_by[j], b_wx[j], b_bx[j], b_conv_w[j], b_conv_b[j], b_wa[j], b_ba[j],
                  b_wi[j], b_bi[j], b_lam[j], b_wo[j])
            yp, hP, cP = rglru_mixer(hp, jnp.zeros((Bp, D_RNN), hp.dtype),
                                     jnp.zeros((Bp, CONV_W - 1, D_RNN), hp.dtype), *bw)
            ys, hS, cS = rglru_mixer(hs, state_b_h[j], state_b_conv[j], *bw)
            lh_p.append(hP)
            lh_s.append(hS)
            lc_p.append(cP)
            lc_s.append(cS)
        elif m == 2:
            cw = (c_wq[j], c_gq[j], c_wk[j], c_gk[j], c_wv[j], c_wiq[j], c_wik[j], c_gik[j], c_wiw[j])
            qP, kP, vP, qiP, kiP, wiP = dsa_project(hp, pos_p, *cw)
            oP = dsa_attend(qP, qiP, wiP, pos_p, kiP, lambda sel: (kP[bp3, sel], vP[bp3, sel]))
            yp = oP.reshape(Bp, Tp, N_HEADS * HEAD_DIM) @ c_wo[j]
            qS, kS, vS, qiS, kiS, wiS = dsa_project(hs, pos_s, *cw)
            ki_all = jnp.concatenate([gather_pages(cache_c_kidx[j], page_table).astype(kiS.dtype), kiS], axis=1)
            oS = dsa_attend(qS, qiS, wiS, pos_s, ki_all,
                            lambda sel: (paged_gather(cache_c_k[j], page_table, kS, sel),
                                         paged_gather(cache_c_v[j], page_table, vS, sel)))
            ys = oS.reshape(Bs, Ts, N_HEADS * HEAD_DIM) @ c_wo[j]
            ck_p.append(kP)
            ck_s.append(kS)
            cv_p.append(vP)
            cv_s.append(vS)
            ci_p.append(kiP)
            ci_s.append(kiS)
        else:
            dw = (d_wq[j], d_gq[j], d_wck[j], d_wcv[j], d_wsk[j], d_wsv[j], d_gsk[j], d_wwk[j], d_wwv[j],
                  d_gwk[j], d_wgate[j])
            cmpw = (d_pwk[j], d_pwv[j], d_phik[j], d_phiv[j], d_gck[j])
            qP, qcP, ckP, cvP, skP, svP, wkP, wvP, gP = nsa_project(hp, pos_p, *dw)
            kcP, vcP = nsa_compress(ckP, cvP, *cmpw)
            oP = nsa_attend(qP, qcP, gP, pos_p, kcP, vcP,
                            lambda pos, hd: (skP[bp4, pos, hd], svP[bp4, pos, hd]),
                            prompt_window_fetch(wkP, wvP, min(NSA_QB, Tp)), Tp)
            yp = oP.reshape(Bp, Tp, N_HEADS * HEAD_DIM) @ d_wo[j]
            wb_p = min(WINDOW, Tp)
            qS, qcS, ckS, cvS, skS, svS, wkS, wvS, gS = nsa_project(hs, pos_s, *dw)
            ck_all = jnp.concatenate([gather_pages(cache_d_ck[j], page_table).astype(ckS.dtype), ckS], axis=1)
            cv_all = jnp.concatenate([gather_pages(cache_d_cv[j], page_table).astype(cvS.dtype), cvS], axis=1)
            kcS, vcS = nsa_compress(ck_all, cv_all, *cmpw)
            wbuf = state_d_wk.shape[2]
            wk_all = jnp.concatenate([state_d_wk[j].astype(wkS.dtype), wkS], axis=1)
            wv_all = jnp.concatenate([state_d_wv[j].astype(wvS.dtype), wvS], axis=1)
            wpos = past_len - wbuf + jnp.arange(wbuf + Ts)
            oS = nsa_attend(qS, qcS, gS, pos_s, kcS, vcS,
                            lambda pos, hd: (paged_gather(cache_d_sk[j], page_table, skS, pos, hd),
                                             paged_gather(cache_d_sv[j], page_table, svS, pos, hd)),
                            lambda bi: (wk_all, wv_all, wpos), n_keys_s)
            ys = oS.reshape(Bs, Ts, N_HEADS * HEAD_DIM) @ d_wo[j]
            dck_p.append(ckP)
            dck_s.append(ckS)
            dcv_p.append(cvP)
            dcv_s.append(cvS)
            dsk_p.append(skP)
            dsk_s.append(skS)
            dsv_p.append(svP)
            dsv_s.append(svS)
            dwk_p.append(wkP[:, Tp - wb_p:])
            dwv_p.append(wvP[:, Tp - wb_p:])
            dwk_s.append(wk_all[:, Ts:])
            dwv_s.append(wv_all[:, Ts:])
        xp = xp + yp
        xs = xs + ys
        xp = macaron_half(xp, ffn2_norm[i], ffn2_wg[i], ffn2_wu[i], ffn2_wd[i])
        xs = macaron_half(xs, ffn2_norm[i], ffn2_wg[i], ffn2_wu[i], ffn2_wd[i])

    st = jnp.stack
    return (xp, xs,
            st(ret_p), st(ret_s), st(lh_p), st(lh_s), st(lc_p), st(lc_s),
            st(ck_p), st(ck_s), st(cv_p), st(cv_s), st(ci_p), st(ci_s),
            st(dck_p), st(dck_s), st(dcv_p), st(dcv_s), st(dsk_p), st(dsk_s), st(dsv_p), st(dsv_s),
            st(dwk_p), st(dwk_s), st(dwv_p), st(dwv_s))
```

```python
import functools
import math

import jax
import jax.numpy as jnp
from jax import lax
from jax.experimental import pallas as pl
from jax.experimental.pallas import tpu as pltpu

F32 = jnp.float32
BF16 = jnp.bfloat16
I32 = jnp.int32

EPS = 1e-6
PAGE_SIZE = 128

RET_HEADS = 4
RET_CHUNK = 128
RET_THETA = 10000.0

LRU_BLOCKS = 4
CONV_W = 4
LRU_C = 8.0

N_HEADS = 8
HEAD_DIM = 128
N_KV = 2
GROUP = N_HEADS // N_KV
ROPE_DIM = HEAD_DIM // 4
ROPE_THETA = 500000.0

IDX_HEADS = 16
IDX_DIM = 64
IDX_ROPE = IDX_DIM // 4
TOPK_MAX = 256

CMP_BLK = 64
SLC_BLK = 64
N_SLC = 16
WINDOW = 512
BIG = 1e4

LANES = 128
VMEM_LIMIT = 56 * 1024 * 1024
ROW_TILE = 512
NEG = -1e30
INT_MIN = -2147483648


def _params(*sem):
    return pltpu.CompilerParams(dimension_semantics=sem, vmem_limit_bytes=VMEM_LIMIT)


def _const_spec(shape):
    nd = len(shape)
    return pl.BlockSpec(shape, lambda *_: (0,) * nd, pipeline_mode=pl.Buffered(1))


def _rms(x):
    return x * lax.rsqrt(jnp.mean(x * x, axis=-1, keepdims=True) + EPS)


def _dot(a, b):
    return jnp.dot(a, b, preferred_element_type=F32)


def _dot_nt(a, b):
    return lax.dot_general(a, b, (((1,), (1,)), ((), ())), preferred_element_type=F32)


def _row_tile(n):
    return ROW_TILE if n % ROW_TILE == 0 else n


def _ffn_kernel(x_ref, g_ref, wg_ref, wu_ref, wd_ref, o_ref):
    x = x_ref[...]
    h = (_rms(x) * g_ref[...]).astype(BF16)
    gt = _dot(h, wg_ref[...])
    ut = _dot(h, wu_ref[...])
    a = (gt * jax.nn.sigmoid(gt) * ut).astype(BF16)
    o_ref[...] = x + 0.5 * _dot(a, wd_ref[...])


def _ffn(x, g, wg, wu, wd):
    n, d = x.shape
    f = wg.shape[1]
    tm = _row_tile(n)
    return pl.pallas_call(
        _ffn_kernel,
        grid=(n // tm,),
        in_specs=[pl.BlockSpec((tm, d), lambda i: (i, 0)), _const_spec((1, d)),
                  _const_spec((d, f)), _const_spec((d, f)), _const_spec((f, d))],
        out_specs=pl.BlockSpec((tm, d), lambda i: (i, 0)),
        out_shape=jax.ShapeDtypeStruct((n, d), F32),
        compiler_params=_params("parallel"),
    )(x, g.reshape(1, d), wg, wu, wd)


def _proj(x, g, w, tables, consts, epilogue, outs, n_tab_rows):
    n, d = x.shape
    tm = _row_tile(n)
    assert n_tab_rows % tm == 0
    nt = n_tab_rows // tm
    nw = w.shape[1]

    def kern(x_ref, g_ref, w_ref, *rest):
        tabs = rest[:len(tables)]
        cs = rest[len(tables):len(tables) + len(consts)]
        o = rest[len(tables) + len(consts):]
        h = (_rms(x_ref[...]) * g_ref[...]).astype(BF16)
        epilogue(h, w_ref, tabs, cs, o)

    in_specs = [pl.BlockSpec((tm, d), lambda i: (i, 0)), _const_spec((1, d)), _const_spec((d, nw))]
    in_specs += [pl.BlockSpec((tm, t.shape[1]), lambda i: (i % nt, 0)) for t in tables]
    in_specs += [_const_spec(c.shape) for c in consts]
    out_shape, out_specs = [], []
    for fn, dt in outs:
        ashape, bshape, imap = fn(n, tm)
        out_shape.append(jax.ShapeDtypeStruct(ashape, dt))
        out_specs.append(pl.BlockSpec(bshape, imap))
    return pl.pallas_call(
        kern, grid=(n // tm,), in_specs=in_specs, out_specs=out_specs, out_shape=out_shape,
        compiler_params=_params("parallel"),
    )(x, g.reshape(1, d), w, *tables, *consts)


def _rows(c):
    return lambda n, tm: ((n, c), (tm, c), lambda i: (i, 0))


def _outproj(x, w, ins, combine):
    n, d = x.shape
    tm = _row_tile(n)

    def kern(x_ref, w_ref, *rest):
        o_ref = rest[-1]
        a = combine(*[r[...] for r in rest[:-1]]).astype(BF16)
        o_ref[...] = x_ref[...] + _dot(a, w_ref[...])

    return pl.pallas_call(
        kern, grid=(n // tm,),
        in_specs=[pl.BlockSpec((tm, d), lambda i: (i, 0)), _const_spec(w.shape)]
        + [pl.BlockSpec((tm, a.shape[1]), lambda i: (i, 0)) for a in ins],
        out_specs=pl.BlockSpec((tm, d), lambda i: (i, 0)),
        out_shape=jax.ShapeDtypeStruct((n, d), F32),
        compiler_params=_params("parallel"),
    )(x, w, *ins)


def _ret_epilogue(dk, dv, h, w_ref, tabs, cs, o):
    cos, sin = tabs[0][...], tabs[1][...]
    q_ref, k_ref, v_ref, g_ref = o
    half = dk // 2
    nqk = RET_HEADS * dk
    for seg, ref, scale in ((0, q_ref, 1.0), (1, k_ref, dk ** -0.5)):
        z = _dot(h, w_ref[:, seg * nqk:(seg + 1) * nqk])
        for hd in range(RET_HEADS):
            x1 = z[:, hd * dk:hd * dk + half]
            x2 = z[:, hd * dk + half:(hd + 1) * dk]
            ref[:, hd * dk:hd * dk + half] = (x1 * cos - x2 * sin) * scale
            ref[:, hd * dk + half:(hd + 1) * dk] = (x2 * cos + x1 * sin) * scale
    nv = RET_HEADS * dv
    v_ref[...] = _dot(h, w_ref[:, 2 * nqk:2 * nqk + nv]).astype(BF16)
    zg = _dot(h, w_ref[:, 2 * nqk + nv:2 * nqk + 2 * nv])
    g_ref[...] = zg * jax.nn.sigmoid(zg)


def _ret_chunk_kernel(q_ref, k_ref, v_ref, s0_ref, dm_ref, xi_ref, zt_ref, gc_ref, o_ref, s_ref):
    c = pl.program_id(2)

    @pl.when(c == 0)
    def _():
        s_ref[...] = s0_ref[...]

    q = q_ref[...]
    k = k_ref[...]
    v = v_ref[...]
    s = s_ref[...]
    att = (_dot_nt(q.astype(BF16), k.astype(BF16)) * dm_ref[...]).astype(BF16)
    o = _dot(att, v) + _dot((q * xi_ref[...]).astype(BF16), s.astype(BF16))
    kz = (k * zt_ref[...]).astype(BF16)
    s_ref[...] = s * gc_ref[...] + lax.dot_general(kz, v, (((0,), (0,)), ((), ())),
                                                   preferred_element_type=F32)
    o_ref[...] = _rms(o)


def _retention(q, k, v, s0, n_valid):
    b, hh, dk, dv = s0.shape
    t = q.shape[0] // b
    c = RET_CHUNK
    nc = t // c
    cc = c if n_valid % c == 0 else n_valid
    lg = jnp.log1p(-jnp.exp2(-5.0 - jnp.arange(hh, dtype=F32)))
    idx = jnp.arange(c, dtype=F32)
    diff = idx[:, None] - idx[None, :]
    dmask = jnp.where(diff >= 0, jnp.exp(lg[:, None, None] * jnp.maximum(diff, 0.0)), 0.0)
    xi = jnp.exp(lg[:, None] * (idx + 1.0))[:, :, None]
    zeta = jnp.exp(lg[:, None] * (cc - 1.0 - idx))[:, :, None]
    g_c = jnp.exp(lg * cc)[:, None, None]
    return pl.pallas_call(
        _ret_chunk_kernel,
        grid=(b, hh, nc),
        in_specs=[pl.BlockSpec((c, dk), lambda i, j, l: (i * nc + l, j)),
                  pl.BlockSpec((c, dk), lambda i, j, l: (i * nc + l, j)),
                  pl.BlockSpec((c, dv), lambda i, j, l: (i * nc + l, j)),
                  pl.BlockSpec((None, None, dk, dv), lambda i, j, l: (i, j, 0, 0)),
                  pl.BlockSpec((None, c, c), lambda i, j, l: (j, 0, 0)),
                  pl.BlockSpec((None, c, 1), lambda i, j, l: (j, 0, 0)),
                  pl.BlockSpec((None, c, 1), lambda i, j, l: (j, 0, 0)),
                  pl.BlockSpec((None, 1, 1), lambda i, j, l: (j, 0, 0))],
        out_specs=[pl.BlockSpec((c, dv), lambda i, j, l: (i * nc + l, j)),
                   pl.BlockSpec((None, None, dk, dv), lambda i, j, l: (i, j, 0, 0))],
        out_shape=[jax.ShapeDtypeStruct((b * t, hh * dv), F32),
                   jax.ShapeDtypeStruct((b, hh, dk, dv), F32)],
        compiler_params=_params("parallel", "parallel", "arbitrary"),
    )(q, k, v, s0, dmask, xi, zeta, g_c)


def _rope_tables_full(pos, dim, theta):
    half = dim // 2
    inv = theta ** (-jnp.arange(half, dtype=F32) / half)
    ang = pos.astype(F32)[:, None] * inv[None, :]
    return jnp.cos(ang), jnp.sin(ang)


def _mixer_a(x, g, pos, bsz, s0, wcat, wo, dk, dv):
    n, d = x.shape
    t = n // bsz
    cos, sin = _rope_tables_full(pos, dk, RET_THETA)
    if t % RET_CHUNK:
        cos = jnp.tile(cos, (bsz, 1))
        sin = jnp.tile(sin, (bsz, 1))
    q, k, v, sg = _proj(
        x, g, wcat, [cos, sin], [], functools.partial(_ret_epilogue, dk, dv),
        [(_rows(RET_HEADS * dk), F32), (_rows(RET_HEADS * dk), F32),
         (_rows(RET_HEADS * dv), BF16), (_rows(RET_HEADS * dv), F32)], cos.shape[0])
    if t % RET_CHUNK:
        pad = lambda a: jnp.pad(a.reshape(bsz, t, -1), ((0, 0), (0, RET_CHUNK - t), (0, 0))
                                ).reshape(bsz * RET_CHUNK, -1)
        o, s_new = _retention(pad(q), pad(k), pad(v), s0, t)
        o = o.reshape(bsz, RET_CHUNK, -1)[:, :t].reshape(n, -1)
    else:
        o, s_new = _retention(q, k, v, s0, t)
    return _outproj(x, wo, [sg, o], lambda a, b: a * b), s_new


def _gelu_tanh(x):
    return 0.5 * x * (1.0 + jnp.tanh(math.sqrt(2.0 / math.pi) * (x + 0.044715 * (x * x * x))))


def _lru_gates(xb, wa_ref, ba, wi_ref, bi, sp):
    bw = wa_ref.shape[1]
    xbb = xb.astype(BF16)
    r = jnp.concatenate([_dot(xbb[:, n * bw:(n + 1) * bw], wa_ref[n]) for n in range(LRU_BLOCKS)], axis=1)
    i = jnp.concatenate([_dot(xbb[:, n * bw:(n + 1) * bw], wi_ref[n]) for n in range(LRU_BLOCKS)], axis=1)
    r = jax.nn.sigmoid(r + ba)
    i = jax.nn.sigmoid(i + bi)
    log_a = -LRU_C * r * sp
    a = jnp.exp(log_a)
    th = jnp.tanh(log_a)
    bt = jnp.sqrt(-2.0 * th / (1.0 - th)) * (i * xb)
    return a, bt


def _lru_prompt_kernel(x_ref, g_ref, wyx_ref, byx_ref, cw_ref, cb_ref, wa_ref, ba_ref, wi_ref, bi_ref,
                       sp_ref, wo_ref, o_ref, hl_ref, cn_ref, xpad_ref, a_ref, b_ref, hs_ref, h_ref):
    j = pl.program_id(1)
    tm, dr = a_ref.shape
    pad = xpad_ref.shape[0] - tm

    @pl.when(j == 0)
    def _():
        h_ref[...] = jnp.zeros_like(h_ref)
        xpad_ref[0:pad, :] = jnp.zeros((pad, dr), F32)

    x = x_ref[...]
    h = (_rms(x) * g_ref[...]).astype(BF16)
    z = _dot(h, wyx_ref[...]) + byx_ref[...]
    gate = _gelu_tanh(z[:, :dr])
    xpad_ref[pad:, :] = z[:, dr:]
    xb = cb_ref[...] + z[:, dr:] * cw_ref[CONV_W - 1:CONV_W, :]
    for s in range(1, CONV_W):
        xb = xb + xpad_ref[pad - s:pad - s + tm, :] * cw_ref[CONV_W - 1 - s:CONV_W - s, :]
    xpad_ref[0:pad, :] = xpad_ref[tm:tm + pad, :]
    a, bt = _lru_gates(xb, wa_ref, ba_ref[...], wi_ref, bi_ref[...], sp_ref[...])
    a_ref[...] = a
    b_ref[...] = bt

    def step(t, hprev):
        hn = a_ref[pl.ds(t, 1), :] * hprev + b_ref[pl.ds(t, 1), :]
        hs_ref[pl.ds(t, 1), :] = hn
        return hn

    hlast = lax.fori_loop(0, tm, step, h_ref[...])
    h_ref[...] = hlast
    o_ref[...] = x + _dot((gate * hs_ref[...]).astype(BF16), wo_ref[...])
    hl_ref[...] = hlast
    cn_ref[...] = xpad_ref[pad - (CONV_W - 1):pad, :]


def _lru_prompt(x, bsz, g, wyx, byx, cw, cb, wa, ba, wi, bi, sp, wo):
    n, d = x.shape
    t = n // bsz
    dr = wo.shape[0]
    tm = 256 if t % 256 == 0 else t
    nt = t // tm
    pad = 8
    vec = lambda a: a.reshape(1, -1)
    out, hl, cn = pl.pallas_call(
        _lru_prompt_kernel,
        grid=(bsz, nt),
        in_specs=[pl.BlockSpec((tm, d), lambda i, j: (i * nt + j, 0)), _const_spec((1, d)),
                  _const_spec(wyx.shape), _const_spec((1, 2 * dr)), _const_spec(cw.shape),
                  _const_spec((1, dr)), _const_spec(wa.shape), _const_spec((1, dr)),
                  _const_spec(wi.shape), _const_spec((1, dr)), _const_spec((1, dr)),
                  _const_spec(wo.shape)],
        out_specs=[pl.BlockSpec((tm, d), lambda i, j: (i * nt + j, 0)),
                   pl.BlockSpec((None, 1, dr), lambda i, j: (i, 0, 0)),
                   pl.BlockSpec((None, CONV_W - 1, dr), lambda i, j: (i, 0, 0))],
        out_shape=[jax.ShapeDtypeStruct((n, d), F32), jax.ShapeDtypeStruct((bsz, 1, dr), F32),
                   jax.ShapeDtypeStruct((bsz, CONV_W - 1, dr), F32)],
        scratch_shapes=[pltpu.VMEM((tm + pad, dr), F32), pltpu.VMEM((tm, dr), F32),
                        pltpu.VMEM((tm, dr), F32), pltpu.VMEM((tm, dr), F32), pltpu.VMEM((1, dr), F32)],
        compiler_params=_params("parallel", "arbitrary"),
    )(x, vec(g), wyx, vec(byx), cw, vec(cb), wa, vec(ba), wi, vec(bi), vec(sp), wo)
    return out, hl.reshape(bsz, dr), cn


def _lru_sample_kernel(bsz, x_ref, g_ref, buf_ref, h0_ref, wyx_ref, byx_ref, cw_ref, cb_ref, wa_ref, ba_ref,
                       wi_ref, bi_ref, sp_ref, wo_ref, o_ref, hl_ref, cn_ref, xpad_ref, hs_ref):
    n, dr = hs_ref.shape
    t = n // bsz
    nb = (CONV_W - 1) * bsz
    x = x_ref[...]
    h = (_rms(x) * g_ref[...]).astype(BF16)
    z = _dot(h, wyx_ref[...]) + byx_ref[...]
    gate = _gelu_tanh(z[:, :dr])
    xpad_ref[0:nb, :] = buf_ref[...]
    xpad_ref[nb:, :] = z[:, dr:]
    xb = cb_ref[...] + xpad_ref[0:n, :] * cw_ref[0:1, :]
    for s in range(1, CONV_W):
        xb = xb + xpad_ref[s * bsz:s * bsz + n, :] * cw_ref[s:s + 1, :]
    a, bt = _lru_gates(xb, wa_ref, ba_ref[...], wi_ref, bi_ref[...], sp_ref[...])
    hcur = h0_ref[...]
    for s in range(t):
        hcur = a[s * bsz:(s + 1) * bsz, :] * hcur + bt[s * bsz:(s + 1) * bsz, :]
        hs_ref[s * bsz:(s + 1) * bsz, :] = hcur
    o_ref[...] = x + _dot((gate * hs_ref[...]).astype(BF16), wo_ref[...])
    hl_ref[...] = hcur
    cn_ref[...] = xpad_ref[n:n + nb, :]


def _lru_sample(x, bsz, h0, conv0, g, wyx, byx, cw, cb, wa, ba, wi, bi, sp, wo):
    n, d = x.shape
    t = n // bsz
    dr = wo.shape[0]
    nb = (CONV_W - 1) * bsz
    tmaj = lambda a, tt: a.reshape(bsz, tt, -1).swapaxes(0, 1).reshape(tt * bsz, -1)
    bmaj = lambda a, tt: a.reshape(tt, bsz, -1).swapaxes(0, 1)
    vec = lambda a: a.reshape(1, -1)
    args = (tmaj(x, t), vec(g), tmaj(conv0, CONV_W - 1), h0, wyx, vec(byx), cw, vec(cb), wa, vec(ba),
            wi, vec(bi), vec(sp), wo)
    out, hl, cn = pl.pallas_call(
        functools.partial(_lru_sample_kernel, bsz),
        in_specs=[pl.BlockSpec(a.shape, lambda nd=a.ndim: (0,) * nd) for a in args],
        out_specs=[pl.BlockSpec((n, d), lambda: (0, 0)), pl.BlockSpec((bsz, dr), lambda: (0, 0)),
                   pl.BlockSpec((nb, dr), lambda: (0, 0))],
        out_shape=[jax.ShapeDtypeStruct((n, d), F32), jax.ShapeDtypeStruct((bsz, dr), F32),
                   jax.ShapeDtypeStruct((nb, dr), F32)],
        scratch_shapes=[pltpu.VMEM((n + nb, dr), F32), pltpu.VMEM((n, dr), F32)],
        compiler_params=pltpu.CompilerParams(vmem_limit_bytes=VMEM_LIMIT),
    )(*args)
    return bmaj(out, t).reshape(n, d), hl, bmaj(cn, CONV_W - 1)


def _rope_tables_partial(pos, period, rot_dim, theta):
    half = rot_dim // 2
    inv = theta ** (-jnp.arange(half, dtype=F32) / half)
    ang = pos.astype(F32)[:, None] * inv[None, :]
    cos, sin = jnp.cos(ang), jnp.sin(ang)
    t = pos.shape[0]
    zh = jnp.zeros((t, half), F32)
    rest = period - rot_dim
    c = jnp.concatenate([cos, cos, jnp.ones((t, rest), F32)], axis=1)
    s1 = jnp.concatenate([-sin, zh, jnp.zeros((t, rest), F32)], axis=1)
    s2 = jnp.concatenate([zh, sin, jnp.zeros((t, rest), F32)], axis=1)
    rep = LANES // period
    return [jnp.tile(a, (1, rep)) for a in (c, s1, s2)]


def _rot(x, c, s1, s2, half):
    n = x.shape[-1]
    return x * c + pltpu.roll(x, n - half, 1) * s1 + pltpu.roll(x, half, 1) * s2


def _sort_key(x):
    bits = pltpu.bitcast(x, I32)
    return bits ^ ((bits >> 31) & 0x7FFFFFFF)


def _kth_largest(count_ge, k, shape):
    t = jnp.where(count_ge(jnp.zeros(shape, I32)) >= k, 0, INT_MIN).astype(I32)

    def body(i, t):
        cand = t | lax.shift_left(jnp.int32(1), 30 - i)
        return jnp.where(count_ge(cand) >= k, cand, t)

    return lax.fori_loop(0, 31, body, t)


def _attn_kernel(tq, tk, lk, q_pos0, k_pos0, mode, n_cmp, q_ref, k_ref, v_ref, *rest):
    if mode == "window":
        o_ref, qs_ref, acc_ref, m_scr, l_scr = rest
        m_ref = None
    else:
        m_ref, o_ref, qs_ref, acc_ref, m_scr, l_scr = rest
    p0 = q_pos0 + pl.program_id(1) * tq
    for g in range(N_KV):
        for h in range(GROUP):
            c0 = (g * GROUP + h) * HEAD_DIM
            qs_ref[g, h * tq:(h + 1) * tq, :] = q_ref[:, c0:c0 + HEAD_DIM]
    m_scr[...] = jnp.full(m_scr.shape, NEG, F32)
    l_scr[...] = jnp.zeros(l_scr.shape, F32)
    acc_ref[...] = jnp.zeros(acc_ref.shape, F32)
    hi = jnp.minimum((p0 + tq - 1 - k_pos0) // tk + 1, lk // tk)
    lo = jnp.maximum(p0 - (WINDOW - 1) - k_pos0, 0) // tk if mode == "window" else 0
    rowpos = p0 + lax.broadcasted_iota(I32, (tq, tk), 0)

    def body(kt, carry):
        k0 = pl.multiple_of(kt * tk, tk)
        colpos = k_pos0 + k0 + lax.broadcasted_iota(I32, (tq, tk), 1)
        valid = colpos <= rowpos
        if mode == "window":
            valid = valid & (rowpos - colpos < WINDOW) & (colpos >= 0)
        if mode == "mask":
            valid = valid & (m_ref[:, pl.ds(k0, tk)].astype(F32) > 0.5)
        for g in range(N_KV):
            vg = valid
            if mode == "blocksel":
                blk = lax.shift_right_logical(k_pos0 + k0 + lax.broadcasted_iota(I32, (LANES, tk), 1), 6)
                e = jnp.where(blk == lax.broadcasted_iota(I32, (LANES, tk), 0), 1.0, 0.0).astype(BF16)
                sel = _dot(m_ref[:, g * LANES:(g + 1) * LANES], e) > 0.5
                vg = valid & (sel | (colpos >= n_cmp * SLC_BLK))
            v4 = jnp.concatenate([vg] * GROUP, axis=0)
            kt_ = k_ref[pl.ds(k0, tk), g * HEAD_DIM:(g + 1) * HEAD_DIM]
            vt_ = v_ref[pl.ds(k0, tk), g * HEAD_DIM:(g + 1) * HEAD_DIM]
            s = jnp.where(v4, _dot_nt(qs_ref[g], kt_), NEG)
            m_old = m_scr[g]
            m_new = jnp.maximum(m_old, jnp.max(s, axis=-1, keepdims=True))
            alpha = jnp.exp(m_old - m_new)
            p = jnp.where(v4, jnp.exp(s - m_new), 0.0)
            l_scr[g] = alpha * l_scr[g] + jnp.sum(p, axis=-1, keepdims=True)
            acc_ref[g] = alpha * acc_ref[g] + _dot(p.astype(BF16), vt_)
            m_scr[g] = m_new
        return carry

    lax.fori_loop(lo, hi, body, 0)
    for g in range(N_KV):
        o = acc_ref[g] / jnp.maximum(l_scr[g], 1e-30)
        for h in range(GROUP):
            c0 = (g * GROUP + h) * HEAD_DIM
            o_ref[:, c0:c0 + HEAD_DIM] = o[h * tq:(h + 1) * tq, :]


def _attention(q, k, v, m, *, bsz, tq, q_pos0, k_pos0, mode, n_cmp=0):
    nq_rows, dq = q.shape
    lk = k.shape[1]
    tk = 512
    assert lk % tk == 0 and (nq_rows // bsz) % tq == 0
    nq = nq_rows // bsz // tq
    kern = functools.partial(_attn_kernel, tq, tk, lk, q_pos0, k_pos0, mode, n_cmp)
    in_specs = [pl.BlockSpec((tq, dq), lambda b, j: (b * nq + j, 0)),
                pl.BlockSpec((None, lk, k.shape[2]), lambda b, j: (b, 0, 0)),
                pl.BlockSpec((None, lk, v.shape[2]), lambda b, j: (b, 0, 0))]
    args = [q, k, v]
    if mode != "window":
        in_specs.append(pl.BlockSpec((tq, m.shape[1]), lambda b, j: (b * nq + j, 0)))
        args.append(m)
    return pl.pallas_call(
        kern, grid=(bsz, nq), in_specs=in_specs,
        out_specs=pl.BlockSpec((tq, dq), lambda b, j: (b * nq + j, 0)),
        out_shape=jax.ShapeDtypeStruct((nq_rows, dq), F32),
        scratch_shapes=[pltpu.VMEM((N_KV, GROUP * tq, HEAD_DIM), BF16),
                        pltpu.VMEM((N_KV, GROUP * tq, HEAD_DIM), F32),
                        pltpu.VMEM((N_KV, GROUP * tq, 1), F32), pltpu.VMEM((N_KV, GROUP * tq, 1), F32)],
        compiler_params=_params("parallel", "arbitrary"),
    )(*args)


def _head_norm_rot(z, g, tabs, n_heads, scale, refs_f32, refs_bf16):
    c, s1, s2 = tabs
    for hd in range(n_heads):
        sl = slice(hd * HEAD_DIM, (hd + 1) * HEAD_DIM)
        r = _rot(_rms(z[:, sl]) * g, c, s1, s2, ROPE_DIM // 2)
        for ref in refs_f32:
            ref[:, sl] = r
        for ref in refs_bf16:
            ref[:, sl] = (r * scale).astype(BF16)


def _dsa_epilogue(h, w_ref, tabs, cs, o):
    tq_ = [t[...] for t in tabs[:3]]
    ti_ = [t[...] for t in tabs[3:]]
    gq, gk, gik = [c[...] for c in cs]
    q_ref, k_ref, kb_ref, v_ref, vb_ref, qi_ref, ki_ref, wi_ref = o
    nq, nkv = N_HEADS * HEAD_DIM, N_KV * HEAD_DIM
    ni = IDX_HEADS * IDX_DIM
    scale = HEAD_DIM ** -0.5
    _head_norm_rot(_dot(h, w_ref[:, 0:nq]), gq, tq_, N_HEADS, scale, [], [q_ref])
    _head_norm_rot(_dot(h, w_ref[:, nq:nq + nkv]), gk, tq_, N_KV, 1.0, [k_ref], [kb_ref])
    zv = _dot(h, w_ref[:, nq + nkv:nq + 2 * nkv])
    v_ref[...] = zv
    vb_ref[...] = zv.astype(BF16)
    c0 = nq + 2 * nkv
    zi = _dot(h, w_ref[:, c0:c0 + ni])
    for ch in range(ni // LANES):
        r = _rot(zi[:, ch * LANES:(ch + 1) * LANES], *ti_, IDX_ROPE // 2).astype(BF16)
        for u in range(LANES // IDX_DIM):
            qi_ref[ch * (LANES // IDX_DIM) + u] = r[:, u * IDX_DIM:(u + 1) * IDX_DIM]
    zl = _dot(h, w_ref[:, c0 + ni:c0 + ni + LANES])
    lane = lax.broadcasted_iota(I32, zl.shape, 1)
    ms = jnp.sum(jnp.where(lane < IDX_DIM, zl * zl, 0.0), axis=-1, keepdims=True) / IDX_DIM
    r = _rot(zl * lax.rsqrt(ms + EPS) * gik, *ti_, IDX_ROPE // 2)
    ki_ref[...] = r[:, :IDX_DIM]
    wi_ref[...] = zl[:, IDX_DIM:IDX_DIM + IDX_HEADS] * (IDX_HEADS ** -0.5 * IDX_DIM ** -0.5)


def _dsa_select_kernel(tq, tk, lk, q_pos0, top, qi_ref, wi_ref, kit_ref, m_ref, key_ref, wib_ref):
    p0 = q_pos0 + pl.program_id(1) * tq
    nk = jnp.minimum((p0 + tq - 1) // tk + 1, lk // tk)
    qi = qi_ref[...].reshape(IDX_HEADS * tq, IDX_DIM)
    wi = wi_ref[...]
    for h in range(IDX_HEADS):
        wib_ref[h] = jnp.broadcast_to(wi[:, h:h + 1], (tq, tk))
    rowpos = p0 + lax.broadcasted_iota(I32, (tq, tk), 0)
    col = lax.broadcasted_iota(I32, (tq, tk), 1)

    def score_tile(kt, carry):
        k0 = pl.multiple_of(kt * tk, tk)
        s = _dot(qi, kit_ref[:, pl.ds(k0, tk)])
        acc = jnp.zeros((tq, tk), F32)
        for h in range(IDX_HEADS):
            acc = acc + jnp.maximum(s[h * tq:(h + 1) * tq, :], 0.0) * wib_ref[h]
        acc = jnp.where(k0 + col <= rowpos, acc, -jnp.inf)
        key_ref[:, pl.ds(k0, tk)] = _sort_key(acc)
        return carry

    lax.fori_loop(0, nk, score_tile, 0)

    def count_ge(cand):
        cb = jnp.broadcast_to(cand, (tq, tk))

        def body(kt, acc):
            k0 = pl.multiple_of(kt * tk, tk)
            return acc + jnp.where(key_ref[:, pl.ds(k0, tk)] >= cb, 1.0, 0.0)

        return jnp.sum(lax.fori_loop(0, nk, body, jnp.zeros((tq, tk), F32)), axis=-1, keepdims=True)

    thr = jnp.broadcast_to(_kth_largest(count_ge, float(top), (tq, 1)), (tq, tk))

    def mask_tile(kt, carry):
        k0 = pl.multiple_of(kt * tk, tk)
        sel = (k0 + col <= rowpos) & (key_ref[:, pl.ds(k0, tk)] >= thr)
        m_ref[:, pl.ds(k0, tk)] = jnp.where(sel, 1.0, 0.0).astype(BF16)
        return carry

    lax.fori_loop(0, nk, mask_tile, 0)

    def zero_tile(kt, carry):
        m_ref[:, pl.ds(pl.multiple_of(kt * tk, tk), tk)] = jnp.zeros((tq, tk), BF16)
        return carry

    lax.fori_loop(nk, lk // tk, zero_tile, 0)


def _dsa_select(qi, wi, kit, *, bsz, tq, q_pos0, top):
    lk = kit.shape[2]
    tk = 256
    assert lk % tk == 0 and top <= tk
    nq_rows = wi.shape[0]
    nq = nq_rows // bsz // tq
    return pl.pallas_call(
        functools.partial(_dsa_select_kernel, tq, tk, lk, q_pos0, top),
        grid=(bsz, nq),
        in_specs=[pl.BlockSpec((IDX_HEADS, tq, IDX_DIM), lambda b, j: (0, b * nq + j, 0)),
                  pl.BlockSpec((tq, IDX_HEADS), lambda b, j: (b * nq + j, 0)),
                  pl.BlockSpec((None, IDX_DIM, lk), lambda b, j: (b, 0, 0))],
        out_specs=pl.BlockSpec((tq, lk), lambda b, j: (b * nq + j, 0)),
        out_shape=jax.ShapeDtypeStruct((nq_rows, lk), BF16),
        scratch_shapes=[pltpu.VMEM((tq, lk), I32), pltpu.VMEM((IDX_HEADS, tq, tk), F32)],
        compiler_params=_params("parallel", "arbitrary"),
    )(qi, wi, kit)


def _nsa_epilogue(h, w_ref, tabs, cs, o):
    tb = [t[...] for t in tabs]
    gq, gsk, gwk = [c[...] for c in cs]
    (qc_ref, q_ref, ck_ref, cv_ref, sk_ref, skb_ref, sv_ref, svb_ref,
     wk_ref, wkb_ref, wv_ref, wvb_ref, gate_ref) = o
    nq, nkv = N_HEADS * HEAD_DIM, N_KV * HEAD_DIM
    scale = HEAD_DIM ** -0.5
    zq = _dot(h, w_ref[:, 0:nq])
    for hd in range(N_HEADS):
        sl = slice(hd * HEAD_DIM, (hd + 1) * HEAD_DIM)
        qn = _rms(zq[:, sl]) * gq
        qc_ref[:, sl] = (qn * scale).astype(BF16)
        q_ref[:, sl] = (_rot(qn, *tb, ROPE_DIM // 2) * scale).astype(BF16)
    seg = lambda i: _dot(h, w_ref[:, nq + i * nkv:nq + (i + 1) * nkv])
    ck_ref[...] = seg(0)
    cv_ref[...] = seg(1)
    _head_norm_rot(seg(2), gsk, tb, N_KV, 1.0, [sk_ref], [skb_ref])
    zsv = seg(3)
    sv_ref[...] = zsv
    svb_ref[...] = zsv.astype(BF16)
    _head_norm_rot(seg(4), gwk, tb, N_KV, 1.0, [wk_ref], [wkb_ref])
    zwv = seg(5)
    wv_ref[...] = zwv
    wvb_ref[...] = zwv.astype(BF16)
    zg = _dot(h, w_ref[:, nq + 6 * nkv:nq + 6 * nkv + LANES])
    gate_ref[...] = jax.nn.sigmoid(zg[:, :N_HEADS * 3])


def _nsa_compress_kernel(ck_ref, cv_ref, pwk_ref, pwv_ref, phik_ref, phiv_ref, g_ref, kc_ref, vc_ref):
    rows, wd = ck_ref.shape
    nb = rows // CMP_BLK
    pk = jnp.sum(ck_ref[...].reshape(nb, CMP_BLK, wd) * pwk_ref[...][None], axis=1)
    pv = jnp.sum(cv_ref[...].reshape(nb, CMP_BLK, wd) * pwv_ref[...][None], axis=1)
    for hd in range(N_KV):
        sl = slice(hd * HEAD_DIM, (hd + 1) * HEAD_DIM)
        kc_ref[:, sl] = (_rms(_dot(pk[:, sl].astype(BF16), phik_ref[hd])) * g_ref[...]).astype(BF16)
        vc_ref[:, sl] = _dot(pv[:, sl].astype(BF16), phiv_ref[hd]).astype(BF16)


def _nsa_compress(ck, cv, pwk, pwv, phik, phiv, gck):
    rows, wd = ck.shape
    step = 2048 if rows % 2048 == 0 else rows
    nb = step // CMP_BLK
    bc = lambda p: jnp.repeat(p, HEAD_DIM, axis=1)
    return pl.pallas_call(
        _nsa_compress_kernel, grid=(rows // step,),
        in_specs=[pl.BlockSpec((step, wd), lambda i: (i, 0)), pl.BlockSpec((step, wd), lambda i: (i, 0)),
                  _const_spec((CMP_BLK, wd)), _const_spec((CMP_BLK, wd)), _const_spec(phik.shape),
                  _const_spec(phiv.shape), _const_spec((1, HEAD_DIM))],
        out_specs=[pl.BlockSpec((nb, wd), lambda i: (i, 0)), pl.BlockSpec((nb, wd), lambda i: (i, 0))],
        out_shape=[jax.ShapeDtypeStruct((rows // CMP_BLK, wd), BF16)] * 2,
        compiler_params=_params("parallel"),
    )(ck, cv, bc(pwk), bc(pwv), phik, phiv, gck.reshape(1, HEAD_DIM))


def _nsa_cmp_kernel(tq, q_pos0, n_sel, qc_ref, kc_ref, vc_ref, oc_ref, bm_ref):
    p0 = q_pos0 + pl.program_id(1) * tq
    nc = kc_ref.shape[0]
    rowpos = p0 + lax.broadcasted_iota(I32, (tq, nc), 0)
    blk = lax.broadcasted_iota(I32, (tq, nc), 1)
    valid = jnp.concatenate([blk * CMP_BLK + (CMP_BLK - 1) <= rowpos] * GROUP, axis=0)
    cur = lax.shift_right_logical(rowpos, 6)
    forced = (blk == 0) | (blk == cur) | (blk == cur - 1)
    for g in range(N_KV):
        sl = slice(g * HEAD_DIM, (g + 1) * HEAD_DIM)
        qs = jnp.concatenate([qc_ref[:, (g * GROUP + h) * HEAD_DIM:(g * GROUP + h + 1) * HEAD_DIM]
                              for h in range(GROUP)], axis=0)
        lc = jnp.where(valid, _dot_nt(qs, kc_ref[:, sl]), NEG)
        m = jnp.max(lc, axis=-1, keepdims=True)
        m = jnp.where(m > 0.5 * NEG, m, 0.0)
        p = jnp.where(valid, jnp.exp(lc - m), 0.0)
        pc = p / jnp.maximum(jnp.sum(p, axis=-1, keepdims=True), 1e-30)
        oc = _dot(pc.astype(BF16), vc_ref[:, sl])
        for h in range(GROUP):
            c0 = (g * GROUP + h) * HEAD_DIM
            oc_ref[:, c0:c0 + HEAD_DIM] = oc[h * tq:(h + 1) * tq, :]
        imp = pc[0:tq]
        for h in range(1, GROUP):
            imp = imp + pc[h * tq:(h + 1) * tq]
        imp = jnp.where(forced, BIG, imp)
        imp = jnp.where(blk > cur, -jnp.inf, imp)
        key = _sort_key(imp)
        count_ge = lambda cand: jnp.sum(jnp.where(key >= cand, 1.0, 0.0), axis=-1, keepdims=True)
        thr = _kth_largest(count_ge, float(n_sel), (tq, 1))
        bm_ref[:, g * LANES:(g + 1) * LANES] = jnp.where(key >= thr, 1.0, 0.0).astype(BF16)


def _nsa_cmp(qc, kc, vc, *, bsz, tq, q_pos0, n_sel):
    nq_rows, dq = qc.shape
    nc = kc.shape[1]
    assert nc == LANES
    nq = nq_rows // bsz // tq
    return pl.pallas_call(
        functools.partial(_nsa_cmp_kernel, tq, q_pos0, n_sel), grid=(bsz, nq),
        in_specs=[pl.BlockSpec((tq, dq), lambda b, j: (b * nq + j, 0)),
                  pl.BlockSpec((None, nc, kc.shape[2]), lambda b, j: (b, 0, 0)),
                  pl.BlockSpec((None, nc, vc.shape[2]), lambda b, j: (b, 0, 0))],
        out_specs=[pl.BlockSpec((tq, dq), lambda b, j: (b * nq + j, 0)),
                   pl.BlockSpec((tq, N_KV * LANES), lambda b, j: (b * nq + j, 0))],
        out_shape=[jax.ShapeDtypeStruct((nq_rows, dq), F32),
                   jax.ShapeDtypeStruct((nq_rows, N_KV * LANES), BF16)],
        compiler_params=_params("parallel", "parallel"),
    )(qc, kc, vc)


def _nsa_combine(gate, oc, os_, ow):
    parts = []
    for hd in range(N_HEADS):
        sl = slice(hd * HEAD_DIM, (hd + 1) * HEAD_DIM)
        parts.append(gate[:, 3 * hd:3 * hd + 1] * oc[:, sl] + gate[:, 3 * hd + 1:3 * hd + 2] * os_[:, sl]
                     + gate[:, 3 * hd + 2:3 * hd + 3] * ow[:, sl])
    return jnp.concatenate(parts, axis=1)


SAMPLE_TQ = 16
PROMPT_TQ = 128
KEY_ALIGN = 512


def _pad_rows(a, bsz, t, tp):
    return jnp.pad(a.reshape(bsz, t, -1), ((0, 0), (0, tp - t), (0, 0))).reshape(bsz * tp, -1)


def _unpad_rows(a, bsz, t, tp):
    return a.reshape(bsz, tp, -1)[:, :t].reshape(bsz * t, -1)


def _cat_keys(old, new, bsz):
    new = new.reshape(bsz, -1, new.shape[-1])
    n = old.shape[1] + new.shape[1]
    return jnp.pad(jnp.concatenate([old.astype(new.dtype), new], axis=1),
                   ((0, 0), (0, -n % KEY_ALIGN), (0, 0)))


def _tile_tables(tabs, bsz, t):
    return tabs if t % ROW_TILE == 0 else [jnp.tile(a, (bsz, 1)) for a in tabs]


def _mixer_c(x, bsz, pos, past, g, wcat, gq, gk, gik, wo):
    n, d = x.shape
    t = n // bsz
    tabs = _tile_tables(_rope_tables_partial(pos, HEAD_DIM, ROPE_DIM, ROPE_THETA)
                        + _rope_tables_partial(pos, IDX_DIM, IDX_ROPE, ROPE_THETA), bsz, t)
    nkv = N_KV * HEAD_DIM
    q, k, kb, v, vb, qi, ki, wi = _proj(
        x, g, wcat, tabs, [gq, gk, gik], _dsa_epilogue,
        [(_rows(N_HEADS * HEAD_DIM), BF16), (_rows(nkv), F32), (_rows(nkv), BF16), (_rows(nkv), F32),
         (_rows(nkv), BF16),
         (lambda n_, tm: ((IDX_HEADS, n_, IDX_DIM), (IDX_HEADS, tm, IDX_DIM), lambda i: (0, i, 0)), BF16),
         (_rows(IDX_DIM), F32), (_rows(IDX_HEADS), F32)], tabs[0].shape[0])
    if past is None:
        tq, q_pos0, n_keys = PROMPT_TQ, 0, t
        kit = ki.reshape(bsz, t, IDX_DIM).swapaxes(1, 2).astype(BF16)
        kall, vall = kb.reshape(bsz, t, nkv), vb.reshape(bsz, t, nkv)
    else:
        tq, q_pos0 = SAMPLE_TQ, past["k"].shape[1]
        n_keys = q_pos0 + t
        kit = _cat_keys(past["kidx"], ki, bsz).swapaxes(1, 2).astype(BF16)
        kall = _cat_keys(past["k"], k, bsz).astype(BF16)
        vall = _cat_keys(past["v"], v, bsz).astype(BF16)
        q, wi = _pad_rows(q, bsz, t, tq), _pad_rows(wi, bsz, t, tq)
        qi = jnp.pad(qi.reshape(IDX_HEADS, bsz, t, IDX_DIM), ((0, 0), (0, 0), (0, tq - t), (0, 0))
                     ).reshape(IDX_HEADS, bsz * tq, IDX_DIM)
    mask = _dsa_select(qi, wi, kit, bsz=bsz, tq=tq, q_pos0=q_pos0, top=min(TOPK_MAX, n_keys // 4))
    o = _attention(q, kall, vall, mask, bsz=bsz, tq=tq, q_pos0=q_pos0, k_pos0=0, mode="mask")
    if past is not None:
        o = _unpad_rows(o, bsz, t, tq)
    return _outproj(x, wo, [o], lambda a: a), k, v, ki


def _mixer_d(x, bsz, pos, past, g, wcat, gq, gsk, gwk, cmpw, wo):
    n, d = x.shape
    t = n // bsz
    tabs = _tile_tables(_rope_tables_partial(pos, HEAD_DIM, ROPE_DIM, ROPE_THETA), bsz, t)
    nkv = N_KV * HEAD_DIM
    kvo = [(_rows(nkv), F32), (_rows(nkv), BF16)]
    (qc, q, ck, cv, sk, skb, sv, svb, wk, wkb, wv, wvb, gate) = _proj(
        x, g, wcat, tabs, [gq, gsk, gwk], _nsa_epilogue,
        [(_rows(N_HEADS * HEAD_DIM), BF16), (_rows(N_HEADS * HEAD_DIM), BF16), (_rows(nkv), F32),
         (_rows(nkv), F32)] + kvo * 4 + [(_rows(N_HEADS * 3), F32)], tabs[0].shape[0])
    if past is None:
        tq, q_pos0, n_keys, win_pos0 = PROMPT_TQ, 0, t, 0
        kc, vc = _nsa_compress(ck, cv, *cmpw)
        three = lambda a: a.reshape(bsz, t, nkv)
        skall, svall, wkall, wvall = three(skb), three(svb), three(wkb), three(wvb)
    else:
        tq, q_pos0 = SAMPLE_TQ, past["sk"].shape[1]
        n_keys = q_pos0 + t
        win_pos0 = q_pos0 - past["wk"].shape[1]
        assert q_pos0 % CMP_BLK == 0 and t < CMP_BLK
        kc, vc = _nsa_compress(past["ck"].reshape(bsz * q_pos0, nkv), past["cv"].reshape(bsz * q_pos0, nkv),
                               *cmpw)
        skall = _cat_keys(past["sk"], sk, bsz).astype(BF16)
        svall = _cat_keys(past["sv"], sv, bsz).astype(BF16)
        wkall = _cat_keys(past["wk"], wk, bsz).astype(BF16)
        wvall = _cat_keys(past["wv"], wv, bsz).astype(BF16)
        qc, q = _pad_rows(qc, bsz, t, tq), _pad_rows(q, bsz, t, tq)
    n_cmp = kc.shape[0] // bsz
    assert n_cmp <= LANES
    lane_pad = lambda a: jnp.pad(a.reshape(bsz, n_cmp, nkv), ((0, 0), (0, LANES - n_cmp), (0, 0)))
    kc, vc = lane_pad(kc), lane_pad(vc)
    n_slc = -(-n_keys // SLC_BLK)
    n_lane = min(n_slc, LANES)
    assert n_slc == n_lane or (n_slc == n_lane + 1 and q_pos0 // SLC_BLK == n_lane)
    n_sel = min(N_SLC, n_slc) - (n_slc - n_lane)
    oc, bm = _nsa_cmp(qc, kc, vc, bsz=bsz, tq=tq, q_pos0=q_pos0, n_sel=n_sel)
    os_ = _attention(q, skall, svall, bm, bsz=bsz, tq=tq, q_pos0=q_pos0, k_pos0=0, mode="blocksel",
                     n_cmp=n_lane)
    ow = _attention(q, wkall, wvall, None, bsz=bsz, tq=tq, q_pos0=q_pos0, k_pos0=win_pos0, mode="window")
    if past is not None:
        oc, os_, ow = (_unpad_rows(a, bsz, t, tq) for a in (oc, os_, ow))
    return _outproj(x, wo, [gate, oc, os_, ow], _nsa_combine), (ck, cv, sk, sv, wk, wv)


def _gather_pages(cache, page_table):
    rows = cache[page_table]
    return rows.reshape(page_table.shape[0], page_table.shape[1] * PAGE_SIZE, -1)


def kernel(x_prompt, x_sample, state_a_ret, state_b_h, state_b_conv, cache_c_k, cache_c_v, cache_c_kidx,
           cache_d_ck, cache_d_cv, cache_d_sk, cache_d_sv, state_d_wk, state_d_wv, page_table,
           ffn1_norm, ffn1_wg, ffn1_wu, ffn1_wd, mix_norm, ffn2_norm, ffn2_wg, ffn2_wu, ffn2_wd,
           a_wq, a_wk, a_wv, a_wg, a_wo,
           b_wy, b_by, b_wx, b_bx, b_conv_w, b_conv_b, b_wa, b_ba, b_wi, b_bi, b_lam, b_wo,
           c_wq, c_gq, c_wk, c_gk, c_wv, c_wo, c_wiq, c_wik, c_gik, c_wiw,
           d_wq, d_gq, d_wck, d_wcv, d_pwk, d_pwv, d_phik, d_phiv, d_gck, d_wsk, d_wsv, d_gsk,
           d_wwk, d_wwv, d_gwk, d_wgate, d_wo):
    bp, tp, d = x_prompt.shape
    bs, ts, _ = x_sample.shape
    depth = ffn1_norm.shape[0]
    past_len = page_table.shape[1] * PAGE_SIZE
    pos_p = jnp.arange(tp, dtype=I32)
    pos_s = past_len + jnp.arange(ts, dtype=I32)
    b16 = lambda a: a.astype(BF16)
    row = lambda a: a.reshape(1, -1)
    nkv = N_KV * HEAD_DIM

    xp = x_prompt.reshape(bp * tp, d)
    xs = x_sample.reshape(bs * ts, d)
    outs = [[] for _ in range(24)]
    for i in range(depth):
        m, j = i % 4, i // 4
        w1 = (ffn1_norm[i], b16(ffn1_wg[i]), b16(ffn1_wu[i]), b16(ffn1_wd[i]))
        xp, xs = _ffn(xp, *w1), _ffn(xs, *w1)
        g = mix_norm[i]
        if m == 0:
            dk = a_wq.shape[2] // RET_HEADS
            dv = a_wv.shape[2] // RET_HEADS
            wcat = b16(jnp.concatenate([a_wq[j], a_wk[j], a_wv[j], a_wg[j]], axis=1))
            wo = b16(a_wo[j])
            xp, s_p = _mixer_a(xp, g, pos_p, bp, jnp.zeros((bp, RET_HEADS, dk, dv), F32), wcat, wo, dk, dv)
            xs, s_s = _mixer_a(xs, g, pos_s, bs, state_a_ret[j], wcat, wo, dk, dv)
            new = [s_p, s_s]
            base = 0
        elif m == 1:
            common = (g, b16(jnp.concatenate([b_wy[j], b_wx[j]], axis=1)), jnp.concatenate([b_by[j], b_bx[j]]),
                      b_conv_w[j], b_conv_b[j], b16(b_wa[j]), b_ba[j].reshape(-1), b16(b_wi[j]),
                      b_bi[j].reshape(-1), jax.nn.softplus(-b_lam[j]), b16(b_wo[j]))
            xp, h_p, c_p = _lru_prompt(xp, bp, *common)
            xs, h_s, c_s = _lru_sample(xs, bs, state_b_h[j], state_b_conv[j], *common)
            new = [h_p, h_s, c_p, c_s]
            base = 2
        elif m == 2:
            zpad = jnp.zeros((d, LANES - IDX_DIM - IDX_HEADS), F32)
            wcat = b16(jnp.concatenate([c_wq[j], c_wk[j], c_wv[j], c_wiq[j], c_wik[j], c_wiw[j], zpad], axis=1))
            gik = jnp.concatenate([c_gik[j], jnp.zeros((LANES - IDX_DIM,), F32)])
            cw = (g, wcat, row(c_gq[j]), row(c_gk[j]), row(gik), b16(c_wo[j]))
            past = dict(kidx=_gather_pages(cache_c_kidx[j], page_table),
                        k=_gather_pages(cache_c_k[j], page_table), v=_gather_pages(cache_c_v[j], page_table))
            xp, k_p, v_p, i_p = _mixer_c(xp, bp, pos_p, None, *cw)
            xs, k_s, v_s, i_s = _mixer_c(xs, bs, pos_s, past, *cw)
            kv = lambda a, b_, t_: a.reshape(b_, t_, N_KV, HEAD_DIM)
            new = [kv(k_p, bp, tp), kv(k_s, bs, ts), kv(v_p, bp, tp), kv(v_s, bs, ts),
                   i_p.reshape(bp, tp, IDX_DIM), i_s.reshape(bs, ts, IDX_DIM)]
            base = 6
        else:
            zpad = jnp.zeros((d, LANES - N_HEADS * 3), F32)
            wcat = b16(jnp.concatenate([d_wq[j], d_wck[j], d_wcv[j], d_wsk[j], d_wsv[j], d_wwk[j], d_wwv[j],
                                        d_wgate[j], zpad], axis=1))
            cmpw = (d_pwk[j], d_pwv[j], b16(d_phik[j]), b16(d_phiv[j]), d_gck[j])
            dw = (g, wcat, row(d_gq[j]), row(d_gsk[j]), row(d_gwk[j]), cmpw, b16(d_wo[j]))
            wbuf = state_d_wk.shape[2]
            past = dict(ck=_gather_pages(cache_d_ck[j], page_table), cv=_gather_pages(cache_d_cv[j], page_table),
                        sk=_gather_pages(cache_d_sk[j], page_table), sv=_gather_pages(cache_d_sv[j], page_table),
                        wk=state_d_wk[j].reshape(bs, wbuf, nkv), wv=state_d_wv[j].reshape(bs, wbuf, nkv))
            xp, rows_p = _mixer_d(xp, bp, pos_p, None, *dw)
            xs, rows_s = _mixer_d(xs, bs, pos_s, past, *dw)
            kv = lambda a, b_: a.reshape(b_, -1, N_KV, HEAD_DIM)
            new = []
            for a_p, a_s in zip(rows_p[:4], rows_s[:4]):
                new += [kv(a_p, bp), kv(a_s, bs)]
            wb_p = min(WINDOW, tp)
            for a_p, a_s, st in ((rows_p[4], rows_s[4], state_d_wk[j]), (rows_p[5], rows_s[5], state_d_wv[j])):
                new += [kv(a_p, bp)[:, tp - wb_p:], jnp.concatenate([st, kv(a_s, bs)], axis=1)[:, ts:]]
            base = 12
        for off, a in enumerate(new):
            outs[base + off].append(a)
        w2 = (ffn2_norm[i], b16(ffn2_wg[i]), b16(ffn2_wu[i]), b16(ffn2_wd[i]))
        xp, xs = _ffn(xp, *w2), _ffn(xs, *w2)
    return (xp.reshape(bp, tp, d), xs.reshape(bs, ts, d)) + tuple(jnp.stack(o) for o in outs)
```

```python
import functools
import math

import jax
import jax.numpy as jnp
from jax import lax
from jax.experimental import pallas as pl
from jax.experimental.pallas import tpu as pltpu

F32 = jnp.float32
BF16 = jnp.bfloat16
I32 = jnp.int32

EPS = 1e-6
PAGE_SIZE = 128

RET_HEADS = 4
RET_CHUNK = 128
RET_THETA = 10000.0

LRU_BLOCKS = 4
CONV_W = 4
LRU_C = 8.0

N_HEADS = 8
HEAD_DIM = 128
N_KV = 2
GROUP = N_HEADS // N_KV
ROPE_DIM = HEAD_DIM // 4
ROPE_THETA = 500000.0

IDX_HEADS = 16
IDX_DIM = 64
IDX_ROPE = IDX_DIM // 4
TOPK_MAX = 256

CMP_BLK = 64
SLC_BLK = 64
N_SLC = 16
WINDOW = 512
BIG = 1e4

LANES = 128
VMEM_LIMIT = 56 * 1024 * 1024
ROW_TILE = 512
NEG = -1e30
ATTN_SCALE = HEAD_DIM ** -0.5
LOG2E = math.log2(math.e)
INT_MIN = -2147483648


def _params(*sem):
    return pltpu.CompilerParams(dimension_semantics=sem, vmem_limit_bytes=VMEM_LIMIT)


def _const_spec(shape):
    nd = len(shape)
    return pl.BlockSpec(shape, lambda *_: (0,) * nd, pipeline_mode=pl.Buffered(1))


def _rms(x):
    return x * lax.rsqrt(jnp.mean(x * x, axis=-1, keepdims=True) + EPS)


def _dot(a, b):
    return jnp.dot(a, b, preferred_element_type=F32)


def _dot_nt(a, b):
    return lax.dot_general(a, b, (((1,), (1,)), ((), ())), preferred_element_type=F32)


def _row_tile(n):
    return ROW_TILE if n % ROW_TILE == 0 else n


def _ffn_kernel(x_ref, g_ref, wg_ref, wu_ref, wd_ref, o_ref):
    x = x_ref[...]
    h = (_rms(x) * g_ref[...]).astype(BF16)
    gt = _dot(h, wg_ref[...])
    ut = _dot(h, wu_ref[...])
    a = (gt * jax.nn.sigmoid(gt) * ut).astype(BF16)
    o_ref[...] = x + 0.5 * _dot(a, wd_ref[...])


def _ffn(x, g, wg, wu, wd):
    n, d = x.shape
    f = wg.shape[1]
    tm = _row_tile(n)
    return pl.pallas_call(
        _ffn_kernel,
        grid=(n // tm,),
        in_specs=[pl.BlockSpec((tm, d), lambda i: (i, 0)), _const_spec((1, d)),
                  _const_spec((d, f)), _const_spec((d, f)), _const_spec((f, d))],
        out_specs=pl.BlockSpec((tm, d), lambda i: (i, 0)),
        out_shape=jax.ShapeDtypeStruct((n, d), F32),
        compiler_params=_params("parallel"), name="ffn",
    )(x, g.reshape(1, d), wg, wu, wd)


def _proj(x, g, w, tables, consts, epilogue, outs, n_tab_rows, name):
    n, d = x.shape
    tm = _row_tile(n)
    assert n_tab_rows % tm == 0
    nt = n_tab_rows // tm
    nw = w.shape[1]

    def kern(x_ref, g_ref, w_ref, *rest):
        tabs = rest[:len(tables)]
        cs = rest[len(tables):len(tables) + len(consts)]
        o = rest[len(tables) + len(consts):]
        h = (_rms(x_ref[...]) * g_ref[...]).astype(BF16)
        epilogue(h, w_ref, tabs, cs, o)

    in_specs = [pl.BlockSpec((tm, d), lambda i: (i, 0)), _const_spec((1, d)), _const_spec((d, nw))]
    in_specs += [pl.BlockSpec((tm, t.shape[1]), lambda i: (i % nt, 0)) for t in tables]
    in_specs += [_const_spec(c.shape) for c in consts]
    out_shape, out_specs = [], []
    for fn, dt in outs:
        ashape, bshape, imap = fn(n, tm)
        out_shape.append(jax.ShapeDtypeStruct(ashape, dt))
        out_specs.append(pl.BlockSpec(bshape, imap))
    return pl.pallas_call(
        kern, grid=(n // tm,), in_specs=in_specs, out_specs=out_specs, out_shape=out_shape,
        compiler_params=_params("parallel"), name=name,
    )(x, g.reshape(1, d), w, *tables, *consts)


def _rows(c):
    return lambda n, tm: ((n, c), (tm, c), lambda i: (i, 0))


def _outproj(x, w, ins, combine):
    n, d = x.shape
    tm = _row_tile(n)

    def kern(x_ref, w_ref, *rest):
        o_ref = rest[-1]
        a = combine(*[r[...] for r in rest[:-1]]).astype(BF16)
        o_ref[...] = x_ref[...] + _dot(a, w_ref[...])

    return pl.pallas_call(
        kern, grid=(n // tm,),
        in_specs=[pl.BlockSpec((tm, d), lambda i: (i, 0)), _const_spec(w.shape)]
        + [pl.BlockSpec((tm, a.shape[1]), lambda i: (i, 0)) for a in ins],
        out_specs=pl.BlockSpec((tm, d), lambda i: (i, 0)),
        out_shape=jax.ShapeDtypeStruct((n, d), F32),
        compiler_params=_params("parallel"), name="outproj",
    )(x, w, *ins)


def _ret_epilogue(dk, dv, h, w_ref, tabs, cs, o):
    cos, sin = tabs[0][...], tabs[1][...]
    q_ref, k_ref, v_ref, g_ref = o
    half = dk // 2
    nqk = RET_HEADS * dk
    for seg, ref, scale in ((0, q_ref, 1.0), (1, k_ref, dk ** -0.5)):
        z = _dot(h, w_ref[:, seg * nqk:(seg + 1) * nqk])
        for hd in range(RET_HEADS):
            x1 = z[:, hd * dk:hd * dk + half]
            x2 = z[:, hd * dk + half:(hd + 1) * dk]
            ref[:, hd * dk:hd * dk + half] = (x1 * cos - x2 * sin) * scale
            ref[:, hd * dk + half:(hd + 1) * dk] = (x2 * cos + x1 * sin) * scale
    nv = RET_HEADS * dv
    v_ref[...] = _dot(h, w_ref[:, 2 * nqk:2 * nqk + nv]).astype(BF16)
    zg = _dot(h, w_ref[:, 2 * nqk + nv:2 * nqk + 2 * nv])
    g_ref[...] = zg * jax.nn.sigmoid(zg)


def _ret_chunk_kernel(q_ref, k_ref, v_ref, s0_ref, dm_ref, xi_ref, zt_ref, gc_ref, o_ref, s_ref):
    c = pl.program_id(2)

    @pl.when(c == 0)
    def _():
        s_ref[...] = s0_ref[...]

    q = q_ref[...]
    k = k_ref[...]
    v = v_ref[...]
    s = s_ref[...]
    att = (_dot_nt(q.astype(BF16), k.astype(BF16)) * dm_ref[...]).astype(BF16)
    o = _dot(att, v) + _dot((q * xi_ref[...]).astype(BF16), s.astype(BF16))
    kz = (k * zt_ref[...]).astype(BF16)
    s_ref[...] = s * gc_ref[...] + lax.dot_general(kz, v, (((0,), (0,)), ((), ())),
                                                   preferred_element_type=F32)
    o_ref[...] = _rms(o)


def _retention(q, k, v, s0, n_valid):
    b, hh, dk, dv = s0.shape
    t = q.shape[0] // b
    c = RET_CHUNK
    nc = t // c
    cc = c if n_valid % c == 0 else n_valid
    lg = jnp.log1p(-jnp.exp2(-5.0 - jnp.arange(hh, dtype=F32)))
    idx = jnp.arange(c, dtype=F32)
    diff = idx[:, None] - idx[None, :]
    dmask = jnp.where(diff >= 0, jnp.exp(lg[:, None, None] * jnp.maximum(diff, 0.0)), 0.0)
    xi = jnp.exp(lg[:, None] * (idx + 1.0))[:, :, None]
    zeta = jnp.exp(lg[:, None] * (cc - 1.0 - idx))[:, :, None]
    g_c = jnp.exp(lg * cc)[:, None, None]
    return pl.pallas_call(
        _ret_chunk_kernel,
        grid=(b, hh, nc),
        in_specs=[pl.BlockSpec((c, dk), lambda i, j, l: (i * nc + l, j)),
                  pl.BlockSpec((c, dk), lambda i, j, l: (i * nc + l, j)),
                  pl.BlockSpec((c, dv), lambda i, j, l: (i * nc + l, j)),
                  pl.BlockSpec((None, None, dk, dv), lambda i, j, l: (i, j, 0, 0)),
                  pl.BlockSpec((None, c, c), lambda i, j, l: (j, 0, 0)),
                  pl.BlockSpec((None, c, 1), lambda i, j, l: (j, 0, 0)),
                  pl.BlockSpec((None, c, 1), lambda i, j, l: (j, 0, 0)),
                  pl.BlockSpec((None, 1, 1), lambda i, j, l: (j, 0, 0))],
        out_specs=[pl.BlockSpec((c, dv), lambda i, j, l: (i * nc + l, j)),
                   pl.BlockSpec((None, None, dk, dv), lambda i, j, l: (i, j, 0, 0))],
        out_shape=[jax.ShapeDtypeStruct((b * t, hh * dv), F32),
                   jax.ShapeDtypeStruct((b, hh, dk, dv), F32)],
        compiler_params=_params("parallel", "parallel", "arbitrary"), name="retention",
    )(q, k, v, s0, dmask, xi, zeta, g_c)


def _rope_tables_full(pos, dim, theta):
    half = dim // 2
    inv = theta ** (-jnp.arange(half, dtype=F32) / half)
    ang = pos.astype(F32)[:, None] * inv[None, :]
    return jnp.cos(ang), jnp.sin(ang)


def _mixer_a(x, g, pos, bsz, s0, wcat, wo, dk, dv):
    n, d = x.shape
    t = n // bsz
    cos, sin = _rope_tables_full(pos, dk, RET_THETA)
    if t % RET_CHUNK:
        cos = jnp.tile(cos, (bsz, 1))
        sin = jnp.tile(sin, (bsz, 1))
    q, k, v, sg = _proj(
        x, g, wcat, [cos, sin], [], functools.partial(_ret_epilogue, dk, dv),
        [(_rows(RET_HEADS * dk), F32), (_rows(RET_HEADS * dk), F32),
         (_rows(RET_HEADS * dv), BF16), (_rows(RET_HEADS * dv), F32)], cos.shape[0], "proj_ret")
    if t % RET_CHUNK:
        pad = lambda a: jnp.pad(a.reshape(bsz, t, -1), ((0, 0), (0, RET_CHUNK - t), (0, 0))
                                ).reshape(bsz * RET_CHUNK, -1)
        o, s_new = _retention(pad(q), pad(k), pad(v), s0, t)
        o = o.reshape(bsz, RET_CHUNK, -1)[:, :t].reshape(n, -1)
    else:
        o, s_new = _retention(q, k, v, s0, t)
    return _outproj(x, wo, [sg, o], lambda a, b: a * b), s_new


def _gelu_tanh(x):
    return 0.5 * x * (1.0 + jnp.tanh(math.sqrt(2.0 / math.pi) * (x + 0.044715 * (x * x * x))))


def _lru_gates(xb, wa_ref, ba, wi_ref, bi, sp):
    bw = wa_ref.shape[1]
    xbb = xb.astype(BF16)
    r = jnp.concatenate([_dot(xbb[:, n * bw:(n + 1) * bw], wa_ref[n]) for n in range(LRU_BLOCKS)], axis=1)
    i = jnp.concatenate([_dot(xbb[:, n * bw:(n + 1) * bw], wi_ref[n]) for n in range(LRU_BLOCKS)], axis=1)
    r = jax.nn.sigmoid(r + ba)
    i = jax.nn.sigmoid(i + bi)
    log_a = -LRU_C * r * sp
    a = jnp.exp(log_a)
    th = jnp.tanh(log_a)
    bt = jnp.sqrt(-2.0 * th / (1.0 - th)) * (i * xb)
    return a, bt


def _lru_prompt_kernel(x_ref, g_ref, wyx_ref, byx_ref, cw_ref, cb_ref, wa_ref, ba_ref, wi_ref, bi_ref,
                       sp_ref, wo_ref, o_ref, hl_ref, cn_ref, xpad_ref, a_ref, b_ref, hs_ref, h_ref):
    j = pl.program_id(1)
    tm, dr = a_ref.shape
    pad = xpad_ref.shape[0] - tm

    @pl.when(j == 0)
    def _():
        h_ref[...] = jnp.zeros_like(h_ref)
        xpad_ref[0:pad, :] = jnp.zeros((pad, dr), F32)

    x = x_ref[...]
    h = (_rms(x) * g_ref[...]).astype(BF16)
    z = _dot(h, wyx_ref[...]) + byx_ref[...]
    gate = _gelu_tanh(z[:, :dr])
    xpad_ref[pad:, :] = z[:, dr:]
    xb = cb_ref[...] + z[:, dr:] * cw_ref[CONV_W - 1:CONV_W, :]
    for s in range(1, CONV_W):
        xb = xb + xpad_ref[pad - s:pad - s + tm, :] * cw_ref[CONV_W - 1 - s:CONV_W - s, :]
    xpad_ref[0:pad, :] = xpad_ref[tm:tm + pad, :]
    a, bt = _lru_gates(xb, wa_ref, ba_ref[...], wi_ref, bi_ref[...], sp_ref[...])
    a_ref[...] = a
    b_ref[...] = bt

    def step(t, hprev):
        hn = a_ref[pl.ds(t, 1), :] * hprev + b_ref[pl.ds(t, 1), :]
        hs_ref[pl.ds(t, 1), :] = hn
        return hn

    hlast = lax.fori_loop(0, tm, step, h_ref[...])
    h_ref[...] = hlast
    o_ref[...] = x + _dot((gate * hs_ref[...]).astype(BF16), wo_ref[...])
    hl_ref[...] = hlast
    cn_ref[...] = xpad_ref[pad - (CONV_W - 1):pad, :]


def _lru_prompt(x, bsz, g, wyx, byx, cw, cb, wa, ba, wi, bi, sp, wo):
    n, d = x.shape
    t = n // bsz
    dr = wo.shape[0]
    tm = 256 if t % 256 == 0 else t
    nt = t // tm
    pad = 8
    vec = lambda a: a.reshape(1, -1)
    out, hl, cn = pl.pallas_call(
        _lru_prompt_kernel,
        grid=(bsz, nt),
        in_specs=[pl.BlockSpec((tm, d), lambda i, j: (i * nt + j, 0)), _const_spec((1, d)),
                  _const_spec(wyx.shape), _const_spec((1, 2 * dr)), _const_spec(cw.shape),
                  _const_spec((1, dr)), _const_spec(wa.shape), _const_spec((1, dr)),
                  _const_spec(wi.shape), _const_spec((1, dr)), _const_spec((1, dr)),
                  _const_spec(wo.shape)],
        out_specs=[pl.BlockSpec((tm, d), lambda i, j: (i * nt + j, 0)),
                   pl.BlockSpec((None, 1, dr), lambda i, j: (i, 0, 0)),
                   pl.BlockSpec((None, CONV_W - 1, dr), lambda i, j: (i, 0, 0))],
        out_shape=[jax.ShapeDtypeStruct((n, d), F32), jax.ShapeDtypeStruct((bsz, 1, dr), F32),
                   jax.ShapeDtypeStruct((bsz, CONV_W - 1, dr), F32)],
        scratch_shapes=[pltpu.VMEM((tm + pad, dr), F32), pltpu.VMEM((tm, dr), F32),
                        pltpu.VMEM((tm, dr), F32), pltpu.VMEM((tm, dr), F32), pltpu.VMEM((1, dr), F32)],
        compiler_params=_params("parallel", "arbitrary"), name="lru_prompt",
    )(x, vec(g), wyx, vec(byx), cw, vec(cb), wa, vec(ba), wi, vec(bi), vec(sp), wo)
    return out, hl.reshape(bsz, dr), cn


def _lru_sample_kernel(bsz, x_ref, g_ref, buf_ref, h0_ref, wyx_ref, byx_ref, cw_ref, cb_ref, wa_ref, ba_ref,
                       wi_ref, bi_ref, sp_ref, wo_ref, o_ref, hl_ref, cn_ref, xpad_ref, hs_ref):
    n, dr = hs_ref.shape
    t = n // bsz
    nb = (CONV_W - 1) * bsz
    x = x_ref[...]
    h = (_rms(x) * g_ref[...]).astype(BF16)
    z = _dot(h, wyx_ref[...]) + byx_ref[...]
    gate = _gelu_tanh(z[:, :dr])
    xpad_ref[0:nb, :] = buf_ref[...]
    xpad_ref[nb:, :] = z[:, dr:]
    xb = cb_ref[...] + xpad_ref[0:n, :] * cw_ref[0:1, :]
    for s in range(1, CONV_W):
        xb = xb + xpad_ref[s * bsz:s * bsz + n, :] * cw_ref[s:s + 1, :]
    a, bt = _lru_gates(xb, wa_ref, ba_ref[...], wi_ref, bi_ref[...], sp_ref[...])
    hcur = h0_ref[...]
    for s in range(t):
        hcur = a[s * bsz:(s + 1) * bsz, :] * hcur + bt[s * bsz:(s + 1) * bsz, :]
        hs_ref[s * bsz:(s + 1) * bsz, :] = hcur
    o_ref[...] = x + _dot((gate * hs_ref[...]).astype(BF16), wo_ref[...])
    hl_ref[...] = hcur
    cn_ref[...] = xpad_ref[n:n + nb, :]


def _lru_sample(x, bsz, h0, conv0, g, wyx, byx, cw, cb, wa, ba, wi, bi, sp, wo):
    n, d = x.shape
    t = n // bsz
    dr = wo.shape[0]
    nb = (CONV_W - 1) * bsz
    tmaj = lambda a, tt: a.reshape(bsz, tt, -1).swapaxes(0, 1).reshape(tt * bsz, -1)
    bmaj = lambda a, tt: a.reshape(tt, bsz, -1).swapaxes(0, 1)
    vec = lambda a: a.reshape(1, -1)
    args = (tmaj(x, t), vec(g), tmaj(conv0, CONV_W - 1), h0, wyx, vec(byx), cw, vec(cb), wa, vec(ba),
            wi, vec(bi), vec(sp), wo)
    out, hl, cn = pl.pallas_call(
        functools.partial(_lru_sample_kernel, bsz),
        in_specs=[pl.BlockSpec(a.shape, lambda nd=a.ndim: (0,) * nd) for a in args],
        out_specs=[pl.BlockSpec((n, d), lambda: (0, 0)), pl.BlockSpec((bsz, dr), lambda: (0, 0)),
                   pl.BlockSpec((nb, dr), lambda: (0, 0))],
        out_shape=[jax.ShapeDtypeStruct((n, d), F32), jax.ShapeDtypeStruct((bsz, dr), F32),
                   jax.ShapeDtypeStruct((nb, dr), F32)],
        scratch_shapes=[pltpu.VMEM((n + nb, dr), F32), pltpu.VMEM((n, dr), F32)],
        compiler_params=pltpu.CompilerParams(vmem_limit_bytes=VMEM_LIMIT), name="lru_sample",
    )(*args)
    return bmaj(out, t).reshape(n, d), hl, bmaj(cn, CONV_W - 1)


def _rope_tables_partial(pos, period, rot_dim, theta):
    half = rot_dim // 2
    inv = theta ** (-jnp.arange(half, dtype=F32) / half)
    ang = pos.astype(F32)[:, None] * inv[None, :]
    cos, sin = jnp.cos(ang), jnp.sin(ang)
    t = pos.shape[0]
    zh = jnp.zeros((t, half), F32)
    rest = period - rot_dim
    c = jnp.concatenate([cos, cos, jnp.ones((t, rest), F32)], axis=1)
    s1 = jnp.concatenate([-sin, zh, jnp.zeros((t, rest), F32)], axis=1)
    s2 = jnp.concatenate([zh, sin, jnp.zeros((t, rest), F32)], axis=1)
    rep = LANES // period
    return [jnp.tile(a, (1, rep)) for a in (c, s1, s2)]


def _rot(x, c, s1, s2, half):
    n = x.shape[-1]
    return x * c + pltpu.roll(x, n - half, 1) * s1 + pltpu.roll(x, half, 1) * s2


def _sort_key(x):
    bits = pltpu.bitcast(x, I32)
    return bits ^ ((bits >> 31) & 0x7FFFFFFF)


def _kth_largest(count_ge, k, shape):
    t = jnp.where(count_ge(jnp.zeros(shape, I32)) >= k, 0, INT_MIN).astype(I32)

    def body(i, t):
        cand = t | lax.shift_left(jnp.int32(1), 30 - i)
        return jnp.where(count_ge(cand) >= k, cand, t)

    return lax.fori_loop(0, 31, body, t)


def _attn_kernel(tq, tk, lk, q_pos0, k_pos0, mode, n_cmp, q_ref, k_ref, v_ref, *rest):
    if mode == "window":
        o_ref, qs_ref, acc_ref, m_scr, l_scr = rest
        m_ref = None
    else:
        m_ref, o_ref, qs_ref, acc_ref, m_scr, l_scr = rest
    p0 = q_pos0 + pl.program_id(1) * tq
    for g in range(N_KV):
        for h in range(GROUP):
            c0 = (g * GROUP + h) * HEAD_DIM
            qs_ref[g, h * tq:(h + 1) * tq, :] = q_ref[:, c0:c0 + HEAD_DIM]
    m_scr[...] = jnp.full(m_scr.shape, NEG, F32)
    l_scr[...] = jnp.zeros(l_scr.shape, F32)
    acc_ref[...] = jnp.zeros(acc_ref.shape, F32)
    hi = jnp.minimum((p0 + tq - 1 - k_pos0) // tk + 1, lk // tk)
    lo = jnp.maximum(p0 - (WINDOW - 1) - k_pos0, 0) // tk if mode == "window" else 0
    rowpos = p0 + lax.broadcasted_iota(I32, (tq, tk), 0)

    def body(kt, carry):
        k0 = pl.multiple_of(kt * tk, tk)
        if mode == "mask":
            bias = m_ref[:, pl.ds(k0, tk)].astype(F32)
        else:
            colpos = k_pos0 + k0 + lax.broadcasted_iota(I32, (tq, tk), 1)
            valid = colpos <= rowpos
            if mode == "window":
                valid = valid & (rowpos - colpos < WINDOW) & (colpos >= 0)
                bias = jnp.where(valid, 0.0, -jnp.inf)
        for g in range(N_KV):
            if mode == "blocksel":
                blk = lax.shift_right_logical(k_pos0 + k0 + lax.broadcasted_iota(I32, (LANES, tk), 1), 6)
                e = jnp.where(blk == lax.broadcasted_iota(I32, (LANES, tk), 0), 1.0, 0.0).astype(BF16)
                sel = _dot(m_ref[:, g * LANES:(g + 1) * LANES], e) > 0.5
                bias = jnp.where(valid & (sel | (colpos >= n_cmp * SLC_BLK)), 0.0, -jnp.inf)
            kt_ = k_ref[pl.ds(k0, tk), g * HEAD_DIM:(g + 1) * HEAD_DIM]
            vt_ = v_ref[pl.ds(k0, tk), g * HEAD_DIM:(g + 1) * HEAD_DIM]
            s = _dot_nt(qs_ref[g], kt_)
            s = (s.reshape(GROUP, tq, tk) + bias[None]).reshape(GROUP * tq, tk)
            m_old = m_scr[g]
            m_new = jnp.maximum(m_old, jnp.max(s, axis=-1, keepdims=True))
            alpha = jnp.exp2(m_old - m_new)
            p = jnp.exp2(s - jnp.concatenate([m_new] * (tk // LANES), axis=1))
            l_scr[g] = alpha * l_scr[g] + jnp.sum(p, axis=-1, keepdims=True)
            acc_ref[g] = alpha * acc_ref[g] + _dot(p.astype(BF16), vt_)
            m_scr[g] = m_new
        return carry

    lax.fori_loop(lo, hi, body, 0)
    for g in range(N_KV):
        o = acc_ref[g] / jnp.maximum(l_scr[g], 1e-30)
        for h in range(GROUP):
            c0 = (g * GROUP + h) * HEAD_DIM
            o_ref[:, c0:c0 + HEAD_DIM] = o[h * tq:(h + 1) * tq, :]


def _attention(q, k, v, m, *, bsz, tq, q_pos0, k_pos0, mode, n_cmp=0):
    nq_rows, dq = q.shape
    lk = k.shape[1]
    tk = 512
    assert lk % tk == 0 and (nq_rows // bsz) % tq == 0
    nq = nq_rows // bsz // tq
    kern = functools.partial(_attn_kernel, tq, tk, lk, q_pos0, k_pos0, mode, n_cmp)
    in_specs = [pl.BlockSpec((tq, dq), lambda b, j: (b * nq + j, 0)),
                pl.BlockSpec((None, lk, k.shape[2]), lambda b, j: (b, 0, 0)),
                pl.BlockSpec((None, lk, v.shape[2]), lambda b, j: (b, 0, 0))]
    args = [q, k, v]
    if mode != "window":
        in_specs.append(pl.BlockSpec((tq, m.shape[1]), lambda b, j: (b * nq + j, 0)))
        args.append(m)
    return pl.pallas_call(
        kern, grid=(bsz, nq), in_specs=in_specs,
        out_specs=pl.BlockSpec((tq, dq), lambda b, j: (b * nq + j, 0)),
        out_shape=jax.ShapeDtypeStruct((nq_rows, dq), F32),
        scratch_shapes=[pltpu.VMEM((N_KV, GROUP * tq, HEAD_DIM), BF16),
                        pltpu.VMEM((N_KV, GROUP * tq, HEAD_DIM), F32),
                        pltpu.VMEM((N_KV, GROUP * tq, LANES), F32),
                        pltpu.VMEM((N_KV, GROUP * tq, LANES), F32)],
        compiler_params=_params("parallel", "arbitrary"), name="attn_" + mode,
    )(*args)


def _head_norm_rot(z, g, tabs, n_heads, scale, refs_f32, refs_bf16):
    c, s1, s2 = tabs
    for hd in range(n_heads):
        sl = slice(hd * HEAD_DIM, (hd + 1) * HEAD_DIM)
        r = _rot(_rms(z[:, sl]) * g, c, s1, s2, ROPE_DIM // 2)
        for ref in refs_f32:
            ref[:, sl] = r
        for ref in refs_bf16:
            ref[:, sl] = (r * scale).astype(BF16)


def _dsa_epilogue(h, w_ref, tabs, cs, o):
    tq_ = [t[...] for t in tabs[:3]]
    ti_ = [t[...] for t in tabs[3:]]
    gq, gk, gik = [c[...] for c in cs]
    q_ref, k_ref, kb_ref, v_ref, vb_ref, qi_ref, ki_ref, wi_ref = o
    nq, nkv = N_HEADS * HEAD_DIM, N_KV * HEAD_DIM
    ni = IDX_HEADS * IDX_DIM
    _head_norm_rot(_dot(h, w_ref[:, 0:nq]), gq, tq_, N_HEADS, ATTN_SCALE * LOG2E, [], [q_ref])
    _head_norm_rot(_dot(h, w_ref[:, nq:nq + nkv]), gk, tq_, N_KV, 1.0, [k_ref], [kb_ref])
    zv = _dot(h, w_ref[:, nq + nkv:nq + 2 * nkv])
    v_ref[...] = zv
    vb_ref[...] = zv.astype(BF16)
    c0 = nq + 2 * nkv
    zi = _dot(h, w_ref[:, c0:c0 + ni])
    for ch in range(ni // LANES):
        r = _rot(zi[:, ch * LANES:(ch + 1) * LANES], *ti_, IDX_ROPE // 2).astype(BF16)
        for u in range(LANES // IDX_DIM):
            qi_ref[ch * (LANES // IDX_DIM) + u] = r[:, u * IDX_DIM:(u + 1) * IDX_DIM]
    zl = _dot(h, w_ref[:, c0 + ni:c0 + ni + LANES])
    lane = lax.broadcasted_iota(I32, zl.shape, 1)
    ms = jnp.sum(jnp.where(lane < IDX_DIM, zl * zl, 0.0), axis=-1, keepdims=True) / IDX_DIM
    r = _rot(zl * lax.rsqrt(ms + EPS) * gik, *ti_, IDX_ROPE // 2)
    ki_ref[...] = r[:, :IDX_DIM]
    wi_ref[...] = zl[:, IDX_DIM:IDX_DIM + IDX_HEADS] * (IDX_HEADS ** -0.5 * IDX_DIM ** -0.5)


def _dsa_select_kernel(tq, tk, lk, q_pos0, top, qi_ref, wi_ref, kit_ref, m_ref, key_ref, wib_ref):
    p0 = q_pos0 + pl.program_id(1) * tq
    nk = jnp.minimum((p0 + tq - 1) // tk + 1, lk // tk)
    wi = wi_ref[...]
    for h in range(IDX_HEADS):
        wib_ref[h] = jnp.broadcast_to(wi[:, h:h + 1], (tq, tk))
    rowpos = p0 + lax.broadcasted_iota(I32, (tq, tk), 0)
    col = lax.broadcasted_iota(I32, (tq, tk), 1)

    def score_tile(kt, carry):
        k0 = pl.multiple_of(kt * tk, tk)
        kt_ = kit_ref[:, pl.ds(k0, tk)]
        acc = jnp.zeros((tq, tk), F32)
        for h in range(IDX_HEADS):
            acc = acc + jnp.maximum(_dot(qi_ref[h], kt_), 0.0) * wib_ref[h]
        acc = jnp.where(k0 + col <= rowpos, acc, -jnp.inf)
        key_ref[:, pl.ds(k0, tk)] = _sort_key(acc)
        return carry

    lax.fori_loop(0, nk, score_tile, 0)

    @pl.when(nk % 2 == 1)
    def _():
        key_ref[:, pl.ds(pl.multiple_of(nk * tk, tk), tk)] = _sort_key(jnp.full((tq, tk), -jnp.inf, F32))

    def count_ge(cand):
        cb = jnp.broadcast_to(cand, (tq, LANES))

        def body(kt, acc):
            for u in range(2 * tk // LANES):
                c0 = pl.multiple_of(kt * (2 * tk) + u * LANES, LANES)
                acc = acc + jnp.where(key_ref[:, pl.ds(c0, LANES)] >= cb, 1.0, 0.0)
            return acc

        acc = lax.fori_loop(0, (nk + 1) // 2, body, jnp.zeros((tq, LANES), F32))
        return jnp.sum(acc, axis=-1, keepdims=True)

    thr = jnp.broadcast_to(_kth_largest(count_ge, float(top), (tq, 1)), (tq, tk))

    def bias_tile(kt, carry):
        k0 = pl.multiple_of(kt * tk, tk)
        sel = (k0 + col <= rowpos) & (key_ref[:, pl.ds(k0, tk)] >= thr)
        m_ref[:, pl.ds(k0, tk)] = jnp.where(sel, 0.0, -jnp.inf).astype(BF16)
        return carry

    lax.fori_loop(0, nk, bias_tile, 0)

    def rest_tile(kt, carry):
        m_ref[:, pl.ds(pl.multiple_of(kt * tk, tk), tk)] = jnp.full((tq, tk), -jnp.inf, BF16)
        return carry

    lax.fori_loop(nk, lk // tk, rest_tile, 0)


def _dsa_select(qi, wi, kit, *, bsz, tq, q_pos0, top):
    lk = kit.shape[2]
    tk = 256
    assert lk % (2 * tk) == 0 and top <= tk
    nq_rows = wi.shape[0]
    nq = nq_rows // bsz // tq
    return pl.pallas_call(
        functools.partial(_dsa_select_kernel, tq, tk, lk, q_pos0, top),
        grid=(bsz, nq),
        in_specs=[pl.BlockSpec((IDX_HEADS, tq, IDX_DIM), lambda b, j: (0, b * nq + j, 0)),
                  pl.BlockSpec((tq, IDX_HEADS), lambda b, j: (b * nq + j, 0)),
                  pl.BlockSpec((None, IDX_DIM, lk), lambda b, j: (b, 0, 0))],
        out_specs=pl.BlockSpec((tq, lk), lambda b, j: (b * nq + j, 0)),
        out_shape=jax.ShapeDtypeStruct((nq_rows, lk), BF16),
        scratch_shapes=[pltpu.VMEM((tq, lk), I32), pltpu.VMEM((IDX_HEADS, tq, tk), F32)],
        compiler_params=_params("parallel", "arbitrary"), name="dsa_select",
    )(qi, wi, kit)


def _nsa_epilogue(h, w_ref, tabs, cs, o):
    tb = [t[...] for t in tabs]
    gq, gsk, gwk = [c[...] for c in cs]
    (qc_ref, q_ref, ck_ref, cv_ref, sk_ref, skb_ref, sv_ref, svb_ref,
     wk_ref, wkb_ref, wv_ref, wvb_ref, gate_ref) = o
    nq, nkv = N_HEADS * HEAD_DIM, N_KV * HEAD_DIM
    zq = _dot(h, w_ref[:, 0:nq])
    for hd in range(N_HEADS):
        sl = slice(hd * HEAD_DIM, (hd + 1) * HEAD_DIM)
        qn = _rms(zq[:, sl]) * gq
        qc_ref[:, sl] = (qn * ATTN_SCALE).astype(BF16)
        q_ref[:, sl] = (_rot(qn, *tb, ROPE_DIM // 2) * (ATTN_SCALE * LOG2E)).astype(BF16)
    seg = lambda i: _dot(h, w_ref[:, nq + i * nkv:nq + (i + 1) * nkv])
    ck_ref[...] = seg(0)
    cv_ref[...] = seg(1)
    _head_norm_rot(seg(2), gsk, tb, N_KV, 1.0, [sk_ref], [skb_ref])
    zsv = seg(3)
    sv_ref[...] = zsv
    svb_ref[...] = zsv.astype(BF16)
    _head_norm_rot(seg(4), gwk, tb, N_KV, 1.0, [wk_ref], [wkb_ref])
    zwv = seg(5)
    wv_ref[...] = zwv
    wvb_ref[...] = zwv.astype(BF16)
    zg = _dot(h, w_ref[:, nq + 6 * nkv:nq + 6 * nkv + LANES])
    gate_ref[...] = jax.nn.sigmoid(zg[:, :N_HEADS * 3])


def _nsa_compress_kernel(ck_ref, cv_ref, pwk_ref, pwv_ref, phik_ref, phiv_ref, g_ref, kc_ref, vc_ref):
    rows, wd = ck_ref.shape
    nb = rows // CMP_BLK
    pk = jnp.sum(ck_ref[...].reshape(nb, CMP_BLK, wd) * pwk_ref[...][None], axis=1)
    pv = jnp.sum(cv_ref[...].reshape(nb, CMP_BLK, wd) * pwv_ref[...][None], axis=1)
    for hd in range(N_KV):
        sl = slice(hd * HEAD_DIM, (hd + 1) * HEAD_DIM)
        kc_ref[:, sl] = (_rms(_dot(pk[:, sl].astype(BF16), phik_ref[hd])) * g_ref[...]).astype(BF16)
        vc_ref[:, sl] = _dot(pv[:, sl].astype(BF16), phiv_ref[hd]).astype(BF16)


def _nsa_compress(ck, cv, pwk, pwv, phik, phiv, gck):
    rows, wd = ck.shape
    step = 2048 if rows % 2048 == 0 else rows
    nb = step // CMP_BLK
    bc = lambda p: jnp.repeat(p, HEAD_DIM, axis=1)
    return pl.pallas_call(
        _nsa_compress_kernel, grid=(rows // step,),
        in_specs=[pl.BlockSpec((step, wd), lambda i: (i, 0)), pl.BlockSpec((step, wd), lambda i: (i, 0)),
                  _const_spec((CMP_BLK, wd)), _const_spec((CMP_BLK, wd)), _const_spec(phik.shape),
                  _const_spec(phiv.shape), _const_spec((1, HEAD_DIM))],
        out_specs=[pl.BlockSpec((nb, wd), lambda i: (i, 0)), pl.BlockSpec((nb, wd), lambda i: (i, 0))],
        out_shape=[jax.ShapeDtypeStruct((rows // CMP_BLK, wd), BF16)] * 2,
        compiler_params=_params("parallel"), name="nsa_compress",
    )(ck, cv, bc(pwk), bc(pwv), phik, phiv, gck.reshape(1, HEAD_DIM))


def _nsa_cmp_kernel(tq, q_pos0, n_sel, qc_ref, kc_ref, vc_ref, oc_ref, bm_ref):
    p0 = q_pos0 + pl.program_id(1) * tq
    nc = kc_ref.shape[0]
    rowpos = p0 + lax.broadcasted_iota(I32, (tq, nc), 0)
    blk = lax.broadcasted_iota(I32, (tq, nc), 1)
    valid = jnp.concatenate([blk * CMP_BLK + (CMP_BLK - 1) <= rowpos] * GROUP, axis=0)
    cur = lax.shift_right_logical(rowpos, 6)
    forced = (blk == 0) | (blk == cur) | (blk == cur - 1)
    keys = []
    for g in range(N_KV):
        sl = slice(g * HEAD_DIM, (g + 1) * HEAD_DIM)
        qs = jnp.concatenate([qc_ref[:, (g * GROUP + h) * HEAD_DIM:(g * GROUP + h + 1) * HEAD_DIM]
                              for h in range(GROUP)], axis=0)
        lc = jnp.where(valid, _dot_nt(qs, kc_ref[:, sl]), NEG)
        m = jnp.max(lc, axis=-1, keepdims=True)
        m = jnp.where(m > 0.5 * NEG, m, 0.0)
        p = jnp.where(valid, jnp.exp(lc - m), 0.0)
        pc = p / jnp.maximum(jnp.sum(p, axis=-1, keepdims=True), 1e-30)
        oc = _dot(pc.astype(BF16), vc_ref[:, sl])
        for h in range(GROUP):
            c0 = (g * GROUP + h) * HEAD_DIM
            oc_ref[:, c0:c0 + HEAD_DIM] = oc[h * tq:(h + 1) * tq, :]
        imp = pc[0:tq]
        for h in range(1, GROUP):
            imp = imp + pc[h * tq:(h + 1) * tq]
        imp = jnp.where(forced, BIG, imp)
        imp = jnp.where(blk > cur, -jnp.inf, imp)
        keys.append(_sort_key(imp))
    key = jnp.concatenate(keys, axis=0)
    count_ge = lambda cand: jnp.sum(jnp.where(key >= cand, 1.0, 0.0), axis=-1, keepdims=True)
    sel = jnp.where(key >= _kth_largest(count_ge, float(n_sel), (N_KV * tq, 1)), 1.0, 0.0).astype(BF16)
    for g in range(N_KV):
        bm_ref[:, g * LANES:(g + 1) * LANES] = sel[g * tq:(g + 1) * tq, :]


def _nsa_cmp(qc, kc, vc, *, bsz, tq, q_pos0, n_sel):
    nq_rows, dq = qc.shape
    nc = kc.shape[1]
    assert nc == LANES
    nq = nq_rows // bsz // tq
    return pl.pallas_call(
        functools.partial(_nsa_cmp_kernel, tq, q_pos0, n_sel), grid=(bsz, nq),
        in_specs=[pl.BlockSpec((tq, dq), lambda b, j: (b * nq + j, 0)),
                  pl.BlockSpec((None, nc, kc.shape[2]), lambda b, j: (b, 0, 0)),
                  pl.BlockSpec((None, nc, vc.shape[2]), lambda b, j: (b, 0, 0))],
        out_specs=[pl.BlockSpec((tq, dq), lambda b, j: (b * nq + j, 0)),
                   pl.BlockSpec((tq, N_KV * LANES), lambda b, j: (b * nq + j, 0))],
        out_shape=[jax.ShapeDtypeStruct((nq_rows, dq), F32),
                   jax.ShapeDtypeStruct((nq_rows, N_KV * LANES), BF16)],
        compiler_params=_params("parallel", "parallel"), name="nsa_cmp",
    )(qc, kc, vc)


def _nsa_combine(gate, oc, os_, ow):
    parts = []
    for hd in range(N_HEADS):
        sl = slice(hd * HEAD_DIM, (hd + 1) * HEAD_DIM)
        parts.append(gate[:, 3 * hd:3 * hd + 1] * oc[:, sl] + gate[:, 3 * hd + 1:3 * hd + 2] * os_[:, sl]
                     + gate[:, 3 * hd + 2:3 * hd + 3] * ow[:, sl])
    return jnp.concatenate(parts, axis=1)


SAMPLE_TQ = 16
PROMPT_TQ = 128
KEY_ALIGN = 512


def _pad_rows(a, bsz, t, tp):
    return jnp.pad(a.reshape(bsz, t, -1), ((0, 0), (0, tp - t), (0, 0))).reshape(bsz * tp, -1)


def _unpad_rows(a, bsz, t, tp):
    return a.reshape(bsz, tp, -1)[:, :t].reshape(bsz * t, -1)


def _cat_keys(old, new, bsz):
    new = new.reshape(bsz, -1, new.shape[-1])
    n = old.shape[1] + new.shape[1]
    return jnp.pad(jnp.concatenate([old.astype(new.dtype), new], axis=1),
                   ((0, 0), (0, -n % KEY_ALIGN), (0, 0)))


def _tile_tables(tabs, bsz, t):
    return tabs if t % ROW_TILE == 0 else [jnp.tile(a, (bsz, 1)) for a in tabs]


def _mixer_c(x, bsz, pos, past, g, wcat, gq, gk, gik, wo):
    n, d = x.shape
    t = n // bsz
    tabs = _tile_tables(_rope_tables_partial(pos, HEAD_DIM, ROPE_DIM, ROPE_THETA)
                        + _rope_tables_partial(pos, IDX_DIM, IDX_ROPE, ROPE_THETA), bsz, t)
    nkv = N_KV * HEAD_DIM
    q, k, kb, v, vb, qi, ki, wi = _proj(
        x, g, wcat, tabs, [gq, gk, gik], _dsa_epilogue,
        [(_rows(N_HEADS * HEAD_DIM), BF16), (_rows(nkv), F32), (_rows(nkv), BF16), (_rows(nkv), F32),
         (_rows(nkv), BF16),
         (lambda n_, tm: ((IDX_HEADS, n_, IDX_DIM), (IDX_HEADS, tm, IDX_DIM), lambda i: (0, i, 0)), BF16),
         (_rows(IDX_DIM), F32), (_rows(IDX_HEADS), F32)], tabs[0].shape[0], "proj_dsa")
    if past is None:
        tq, q_pos0, n_keys = PROMPT_TQ, 0, t
        kit = ki.reshape(bsz, t, IDX_DIM).swapaxes(1, 2).astype(BF16)
        kall, vall = kb.reshape(bsz, t, nkv), vb.reshape(bsz, t, nkv)
    else:
        tq, q_pos0 = SAMPLE_TQ, past["k"].shape[1]
        n_keys = q_pos0 + t
        kit = _cat_keys(past["kidx"], ki, bsz).swapaxes(1, 2).astype(BF16)
        kall = _cat_keys(past["k"], k, bsz).astype(BF16)
        vall = _cat_keys(past["v"], v, bsz).astype(BF16)
        q, wi = _pad_rows(q, bsz, t, tq), _pad_rows(wi, bsz, t, tq)
        qi = jnp.pad(qi.reshape(IDX_HEADS, bsz, t, IDX_DIM), ((0, 0), (0, 0), (0, tq - t), (0, 0))
                     ).reshape(IDX_HEADS, bsz * tq, IDX_DIM)
    mask = _dsa_select(qi, wi, kit, bsz=bsz, tq=tq, q_pos0=q_pos0, top=min(TOPK_MAX, n_keys // 4))
    o = _attention(q, kall, vall, mask, bsz=bsz, tq=tq, q_pos0=q_pos0, k_pos0=0, mode="mask")
    if past is not None:
        o = _unpad_rows(o, bsz, t, tq)
    return _outproj(x, wo, [o], lambda a: a), k, v, ki


def _mixer_d(x, bsz, pos, past, g, wcat, gq, gsk, gwk, cmpw, wo):
    n, d = x.shape
    t = n // bsz
    tabs = _tile_tables(_rope_tables_partial(pos, HEAD_DIM, ROPE_DIM, ROPE_THETA), bsz, t)
    nkv = N_KV * HEAD_DIM
    kvo = [(_rows(nkv), F32), (_rows(nkv), BF16)]
    (qc, q, ck, cv, sk, skb, sv, svb, wk, wkb, wv, wvb, gate) = _proj(
        x, g, wcat, tabs, [gq, gsk, gwk], _nsa_epilogue,
        [(_rows(N_HEADS * HEAD_DIM), BF16), (_rows(N_HEADS * HEAD_DIM), BF16), (_rows(nkv), F32),
         (_rows(nkv), F32)] + kvo * 4 + [(_rows(N_HEADS * 3), F32)], tabs[0].shape[0], "proj_nsa")
    if past is None:
        tq, q_pos0, n_keys, win_pos0 = PROMPT_TQ, 0, t, 0
        kc, vc = _nsa_compress(ck, cv, *cmpw)
        three = lambda a: a.reshape(bsz, t, nkv)
        skall, svall, wkall, wvall = three(skb), three(svb), three(wkb), three(wvb)
    else:
        tq, q_pos0 = SAMPLE_TQ, past["sk"].shape[1]
        n_keys = q_pos0 + t
        win_pos0 = q_pos0 - past["wk"].shape[1]
        assert q_pos0 % CMP_BLK == 0 and t < CMP_BLK
        kc, vc = _nsa_compress(past["ck"].reshape(bsz * q_pos0, nkv), past["cv"].reshape(bsz * q_pos0, nkv),
                               *cmpw)
        skall = _cat_keys(past["sk"], sk, bsz).astype(BF16)
        svall = _cat_keys(past["sv"], sv, bsz).astype(BF16)
        wkall = _cat_keys(past["wk"], wk, bsz).astype(BF16)
        wvall = _cat_keys(past["wv"], wv, bsz).astype(BF16)
        qc, q = _pad_rows(qc, bsz, t, tq), _pad_rows(q, bsz, t, tq)
    n_cmp = kc.shape[0] // bsz
    assert n_cmp <= LANES
    lane_pad = lambda a: jnp.pad(a.reshape(bsz, n_cmp, nkv), ((0, 0), (0, LANES - n_cmp), (0, 0)))
    kc, vc = lane_pad(kc), lane_pad(vc)
    n_slc = -(-n_keys // SLC_BLK)
    n_lane = min(n_slc, LANES)
    assert n_slc == n_lane or (n_slc == n_lane + 1 and q_pos0 // SLC_BLK == n_lane)
    n_sel = min(N_SLC, n_slc) - (n_slc - n_lane)
    tq_cmp = 2 * tq if (past is None and t % (2 * tq) == 0) else tq
    oc, bm = _nsa_cmp(qc, kc, vc, bsz=bsz, tq=tq_cmp, q_pos0=q_pos0, n_sel=n_sel)
    os_ = _attention(q, skall, svall, bm, bsz=bsz, tq=tq, q_pos0=q_pos0, k_pos0=0, mode="blocksel",
                     n_cmp=n_lane)
    ow = _attention(q, wkall, wvall, None, bsz=bsz, tq=tq, q_pos0=q_pos0, k_pos0=win_pos0, mode="window")
    if past is not None:
        oc, os_, ow = (_unpad_rows(a, bsz, t, tq) for a in (oc, os_, ow))
    return _outproj(x, wo, [gate, oc, os_, ow], _nsa_combine), (ck, cv, sk, sv, wk, wv)


def _gather_pages(cache, page_table):
    rows = cache[page_table]
    return rows.reshape(page_table.shape[0], page_table.shape[1] * PAGE_SIZE, -1)


def kernel(x_prompt, x_sample, state_a_ret, state_b_h, state_b_conv, cache_c_k, cache_c_v, cache_c_kidx,
           cache_d_ck, cache_d_cv, cache_d_sk, cache_d_sv, state_d_wk, state_d_wv, page_table,
           ffn1_norm, ffn1_wg, ffn1_wu, ffn1_wd, mix_norm, ffn2_norm, ffn2_wg, ffn2_wu, ffn2_wd,
           a_wq, a_wk, a_wv, a_wg, a_wo,
           b_wy, b_by, b_wx, b_bx, b_conv_w, b_conv_b, b_wa, b_ba, b_wi, b_bi, b_lam, b_wo,
           c_wq, c_gq, c_wk, c_gk, c_wv, c_wo, c_wiq, c_wik, c_gik, c_wiw,
           d_wq, d_gq, d_wck, d_wcv, d_pwk, d_pwv, d_phik, d_phiv, d_gck, d_wsk, d_wsv, d_gsk,
           d_wwk, d_wwv, d_gwk, d_wgate, d_wo):
    bp, tp, d = x_prompt.shape
    bs, ts, _ = x_sample.shape
    depth = ffn1_norm.shape[0]
    past_len = page_table.shape[1] * PAGE_SIZE
    pos_p = jnp.arange(tp, dtype=I32)
    pos_s = past_len + jnp.arange(ts, dtype=I32)
    b16 = lambda a: a.astype(BF16)
    row = lambda a: a.reshape(1, -1)
    nkv = N_KV * HEAD_DIM

    xp = x_prompt.reshape(bp * tp, d)
    xs = x_sample.reshape(bs * ts, d)
    outs = [[] for _ in range(24)]
    for i in range(depth):
        m, j = i % 4, i // 4
        w1 = (ffn1_norm[i], b16(ffn1_wg[i]), b16(ffn1_wu[i]), b16(ffn1_wd[i]))
        xp, xs = _ffn(xp, *w1), _ffn(xs, *w1)
        g = mix_norm[i]
        if m == 0:
            dk = a_wq.shape[2] // RET_HEADS
            dv = a_wv.shape[2] // RET_HEADS
            wcat = b16(jnp.concatenate([a_wq[j], a_wk[j], a_wv[j], a_wg[j]], axis=1))
            wo = b16(a_wo[j])
            xp, s_p = _mixer_a(xp, g, pos_p, bp, jnp.zeros((bp, RET_HEADS, dk, dv), F32), wcat, wo, dk, dv)
            xs, s_s = _mixer_a(xs, g, pos_s, bs, state_a_ret[j], wcat, wo, dk, dv)
            new = [s_p, s_s]
            base = 0
        elif m == 1:
            common = (g, b16(jnp.concatenate([b_wy[j], b_wx[j]], axis=1)), jnp.concatenate([b_by[j], b_bx[j]]),
                      b_conv_w[j], b_conv_b[j], b16(b_wa[j]), b_ba[j].reshape(-1), b16(b_wi[j]),
                      b_bi[j].reshape(-1), jax.nn.softplus(-b_lam[j]), b16(b_wo[j]))
            xp, h_p, c_p = _lru_prompt(xp, bp, *common)
            xs, h_s, c_s = _lru_sample(xs, bs, state_b_h[j], state_b_conv[j], *common)
            new = [h_p, h_s, c_p, c_s]
            base = 2
        elif m == 2:
            zpad = jnp.zeros((d, LANES - IDX_DIM - IDX_HEADS), F32)
            wcat = b16(jnp.concatenate([c_wq[j], c_wk[j], c_wv[j], c_wiq[j], c_wik[j], c_wiw[j], zpad], axis=1))
            gik = jnp.concatenate([c_gik[j], jnp.zeros((LANES - IDX_DIM,), F32)])
            cw = (g, wcat, row(c_gq[j]), row(c_gk[j]), row(gik), b16(c_wo[j]))
            past = dict(kidx=_gather_pages(cache_c_kidx[j], page_table),
                        k=_gather_pages(cache_c_k[j], page_table), v=_gather_pages(cache_c_v[j], page_table))
            xp, k_p, v_p, i_p = _mixer_c(xp, bp, pos_p, None, *cw)
            xs, k_s, v_s, i_s = _mixer_c(xs, bs, pos_s, past, *cw)
            kv = lambda a, b_, t_: a.reshape(b_, t_, N_KV, HEAD_DIM)
            new = [kv(k_p, bp, tp), kv(k_s, bs, ts), kv(v_p, bp, tp), kv(v_s, bs, ts),
                   i_p.reshape(bp, tp, IDX_DIM), i_s.reshape(bs, ts, IDX_DIM)]
            base = 6
        else:
            zpad = jnp.zeros((d, LANES - N_HEADS * 3), F32)
            wcat = b16(jnp.concatenate([d_wq[j], d_wck[j], d_wcv[j], d_wsk[j], d_wsv[j], d_wwk[j], d_wwv[j],
                                        d_wgate[j], zpad], axis=1))
            cmpw = (d_pwk[j], d_pwv[j], b16(d_phik[j]), b16(d_phiv[j]), d_gck[j])
            dw = (g, wcat, row(d_gq[j]), row(d_gsk[j]), row(d_gwk[j]), cmpw, b16(d_wo[j]))
            wbuf = state_d_wk.shape[2]
            past = dict(ck=_gather_pages(cache_d_ck[j], page_table), cv=_gather_pages(cache_d_cv[j], page_table),
                        sk=_gather_pages(cache_d_sk[j], page_table), sv=_gather_pages(cache_d_sv[j], page_table),
                        wk=state_d_wk[j].reshape(bs, wbuf, nkv), wv=state_d_wv[j].reshape(bs, wbuf, nkv))
            xp, rows_p = _mixer_d(xp, bp, pos_p, None, *dw)
            xs, rows_s = _mixer_d(xs, bs, pos_s, past, *dw)
            kv = lambda a, b_: a.reshape(b_, -1, N_KV, HEAD_DIM)
            new = []
            for a_p, a_s in zip(rows_p[:4], rows_s[:4]):
                new += [kv(a_p, bp), kv(a_s, bs)]
            wb_p = min(WINDOW, tp)
            for a_p, a_s, st in ((rows_p[4], rows_s[4], state_d_wk[j]), (rows_p[5], rows_s[5], state_d_wv[j])):
                new += [kv(a_p, bp)[:, tp - wb_p:], jnp.concatenate([st, kv(a_s, bs)], axis=1)[:, ts:]]
            base = 12
        for off, a in enumerate(new):
            outs[base + off].append(a)
        w2 = (ffn2_norm[i], b16(ffn2_wg[i]), b16(ffn2_wu[i]), b16(ffn2_wd[i]))
        xp, xs = _ffn(xp, *w2), _ffn(xs, *w2)
    return (xp.reshape(bp, tp, d), xs.reshape(bs, ts, d)) + tuple(jnp.stack(o) for o in outs)
```

```python
import functools
import math

import jax
import jax.numpy as jnp
from jax import lax
from jax.experimental import pallas as pl
from jax.experimental.pallas import tpu as pltpu

F32 = jnp.float32
BF16 = jnp.bfloat16
I32 = jnp.int32

EPS = 1e-6
PAGE_SIZE = 128

RET_HEADS = 4
RET_CHUNK = 128
RET_THETA = 10000.0

LRU_BLOCKS = 4
CONV_W = 4
LRU_C = 8.0

N_HEADS = 8
HEAD_DIM = 128
N_KV = 2
GROUP = N_HEADS // N_KV
ROPE_DIM = HEAD_DIM // 4
ROPE_THETA = 500000.0

IDX_HEADS = 16
IDX_DIM = 64
IDX_ROPE = IDX_DIM // 4
TOPK_MAX = 256

CMP_BLK = 64
SLC_BLK = 64
N_SLC = 16
WINDOW = 512
BIG = 1e4

LANES = 128
VMEM_LIMIT = 56 * 1024 * 1024
ROW_TILE = 512
NEG = -1e30
ATTN_SCALE = HEAD_DIM ** -0.5
LOG2E = math.log2(math.e)
INT_MIN = -2147483648


def _params(*sem):
    return pltpu.CompilerParams(dimension_semantics=sem, vmem_limit_bytes=VMEM_LIMIT)


def _const_spec(shape):
    nd = len(shape)
    return pl.BlockSpec(shape, lambda *_: (0,) * nd, pipeline_mode=pl.Buffered(1))


def _rms(x):
    return x * lax.rsqrt(jnp.mean(x * x, axis=-1, keepdims=True) + EPS)


def _dot(a, b):
    return jnp.dot(a, b, preferred_element_type=F32)


def _dot_nt(a, b):
    return lax.dot_general(a, b, (((1,), (1,)), ((), ())), preferred_element_type=F32)


def _row_tile(n):
    return ROW_TILE if n % ROW_TILE == 0 else n


def _ffn_kernel(x_ref, g_ref, wg_ref, wu_ref, wd_ref, o_ref):
    x = x_ref[...]
    h = (_rms(x) * g_ref[...]).astype(BF16)
    gt = _dot(h, wg_ref[...])
    ut = _dot(h, wu_ref[...])
    a = (gt * jax.nn.sigmoid(gt) * ut).astype(BF16)
    o_ref[...] = x + 0.5 * _dot(a, wd_ref[...])


def _ffn(x, g, wg, wu, wd):
    n, d = x.shape
    f = wg.shape[1]
    tm = _row_tile(n)
    return pl.pallas_call(
        _ffn_kernel,
        grid=(n // tm,),
        in_specs=[pl.BlockSpec((tm, d), lambda i: (i, 0)), _const_spec((1, d)),
                  _const_spec((d, f)), _const_spec((d, f)), _const_spec((f, d))],
        out_specs=pl.BlockSpec((tm, d), lambda i: (i, 0)),
        out_shape=jax.ShapeDtypeStruct((n, d), F32),
        compiler_params=_params("parallel"), name="ffn",
    )(x, g.reshape(1, d), wg, wu, wd)


def _proj(x, g, w, tables, consts, epilogue, outs, n_tab_rows, name):
    n, d = x.shape
    tm = _row_tile(n)
    assert n_tab_rows % tm == 0
    nt = n_tab_rows // tm
    nw = w.shape[1]

    def kern(x_ref, g_ref, w_ref, *rest):
        tabs = rest[:len(tables)]
        cs = rest[len(tables):len(tables) + len(consts)]
        o = rest[len(tables) + len(consts):]
        h = (_rms(x_ref[...]) * g_ref[...]).astype(BF16)
        epilogue(h, w_ref, tabs, cs, o)

    in_specs = [pl.BlockSpec((tm, d), lambda i: (i, 0)), _const_spec((1, d)), _const_spec((d, nw))]
    in_specs += [pl.BlockSpec((tm, t.shape[1]), lambda i: (i % nt, 0)) for t in tables]
    in_specs += [_const_spec(c.shape) for c in consts]
    out_shape, out_specs = [], []
    for fn, dt in outs:
        ashape, bshape, imap = fn(n, tm)
        out_shape.append(jax.ShapeDtypeStruct(ashape, dt))
        out_specs.append(pl.BlockSpec(bshape, imap))
    return pl.pallas_call(
        kern, grid=(n // tm,), in_specs=in_specs, out_specs=out_specs, out_shape=out_shape,
        compiler_params=_params("parallel"), name=name,
    )(x, g.reshape(1, d), w, *tables, *consts)


def _rows(c):
    return lambda n, tm: ((n, c), (tm, c), lambda i: (i, 0))


def _outproj(x, w, ins, combine):
    n, d = x.shape
    tm = _row_tile(n)

    def kern(x_ref, w_ref, *rest):
        o_ref = rest[-1]
        a = combine(*[r[...] for r in rest[:-1]]).astype(BF16)
        o_ref[...] = x_ref[...] + _dot(a, w_ref[...])

    return pl.pallas_call(
        kern, grid=(n // tm,),
        in_specs=[pl.BlockSpec((tm, d), lambda i: (i, 0)), _const_spec(w.shape)]
        + [pl.BlockSpec((tm, a.shape[1]), lambda i: (i, 0)) for a in ins],
        out_specs=pl.BlockSpec((tm, d), lambda i: (i, 0)),
        out_shape=jax.ShapeDtypeStruct((n, d), F32),
        compiler_params=_params("parallel"), name="outproj",
    )(x, w, *ins)


def _ret_epilogue(dk, dv, h, w_ref, tabs, cs, o):
    cos, sin = tabs[0][...], tabs[1][...]
    q_ref, k_ref, v_ref, g_ref = o
    half = dk // 2
    nqk = RET_HEADS * dk
    for seg, ref, scale in ((0, q_ref, 1.0), (1, k_ref, dk ** -0.5)):
        z = _dot(h, w_ref[:, seg * nqk:(seg + 1) * nqk])
        for hd in range(RET_HEADS):
            x1 = z[:, hd * dk:hd * dk + half]
            x2 = z[:, hd * dk + half:(hd + 1) * dk]
            ref[:, hd * dk:hd * dk + half] = (x1 * cos - x2 * sin) * scale
            ref[:, hd * dk + half:(hd + 1) * dk] = (x2 * cos + x1 * sin) * scale
    nv = RET_HEADS * dv
    v_ref[...] = _dot(h, w_ref[:, 2 * nqk:2 * nqk + nv]).astype(BF16)
    zg = _dot(h, w_ref[:, 2 * nqk + nv:2 * nqk + 2 * nv])
    g_ref[...] = zg * jax.nn.sigmoid(zg)


def _ret_chunk_kernel(q_ref, k_ref, v_ref, s0_ref, dm_ref, xi_ref, zt_ref, gc_ref, o_ref, s_ref):
    c = pl.program_id(2)

    @pl.when(c == 0)
    def _():
        s_ref[...] = s0_ref[...]

    q = q_ref[...]
    k = k_ref[...]
    v = v_ref[...]
    s = s_ref[...]
    att = (_dot_nt(q.astype(BF16), k.astype(BF16)) * dm_ref[...]).astype(BF16)
    o = _dot(att, v) + _dot((q * xi_ref[...]).astype(BF16), s.astype(BF16))
    kz = (k * zt_ref[...]).astype(BF16)
    s_ref[...] = s * gc_ref[...] + lax.dot_general(kz, v, (((0,), (0,)), ((), ())),
                                                   preferred_element_type=F32)
    o_ref[...] = _rms(o)


def _retention(q, k, v, s0, n_valid):
    b, hh, dk, dv = s0.shape
    t = q.shape[0] // b
    c = RET_CHUNK
    nc = t // c
    cc = c if n_valid % c == 0 else n_valid
    lg = jnp.log1p(-jnp.exp2(-5.0 - jnp.arange(hh, dtype=F32)))
    idx = jnp.arange(c, dtype=F32)
    diff = idx[:, None] - idx[None, :]
    dmask = jnp.where(diff >= 0, jnp.exp(lg[:, None, None] * jnp.maximum(diff, 0.0)), 0.0)
    xi = jnp.exp(lg[:, None] * (idx + 1.0))[:, :, None]
    zeta = jnp.exp(lg[:, None] * (cc - 1.0 - idx))[:, :, None]
    g_c = jnp.exp(lg * cc)[:, None, None]
    return pl.pallas_call(
        _ret_chunk_kernel,
        grid=(b, hh, nc),
        in_specs=[pl.BlockSpec((c, dk), lambda i, j, l: (i * nc + l, j)),
                  pl.BlockSpec((c, dk), lambda i, j, l: (i * nc + l, j)),
                  pl.BlockSpec((c, dv), lambda i, j, l: (i * nc + l, j)),
                  pl.BlockSpec((None, None, dk, dv), lambda i, j, l: (i, j, 0, 0)),
                  pl.BlockSpec((None, c, c), lambda i, j, l: (j, 0, 0)),
                  pl.BlockSpec((None, c, 1), lambda i, j, l: (j, 0, 0)),
                  pl.BlockSpec((None, c, 1), lambda i, j, l: (j, 0, 0)),
                  pl.BlockSpec((None, 1, 1), lambda i, j, l: (j, 0, 0))],
        out_specs=[pl.BlockSpec((c, dv), lambda i, j, l: (i * nc + l, j)),
                   pl.BlockSpec((None, None, dk, dv), lambda i, j, l: (i, j, 0, 0))],
        out_shape=[jax.ShapeDtypeStruct((b * t, hh * dv), F32),
                   jax.ShapeDtypeStruct((b, hh, dk, dv), F32)],
        compiler_params=_params("parallel", "parallel", "arbitrary"), name="retention",
    )(q, k, v, s0, dmask, xi, zeta, g_c)


def _rope_tables_full(pos, dim, theta):
    half = dim // 2
    inv = theta ** (-jnp.arange(half, dtype=F32) / half)
    ang = pos.astype(F32)[:, None] * inv[None, :]
    return jnp.cos(ang), jnp.sin(ang)


def _mixer_a(x, g, pos, bsz, s0, wcat, wo, dk, dv):
    n, d = x.shape
    t = n // bsz
    cos, sin = _rope_tables_full(pos, dk, RET_THETA)
    if t % RET_CHUNK:
        cos = jnp.tile(cos, (bsz, 1))
        sin = jnp.tile(sin, (bsz, 1))
    q, k, v, sg = _proj(
        x, g, wcat, [cos, sin], [], functools.partial(_ret_epilogue, dk, dv),
        [(_rows(RET_HEADS * dk), F32), (_rows(RET_HEADS * dk), F32),
         (_rows(RET_HEADS * dv), BF16), (_rows(RET_HEADS * dv), F32)], cos.shape[0], "proj_ret")
    if t % RET_CHUNK:
        pad = lambda a: jnp.pad(a.reshape(bsz, t, -1), ((0, 0), (0, RET_CHUNK - t), (0, 0))
                                ).reshape(bsz * RET_CHUNK, -1)
        o, s_new = _retention(pad(q), pad(k), pad(v), s0, t)
        o = o.reshape(bsz, RET_CHUNK, -1)[:, :t].reshape(n, -1)
    else:
        o, s_new = _retention(q, k, v, s0, t)
    return _outproj(x, wo, [sg, o], lambda a, b: a * b), s_new


def _gelu_tanh(x):
    return 0.5 * x * (1.0 + jnp.tanh(math.sqrt(2.0 / math.pi) * (x + 0.044715 * (x * x * x))))


def _lru_gates(xb, wa_ref, ba, wi_ref, bi, sp):
    bw = wa_ref.shape[1]
    xbb = xb.astype(BF16)
    r = jnp.concatenate([_dot(xbb[:, n * bw:(n + 1) * bw], wa_ref[n]) for n in range(LRU_BLOCKS)], axis=1)
    i = jnp.concatenate([_dot(xbb[:, n * bw:(n + 1) * bw], wi_ref[n]) for n in range(LRU_BLOCKS)], axis=1)
    r = jax.nn.sigmoid(r + ba)
    i = jax.nn.sigmoid(i + bi)
    log_a = -LRU_C * r * sp
    a = jnp.exp(log_a)
    th = jnp.tanh(log_a)
    bt = jnp.sqrt(-2.0 * th / (1.0 - th)) * (i * xb)
    return a, bt


def _lru_prompt_kernel(x_ref, g_ref, wyx_ref, byx_ref, cw_ref, cb_ref, wa_ref, ba_ref, wi_ref, bi_ref,
                       sp_ref, wo_ref, o_ref, hl_ref, cn_ref, xpad_ref, a_ref, b_ref, hs_ref, h_ref):
    j = pl.program_id(1)
    tm, dr = a_ref.shape
    pad = xpad_ref.shape[0] - tm

    @pl.when(j == 0)
    def _():
        h_ref[...] = jnp.zeros_like(h_ref)
        xpad_ref[0:pad, :] = jnp.zeros((pad, dr), F32)

    x = x_ref[...]
    h = (_rms(x) * g_ref[...]).astype(BF16)
    z = _dot(h, wyx_ref[...]) + byx_ref[...]
    gate = _gelu_tanh(z[:, :dr])
    xpad_ref[pad:, :] = z[:, dr:]
    xb = cb_ref[...] + z[:, dr:] * cw_ref[CONV_W - 1:CONV_W, :]
    for s in range(1, CONV_W):
        xb = xb + xpad_ref[pad - s:pad - s + tm, :] * cw_ref[CONV_W - 1 - s:CONV_W - s, :]
    xpad_ref[0:pad, :] = xpad_ref[tm:tm + pad, :]
    a, bt = _lru_gates(xb, wa_ref, ba_ref[...], wi_ref, bi_ref[...], sp_ref[...])
    a_ref[...] = a
    b_ref[...] = bt

    def step(t, hprev):
        hn = a_ref[pl.ds(t, 1), :] * hprev + b_ref[pl.ds(t, 1), :]
        hs_ref[pl.ds(t, 1), :] = hn
        return hn

    hlast = lax.fori_loop(0, tm, step, h_ref[...])
    h_ref[...] = hlast
    o_ref[...] = x + _dot((gate * hs_ref[...]).astype(BF16), wo_ref[...])
    hl_ref[...] = hlast
    cn_ref[...] = xpad_ref[pad - (CONV_W - 1):pad, :]


def _lru_prompt(x, bsz, g, wyx, byx, cw, cb, wa, ba, wi, bi, sp, wo):
    n, d = x.shape
    t = n // bsz
    dr = wo.shape[0]
    tm = 256 if t % 256 == 0 else t
    nt = t // tm
    pad = 8
    vec = lambda a: a.reshape(1, -1)
    out, hl, cn = pl.pallas_call(
        _lru_prompt_kernel,
        grid=(bsz, nt),
        in_specs=[pl.BlockSpec((tm, d), lambda i, j: (i * nt + j, 0)), _const_spec((1, d)),
                  _const_spec(wyx.shape), _const_spec((1, 2 * dr)), _const_spec(cw.shape),
                  _const_spec((1, dr)), _const_spec(wa.shape), _const_spec((1, dr)),
                  _const_spec(wi.shape), _const_spec((1, dr)), _const_spec((1, dr)),
                  _const_spec(wo.shape)],
        out_specs=[pl.BlockSpec((tm, d), lambda i, j: (i * nt + j, 0)),
                   pl.BlockSpec((None, 1, dr), lambda i, j: (i, 0, 0)),
                   pl.BlockSpec((None, CONV_W - 1, dr), lambda i, j: (i, 0, 0))],
        out_shape=[jax.ShapeDtypeStruct((n, d), F32), jax.ShapeDtypeStruct((bsz, 1, dr), F32),
                   jax.ShapeDtypeStruct((bsz, CONV_W - 1, dr), F32)],
        scratch_shapes=[pltpu.VMEM((tm + pad, dr), F32), pltpu.VMEM((tm, dr), F32),
                        pltpu.VMEM((tm, dr), F32), pltpu.VMEM((tm, dr), F32), pltpu.VMEM((1, dr), F32)],
        compiler_params=_params("parallel", "arbitrary"), name="lru_prompt",
    )(x, vec(g), wyx, vec(byx), cw, vec(cb), wa, vec(ba), wi, vec(bi), vec(sp), wo)
    return out, hl.reshape(bsz, dr), cn


def _lru_sample_kernel(bsz, x_ref, g_ref, buf_ref, h0_ref, wyx_ref, byx_ref, cw_ref, cb_ref, wa_ref, ba_ref,
                       wi_ref, bi_ref, sp_ref, wo_ref, o_ref, hl_ref, cn_ref, xpad_ref, hs_ref):
    n, dr = hs_ref.shape
    t = n // bsz
    nb = (CONV_W - 1) * bsz
    x = x_ref[...]
    h = (_rms(x) * g_ref[...]).astype(BF16)
    z = _dot(h, wyx_ref[...]) + byx_ref[...]
    gate = _gelu_tanh(z[:, :dr])
    xpad_ref[0:nb, :] = buf_ref[...]
    xpad_ref[nb:, :] = z[:, dr:]
    xb = cb_ref[...] + xpad_ref[0:n, :] * cw_ref[0:1, :]
    for s in range(1, CONV_W):
        xb = xb + xpad_ref[s * bsz:s * bsz + n, :] * cw_ref[s:s + 1, :]
    a, bt = _lru_gates(xb, wa_ref, ba_ref[...], wi_ref, bi_ref[...], sp_ref[...])
    hcur = h0_ref[...]
    for s in range(t):
        hcur = a[s * bsz:(s + 1) * bsz, :] * hcur + bt[s * bsz:(s + 1) * bsz, :]
        hs_ref[s * bsz:(s + 1) * bsz, :] = hcur
    o_ref[...] = x + _dot((gate * hs_ref[...]).astype(BF16), wo_ref[...])
    hl_ref[...] = hcur
    cn_ref[...] = xpad_ref[n:n + nb, :]


def _lru_sample(x, bsz, h0, conv0, g, wyx, byx, cw, cb, wa, ba, wi, bi, sp, wo):
    n, d = x.shape
    t = n // bsz
    dr = wo.shape[0]
    nb = (CONV_W - 1) * bsz
    tmaj = lambda a, tt: a.reshape(bsz, tt, -1).swapaxes(0, 1).reshape(tt * bsz, -1)
    bmaj = lambda a, tt: a.reshape(tt, bsz, -1).swapaxes(0, 1)
    vec = lambda a: a.reshape(1, -1)
    args = (tmaj(x, t), vec(g), tmaj(conv0, CONV_W - 1), h0, wyx, vec(byx), cw, vec(cb), wa, vec(ba),
            wi, vec(bi), vec(sp), wo)
    out, hl, cn = pl.pallas_call(
        functools.partial(_lru_sample_kernel, bsz),
        in_specs=[pl.BlockSpec(a.shape, lambda nd=a.ndim: (0,) * nd) for a in args],
        out_specs=[pl.BlockSpec((n, d), lambda: (0, 0)), pl.BlockSpec((bsz, dr), lambda: (0, 0)),
                   pl.BlockSpec((nb, dr), lambda: (0, 0))],
        out_shape=[jax.ShapeDtypeStruct((n, d), F32), jax.ShapeDtypeStruct((bsz, dr), F32),
                   jax.ShapeDtypeStruct((nb, dr), F32)],
        scratch_shapes=[pltpu.VMEM((n + nb, dr), F32), pltpu.VMEM((n, dr), F32)],
        compiler_params=pltpu.CompilerParams(vmem_limit_bytes=VMEM_LIMIT), name="lru_sample",
    )(*args)
    return bmaj(out, t).reshape(n, d), hl, bmaj(cn, CONV_W - 1)


def _rope_tables_partial(pos, period, rot_dim, theta):
    half = rot_dim // 2
    inv = theta ** (-jnp.arange(half, dtype=F32) / half)
    ang = pos.astype(F32)[:, None] * inv[None, :]
    cos, sin = jnp.cos(ang), jnp.sin(ang)
    t = pos.shape[0]
    zh = jnp.zeros((t, half), F32)
    rest = period - rot_dim
    c = jnp.concatenate([cos, cos, jnp.ones((t, rest), F32)], axis=1)
    s1 = jnp.concatenate([-sin, zh, jnp.zeros((t, rest), F32)], axis=1)
    s2 = jnp.concatenate([zh, sin, jnp.zeros((t, rest), F32)], axis=1)
    rep = LANES // period
    return [jnp.tile(a, (1, rep)) for a in (c, s1, s2)]


def _rot(x, c, s1, s2, half):
    n = x.shape[-1]
    return x * c + pltpu.roll(x, n - half, 1) * s1 + pltpu.roll(x, half, 1) * s2


def _sort_key(x):
    bits = pltpu.bitcast(x, I32)
    return bits ^ ((bits >> 31) & 0x7FFFFFFF)


def _kth_largest(count_ge, k, shape):
    t = jnp.where(count_ge(jnp.zeros(shape, I32)) >= k, 0, INT_MIN).astype(I32)

    def body(i, t):
        cand = t | lax.shift_left(jnp.int32(1), 30 - i)
        return jnp.where(count_ge(cand) >= k, cand, t)

    return lax.fori_loop(0, 31, body, t)


def _attn_kernel(tq, tk, lk, q_pos0, k_pos0, mode, n_cmp, q_ref, k_ref, v_ref, *rest):
    if mode == "window":
        o_ref, qs_ref, acc_ref, m_scr, l_scr = rest
        m_ref = None
    else:
        m_ref, o_ref, qs_ref, acc_ref, m_scr, l_scr = rest
    p0 = q_pos0 + pl.program_id(1) * tq
    for g in range(N_KV):
        for h in range(GROUP):
            c0 = (g * GROUP + h) * HEAD_DIM
            qs_ref[g, h * tq:(h + 1) * tq, :] = q_ref[:, c0:c0 + HEAD_DIM]
    m_scr[...] = jnp.full(m_scr.shape, NEG, F32)
    l_scr[...] = jnp.zeros(l_scr.shape, F32)
    acc_ref[...] = jnp.zeros(acc_ref.shape, F32)
    hi = jnp.minimum((p0 + tq - 1 - k_pos0) // tk + 1, lk // tk)
    lo = jnp.maximum(p0 - (WINDOW - 1) - k_pos0, 0) // tk if mode == "window" else 0
    rowpos = p0 + lax.broadcasted_iota(I32, (tq, tk), 0)

    def body(kt, carry):
        k0 = pl.multiple_of(kt * tk, tk)
        if mode == "mask":
            bias = m_ref[:, pl.ds(k0, tk)].astype(F32)
        else:
            colpos = k_pos0 + k0 + lax.broadcasted_iota(I32, (tq, tk), 1)
            valid = colpos <= rowpos
            if mode == "window":
                valid = valid & (rowpos - colpos < WINDOW) & (colpos >= 0)
                bias = jnp.where(valid, 0.0, -jnp.inf)
        for g in range(N_KV):
            if mode == "blocksel":
                blk = lax.shift_right_logical(k_pos0 + k0 + lax.broadcasted_iota(I32, (LANES, tk), 1), 6)
                e = jnp.where(blk == lax.broadcasted_iota(I32, (LANES, tk), 0), 1.0, 0.0).astype(BF16)
                sel = _dot(m_ref[:, g * LANES:(g + 1) * LANES], e) > 0.5
                bias = jnp.where(valid & (sel | (colpos >= n_cmp * SLC_BLK)), 0.0, -jnp.inf)
            kt_ = k_ref[pl.ds(k0, tk), g * HEAD_DIM:(g + 1) * HEAD_DIM]
            vt_ = v_ref[pl.ds(k0, tk), g * HEAD_DIM:(g + 1) * HEAD_DIM]
            s = _dot_nt(qs_ref[g], kt_)
            s = (s.reshape(GROUP, tq, tk) + bias[None]).reshape(GROUP * tq, tk)
            _softmax_update(g, s, vt_, acc_ref, m_scr, l_scr)
        return carry

    lax.fori_loop(lo, hi, body, 0)
    for g in range(N_KV):
        o = acc_ref[g] / jnp.maximum(l_scr[g], 1e-30)
        for h in range(GROUP):
            c0 = (g * GROUP + h) * HEAD_DIM
            o_ref[:, c0:c0 + HEAD_DIM] = o[h * tq:(h + 1) * tq, :]


def _attention(q, k, v, m, *, bsz, tq, q_pos0, k_pos0, mode, n_cmp=0):
    nq_rows, dq = q.shape
    lk = k.shape[1]
    tk = 512
    assert lk % tk == 0 and (nq_rows // bsz) % tq == 0
    nq = nq_rows // bsz // tq
    kern = functools.partial(_attn_kernel, tq, tk, lk, q_pos0, k_pos0, mode, n_cmp)
    in_specs = [pl.BlockSpec((tq, dq), lambda b, j: (b * nq + j, 0)),
                pl.BlockSpec((None, lk, k.shape[2]), lambda b, j: (b, 0, 0)),
                pl.BlockSpec((None, lk, v.shape[2]), lambda b, j: (b, 0, 0))]
    args = [q, k, v]
    if mode != "window":
        in_specs.append(pl.BlockSpec((tq, m.shape[1]), lambda b, j: (b * nq + j, 0)))
        args.append(m)
    return pl.pallas_call(
        kern, grid=(bsz, nq), in_specs=in_specs,
        out_specs=pl.BlockSpec((tq, dq), lambda b, j: (b * nq + j, 0)),
        out_shape=jax.ShapeDtypeStruct((nq_rows, dq), F32),
        scratch_shapes=[pltpu.VMEM((N_KV, GROUP * tq, HEAD_DIM), BF16),
                        pltpu.VMEM((N_KV, GROUP * tq, HEAD_DIM), F32),
                        pltpu.VMEM((N_KV, GROUP * tq, LANES), F32),
                        pltpu.VMEM((N_KV, GROUP * tq, LANES), F32)],
        compiler_params=_params("parallel", "arbitrary"), name="attn_" + mode,
    )(*args)


def _softmax_update(g, s, vt, acc_ref, m_scr, l_scr):
    m_old = m_scr[g]
    m_new = jnp.maximum(m_old, jnp.max(s, axis=-1, keepdims=True))
    alpha = jnp.exp2(m_old - m_new)
    p = jnp.exp2(s - jnp.concatenate([m_new] * (s.shape[1] // LANES), axis=1))
    l_scr[g] = alpha * l_scr[g] + jnp.sum(p, axis=-1, keepdims=True)
    acc_ref[g] = alpha * acc_ref[g] + _dot(p.astype(BF16), vt)
    m_scr[g] = m_new


def _paged_attn_kernel(tq, pp, n_steps, mode, pt_ref, q_ref, *rest):
    kp, vp = rest[:pp], rest[pp:2 * pp]
    kn_ref, vn_ref = rest[2 * pp:2 * pp + 2]
    if mode == "mask":
        mb_ref, mt_ref, o_ref, qs_ref, acc_ref, m_scr, l_scr, kbuf, vbuf = rest[2 * pp + 2:]
    else:
        mb_ref, o_ref, qs_ref, acc_ref, m_scr, l_scr, kbuf, vbuf = rest[2 * pp + 2:]
    step = pl.program_id(1)
    tk = pp * PAGE_SIZE

    @pl.when(step == 0)
    def _():
        for g in range(N_KV):
            for h in range(GROUP):
                c0 = (g * GROUP + h) * HEAD_DIM
                qs_ref[g, h * tq:(h + 1) * tq, :] = q_ref[:, c0:c0 + HEAD_DIM]
        m_scr[...] = jnp.full(m_scr.shape, NEG, F32)
        l_scr[...] = jnp.zeros(l_scr.shape, F32)
        acc_ref[...] = jnp.zeros(acc_ref.shape, F32)

    for i in range(pp):
        for g in range(N_KV):
            rows = pl.ds(g, PAGE_SIZE, stride=N_KV)
            kbuf[g, i * PAGE_SIZE:(i + 1) * PAGE_SIZE, :] = kp[i][rows, :].astype(BF16)
            vbuf[g, i * PAGE_SIZE:(i + 1) * PAGE_SIZE, :] = vp[i][rows, :].astype(BF16)

    def biased(g, s, bias):
        n = s.shape[1]
        return (s.reshape(GROUP, tq, n) + bias[None]).reshape(GROUP * tq, n)

    for g in range(N_KV):
        if mode == "mask":
            bias = mb_ref[...].astype(F32)
        else:
            blk = lax.shift_right_logical(step * tk + lax.broadcasted_iota(I32, (LANES, tk), 1), 6)
            e = jnp.where(blk == lax.broadcasted_iota(I32, (LANES, tk), 0), 1.0, 0.0).astype(BF16)
            bias = jnp.where(_dot(mb_ref[:, g * LANES:(g + 1) * LANES], e) > 0.5, 0.0, -jnp.inf)
        _softmax_update(g, biased(g, _dot_nt(qs_ref[g], kbuf[g]), bias), vbuf[g], acc_ref, m_scr, l_scr)

    @pl.when(step == n_steps - 1)
    def _():
        tn = kn_ref.shape[0]
        causal = lax.broadcasted_iota(I32, (tq, tn), 1) <= lax.broadcasted_iota(I32, (tq, tn), 0)
        for g in range(N_KV):
            sl = slice(g * HEAD_DIM, (g + 1) * HEAD_DIM)
            bias = mt_ref[...].astype(F32) if mode == "mask" else jnp.where(causal, 0.0, -jnp.inf)
            _softmax_update(g, biased(g, _dot_nt(qs_ref[g], kn_ref[:, sl]), bias), vn_ref[:, sl],
                            acc_ref, m_scr, l_scr)
            o = acc_ref[g] / jnp.maximum(l_scr[g], 1e-30)
            for h in range(GROUP):
                c0 = (g * GROUP + h) * HEAD_DIM
                o_ref[:, c0:c0 + HEAD_DIM] = o[h * tq:(h + 1) * tq, :]


PAGES_PER_STEP = 8
NEW_TILE = LANES


def _paged_attention(q, k_pool, v_pool, page0, page_table, kn, vn, m, *, bsz, tq, mode):
    dq = q.shape[1]
    n_pages = page_table.shape[1]
    pp = PAGES_PER_STEP
    assert n_pages % pp == 0
    n_steps = n_pages // pp
    tk = pp * PAGE_SIZE
    prow = PAGE_SIZE * N_KV
    page = lambda i: pl.BlockSpec((prow, HEAD_DIM), lambda b, s, pt: (page0 + pt[b, s * pp + i], 0))
    in_specs = [pl.BlockSpec((tq, dq), lambda b, s, pt: (b, 0))]
    in_specs += [page(i) for i in range(pp)] * 2
    in_specs += [pl.BlockSpec((NEW_TILE, kn.shape[1]), lambda b, s, pt: (b, 0))] * 2
    args = [q] + [k_pool] * pp + [v_pool] * pp + [kn, vn]
    if mode == "mask":
        in_specs += [pl.BlockSpec((tq, tk), lambda b, s, pt: (b, s)),
                     pl.BlockSpec((tq, NEW_TILE), lambda b, s, pt: (b, n_pages * PAGE_SIZE // NEW_TILE))]
        args += [m, m]
    else:
        in_specs.append(pl.BlockSpec((tq, m.shape[1]), lambda b, s, pt: (b, 0)))
        args.append(m)
    rows = GROUP * tq
    return pl.pallas_call(
        functools.partial(_paged_attn_kernel, tq, pp, n_steps, mode),
        grid_spec=pltpu.PrefetchScalarGridSpec(
            num_scalar_prefetch=1, grid=(bsz, n_steps), in_specs=in_specs,
            out_specs=pl.BlockSpec((tq, dq), lambda b, s, pt: (b, 0)),
            scratch_shapes=[pltpu.VMEM((N_KV, rows, HEAD_DIM), BF16), pltpu.VMEM((N_KV, rows, HEAD_DIM), F32),
                            pltpu.VMEM((N_KV, rows, LANES), F32), pltpu.VMEM((N_KV, rows, LANES), F32),
                            pltpu.VMEM((N_KV, tk, HEAD_DIM), BF16), pltpu.VMEM((N_KV, tk, HEAD_DIM), BF16)]),
        out_shape=jax.ShapeDtypeStruct((bsz * tq, dq), F32),
        compiler_params=_params("parallel", "arbitrary"), name="paged_attn_" + mode,
    )(page_table, *args)


def _head_norm_rot(z, g, tabs, n_heads, scale, refs_f32, refs_bf16):
    c, s1, s2 = tabs
    for hd in range(n_heads):
        sl = slice(hd * HEAD_DIM, (hd + 1) * HEAD_DIM)
        r = _rot(_rms(z[:, sl]) * g, c, s1, s2, ROPE_DIM // 2)
        for ref in refs_f32:
            ref[:, sl] = r
        for ref in refs_bf16:
            ref[:, sl] = (r * scale).astype(BF16)


def _dsa_epilogue(h, w_ref, tabs, cs, o):
    tq_ = [t[...] for t in tabs[:3]]
    ti_ = [t[...] for t in tabs[3:]]
    gq, gk, gik = [c[...] for c in cs]
    q_ref, k_ref, kb_ref, v_ref, vb_ref, qi_ref, ki_ref, wi_ref = o
    nq, nkv = N_HEADS * HEAD_DIM, N_KV * HEAD_DIM
    ni = IDX_HEADS * IDX_DIM
    _head_norm_rot(_dot(h, w_ref[:, 0:nq]), gq, tq_, N_HEADS, ATTN_SCALE * LOG2E, [], [q_ref])
    _head_norm_rot(_dot(h, w_ref[:, nq:nq + nkv]), gk, tq_, N_KV, 1.0, [k_ref], [kb_ref])
    zv = _dot(h, w_ref[:, nq + nkv:nq + 2 * nkv])
    v_ref[...] = zv
    vb_ref[...] = zv.astype(BF16)
    c0 = nq + 2 * nkv
    zi = _dot(h, w_ref[:, c0:c0 + ni])
    for ch in range(ni // LANES):
        r = _rot(zi[:, ch * LANES:(ch + 1) * LANES], *ti_, IDX_ROPE // 2).astype(BF16)
        for u in range(LANES // IDX_DIM):
            qi_ref[ch * (LANES // IDX_DIM) + u] = r[:, u * IDX_DIM:(u + 1) * IDX_DIM]
    zl = _dot(h, w_ref[:, c0 + ni:c0 + ni + LANES])
    lane = lax.broadcasted_iota(I32, zl.shape, 1)
    ms = jnp.sum(jnp.where(lane < IDX_DIM, zl * zl, 0.0), axis=-1, keepdims=True) / IDX_DIM
    r = _rot(zl * lax.rsqrt(ms + EPS) * gik, *ti_, IDX_ROPE // 2)
    ki_ref[...] = r[:, :IDX_DIM]
    wi_ref[...] = zl[:, IDX_DIM:IDX_DIM + IDX_HEADS] * (IDX_HEADS ** -0.5 * IDX_DIM ** -0.5)


def _dsa_select_kernel(tq, tk, lk, q_pos0, top, qi_ref, wi_ref, kit_ref, m_ref, key_ref, wib_ref):
    p0 = q_pos0 + pl.program_id(1) * tq
    nk = jnp.minimum((p0 + tq - 1) // tk + 1, lk // tk)
    wi = wi_ref[...]
    for h in range(IDX_HEADS):
        wib_ref[h] = jnp.broadcast_to(wi[:, h:h + 1], (tq, tk))
    rowpos = p0 + lax.broadcasted_iota(I32, (tq, tk), 0)
    col = lax.broadcasted_iota(I32, (tq, tk), 1)

    def score_tile(kt, carry):
        k0 = pl.multiple_of(kt * tk, tk)
        kt_ = kit_ref[:, pl.ds(k0, tk)]
        acc = jnp.zeros((tq, tk), F32)
        for h in range(IDX_HEADS):
            acc = acc + jnp.maximum(_dot(qi_ref[h], kt_), 0.0) * wib_ref[h]
        acc = jnp.where(k0 + col <= rowpos, acc, -jnp.inf)
        key_ref[:, pl.ds(k0, tk)] = _sort_key(acc)
        return carry

    lax.fori_loop(0, nk, score_tile, 0)

    @pl.when(nk % 2 == 1)
    def _():
        key_ref[:, pl.ds(pl.multiple_of(nk * tk, tk), tk)] = _sort_key(jnp.full((tq, tk), -jnp.inf, F32))

    def count_ge(cand):
        cb = jnp.broadcast_to(cand, (tq, LANES))

        def body(kt, acc):
            for u in range(2 * tk // LANES):
                c0 = pl.multiple_of(kt * (2 * tk) + u * LANES, LANES)
                acc = acc + jnp.where(key_ref[:, pl.ds(c0, LANES)] >= cb, 1.0, 0.0)
            return acc

        acc = lax.fori_loop(0, (nk + 1) // 2, body, jnp.zeros((tq, LANES), F32))
        return jnp.sum(acc, axis=-1, keepdims=True)

    thr = jnp.broadcast_to(_kth_largest(count_ge, float(top), (tq, 1)), (tq, tk))

    def bias_tile(kt, carry):
        k0 = pl.multiple_of(kt * tk, tk)
        sel = (k0 + col <= rowpos) & (key_ref[:, pl.ds(k0, tk)] >= thr)
        m_ref[:, pl.ds(k0, tk)] = jnp.where(sel, 0.0, -jnp.inf).astype(BF16)
        return carry

    lax.fori_loop(0, nk, bias_tile, 0)

    def rest_tile(kt, carry):
        m_ref[:, pl.ds(pl.multiple_of(kt * tk, tk), tk)] = jnp.full((tq, tk), -jnp.inf, BF16)
        return carry

    lax.fori_loop(nk, lk // tk, rest_tile, 0)


def _dsa_select(qi, wi, kit, *, bsz, tq, q_pos0, top):
    lk = kit.shape[2]
    tk = 256
    assert lk % (2 * tk) == 0 and top <= tk
    nq_rows = wi.shape[0]
    nq = nq_rows // bsz // tq
    return pl.pallas_call(
        functools.partial(_dsa_select_kernel, tq, tk, lk, q_pos0, top),
        grid=(bsz, nq),
        in_specs=[pl.BlockSpec((IDX_HEADS, tq, IDX_DIM), lambda b, j: (0, b * nq + j, 0)),
                  pl.BlockSpec((tq, IDX_HEADS), lambda b, j: (b * nq + j, 0)),
                  pl.BlockSpec((None, IDX_DIM, lk), lambda b, j: (b, 0, 0))],
        out_specs=pl.BlockSpec((tq, lk), lambda b, j: (b * nq + j, 0)),
        out_shape=jax.ShapeDtypeStruct((nq_rows, lk), BF16),
        scratch_shapes=[pltpu.VMEM((tq, lk), I32), pltpu.VMEM((IDX_HEADS, tq, tk), F32)],
        compiler_params=_params("parallel", "arbitrary"), name="dsa_select",
    )(qi, wi, kit)


def _nsa_epilogue(h, w_ref, tabs, cs, o):
    tb = [t[...] for t in tabs]
    gq, gsk, gwk = [c[...] for c in cs]
    (qc_ref, q_ref, ck_ref, cv_ref, sk_ref, skb_ref, sv_ref, svb_ref,
     wk_ref, wkb_ref, wv_ref, wvb_ref, gate_ref) = o
    nq, nkv = N_HEADS * HEAD_DIM, N_KV * HEAD_DIM
    zq = _dot(h, w_ref[:, 0:nq])
    for hd in range(N_HEADS):
        sl = slice(hd * HEAD_DIM, (hd + 1) * HEAD_DIM)
        qn = _rms(zq[:, sl]) * gq
        qc_ref[:, sl] = (qn * ATTN_SCALE).astype(BF16)
        q_ref[:, sl] = (_rot(qn, *tb, ROPE_DIM // 2) * (ATTN_SCALE * LOG2E)).astype(BF16)
    seg = lambda i: _dot(h, w_ref[:, nq + i * nkv:nq + (i + 1) * nkv])
    ck_ref[...] = seg(0)
    cv_ref[...] = seg(1)
    _head_norm_rot(seg(2), gsk, tb, N_KV, 1.0, [sk_ref], [skb_ref])
    zsv = seg(3)
    sv_ref[...] = zsv
    svb_ref[...] = zsv.astype(BF16)
    _head_norm_rot(seg(4), gwk, tb, N_KV, 1.0, [wk_ref], [wkb_ref])
    zwv = seg(5)
    wv_ref[...] = zwv
    wvb_ref[...] = zwv.astype(BF16)
    zg = _dot(h, w_ref[:, nq + 6 * nkv:nq + 6 * nkv + LANES])
    gate_ref[...] = jax.nn.sigmoid(zg[:, :N_HEADS * 3])


def _nsa_compress_kernel(ck_ref, cv_ref, pwk_ref, pwv_ref, phik_ref, phiv_ref, g_ref, kc_ref, vc_ref):
    rows, wd = ck_ref.shape
    nb = rows // CMP_BLK
    pk = jnp.sum(ck_ref[...].reshape(nb, CMP_BLK, wd) * pwk_ref[...][None], axis=1)
    pv = jnp.sum(cv_ref[...].reshape(nb, CMP_BLK, wd) * pwv_ref[...][None], axis=1)
    for hd in range(N_KV):
        sl = slice(hd * HEAD_DIM, (hd + 1) * HEAD_DIM)
        kc_ref[:, sl] = (_rms(_dot(pk[:, sl].astype(BF16), phik_ref[hd])) * g_ref[...]).astype(BF16)
        vc_ref[:, sl] = _dot(pv[:, sl].astype(BF16), phiv_ref[hd]).astype(BF16)


def _nsa_compress(ck, cv, pwk, pwv, phik, phiv, gck):
    rows, wd = ck.shape
    step = 2048 if rows % 2048 == 0 else rows
    nb = step // CMP_BLK
    bc = lambda p: jnp.repeat(p, HEAD_DIM, axis=1)
    return pl.pallas_call(
        _nsa_compress_kernel, grid=(rows // step,),
        in_specs=[pl.BlockSpec((step, wd), lambda i: (i, 0)), pl.BlockSpec((step, wd), lambda i: (i, 0)),
                  _const_spec((CMP_BLK, wd)), _const_spec((CMP_BLK, wd)), _const_spec(phik.shape),
                  _const_spec(phiv.shape), _const_spec((1, HEAD_DIM))],
        out_specs=[pl.BlockSpec((nb, wd), lambda i: (i, 0)), pl.BlockSpec((nb, wd), lambda i: (i, 0))],
        out_shape=[jax.ShapeDtypeStruct((rows // CMP_BLK, wd), BF16)] * 2,
        compiler_params=_params("parallel"), name="nsa_compress",
    )(ck, cv, bc(pwk), bc(pwv), phik, phiv, gck.reshape(1, HEAD_DIM))


def _paged_compress_kernel(pp, pt_ref, *rest):
    ckp, cvp = rest[:pp], rest[pp:2 * pp]
    pwk_ref, pwv_ref, phik_ref, phiv_ref, g_ref, kc_ref, vc_ref, pk_scr, pv_scr = rest[2 * pp:]
    nb = PAGE_SIZE // CMP_BLK
    for hd in range(N_KV):
        sl = slice(hd * HEAD_DIM, (hd + 1) * HEAD_DIM)
        rows = pl.ds(hd, PAGE_SIZE, stride=N_KV)
        for i in range(pp):
            pool = lambda ref, pw: jnp.sum(ref[rows, :].reshape(nb, CMP_BLK, HEAD_DIM) * pw[:, sl][None], axis=1)
            pk_scr[hd, i * nb:(i + 1) * nb, :] = pool(ckp[i], pwk_ref)
            pv_scr[hd, i * nb:(i + 1) * nb, :] = pool(cvp[i], pwv_ref)
        kc_ref[:, sl] = (_rms(_dot(pk_scr[hd].astype(BF16), phik_ref[hd])) * g_ref[...]).astype(BF16)
        vc_ref[:, sl] = _dot(pv_scr[hd].astype(BF16), phiv_ref[hd]).astype(BF16)


def _paged_compress(ck_pool, cv_pool, page0, page_table, pwk, pwv, phik, phiv, gck):
    bsz, n_pages = page_table.shape
    pp = PAGES_PER_STEP
    assert n_pages % pp == 0
    n_steps = n_pages // pp
    nb = pp * PAGE_SIZE // CMP_BLK
    wd = N_KV * HEAD_DIM
    prow = PAGE_SIZE * N_KV
    bc = lambda p: jnp.repeat(p, HEAD_DIM, axis=1)
    page = lambda i: pl.BlockSpec((prow, HEAD_DIM), lambda b, s, pt: (page0 + pt[b, s * pp + i], 0))
    const = lambda shape: pl.BlockSpec(shape, lambda b, s, pt: (0,) * len(shape))
    out = pl.BlockSpec((nb, wd), lambda b, s, pt: (b * n_steps + s, 0))
    return pl.pallas_call(
        functools.partial(_paged_compress_kernel, pp),
        grid_spec=pltpu.PrefetchScalarGridSpec(
            num_scalar_prefetch=1, grid=(bsz, n_steps),
            in_specs=[page(i) for i in range(pp)] * 2
            + [const((CMP_BLK, wd)), const((CMP_BLK, wd)), const(phik.shape), const(phiv.shape),
               const((1, HEAD_DIM))],
            out_specs=[out, out],
            scratch_shapes=[pltpu.VMEM((N_KV, nb, HEAD_DIM), F32), pltpu.VMEM((N_KV, nb, HEAD_DIM), F32)]),
        out_shape=[jax.ShapeDtypeStruct((bsz * n_steps * nb, wd), BF16)] * 2,
        compiler_params=_params("parallel", "parallel"), name="paged_compress",
    )(page_table, *([ck_pool] * pp), *([cv_pool] * pp), bc(pwk), bc(pwv), phik, phiv,
      gck.reshape(1, HEAD_DIM))


def _nsa_cmp_kernel(tq, q_pos0, n_sel, qc_ref, kc_ref, vc_ref, oc_ref, bm_ref):
    p0 = q_pos0 + pl.program_id(1) * tq
    nc = kc_ref.shape[0]
    rowpos = p0 + lax.broadcasted_iota(I32, (tq, nc), 0)
    blk = lax.broadcasted_iota(I32, (tq, nc), 1)
    valid = jnp.concatenate([blk * CMP_BLK + (CMP_BLK - 1) <= rowpos] * GROUP, axis=0)
    cur = lax.shift_right_logical(rowpos, 6)
    forced = (blk == 0) | (blk == cur) | (blk == cur - 1)
    keys = []
    for g in range(N_KV):
        sl = slice(g * HEAD_DIM, (g + 1) * HEAD_DIM)
        qs = jnp.concatenate([qc_ref[:, (g * GROUP + h) * HEAD_DIM:(g * GROUP + h + 1) * HEAD_DIM]
                              for h in range(GROUP)], axis=0)
        lc = jnp.where(valid, _dot_nt(qs, kc_ref[:, sl]), NEG)
        m = jnp.max(lc, axis=-1, keepdims=True)
        m = jnp.where(m > 0.5 * NEG, m, 0.0)
        p = jnp.where(valid, jnp.exp(lc - m), 0.0)
        pc = p / jnp.maximum(jnp.sum(p, axis=-1, keepdims=True), 1e-30)
        oc = _dot(pc.astype(BF16), vc_ref[:, sl])
        for h in range(GROUP):
            c0 = (g * GROUP + h) * HEAD_DIM
            oc_ref[:, c0:c0 + HEAD_DIM] = oc[h * tq:(h + 1) * tq, :]
        imp = pc[0:tq]
        for h in range(1, GROUP):
            imp = imp + pc[h * tq:(h + 1) * tq]
        imp = jnp.where(forced, BIG, imp)
        imp = jnp.where(blk > cur, -jnp.inf, imp)
        keys.append(_sort_key(imp))
    key = jnp.concatenate(keys, axis=0)
    count_ge = lambda cand: jnp.sum(jnp.where(key >= cand, 1.0, 0.0), axis=-1, keepdims=True)
    sel = jnp.where(key >= _kth_largest(count_ge, float(n_sel), (N_KV * tq, 1)), 1.0, 0.0).astype(BF16)
    for g in range(N_KV):
        bm_ref[:, g * LANES:(g + 1) * LANES] = sel[g * tq:(g + 1) * tq, :]


def _nsa_cmp(qc, kc, vc, *, bsz, tq, q_pos0, n_sel):
    nq_rows, dq = qc.shape
    nc = kc.shape[1]
    assert nc == LANES
    nq = nq_rows // bsz // tq
    return pl.pallas_call(
        functools.partial(_nsa_cmp_kernel, tq, q_pos0, n_sel), grid=(bsz, nq),
        in_specs=[pl.BlockSpec((tq, dq), lambda b, j: (b * nq + j, 0)),
                  pl.BlockSpec((None, nc, kc.shape[2]), lambda b, j: (b, 0, 0)),
                  pl.BlockSpec((None, nc, vc.shape[2]), lambda b, j: (b, 0, 0))],
        out_specs=[pl.BlockSpec((tq, dq), lambda b, j: (b * nq + j, 0)),
                   pl.BlockSpec((tq, N_KV * LANES), lambda b, j: (b * nq + j, 0))],
        out_shape=[jax.ShapeDtypeStruct((nq_rows, dq), F32),
                   jax.ShapeDtypeStruct((nq_rows, N_KV * LANES), BF16)],
        compiler_params=_params("parallel", "parallel"), name="nsa_cmp",
    )(qc, kc, vc)


def _nsa_combine(gate, oc, os_, ow):
    parts = []
    for hd in range(N_HEADS):
        sl = slice(hd * HEAD_DIM, (hd + 1) * HEAD_DIM)
        parts.append(gate[:, 3 * hd:3 * hd + 1] * oc[:, sl] + gate[:, 3 * hd + 1:3 * hd + 2] * os_[:, sl]
                     + gate[:, 3 * hd + 2:3 * hd + 3] * ow[:, sl])
    return jnp.concatenate(parts, axis=1)


SAMPLE_TQ = 16
PROMPT_TQ = 128
KEY_ALIGN = 512


def _pad_rows(a, bsz, t, tp):
    return jnp.pad(a.reshape(bsz, t, -1), ((0, 0), (0, tp - t), (0, 0))).reshape(bsz * tp, -1)


def _unpad_rows(a, bsz, t, tp):
    return a.reshape(bsz, tp, -1)[:, :t].reshape(bsz * t, -1)


def _cat_keys(old, new, bsz):
    new = new.reshape(bsz, -1, new.shape[-1])
    n = old.shape[1] + new.shape[1]
    return jnp.pad(jnp.concatenate([old.astype(new.dtype), new], axis=1),
                   ((0, 0), (0, -n % KEY_ALIGN), (0, 0)))


def _tile_tables(tabs, bsz, t):
    return tabs if t % ROW_TILE == 0 else [jnp.tile(a, (bsz, 1)) for a in tabs]


def _mixer_c(x, bsz, pos, past, g, wcat, gq, gk, gik, wo):
    n, d = x.shape
    t = n // bsz
    tabs = _tile_tables(_rope_tables_partial(pos, HEAD_DIM, ROPE_DIM, ROPE_THETA)
                        + _rope_tables_partial(pos, IDX_DIM, IDX_ROPE, ROPE_THETA), bsz, t)
    nkv = N_KV * HEAD_DIM
    q, k, kb, v, vb, qi, ki, wi = _proj(
        x, g, wcat, tabs, [gq, gk, gik], _dsa_epilogue,
        [(_rows(N_HEADS * HEAD_DIM), BF16), (_rows(nkv), F32), (_rows(nkv), BF16), (_rows(nkv), F32),
         (_rows(nkv), BF16),
         (lambda n_, tm: ((IDX_HEADS, n_, IDX_DIM), (IDX_HEADS, tm, IDX_DIM), lambda i: (0, i, 0)), BF16),
         (_rows(IDX_DIM), F32), (_rows(IDX_HEADS), F32)], tabs[0].shape[0], "proj_dsa")
    if past is None:
        tq, q_pos0, n_keys = PROMPT_TQ, 0, t
        kit = ki.reshape(bsz, t, IDX_DIM).swapaxes(1, 2).astype(BF16)
        kall, vall = kb.reshape(bsz, t, nkv), vb.reshape(bsz, t, nkv)
    else:
        tq, q_pos0 = SAMPLE_TQ, past["kidx"].shape[1]
        n_keys = q_pos0 + t
        kit = _cat_keys(past["kidx"], ki, bsz).swapaxes(1, 2).astype(BF16)
        q, wi = _pad_rows(q, bsz, t, tq), _pad_rows(wi, bsz, t, tq)
        qi = jnp.pad(qi.reshape(IDX_HEADS, bsz, t, IDX_DIM), ((0, 0), (0, 0), (0, tq - t), (0, 0))
                     ).reshape(IDX_HEADS, bsz * tq, IDX_DIM)
    bias = _dsa_select(qi, wi, kit, bsz=bsz, tq=tq, q_pos0=q_pos0, top=min(TOPK_MAX, n_keys // 4))
    if past is None:
        o = _attention(q, kall, vall, bias, bsz=bsz, tq=tq, q_pos0=q_pos0, k_pos0=0, mode="mask")
    else:
        new = lambda a: _pad_rows(a, bsz, t, NEW_TILE).astype(BF16)
        o = _paged_attention(q, past["k_pool"], past["v_pool"], past["page0"], past["pt"], new(k), new(v), bias,
                             bsz=bsz, tq=tq, mode="mask")
        o = _unpad_rows(o, bsz, t, tq)
    return _outproj(x, wo, [o], lambda a: a), k, v, ki


def _mixer_d(x, bsz, pos, past, g, wcat, gq, gsk, gwk, cmpw, wo):
    n, d = x.shape
    t = n // bsz
    tabs = _tile_tables(_rope_tables_partial(pos, HEAD_DIM, ROPE_DIM, ROPE_THETA), bsz, t)
    nkv = N_KV * HEAD_DIM
    kvo = [(_rows(nkv), F32), (_rows(nkv), BF16)]
    (qc, q, ck, cv, sk, skb, sv, svb, wk, wkb, wv, wvb, gate) = _proj(
        x, g, wcat, tabs, [gq, gsk, gwk], _nsa_epilogue,
        [(_rows(N_HEADS * HEAD_DIM), BF16), (_rows(N_HEADS * HEAD_DIM), BF16), (_rows(nkv), F32),
         (_rows(nkv), F32)] + kvo * 4 + [(_rows(N_HEADS * 3), F32)], tabs[0].shape[0], "proj_nsa")
    if past is None:
        tq, q_pos0, n_keys, win_pos0 = PROMPT_TQ, 0, t, 0
        kc, vc = _nsa_compress(ck, cv, *cmpw)
        three = lambda a: a.reshape(bsz, t, nkv)
        skall, svall, wkall, wvall = three(skb), three(svb), three(wkb), three(wvb)
    else:
        tq, q_pos0 = SAMPLE_TQ, past["pt"].shape[1] * PAGE_SIZE
        n_keys = q_pos0 + t
        win_pos0 = q_pos0 - past["wk"].shape[1]
        assert q_pos0 % CMP_BLK == 0 and t < CMP_BLK
        kc, vc = _paged_compress(past["ck_pool"], past["cv_pool"], past["page0"], past["pt"], *cmpw)
        wkall = _cat_keys(past["wk"], wk, bsz).astype(BF16)
        wvall = _cat_keys(past["wv"], wv, bsz).astype(BF16)
        qc, q = _pad_rows(qc, bsz, t, tq), _pad_rows(q, bsz, t, tq)
    n_cmp = kc.shape[0] // bsz
    assert n_cmp <= LANES
    lane_pad = lambda a: jnp.pad(a.reshape(bsz, n_cmp, nkv), ((0, 0), (0, LANES - n_cmp), (0, 0)))
    kc, vc = lane_pad(kc), lane_pad(vc)
    n_slc = -(-n_keys // SLC_BLK)
    n_lane = min(n_slc, LANES)
    assert n_slc == n_lane or (n_slc == n_lane + 1 and q_pos0 // SLC_BLK == n_lane)
    n_sel = min(N_SLC, n_slc) - (n_slc - n_lane)
    tq_cmp = 2 * tq if (past is None and t % (2 * tq) == 0) else tq
    oc, bm = _nsa_cmp(qc, kc, vc, bsz=bsz, tq=tq_cmp, q_pos0=q_pos0, n_sel=n_sel)
    if past is None:
        os_ = _attention(q, skall, svall, bm, bsz=bsz, tq=tq, q_pos0=q_pos0, k_pos0=0, mode="blocksel",
                         n_cmp=n_lane)
    else:
        new = lambda a: _pad_rows(a, bsz, t, NEW_TILE).astype(BF16)
        os_ = _paged_attention(q, past["sk_pool"], past["sv_pool"], past["page0"], past["pt"], new(sk), new(sv),
                               bm, bsz=bsz, tq=tq, mode="blocksel")
    ow = _attention(q, wkall, wvall, None, bsz=bsz, tq=tq, q_pos0=q_pos0, k_pos0=win_pos0, mode="window")
    if past is not None:
        oc, os_, ow = (_unpad_rows(a, bsz, t, tq) for a in (oc, os_, ow))
    return _outproj(x, wo, [gate, oc, os_, ow], _nsa_combine), (ck, cv, sk, sv, wk, wv)


def _gather_pages(cache, page_table):
    rows = cache[page_table]
    return rows.reshape(page_table.shape[0], page_table.shape[1] * PAGE_SIZE, -1)


def kernel(x_prompt, x_sample, state_a_ret, state_b_h, state_b_conv, cache_c_k, cache_c_v, cache_c_kidx,
           cache_d_ck, cache_d_cv, cache_d_sk, cache_d_sv, state_d_wk, state_d_wv, page_table,
           ffn1_norm, ffn1_wg, ffn1_wu, ffn1_wd, mix_norm, ffn2_norm, ffn2_wg, ffn2_wu, ffn2_wd,
           a_wq, a_wk, a_wv, a_wg, a_wo,
           b_wy, b_by, b_wx, b_bx, b_conv_w, b_conv_b, b_wa, b_ba, b_wi, b_bi, b_lam, b_wo,
           c_wq, c_gq, c_wk, c_gk, c_wv, c_wo, c_wiq, c_wik, c_gik, c_wiw,
           d_wq, d_gq, d_wck, d_wcv, d_pwk, d_pwv, d_phik, d_phiv, d_gck, d_wsk, d_wsv, d_gsk,
           d_wwk, d_wwv, d_gwk, d_wgate, d_wo):
    bp, tp, d = x_prompt.shape
    bs, ts, _ = x_sample.shape
    depth = ffn1_norm.shape[0]
    past_len = page_table.shape[1] * PAGE_SIZE
    pos_p = jnp.arange(tp, dtype=I32)
    pos_s = past_len + jnp.arange(ts, dtype=I32)
    b16 = lambda a: a.astype(BF16)
    row = lambda a: a.reshape(1, -1)
    nkv = N_KV * HEAD_DIM
    pool = lambda c: c.reshape(-1, HEAD_DIM)

    xp = x_prompt.reshape(bp * tp, d)
    xs = x_sample.reshape(bs * ts, d)
    outs = [[] for _ in range(24)]
    for i in range(depth):
        m, j = i % 4, i // 4
        w1 = (ffn1_norm[i], b16(ffn1_wg[i]), b16(ffn1_wu[i]), b16(ffn1_wd[i]))
        xp, xs = _ffn(xp, *w1), _ffn(xs, *w1)
        g = mix_norm[i]
        if m == 0:
            dk = a_wq.shape[2] // RET_HEADS
            dv = a_wv.shape[2] // RET_HEADS
            wcat = b16(jnp.concatenate([a_wq[j], a_wk[j], a_wv[j], a_wg[j]], axis=1))
            wo = b16(a_wo[j])
            xp, s_p = _mixer_a(xp, g, pos_p, bp, jnp.zeros((bp, RET_HEADS, dk, dv), F32), wcat, wo, dk, dv)
            xs, s_s = _mixer_a(xs, g, pos_s, bs, state_a_ret[j], wcat, wo, dk, dv)
            new = [s_p, s_s]
            base = 0
        elif m == 1:
            common = (g, b16(jnp.concatenate([b_wy[j], b_wx[j]], axis=1)), jnp.concatenate([b_by[j], b_bx[j]]),
                      b_conv_w[j], b_conv_b[j], b16(b_wa[j]), b_ba[j].reshape(-1), b16(b_wi[j]),
                      b_bi[j].reshape(-1), jax.nn.softplus(-b_lam[j]), b16(b_wo[j]))
            xp, h_p, c_p = _lru_prompt(xp, bp, *common)
            xs, h_s, c_s = _lru_sample(xs, bs, state_b_h[j], state_b_conv[j], *common)
            new = [h_p, h_s, c_p, c_s]
            base = 2
        elif m == 2:
            zpad = jnp.zeros((d, LANES - IDX_DIM - IDX_HEADS), F32)
            wcat = b16(jnp.concatenate([c_wq[j], c_wk[j], c_wv[j], c_wiq[j], c_wik[j], c_wiw[j], zpad], axis=1))
            gik = jnp.concatenate([c_gik[j], jnp.zeros((LANES - IDX_DIM,), F32)])
            cw = (g, wcat, row(c_gq[j]), row(c_gk[j]), row(gik), b16(c_wo[j]))
            past = dict(kidx=_gather_pages(cache_c_kidx[j], page_table), k_pool=pool(cache_c_k),
                        v_pool=pool(cache_c_v), page0=j * cache_c_k.shape[1], pt=page_table)
            xp, k_p, v_p, i_p = _mixer_c(xp, bp, pos_p, None, *cw)
            xs, k_s, v_s, i_s = _mixer_c(xs, bs, pos_s, past, *cw)
            kv = lambda a, b_, t_: a.reshape(b_, t_, N_KV, HEAD_DIM)
            new = [kv(k_p, bp, tp), kv(k_s, bs, ts), kv(v_p, bp, tp), kv(v_s, bs, ts),
                   i_p.reshape(bp, tp, IDX_DIM), i_s.reshape(bs, ts, IDX_DIM)]
            base = 6
        else:
            zpad = jnp.zeros((d, LANES - N_HEADS * 3), F32)
            wcat = b16(jnp.concatenate([d_wq[j], d_wck[j], d_wcv[j], d_wsk[j], d_wsv[j], d_wwk[j], d_wwv[j],
                                        d_wgate[j], zpad], axis=1))
            cmpw = (d_pwk[j], d_pwv[j], b16(d_phik[j]), b16(d_phiv[j]), d_gck[j])
            dw = (g, wcat, row(d_gq[j]), row(d_gsk[j]), row(d_gwk[j]), cmpw, b16(d_wo[j]))
            wbuf = state_d_wk.shape[2]
            past = dict(ck_pool=pool(cache_d_ck), cv_pool=pool(cache_d_cv), sk_pool=pool(cache_d_sk),
                        sv_pool=pool(cache_d_sv), page0=j * cache_d_ck.shape[1], pt=page_table,
                        wk=state_d_wk[j].reshape(bs, wbuf, nkv), wv=state_d_wv[j].reshape(bs, wbuf, nkv))
            xp, rows_p = _mixer_d(xp, bp, pos_p, None, *dw)
            xs, rows_s = _mixer_d(xs, bs, pos_s, past, *dw)
            kv = lambda a, b_: a.reshape(b_, -1, N_KV, HEAD_DIM)
            new = []
            for a_p, a_s in zip(rows_p[:4], rows_s[:4]):
                new += [kv(a_p, bp), kv(a_s, bs)]
            wb_p = min(WINDOW, tp)
            for a_p, a_s, st in ((rows_p[4], rows_s[4], state_d_wk[j]), (rows_p[5], rows_s[5], state_d_wv[j])):
                new += [kv(a_p, bp)[:, tp - wb_p:], jnp.concatenate([st, kv(a_s, bs)], axis=1)[:, ts:]]
            base = 12
        for off, a in enumerate(new):
            outs[base + off].append(a)
        w2 = (ffn2_norm[i], b16(ffn2_wg[i]), b16(ffn2_wu[i]), b16(ffn2_wd[i]))
        xp, xs = _ffn(xp, *w2), _ffn(xs, *w2)
    return (xp.reshape(bp, tp, d), xs.reshape(bs, ts, d)) + tuple(jnp.stack(o) for o in outs)
```

```python
import functools
import math

import jax
import jax.numpy as jnp
from jax import lax
from jax.experimental import pallas as pl
from jax.experimental.pallas import tpu as pltpu

F32 = jnp.float32
BF16 = jnp.bfloat16
I32 = jnp.int32

EPS = 1e-6
PAGE_SIZE = 128

RET_HEADS = 4
RET_CHUNK = 128
RET_THETA = 10000.0

LRU_BLOCKS = 4
CONV_W = 4
LRU_C = 8.0

N_HEADS = 8
HEAD_DIM = 128
N_KV = 2
GROUP = N_HEADS // N_KV
ROPE_DIM = HEAD_DIM // 4
ROPE_THETA = 500000.0

IDX_HEADS = 16
IDX_DIM = 64
IDX_ROPE = IDX_DIM // 4
TOPK_MAX = 256

CMP_BLK = 64
SLC_BLK = 64
N_SLC = 16
WINDOW = 512
BIG = 1e4

LANES = 128
VMEM_LIMIT = 56 * 1024 * 1024
ROW_TILE = 512
NEG = -1e30
ATTN_SCALE = HEAD_DIM ** -0.5
LOG2E = math.log2(math.e)
INT_MIN = -2147483648


def _params(*sem):
    return pltpu.CompilerParams(dimension_semantics=sem, vmem_limit_bytes=VMEM_LIMIT)


def _const_spec(shape):
    nd = len(shape)
    return pl.BlockSpec(shape, lambda *_: (0,) * nd, pipeline_mode=pl.Buffered(1))


def _rms(x):
    return x * lax.rsqrt(jnp.mean(x * x, axis=-1, keepdims=True) + EPS)


def _dot(a, b):
    return jnp.dot(a, b, preferred_element_type=F32)


def _dot_nt(a, b):
    return lax.dot_general(a, b, (((1,), (1,)), ((), ())), preferred_element_type=F32)


def _row_tile(n):
    return ROW_TILE if n % ROW_TILE == 0 else n


def _ffn_kernel(x_ref, g_ref, wg_ref, wu_ref, wd_ref, o_ref):
    x = x_ref[...]
    h = (_rms(x) * g_ref[...]).astype(BF16)
    gt = _dot(h, wg_ref[...])
    ut = _dot(h, wu_ref[...])
    a = (gt * jax.nn.sigmoid(gt) * ut).astype(BF16)
    o_ref[...] = x + 0.5 * _dot(a, wd_ref[...])


def _ffn(x, g, wg, wu, wd, layer):
    n, d = x.shape
    f = wg.shape[2]
    tm = _row_tile(n)
    wspec = lambda a, b_: pl.BlockSpec((None, a, b_), lambda i: (layer, 0, 0), pipeline_mode=pl.Buffered(1))
    return pl.pallas_call(
        _ffn_kernel,
        grid=(n // tm,),
        in_specs=[pl.BlockSpec((tm, d), lambda i: (i, 0)), _const_spec((1, d)),
                  wspec(d, f), wspec(d, f), wspec(f, d)],
        out_specs=pl.BlockSpec((tm, d), lambda i: (i, 0)),
        out_shape=jax.ShapeDtypeStruct((n, d), F32),
        compiler_params=_params("parallel"), name="ffn",
    )(x, g.reshape(1, d), wg, wu, wd)


def _cast_kernel(x_ref, o_ref):
    o_ref[...] = x_ref[...].astype(o_ref.dtype)


def _to_bf16(w):
    c = w.shape[-1]
    r = w.size // c
    tm = _row_tile(r)
    out = pl.pallas_call(
        _cast_kernel, grid=(r // tm,),
        in_specs=[pl.BlockSpec((tm, c), lambda i: (i, 0))],
        out_specs=pl.BlockSpec((tm, c), lambda i: (i, 0)),
        out_shape=jax.ShapeDtypeStruct((r, c), BF16),
        compiler_params=_params("parallel"), name="to_bf16",
    )(w.reshape(r, c))
    return out.reshape(w.shape)


def _proj(x, g, w, tables, consts, epilogue, outs, n_tab_rows, name):
    n, d = x.shape
    tm = _row_tile(n)
    assert n_tab_rows % tm == 0
    nt = n_tab_rows // tm
    nw = w.shape[1]

    def kern(x_ref, g_ref, w_ref, *rest):
        tabs = rest[:len(tables)]
        cs = rest[len(tables):len(tables) + len(consts)]
        o = rest[len(tables) + len(consts):]
        h = (_rms(x_ref[...]) * g_ref[...]).astype(BF16)
        epilogue(h, w_ref, tabs, cs, o)

    in_specs = [pl.BlockSpec((tm, d), lambda i: (i, 0)), _const_spec((1, d)), _const_spec((d, nw))]
    in_specs += [pl.BlockSpec((tm, t.shape[1]), lambda i: (i % nt, 0)) for t in tables]
    in_specs += [_const_spec(c.shape) for c in consts]
    out_shape, out_specs = [], []
    for fn, dt in outs:
        ashape, bshape, imap = fn(n, tm)
        out_shape.append(jax.ShapeDtypeStruct(ashape, dt))
        out_specs.append(pl.BlockSpec(bshape, imap))
    return pl.pallas_call(
        kern, grid=(n // tm,), in_specs=in_specs, out_specs=out_specs, out_shape=out_shape,
        compiler_params=_params("parallel"), name=name,
    )(x, g.reshape(1, d), w, *tables, *consts)


def _rows(c):
    return lambda n, tm: ((n, c), (tm, c), lambda i: (i, 0))


def _outproj(x, w, ins, combine):
    n, d = x.shape
    tm = _row_tile(n)

    def kern(x_ref, w_ref, *rest):
        o_ref = rest[-1]
        a = combine(*[r[...] for r in rest[:-1]]).astype(BF16)
        o_ref[...] = x_ref[...] + _dot(a, w_ref[...])

    return pl.pallas_call(
        kern, grid=(n // tm,),
        in_specs=[pl.BlockSpec((tm, d), lambda i: (i, 0)), _const_spec(w.shape)]
        + [pl.BlockSpec((tm, a.shape[1]), lambda i: (i, 0)) for a in ins],
        out_specs=pl.BlockSpec((tm, d), lambda i: (i, 0)),
        out_shape=jax.ShapeDtypeStruct((n, d), F32),
        compiler_params=_params("parallel"), name="outproj",
    )(x, w, *ins)


def _ret_epilogue(dk, dv, h, w_ref, tabs, cs, o):
    cos, sin = tabs[0][...], tabs[1][...]
    q_ref, k_ref, v_ref, g_ref = o
    half = dk // 2
    nqk = RET_HEADS * dk
    for seg, ref, scale in ((0, q_ref, 1.0), (1, k_ref, dk ** -0.5)):
        z = _dot(h, w_ref[:, seg * nqk:(seg + 1) * nqk])
        for hd in range(RET_HEADS):
            x1 = z[:, hd * dk:hd * dk + half]
            x2 = z[:, hd * dk + half:(hd + 1) * dk]
            ref[:, hd * dk:hd * dk + half] = (x1 * cos - x2 * sin) * scale
            ref[:, hd * dk + half:(hd + 1) * dk] = (x2 * cos + x1 * sin) * scale
    nv = RET_HEADS * dv
    v_ref[...] = _dot(h, w_ref[:, 2 * nqk:2 * nqk + nv]).astype(BF16)
    zg = _dot(h, w_ref[:, 2 * nqk + nv:2 * nqk + 2 * nv])
    g_ref[...] = zg * jax.nn.sigmoid(zg)


def _ret_chunk_kernel(q_ref, k_ref, v_ref, s0_ref, dm_ref, xi_ref, zt_ref, gc_ref, o_ref, s_ref):
    c = pl.program_id(2)

    @pl.when(c == 0)
    def _():
        s_ref[...] = s0_ref[...]

    nh, dk, dv = s_ref.shape
    for hd in range(nh):
        q = q_ref[:, hd * dk:(hd + 1) * dk]
        k = k_ref[:, hd * dk:(hd + 1) * dk]
        v = v_ref[:, hd * dv:(hd + 1) * dv]
        s = s_ref[hd]
        att = (_dot_nt(q.astype(BF16), k.astype(BF16)) * dm_ref[hd]).astype(BF16)
        o = _dot(att, v) + _dot((q * xi_ref[hd]).astype(BF16), s.astype(BF16))
        kz = (k * zt_ref[hd]).astype(BF16)
        s_ref[hd] = s * gc_ref[hd] + lax.dot_general(kz, v, (((0,), (0,)), ((), ())),
                                                     preferred_element_type=F32)
        o_ref[:, hd * dv:(hd + 1) * dv] = _rms(o)


def _retention(q, k, v, s0, n_valid):
    b, hh, dk, dv = s0.shape
    t = q.shape[0] // b
    c = RET_CHUNK
    nc = t // c
    cc = c if n_valid % c == 0 else n_valid
    lg = jnp.log1p(-jnp.exp2(-5.0 - jnp.arange(hh, dtype=F32)))
    idx = jnp.arange(c, dtype=F32)
    diff = idx[:, None] - idx[None, :]
    dmask = jnp.where(diff >= 0, jnp.exp(lg[:, None, None] * jnp.maximum(diff, 0.0)), 0.0)
    xi = jnp.exp(lg[:, None] * (idx + 1.0))[:, :, None]
    zeta = jnp.exp(lg[:, None] * (cc - 1.0 - idx))[:, :, None]
    g_c = jnp.exp(lg * cc)[:, None, None]
    nh = 2 if hh % 2 == 0 else 1
    return pl.pallas_call(
        _ret_chunk_kernel,
        grid=(b, hh // nh, nc),
        in_specs=[pl.BlockSpec((c, nh * dk), lambda i, j, l: (i * nc + l, j)),
                  pl.BlockSpec((c, nh * dk), lambda i, j, l: (i * nc + l, j)),
                  pl.BlockSpec((c, nh * dv), lambda i, j, l: (i * nc + l, j)),
                  pl.BlockSpec((None, nh, dk, dv), lambda i, j, l: (i, j, 0, 0)),
                  pl.BlockSpec((nh, c, c), lambda i, j, l: (j, 0, 0)),
                  pl.BlockSpec((nh, c, 1), lambda i, j, l: (j, 0, 0)),
                  pl.BlockSpec((nh, c, 1), lambda i, j, l: (j, 0, 0)),
                  pl.BlockSpec((nh, 1, 1), lambda i, j, l: (j, 0, 0))],
        out_specs=[pl.BlockSpec((c, nh * dv), lambda i, j, l: (i * nc + l, j)),
                   pl.BlockSpec((None, nh, dk, dv), lambda i, j, l: (i, j, 0, 0))],
        out_shape=[jax.ShapeDtypeStruct((b * t, hh * dv), F32),
                   jax.ShapeDtypeStruct((b, hh, dk, dv), F32)],
        compiler_params=_params("parallel", "parallel", "arbitrary"), name="retention",
    )(q, k, v, s0, dmask, xi, zeta, g_c)


def _rope_tables_full(pos, dim, theta):
    half = dim // 2
    inv = theta ** (-jnp.arange(half, dtype=F32) / half)
    ang = pos.astype(F32)[:, None] * inv[None, :]
    return jnp.cos(ang), jnp.sin(ang)


def _mixer_a(x, g, pos, bsz, s0, wcat, wo, dk, dv):
    n, d = x.shape
    t = n // bsz
    cos, sin = _rope_tables_full(pos, dk, RET_THETA)
    if t % RET_CHUNK:
        cos = jnp.tile(cos, (bsz, 1))
        sin = jnp.tile(sin, (bsz, 1))
    q, k, v, sg = _proj(
        x, g, wcat, [cos, sin], [], functools.partial(_ret_epilogue, dk, dv),
        [(_rows(RET_HEADS * dk), F32), (_rows(RET_HEADS * dk), F32),
         (_rows(RET_HEADS * dv), BF16), (_rows(RET_HEADS * dv), F32)], cos.shape[0], "proj_ret")
    if t % RET_CHUNK:
        pad = lambda a: jnp.pad(a.reshape(bsz, t, -1), ((0, 0), (0, RET_CHUNK - t), (0, 0))
                                ).reshape(bsz * RET_CHUNK, -1)
        o, s_new = _retention(pad(q), pad(k), pad(v), s0, t)
        o = o.reshape(bsz, RET_CHUNK, -1)[:, :t].reshape(n, -1)
    else:
        o, s_new = _retention(q, k, v, s0, t)
    return _outproj(x, wo, [sg, o], lambda a, b: a * b), s_new


def _gelu_tanh(x):
    return 0.5 * x * (1.0 + jnp.tanh(math.sqrt(2.0 / math.pi) * (x + 0.044715 * (x * x * x))))


def _lru_gates(xb, wa_ref, ba, wi_ref, bi, sp):
    bw = wa_ref.shape[1]
    xbb = xb.astype(BF16)
    r = jnp.concatenate([_dot(xbb[:, n * bw:(n + 1) * bw], wa_ref[n]) for n in range(LRU_BLOCKS)], axis=1)
    i = jnp.concatenate([_dot(xbb[:, n * bw:(n + 1) * bw], wi_ref[n]) for n in range(LRU_BLOCKS)], axis=1)
    r = jax.nn.sigmoid(r + ba)
    i = jax.nn.sigmoid(i + bi)
    log_a = -LRU_C * r * sp
    a = jnp.exp(log_a)
    th = jnp.tanh(log_a)
    bt = jnp.sqrt(-2.0 * th / (1.0 - th)) * (i * xb)
    return a, bt


def _lru_prompt_kernel(x_ref, g_ref, wyx_ref, byx_ref, cw_ref, cb_ref, wa_ref, ba_ref, wi_ref, bi_ref,
                       sp_ref, wo_ref, o_ref, hl_ref, cn_ref, xpad_ref, a_ref, b_ref, hs_ref, h_ref):
    j = pl.program_id(1)
    tm, dr = a_ref.shape
    pad = xpad_ref.shape[0] - tm

    @pl.when(j == 0)
    def _():
        h_ref[...] = jnp.zeros_like(h_ref)
        xpad_ref[0:pad, :] = jnp.zeros((pad, dr), F32)

    x = x_ref[...]
    h = (_rms(x) * g_ref[...]).astype(BF16)
    z = _dot(h, wyx_ref[...]) + byx_ref[...]
    gate = _gelu_tanh(z[:, :dr])
    xpad_ref[pad:, :] = z[:, dr:]
    xb = cb_ref[...] + z[:, dr:] * cw_ref[CONV_W - 1:CONV_W, :]
    for s in range(1, CONV_W):
        xb = xb + xpad_ref[pad - s:pad - s + tm, :] * cw_ref[CONV_W - 1 - s:CONV_W - s, :]
    xpad_ref[0:pad, :] = xpad_ref[tm:tm + pad, :]
    a, bt = _lru_gates(xb, wa_ref, ba_ref[...], wi_ref, bi_ref[...], sp_ref[...])
    a_ref[...] = a
    b_ref[...] = bt

    def step(t, hprev):
        hn = a_ref[pl.ds(t, 1), :] * hprev + b_ref[pl.ds(t, 1), :]
        hs_ref[pl.ds(t, 1), :] = hn
        return hn

    hlast = lax.fori_loop(0, tm, step, h_ref[...])
    h_ref[...] = hlast
    o_ref[...] = x + _dot((gate * hs_ref[...]).astype(BF16), wo_ref[...])
    hl_ref[...] = hlast
    cn_ref[...] = xpad_ref[pad - (CONV_W - 1):pad, :]


def _lru_prompt(x, bsz, g, wyx, byx, cw, cb, wa, ba, wi, bi, sp, wo):
    n, d = x.shape
    t = n // bsz
    dr = wo.shape[0]
    tm = 256 if t % 256 == 0 else t
    nt = t // tm
    pad = 8
    vec = lambda a: a.reshape(1, -1)
    out, hl, cn = pl.pallas_call(
        _lru_prompt_kernel,
        grid=(bsz, nt),
        in_specs=[pl.BlockSpec((tm, d), lambda i, j: (i * nt + j, 0)), _const_spec((1, d)),
                  _const_spec(wyx.shape), _const_spec((1, 2 * dr)), _const_spec(cw.shape),
                  _const_spec((1, dr)), _const_spec(wa.shape), _const_spec((1, dr)),
                  _const_spec(wi.shape), _const_spec((1, dr)), _const_spec((1, dr)),
                  _const_spec(wo.shape)],
        out_specs=[pl.BlockSpec((tm, d), lambda i, j: (i * nt + j, 0)),
                   pl.BlockSpec((None, 1, dr), lambda i, j: (i, 0, 0)),
                   pl.BlockSpec((None, CONV_W - 1, dr), lambda i, j: (i, 0, 0))],
        out_shape=[jax.ShapeDtypeStruct((n, d), F32), jax.ShapeDtypeStruct((bsz, 1, dr), F32),
                   jax.ShapeDtypeStruct((bsz, CONV_W - 1, dr), F32)],
        scratch_shapes=[pltpu.VMEM((tm + pad, dr), F32), pltpu.VMEM((tm, dr), F32),
                        pltpu.VMEM((tm, dr), F32), pltpu.VMEM((tm, dr), F32), pltpu.VMEM((1, dr), F32)],
        compiler_params=_params("parallel", "arbitrary"), name="lru_prompt",
    )(x, vec(g), wyx, vec(byx), cw, vec(cb), wa, vec(ba), wi, vec(bi), vec(sp), wo)
    return out, hl.reshape(bsz, dr), cn


def _lru_sample_kernel(bsz, x_ref, g_ref, buf_ref, h0_ref, wyx_ref, byx_ref, cw_ref, cb_ref, wa_ref, ba_ref,
                       wi_ref, bi_ref, sp_ref, wo_ref, o_ref, hl_ref, cn_ref, xpad_ref, hs_ref):
    n, dr = hs_ref.shape
    t = n // bsz
    nb = (CONV_W - 1) * bsz
    x = x_ref[...]
    h = (_rms(x) * g_ref[...]).astype(BF16)
    z = _dot(h, wyx_ref[...]) + byx_ref[...]
    gate = _gelu_tanh(z[:, :dr])
    xpad_ref[0:nb, :] = buf_ref[...]
    xpad_ref[nb:, :] = z[:, dr:]
    xb = cb_ref[...] + xpad_ref[0:n, :] * cw_ref[0:1, :]
    for s in range(1, CONV_W):
        xb = xb + xpad_ref[s * bsz:s * bsz + n, :] * cw_ref[s:s + 1, :]
    a, bt = _lru_gates(xb, wa_ref, ba_ref[...], wi_ref, bi_ref[...], sp_ref[...])
    hcur = h0_ref[...]
    for s in range(t):
        hcur = a[s * bsz:(s + 1) * bsz, :] * hcur + bt[s * bsz:(s + 1) * bsz, :]
        hs_ref[s * bsz:(s + 1) * bsz, :] = hcur
    o_ref[...] = x + _dot((gate * hs_ref[...]).astype(BF16), wo_ref[...])
    hl_ref[...] = hcur
    cn_ref[...] = xpad_ref[n:n + nb, :]


def _lru_sample(x, bsz, h0, conv0, g, wyx, byx, cw, cb, wa, ba, wi, bi, sp, wo):
    n, d = x.shape
    t = n // bsz
    dr = wo.shape[0]
    nb = (CONV_W - 1) * bsz
    tmaj = lambda a, tt: a.reshape(bsz, tt, -1).swapaxes(0, 1).reshape(tt * bsz, -1)
    bmaj = lambda a, tt: a.reshape(tt, bsz, -1).swapaxes(0, 1)
    vec = lambda a: a.reshape(1, -1)
    args = (tmaj(x, t), vec(g), tmaj(conv0, CONV_W - 1), h0, wyx, vec(byx), cw, vec(cb), wa, vec(ba),
            wi, vec(bi), vec(sp), wo)
    out, hl, cn = pl.pallas_call(
        functools.partial(_lru_sample_kernel, bsz),
        in_specs=[pl.BlockSpec(a.shape, lambda nd=a.ndim: (0,) * nd) for a in args],
        out_specs=[pl.BlockSpec((n, d), lambda: (0, 0)), pl.BlockSpec((bsz, dr), lambda: (0, 0)),
                   pl.BlockSpec((nb, dr), lambda: (0, 0))],
        out_shape=[jax.ShapeDtypeStruct((n, d), F32), jax.ShapeDtypeStruct((bsz, dr), F32),
                   jax.ShapeDtypeStruct((nb, dr), F32)],
        scratch_shapes=[pltpu.VMEM((n + nb, dr), F32), pltpu.VMEM((n, dr), F32)],
        compiler_params=pltpu.CompilerParams(vmem_limit_bytes=VMEM_LIMIT), name="lru_sample",
    )(*args)
    return bmaj(out, t).reshape(n, d), hl, bmaj(cn, CONV_W - 1)


def _rope_tables_partial(pos, period, rot_dim, theta):
    half = rot_dim // 2
    inv = theta ** (-jnp.arange(half, dtype=F32) / half)
    ang = pos.astype(F32)[:, None] * inv[None, :]
    cos, sin = jnp.cos(ang), jnp.sin(ang)
    t = pos.shape[0]
    zh = jnp.zeros((t, half), F32)
    rest = period - rot_dim
    c = jnp.concatenate([cos, cos, jnp.ones((t, rest), F32)], axis=1)
    s1 = jnp.concatenate([-sin, zh, jnp.zeros((t, rest), F32)], axis=1)
    s2 = jnp.concatenate([zh, sin, jnp.zeros((t, rest), F32)], axis=1)
    rep = LANES // period
    return [jnp.tile(a, (1, rep)) for a in (c, s1, s2)]


def _rot(x, c, s1, s2, half):
    n = x.shape[-1]
    return x * c + pltpu.roll(x, n - half, 1) * s1 + pltpu.roll(x, half, 1) * s2


def _sort_key(x):
    bits = pltpu.bitcast(x, I32)
    return bits ^ ((bits >> 31) & 0x7FFFFFFF)


def _kth_largest(count_ge, k, shape):
    t = jnp.where(count_ge(jnp.zeros(shape, I32)) >= k, 0, INT_MIN).astype(I32)

    def body(i, t):
        cand = t | lax.shift_left(jnp.int32(1), 30 - i)
        return jnp.where(count_ge(cand) >= k, cand, t)

    return lax.fori_loop(0, 31, body, t)


def _attn_kernel(tq, tk, lk, q_pos0, k_pos0, mode, n_cmp, q_ref, k_ref, v_ref, *rest):
    if mode == "window":
        o_ref, qs_ref, acc_ref, m_scr, l_scr = rest
        m_ref = None
    else:
        m_ref, o_ref, qs_ref, acc_ref, m_scr, l_scr = rest
    p0 = q_pos0 + pl.program_id(1) * tq
    for g in range(N_KV):
        for h in range(GROUP):
            c0 = (g * GROUP + h) * HEAD_DIM
            qs_ref[g, h * tq:(h + 1) * tq, :] = q_ref[:, c0:c0 + HEAD_DIM]
    m_scr[...] = jnp.full(m_scr.shape, NEG, F32)
    l_scr[...] = jnp.zeros(l_scr.shape, F32)
    acc_ref[...] = jnp.zeros(acc_ref.shape, F32)
    hi = jnp.minimum((p0 + tq - 1 - k_pos0) // tk + 1, lk // tk)
    lo = jnp.maximum(p0 - (WINDOW - 1) - k_pos0, 0) // tk if mode == "window" else 0
    rowpos = p0 + lax.broadcasted_iota(I32, (tq, tk), 0)

    def body(kt, carry):
        k0 = pl.multiple_of(kt * tk, tk)
        if mode == "mask":
            bias = m_ref[:, pl.ds(k0, tk)].astype(F32)
        else:
            colpos = k_pos0 + k0 + lax.broadcasted_iota(I32, (tq, tk), 1)
            valid = colpos <= rowpos
            if mode == "window":
                valid = valid & (rowpos - colpos < WINDOW) & (colpos >= 0)
                bias = jnp.where(valid, 0.0, -jnp.inf)
        for g in range(N_KV):
            if mode == "blocksel":
                blk = lax.shift_right_logical(k_pos0 + k0 + lax.broadcasted_iota(I32, (LANES, tk), 1), 6)
                e = jnp.where(blk == lax.broadcasted_iota(I32, (LANES, tk), 0), 1.0, 0.0).astype(BF16)
                sel = _dot(m_ref[:, g * LANES:(g + 1) * LANES], e) > 0.5
                bias = jnp.where(valid & (sel | (colpos >= n_cmp * SLC_BLK)), 0.0, -jnp.inf)
            kt_ = k_ref[pl.ds(k0, tk), g * HEAD_DIM:(g + 1) * HEAD_DIM]
            vt_ = v_ref[pl.ds(k0, tk), g * HEAD_DIM:(g + 1) * HEAD_DIM]
            s = _dot_nt(qs_ref[g], kt_)
            s = (s.reshape(GROUP, tq, tk) + bias[None]).reshape(GROUP * tq, tk)
            _softmax_update(g, s, vt_, acc_ref, m_scr, l_scr)
        return carry

    lax.fori_loop(lo, hi, body, 0)
    for g in range(N_KV):
        o = acc_ref[g] / jnp.maximum(l_scr[g], 1e-30)
        for h in range(GROUP):
            c0 = (g * GROUP + h) * HEAD_DIM
            o_ref[:, c0:c0 + HEAD_DIM] = o[h * tq:(h + 1) * tq, :]


def _attention(q, k, v, m, *, bsz, tq, q_pos0, k_pos0, mode, n_cmp=0):
    nq_rows, dq = q.shape
    lk = k.shape[1]
    tk = 512
    assert lk % tk == 0 and (nq_rows // bsz) % tq == 0
    nq = nq_rows // bsz // tq
    kern = functools.partial(_attn_kernel, tq, tk, lk, q_pos0, k_pos0, mode, n_cmp)
    in_specs = [pl.BlockSpec((tq, dq), lambda b, j: (b * nq + j, 0)),
                pl.BlockSpec((None, lk, k.shape[2]), lambda b, j: (b, 0, 0)),
                pl.BlockSpec((None, lk, v.shape[2]), lambda b, j: (b, 0, 0))]
    args = [q, k, v]
    if mode != "window":
        in_specs.append(pl.BlockSpec((tq, m.shape[1]), lambda b, j: (b * nq + j, 0)))
        args.append(m)
    return pl.pallas_call(
        kern, grid=(bsz, nq), in_specs=in_specs,
        out_specs=pl.BlockSpec((tq, dq), lambda b, j: (b * nq + j, 0)),
        out_shape=jax.ShapeDtypeStruct((nq_rows, dq), F32),
        scratch_shapes=[pltpu.VMEM((N_KV, GROUP * tq, HEAD_DIM), BF16),
                        pltpu.VMEM((N_KV, GROUP * tq, HEAD_DIM), F32),
                        pltpu.VMEM((N_KV, GROUP * tq, LANES), F32),
                        pltpu.VMEM((N_KV, GROUP * tq, LANES), F32)],
        compiler_params=_params("parallel", "arbitrary"), name="attn_" + mode,
    )(*args)


def _softmax_update(g, s, vt, acc_ref, m_scr, l_scr):
    m_old = m_scr[g]
    m_new = jnp.maximum(m_old, jnp.max(s, axis=-1, keepdims=True))
    alpha = jnp.exp2(m_old - m_new)
    p = jnp.exp2(s - jnp.concatenate([m_new] * (s.shape[1] // LANES), axis=1))
    l_scr[g] = alpha * l_scr[g] + jnp.sum(p, axis=-1, keepdims=True)
    acc_ref[g] = alpha * acc_ref[g] + _dot(p.astype(BF16), vt)
    m_scr[g] = m_new


def _paged_attn_kernel(tq, pp, n_steps, mode, pt_ref, q_ref, *rest):
    kp, vp = rest[:pp], rest[pp:2 * pp]
    kn_ref, vn_ref = rest[2 * pp:2 * pp + 2]
    if mode == "mask":
        mb_ref, mt_ref, o_ref, qs_ref, acc_ref, m_scr, l_scr, kbuf, vbuf = rest[2 * pp + 2:]
    else:
        mb_ref, o_ref, qs_ref, acc_ref, m_scr, l_scr, kbuf, vbuf = rest[2 * pp + 2:]
    step = pl.program_id(1)
    tk = pp * PAGE_SIZE

    @pl.when(step == 0)
    def _():
        for g in range(N_KV):
            for h in range(GROUP):
                c0 = (g * GROUP + h) * HEAD_DIM
                qs_ref[g, h * tq:(h + 1) * tq, :] = q_ref[:, c0:c0 + HEAD_DIM]
        m_scr[...] = jnp.full(m_scr.shape, NEG, F32)
        l_scr[...] = jnp.zeros(l_scr.shape, F32)
        acc_ref[...] = jnp.zeros(acc_ref.shape, F32)

    for i in range(pp):
        for g in range(N_KV):
            rows = pl.ds(g, PAGE_SIZE, stride=N_KV)
            kbuf[g, i * PAGE_SIZE:(i + 1) * PAGE_SIZE, :] = kp[i][rows, :].astype(BF16)
            vbuf[g, i * PAGE_SIZE:(i + 1) * PAGE_SIZE, :] = vp[i][rows, :].astype(BF16)

    def biased(g, s, bias):
        n = s.shape[1]
        return (s.reshape(GROUP, tq, n) + bias[None]).reshape(GROUP * tq, n)

    for g in range(N_KV):
        if mode == "mask":
            bias = mb_ref[...].astype(F32)
        else:
            blk = lax.shift_right_logical(step * tk + lax.broadcasted_iota(I32, (LANES, tk), 1), 6)
            e = jnp.where(blk == lax.broadcasted_iota(I32, (LANES, tk), 0), 1.0, 0.0).astype(BF16)
            bias = jnp.where(_dot(mb_ref[:, g * LANES:(g + 1) * LANES], e) > 0.5, 0.0, -jnp.inf)
        _softmax_update(g, biased(g, _dot_nt(qs_ref[g], kbuf[g]), bias), vbuf[g], acc_ref, m_scr, l_scr)

    @pl.when(step == n_steps - 1)
    def _():
        tn = kn_ref.shape[0]
        causal = lax.broadcasted_iota(I32, (tq, tn), 1) <= lax.broadcasted_iota(I32, (tq, tn), 0)
        for g in range(N_KV):
            sl = slice(g * HEAD_DIM, (g + 1) * HEAD_DIM)
            bias = mt_ref[...].astype(F32) if mode == "mask" else jnp.where(causal, 0.0, -jnp.inf)
            _softmax_update(g, biased(g, _dot_nt(qs_ref[g], kn_ref[:, sl]), bias), vn_ref[:, sl],
                            acc_ref, m_scr, l_scr)
            o = acc_ref[g] / jnp.maximum(l_scr[g], 1e-30)
            for h in range(GROUP):
                c0 = (g * GROUP + h) * HEAD_DIM
                o_ref[:, c0:c0 + HEAD_DIM] = o[h * tq:(h + 1) * tq, :]


PAGES_PER_STEP = 16
NEW_TILE = LANES


def _paged_attention(q, k_pool, v_pool, page0, page_table, kn, vn, m, *, bsz, tq, mode):
    dq = q.shape[1]
    n_pages = page_table.shape[1]
    pp = PAGES_PER_STEP
    assert n_pages % pp == 0
    n_steps = n_pages // pp
    tk = pp * PAGE_SIZE
    prow = PAGE_SIZE * N_KV
    page = lambda i: pl.BlockSpec((prow, HEAD_DIM), lambda b, s, pt: (page0 + pt[b, s * pp + i], 0))
    in_specs = [pl.BlockSpec((tq, dq), lambda b, s, pt: (b, 0))]
    in_specs += [page(i) for i in range(pp)] * 2
    in_specs += [pl.BlockSpec((NEW_TILE, kn.shape[1]), lambda b, s, pt: (b, 0))] * 2
    args = [q] + [k_pool] * pp + [v_pool] * pp + [kn, vn]
    if mode == "mask":
        in_specs += [pl.BlockSpec((tq, tk), lambda b, s, pt: (b, s)),
                     pl.BlockSpec((tq, NEW_TILE), lambda b, s, pt: (b, n_pages * PAGE_SIZE // NEW_TILE))]
        args += [m, m]
    else:
        in_specs.append(pl.BlockSpec((tq, m.shape[1]), lambda b, s, pt: (b, 0)))
        args.append(m)
    rows = GROUP * tq
    return pl.pallas_call(
        functools.partial(_paged_attn_kernel, tq, pp, n_steps, mode),
        grid_spec=pltpu.PrefetchScalarGridSpec(
            num_scalar_prefetch=1, grid=(bsz, n_steps), in_specs=in_specs,
            out_specs=pl.BlockSpec((tq, dq), lambda b, s, pt: (b, 0)),
            scratch_shapes=[pltpu.VMEM((N_KV, rows, HEAD_DIM), BF16), pltpu.VMEM((N_KV, rows, HEAD_DIM), F32),
                            pltpu.VMEM((N_KV, rows, LANES), F32), pltpu.VMEM((N_KV, rows, LANES), F32),
                            pltpu.VMEM((N_KV, tk, HEAD_DIM), BF16), pltpu.VMEM((N_KV, tk, HEAD_DIM), BF16)]),
        out_shape=jax.ShapeDtypeStruct((bsz * tq, dq), F32),
        compiler_params=_params("parallel", "arbitrary"), name="paged_attn_" + mode,
    )(page_table, *args)


def _head_norm_rot(z, g, tabs, n_heads, scale, refs_f32, refs_bf16):
    c, s1, s2 = tabs
    for hd in range(n_heads):
        sl = slice(hd * HEAD_DIM, (hd + 1) * HEAD_DIM)
        r = _rot(_rms(z[:, sl]) * g, c, s1, s2, ROPE_DIM // 2)
        for ref in refs_f32:
            ref[:, sl] = r
        for ref in refs_bf16:
            ref[:, sl] = (r * scale).astype(BF16)


def _dsa_epilogue(h, w_ref, tabs, cs, o):
    tq_ = [t[...] for t in tabs[:3]]
    ti_ = [t[...] for t in tabs[3:]]
    gq, gk, gik = [c[...] for c in cs]
    q_ref, k_ref, kb_ref, v_ref, vb_ref, qi_ref, ki_ref, wi_ref = o
    nq, nkv = N_HEADS * HEAD_DIM, N_KV * HEAD_DIM
    ni = IDX_HEADS * IDX_DIM
    _head_norm_rot(_dot(h, w_ref[:, 0:nq]), gq, tq_, N_HEADS, ATTN_SCALE * LOG2E, [], [q_ref])
    _head_norm_rot(_dot(h, w_ref[:, nq:nq + nkv]), gk, tq_, N_KV, 1.0, [k_ref], [kb_ref])
    zv = _dot(h, w_ref[:, nq + nkv:nq + 2 * nkv])
    v_ref[...] = zv
    vb_ref[...] = zv.astype(BF16)
    c0 = nq + 2 * nkv
    zi = _dot(h, w_ref[:, c0:c0 + ni])
    for ch in range(ni // LANES):
        r = _rot(zi[:, ch * LANES:(ch + 1) * LANES], *ti_, IDX_ROPE // 2).astype(BF16)
        for u in range(LANES // IDX_DIM):
            qi_ref[ch * (LANES // IDX_DIM) + u] = r[:, u * IDX_DIM:(u + 1) * IDX_DIM]
    zl = _dot(h, w_ref[:, c0 + ni:c0 + ni + LANES])
    lane = lax.broadcasted_iota(I32, zl.shape, 1)
    ms = jnp.sum(jnp.where(lane < IDX_DIM, zl * zl, 0.0), axis=-1, keepdims=True) / IDX_DIM
    r = _rot(zl * lax.rsqrt(ms + EPS) * gik, *ti_, IDX_ROPE // 2)
    ki_ref[...] = r[:, :IDX_DIM]
    wi_ref[...] = zl[:, IDX_DIM:IDX_DIM + IDX_HEADS] * (IDX_HEADS ** -0.5 * IDX_DIM ** -0.5)


def _dsa_select_kernel(tq, nb, tk, lk, q_pos0, top, qi_ref, wi_ref, kit_ref, m_ref, key_ref, wib_ref):
    rows = nb * tq
    p0 = q_pos0 + pl.program_id(1) * tq
    nk = jnp.minimum((p0 + tq - 1) // tk + 1, lk // tk)
    wi = wi_ref[...]
    for h in range(IDX_HEADS):
        wib_ref[h] = jnp.broadcast_to(wi[:, h:h + 1], (rows, tk))
    rowpos1 = p0 + lax.broadcasted_iota(I32, (tq, tk), 0)
    col1 = lax.broadcasted_iota(I32, (tq, tk), 1)
    rowpos = jnp.concatenate([rowpos1] * nb, axis=0)
    col = lax.broadcasted_iota(I32, (rows, tk), 1)

    for i in range(nb):
        def score_pair(kp, carry, i=i):
            for u in range(2):
                k0 = pl.multiple_of((2 * kp + u) * tk, tk)
                kt_ = kit_ref[i, :, pl.ds(k0, tk)]
                acc = jnp.zeros((tq, tk), F32)
                if IDX_HEADS * tq <= 512:
                    s_all = _dot(qi_ref[:, i * tq:(i + 1) * tq, :].reshape(IDX_HEADS * tq, IDX_DIM), kt_)
                for h in range(IDX_HEADS):
                    if IDX_HEADS * tq <= 512:
                        s = s_all[h * tq:(h + 1) * tq, :]
                    else:
                        s = _dot(qi_ref[h, i * tq:(i + 1) * tq, :], kt_)
                    acc = acc + jnp.maximum(s, 0.0) * wib_ref[h, i * tq:(i + 1) * tq, :]
                acc = jnp.where(k0 + col1 <= rowpos1, acc, -jnp.inf)
                key_ref[i * tq:(i + 1) * tq, pl.ds(k0, tk)] = _sort_key(acc)
            return carry

        lax.fori_loop(0, (nk + 1) // 2, score_pair, 0)

    def count_ge(cand):
        cb = jnp.broadcast_to(cand, (rows, LANES))

        def body(kt, acc):
            for u in range(2 * tk // LANES):
                c0 = pl.multiple_of(kt * (2 * tk) + u * LANES, LANES)
                acc = acc + jnp.where(key_ref[:, pl.ds(c0, LANES)] >= cb, 1.0, 0.0)
            return acc

        acc = lax.fori_loop(0, (nk + 1) // 2, body, jnp.zeros((rows, LANES), F32))
        return jnp.sum(acc, axis=-1, keepdims=True)

    thr = jnp.broadcast_to(_kth_largest(count_ge, float(top), (rows, 1)), (rows, tk))

    def bias_tile(kt, carry):
        k0 = pl.multiple_of(kt * tk, tk)
        sel = (k0 + col <= rowpos) & (key_ref[:, pl.ds(k0, tk)] >= thr)
        m_ref[:, pl.ds(k0, tk)] = jnp.where(sel, 0.0, -jnp.inf).astype(BF16)
        return carry

    lax.fori_loop(0, nk, bias_tile, 0)

    def rest_tile(kt, carry):
        m_ref[:, pl.ds(pl.multiple_of(kt * tk, tk), tk)] = jnp.full((rows, tk), -jnp.inf, BF16)
        return carry

    lax.fori_loop(nk, lk // tk, rest_tile, 0)


SELECT_ROWS = 128


def _dsa_select(qi, wi, kit, *, bsz, tq, q_pos0, top):
    lk = kit.shape[2]
    tk = 256
    assert lk % (2 * tk) == 0 and top <= tk
    nq_rows = wi.shape[0]
    nq = nq_rows // bsz // tq
    nb = SELECT_ROWS // tq if (nq == 1 and SELECT_ROWS % tq == 0 and bsz % (SELECT_ROWS // tq) == 0) else 1
    rows = nb * tq
    return pl.pallas_call(
        functools.partial(_dsa_select_kernel, tq, nb, tk, lk, q_pos0, top),
        grid=(bsz // nb, nq),
        in_specs=[pl.BlockSpec((IDX_HEADS, rows, IDX_DIM), lambda b, j: (0, b * nq + j, 0)),
                  pl.BlockSpec((rows, IDX_HEADS), lambda b, j: (b * nq + j, 0)),
                  pl.BlockSpec((nb, IDX_DIM, lk), lambda b, j: (b, 0, 0))],
        out_specs=pl.BlockSpec((rows, lk), lambda b, j: (b * nq + j, 0)),
        out_shape=jax.ShapeDtypeStruct((nq_rows, lk), BF16),
        scratch_shapes=[pltpu.VMEM((rows, lk), I32), pltpu.VMEM((IDX_HEADS, rows, tk), F32)],
        compiler_params=_params("parallel", "arbitrary"), name="dsa_select",
    )(qi, wi, kit)


def _nsa_epilogue(h, w_ref, tabs, cs, o):
    tb = [t[...] for t in tabs]
    gq, gsk, gwk = [c[...] for c in cs]
    (qc_ref, q_ref, ck_ref, cv_ref, sk_ref, skb_ref, sv_ref, svb_ref,
     wk_ref, wkb_ref, wv_ref, wvb_ref, gate_ref) = o
    nq, nkv = N_HEADS * HEAD_DIM, N_KV * HEAD_DIM
    zq = _dot(h, w_ref[:, 0:nq])
    for hd in range(N_HEADS):
        sl = slice(hd * HEAD_DIM, (hd + 1) * HEAD_DIM)
        qn = _rms(zq[:, sl]) * gq
        qc_ref[:, sl] = (qn * ATTN_SCALE).astype(BF16)
        q_ref[:, sl] = (_rot(qn, *tb, ROPE_DIM // 2) * (ATTN_SCALE * LOG2E)).astype(BF16)
    seg = lambda i: _dot(h, w_ref[:, nq + i * nkv:nq + (i + 1) * nkv])
    ck_ref[...] = seg(0)
    cv_ref[...] = seg(1)
    _head_norm_rot(seg(2), gsk, tb, N_KV, 1.0, [sk_ref], [skb_ref])
    zsv = seg(3)
    sv_ref[...] = zsv
    svb_ref[...] = zsv.astype(BF16)
    _head_norm_rot(seg(4), gwk, tb, N_KV, 1.0, [wk_ref], [wkb_ref])
    zwv = seg(5)
    wv_ref[...] = zwv
    wvb_ref[...] = zwv.astype(BF16)
    zg = _dot(h, w_ref[:, nq + 6 * nkv:nq + 6 * nkv + LANES])
    gate_ref[...] = jax.nn.sigmoid(zg[:, :N_HEADS * 3])


def _nsa_compress_kernel(ck_ref, cv_ref, pwk_ref, pwv_ref, phik_ref, phiv_ref, g_ref, kc_ref, vc_ref):
    rows, wd = ck_ref.shape
    nb = rows // CMP_BLK
    pk = jnp.sum(ck_ref[...].reshape(nb, CMP_BLK, wd) * pwk_ref[...][None], axis=1)
    pv = jnp.sum(cv_ref[...].reshape(nb, CMP_BLK, wd) * pwv_ref[...][None], axis=1)
    for hd in range(N_KV):
        sl = slice(hd * HEAD_DIM, (hd + 1) * HEAD_DIM)
        kc_ref[:, sl] = (_rms(_dot(pk[:, sl].astype(BF16), phik_ref[hd])) * g_ref[...]).astype(BF16)
        vc_ref[:, sl] = _dot(pv[:, sl].astype(BF16), phiv_ref[hd]).astype(BF16)


def _nsa_compress(ck, cv, pwk, pwv, phik, phiv, gck):
    rows, wd = ck.shape
    step = 2048 if rows % 2048 == 0 else rows
    nb = step // CMP_BLK
    bc = lambda p: jnp.repeat(p, HEAD_DIM, axis=1)
    return pl.pallas_call(
        _nsa_compress_kernel, grid=(rows // step,),
        in_specs=[pl.BlockSpec((step, wd), lambda i: (i, 0)), pl.BlockSpec((step, wd), lambda i: (i, 0)),
                  _const_spec((CMP_BLK, wd)), _const_spec((CMP_BLK, wd)), _const_spec(phik.shape),
                  _const_spec(phiv.shape), _const_spec((1, HEAD_DIM))],
        out_specs=[pl.BlockSpec((nb, wd), lambda i: (i, 0)), pl.BlockSpec((nb, wd), lambda i: (i, 0))],
        out_shape=[jax.ShapeDtypeStruct((rows // CMP_BLK, wd), BF16)] * 2,
        compiler_params=_params("parallel"), name="nsa_compress",
    )(ck, cv, bc(pwk), bc(pwv), phik, phiv, gck.reshape(1, HEAD_DIM))


def _paged_compress_kernel(pp, pt_ref, *rest):
    ckp, cvp = rest[:pp], rest[pp:2 * pp]
    pwk_ref, pwv_ref, phik_ref, phiv_ref, g_ref, kc_ref, vc_ref, pk_scr, pv_scr = rest[2 * pp:]
    nb = PAGE_SIZE // CMP_BLK
    for hd in range(N_KV):
        sl = slice(hd * HEAD_DIM, (hd + 1) * HEAD_DIM)
        rows = pl.ds(hd, PAGE_SIZE, stride=N_KV)
        for i in range(pp):
            pool = lambda ref, pw: jnp.sum(ref[rows, :].reshape(nb, CMP_BLK, HEAD_DIM) * pw[:, sl][None], axis=1)
            pk_scr[hd, i * nb:(i + 1) * nb, :] = pool(ckp[i], pwk_ref)
            pv_scr[hd, i * nb:(i + 1) * nb, :] = pool(cvp[i], pwv_ref)
        kc_ref[:, sl] = (_rms(_dot(pk_scr[hd].astype(BF16), phik_ref[hd])) * g_ref[...]).astype(BF16)
        vc_ref[:, sl] = _dot(pv_scr[hd].astype(BF16), phiv_ref[hd]).astype(BF16)


def _paged_compress(ck_pool, cv_pool, page0, page_table, pwk, pwv, phik, phiv, gck):
    bsz, n_pages = page_table.shape
    pp = PAGES_PER_STEP
    assert n_pages % pp == 0
    n_steps = n_pages // pp
    nb = pp * PAGE_SIZE // CMP_BLK
    wd = N_KV * HEAD_DIM
    prow = PAGE_SIZE * N_KV
    bc = lambda p: jnp.repeat(p, HEAD_DIM, axis=1)
    page = lambda i: pl.BlockSpec((prow, HEAD_DIM), lambda b, s, pt: (page0 + pt[b, s * pp + i], 0))
    const = lambda shape: pl.BlockSpec(shape, lambda b, s, pt: (0,) * len(shape))
    out = pl.BlockSpec((nb, wd), lambda b, s, pt: (b * n_steps + s, 0))
    return pl.pallas_call(
        functools.partial(_paged_compress_kernel, pp),
        grid_spec=pltpu.PrefetchScalarGridSpec(
            num_scalar_prefetch=1, grid=(bsz, n_steps),
            in_specs=[page(i) for i in range(pp)] * 2
            + [const((CMP_BLK, wd)), const((CMP_BLK, wd)), const(phik.shape), const(phiv.shape),
               const((1, HEAD_DIM))],
            out_specs=[out, out],
            scratch_shapes=[pltpu.VMEM((N_KV, nb, HEAD_DIM), F32), pltpu.VMEM((N_KV, nb, HEAD_DIM), F32)]),
        out_shape=[jax.ShapeDtypeStruct((bsz * n_steps * nb, wd), BF16)] * 2,
        compiler_params=_params("parallel", "parallel"), name="paged_compress",
    )(page_table, *([ck_pool] * pp), *([cv_pool] * pp), bc(pwk), bc(pwv), phik, phiv,
      gck.reshape(1, HEAD_DIM))


def _nsa_cmp_kernel(tq, q_pos0, n_sel, qc_ref, kc_ref, vc_ref, oc_ref, bm_ref):
    p0 = q_pos0 + pl.program_id(1) * tq
    nc = kc_ref.shape[0]
    rowpos = p0 + lax.broadcasted_iota(I32, (tq, nc), 0)
    blk = lax.broadcasted_iota(I32, (tq, nc), 1)
    valid = jnp.concatenate([blk * CMP_BLK + (CMP_BLK - 1) <= rowpos] * GROUP, axis=0)
    cur = lax.shift_right_logical(rowpos, 6)
    forced = (blk == 0) | (blk == cur) | (blk == cur - 1)
    keys = []
    for g in range(N_KV):
        sl = slice(g * HEAD_DIM, (g + 1) * HEAD_DIM)
        qs = jnp.concatenate([qc_ref[:, (g * GROUP + h) * HEAD_DIM:(g * GROUP + h + 1) * HEAD_DIM]
                              for h in range(GROUP)], axis=0)
        lc = jnp.where(valid, _dot_nt(qs, kc_ref[:, sl]), NEG)
        m = jnp.max(lc, axis=-1, keepdims=True)
        m = jnp.where(m > 0.5 * NEG, m, 0.0)
        p = jnp.where(valid, jnp.exp(lc - m), 0.0)
        pc = p / jnp.maximum(jnp.sum(p, axis=-1, keepdims=True), 1e-30)
        oc = _dot(pc.astype(BF16), vc_ref[:, sl])
        for h in range(GROUP):
            c0 = (g * GROUP + h) * HEAD_DIM
            oc_ref[:, c0:c0 + HEAD_DIM] = oc[h * tq:(h + 1) * tq, :]
        imp = pc[0:tq]
        for h in range(1, GROUP):
            imp = imp + pc[h * tq:(h + 1) * tq]
        imp = jnp.where(forced, BIG, imp)
        imp = jnp.where(blk > cur, -jnp.inf, imp)
        keys.append(_sort_key(imp))
    key = jnp.concatenate(keys, axis=0)
    count_ge = lambda cand: jnp.sum(jnp.where(key >= cand, 1.0, 0.0), axis=-1, keepdims=True)
    sel = jnp.where(key >= _kth_largest(count_ge, float(n_sel), (N_KV * tq, 1)), 1.0, 0.0).astype(BF16)
    for g in range(N_KV):
        bm_ref[:, g * LANES:(g + 1) * LANES] = sel[g * tq:(g + 1) * tq, :]


def _nsa_cmp(qc, kc, vc, *, bsz, tq, q_pos0, n_sel):
    nq_rows, dq = qc.shape
    nc = kc.shape[1]
    assert nc == LANES
    nq = nq_rows // bsz // tq
    return pl.pallas_call(
        functools.partial(_nsa_cmp_kernel, tq, q_pos0, n_sel), grid=(bsz, nq),
        in_specs=[pl.BlockSpec((tq, dq), lambda b, j: (b * nq + j, 0)),
                  pl.BlockSpec((None, nc, kc.shape[2]), lambda b, j: (b, 0, 0)),
                  pl.BlockSpec((None, nc, vc.shape[2]), lambda b, j: (b, 0, 0))],
        out_specs=[pl.BlockSpec((tq, dq), lambda b, j: (b * nq + j, 0)),
                   pl.BlockSpec((tq, N_KV * LANES), lambda b, j: (b * nq + j, 0))],
        out_shape=[jax.ShapeDtypeStruct((nq_rows, dq), F32),
                   jax.ShapeDtypeStruct((nq_rows, N_KV * LANES), BF16)],
        compiler_params=_params("parallel", "parallel"), name="nsa_cmp",
    )(qc, kc, vc)


def _nsa_combine(gate, oc, os_, ow):
    parts = []
    for hd in range(N_HEADS):
        sl = slice(hd * HEAD_DIM, (hd + 1) * HEAD_DIM)
        parts.append(gate[:, 3 * hd:3 * hd + 1] * oc[:, sl] + gate[:, 3 * hd + 1:3 * hd + 2] * os_[:, sl]
                     + gate[:, 3 * hd + 2:3 * hd + 3] * ow[:, sl])
    return jnp.concatenate(parts, axis=1)


SAMPLE_TQ = 16
PROMPT_TQ = 128
PROMPT_ATTN_TQ = 256
PROMPT_CMP_TQ = 512


def _wide_tile(t, want, base):
    return want if t % want == 0 else base
KEY_ALIGN = 512


def _pad_rows(a, bsz, t, tp):
    return jnp.pad(a.reshape(bsz, t, -1), ((0, 0), (0, tp - t), (0, 0))).reshape(bsz * tp, -1)


def _unpad_rows(a, bsz, t, tp):
    return a.reshape(bsz, tp, -1)[:, :t].reshape(bsz * t, -1)


def _cat_keys(old, new, bsz):
    new = new.reshape(bsz, -1, new.shape[-1])
    n = old.shape[1] + new.shape[1]
    return jnp.pad(jnp.concatenate([old.astype(new.dtype), new], axis=1),
                   ((0, 0), (0, -n % KEY_ALIGN), (0, 0)))


def _tile_tables(tabs, bsz, t):
    return tabs if t % ROW_TILE == 0 else [jnp.tile(a, (bsz, 1)) for a in tabs]


def _mixer_c(x, bsz, pos, past, g, wcat, gq, gk, gik, wo):
    n, d = x.shape
    t = n // bsz
    tabs = _tile_tables(_rope_tables_partial(pos, HEAD_DIM, ROPE_DIM, ROPE_THETA)
                        + _rope_tables_partial(pos, IDX_DIM, IDX_ROPE, ROPE_THETA), bsz, t)
    nkv = N_KV * HEAD_DIM
    q, k, kb, v, vb, qi, ki, wi = _proj(
        x, g, wcat, tabs, [gq, gk, gik], _dsa_epilogue,
        [(_rows(N_HEADS * HEAD_DIM), BF16), (_rows(nkv), F32), (_rows(nkv), BF16), (_rows(nkv), F32),
         (_rows(nkv), BF16),
         (lambda n_, tm: ((IDX_HEADS, n_, IDX_DIM), (IDX_HEADS, tm, IDX_DIM), lambda i: (0, i, 0)), BF16),
         (_rows(IDX_DIM), F32), (_rows(IDX_HEADS), F32)], tabs[0].shape[0], "proj_dsa")
    if past is None:
        tq, q_pos0, n_keys = PROMPT_TQ, 0, t
        kit = ki.reshape(bsz, t, IDX_DIM).swapaxes(1, 2).astype(BF16)
        kall, vall = kb.reshape(bsz, t, nkv), vb.reshape(bsz, t, nkv)
    else:
        tq, q_pos0 = SAMPLE_TQ, past["kidx"].shape[1]
        n_keys = q_pos0 + t
        kit = _cat_keys(past["kidx"], ki, bsz).swapaxes(1, 2).astype(BF16)
        q, wi = _pad_rows(q, bsz, t, tq), _pad_rows(wi, bsz, t, tq)
        qi = jnp.pad(qi.reshape(IDX_HEADS, bsz, t, IDX_DIM), ((0, 0), (0, 0), (0, tq - t), (0, 0))
                     ).reshape(IDX_HEADS, bsz * tq, IDX_DIM)
    bias = _dsa_select(qi, wi, kit, bsz=bsz, tq=tq, q_pos0=q_pos0, top=min(TOPK_MAX, n_keys // 4))
    if past is None:
        o = _attention(q, kall, vall, bias, bsz=bsz, tq=_wide_tile(t, PROMPT_ATTN_TQ, tq), q_pos0=q_pos0,
                       k_pos0=0, mode="mask")
    else:
        new = lambda a: _pad_rows(a, bsz, t, NEW_TILE).astype(BF16)
        o = _paged_attention(q, past["k_pool"], past["v_pool"], past["page0"], past["pt"], new(k), new(v), bias,
                             bsz=bsz, tq=tq, mode="mask")
        o = _unpad_rows(o, bsz, t, tq)
    return _outproj(x, wo, [o], lambda a: a), k, v, ki


def _mixer_d(x, bsz, pos, past, g, wcat, gq, gsk, gwk, cmpw, wo):
    n, d = x.shape
    t = n // bsz
    tabs = _tile_tables(_rope_tables_partial(pos, HEAD_DIM, ROPE_DIM, ROPE_THETA), bsz, t)
    nkv = N_KV * HEAD_DIM
    kvo = [(_rows(nkv), F32), (_rows(nkv), BF16)]
    (qc, q, ck, cv, sk, skb, sv, svb, wk, wkb, wv, wvb, gate) = _proj(
        x, g, wcat, tabs, [gq, gsk, gwk], _nsa_epilogue,
        [(_rows(N_HEADS * HEAD_DIM), BF16), (_rows(N_HEADS * HEAD_DIM), BF16), (_rows(nkv), F32),
         (_rows(nkv), F32)] + kvo * 4 + [(_rows(N_HEADS * 3), F32)], tabs[0].shape[0], "proj_nsa")
    if past is None:
        tq, q_pos0, n_keys, win_pos0 = PROMPT_TQ, 0, t, 0
        kc, vc = _nsa_compress(ck, cv, *cmpw)
        three = lambda a: a.reshape(bsz, t, nkv)
        skall, svall, wkall, wvall = three(skb), three(svb), three(wkb), three(wvb)
    else:
        tq, q_pos0 = SAMPLE_TQ, past["pt"].shape[1] * PAGE_SIZE
        n_keys = q_pos0 + t
        win_pos0 = q_pos0 - past["wk"].shape[1]
        assert q_pos0 % CMP_BLK == 0 and t < CMP_BLK
        kc, vc = _paged_compress(past["ck_pool"], past["cv_pool"], past["page0"], past["pt"], *cmpw)
        wkall = _cat_keys(past["wk"], wk, bsz).astype(BF16)
        wvall = _cat_keys(past["wv"], wv, bsz).astype(BF16)
        qc, q = _pad_rows(qc, bsz, t, tq), _pad_rows(q, bsz, t, tq)
    n_cmp = kc.shape[0] // bsz
    assert n_cmp <= LANES
    lane_pad = lambda a: jnp.pad(a.reshape(bsz, n_cmp, nkv), ((0, 0), (0, LANES - n_cmp), (0, 0)))
    kc, vc = lane_pad(kc), lane_pad(vc)
    n_slc = -(-n_keys // SLC_BLK)
    n_lane = min(n_slc, LANES)
    assert n_slc == n_lane or (n_slc == n_lane + 1 and q_pos0 // SLC_BLK == n_lane)
    n_sel = min(N_SLC, n_slc) - (n_slc - n_lane)
    tq_cmp = _wide_tile(t, PROMPT_CMP_TQ, tq) if past is None else tq
    oc, bm = _nsa_cmp(qc, kc, vc, bsz=bsz, tq=tq_cmp, q_pos0=q_pos0, n_sel=n_sel)
    if past is None:
        os_ = _attention(q, skall, svall, bm, bsz=bsz, tq=_wide_tile(t, PROMPT_ATTN_TQ, tq), q_pos0=q_pos0,
                         k_pos0=0, mode="blocksel", n_cmp=n_lane)
    else:
        new = lambda a: _pad_rows(a, bsz, t, NEW_TILE).astype(BF16)
        os_ = _paged_attention(q, past["sk_pool"], past["sv_pool"], past["page0"], past["pt"], new(sk), new(sv),
                               bm, bsz=bsz, tq=tq, mode="blocksel")
    ow = _attention(q, wkall, wvall, None, bsz=bsz, tq=tq if past else _wide_tile(t, PROMPT_ATTN_TQ, tq),
                    q_pos0=q_pos0, k_pos0=win_pos0, mode="window")
    if past is not None:
        oc, os_, ow = (_unpad_rows(a, bsz, t, tq) for a in (oc, os_, ow))
    return _outproj(x, wo, [gate, oc, os_, ow], _nsa_combine), (ck, cv, sk, sv, wk, wv)


def _gather_pages(cache, page_table):
    rows = cache[page_table]
    return rows.reshape(page_table.shape[0], page_table.shape[1] * PAGE_SIZE, -1)


def kernel(x_prompt, x_sample, state_a_ret, state_b_h, state_b_conv, cache_c_k, cache_c_v, cache_c_kidx,
           cache_d_ck, cache_d_cv, cache_d_sk, cache_d_sv, state_d_wk, state_d_wv, page_table,
           ffn1_norm, ffn1_wg, ffn1_wu, ffn1_wd, mix_norm, ffn2_norm, ffn2_wg, ffn2_wu, ffn2_wd,
           a_wq, a_wk, a_wv, a_wg, a_wo,
           b_wy, b_by, b_wx, b_bx, b_conv_w, b_conv_b, b_wa, b_ba, b_wi, b_bi, b_lam, b_wo,
           c_wq, c_gq, c_wk, c_gk, c_wv, c_wo, c_wiq, c_wik, c_gik, c_wiw,
           d_wq, d_gq, d_wck, d_wcv, d_pwk, d_pwv, d_phik, d_phiv, d_gck, d_wsk, d_wsv, d_gsk,
           d_wwk, d_wwv, d_gwk, d_wgate, d_wo):
    bp, tp, d = x_prompt.shape
    bs, ts, _ = x_sample.shape
    depth = ffn1_norm.shape[0]
    past_len = page_table.shape[1] * PAGE_SIZE
    pos_p = jnp.arange(tp, dtype=I32)
    pos_s = past_len + jnp.arange(ts, dtype=I32)
    b16 = lambda a: a.astype(BF16)
    row = lambda a: a.reshape(1, -1)
    nkv = N_KV * HEAD_DIM
    pool = lambda c: c.reshape(-1, HEAD_DIM)
    ffn1_w = [_to_bf16(w) for w in (ffn1_wg, ffn1_wu, ffn1_wd)]
    ffn2_w = [_to_bf16(w) for w in (ffn2_wg, ffn2_wu, ffn2_wd)]

    xp = x_prompt.reshape(bp * tp, d)
    xs = x_sample.reshape(bs * ts, d)
    outs = [[] for _ in range(24)]
    for i in range(depth):
        m, j = i % 4, i // 4
        xp, xs = _ffn(xp, ffn1_norm[i], *ffn1_w, i), _ffn(xs, ffn1_norm[i], *ffn1_w, i)
        g = mix_norm[i]
        if m == 0:
            dk = a_wq.shape[2] // RET_HEADS
            dv = a_wv.shape[2] // RET_HEADS
            wcat = b16(jnp.concatenate([a_wq[j], a_wk[j], a_wv[j], a_wg[j]], axis=1))
            wo = b16(a_wo[j])
            xp, s_p = _mixer_a(xp, g, pos_p, bp, jnp.zeros((bp, RET_HEADS, dk, dv), F32), wcat, wo, dk, dv)
            xs, s_s = _mixer_a(xs, g, pos_s, bs, state_a_ret[j], wcat, wo, dk, dv)
            new = [s_p, s_s]
            base = 0
        elif m == 1:
            common = (g, b16(jnp.concatenate([b_wy[j], b_wx[j]], axis=1)), jnp.concatenate([b_by[j], b_bx[j]]),
                      b_conv_w[j], b_conv_b[j], b16(b_wa[j]), b_ba[j].reshape(-1), b16(b_wi[j]),
                      b_bi[j].reshape(-1), jax.nn.softplus(-b_lam[j]), b16(b_wo[j]))
            xp, h_p, c_p = _lru_prompt(xp, bp, *common)
            xs, h_s, c_s = _lru_sample(xs, bs, state_b_h[j], state_b_conv[j], *common)
            new = [h_p, h_s, c_p, c_s]
            base = 2
        elif m == 2:
            zpad = jnp.zeros((d, LANES - IDX_DIM - IDX_HEADS), F32)
            wcat = b16(jnp.concatenate([c_wq[j], c_wk[j], c_wv[j], c_wiq[j], c_wik[j], c_wiw[j], zpad], axis=1))
            gik = jnp.concatenate([c_gik[j], jnp.zeros((LANES - IDX_DIM,), F32)])
            cw = (g, wcat, row(c_gq[j]), row(c_gk[j]), row(gik), b16(c_wo[j]))
            past = dict(kidx=_gather_pages(cache_c_kidx[j], page_table), k_pool=pool(cache_c_k),
                        v_pool=pool(cache_c_v), page0=j * cache_c_k.shape[1], pt=page_table)
            xp, k_p, v_p, i_p = _mixer_c(xp, bp, pos_p, None, *cw)
            xs, k_s, v_s, i_s = _mixer_c(xs, bs, pos_s, past, *cw)
            kv = lambda a, b_, t_: a.reshape(b_, t_, N_KV, HEAD_DIM)
            new = [kv(k_p, bp, tp), kv(k_s, bs, ts), kv(v_p, bp, tp), kv(v_s, bs, ts),
                   i_p.reshape(bp, tp, IDX_DIM), i_s.reshape(bs, ts, IDX_DIM)]
            base = 6
        else:
            zpad = jnp.zeros((d, LANES - N_HEADS * 3), F32)
            wcat = b16(jnp.concatenate([d_wq[j], d_wck[j], d_wcv[j], d_wsk[j], d_wsv[j], d_wwk[j], d_wwv[j],
                                        d_wgate[j], zpad], axis=1))
            cmpw = (d_pwk[j], d_pwv[j], b16(d_phik[j]), b16(d_phiv[j]), d_gck[j])
            dw = (g, wcat, row(d_gq[j]), row(d_gsk[j]), row(d_gwk[j]), cmpw, b16(d_wo[j]))
            wbuf = state_d_wk.shape[2]
            past = dict(ck_pool=pool(cache_d_ck), cv_pool=pool(cache_d_cv), sk_pool=pool(cache_d_sk),
                        sv_pool=pool(cache_d_sv), page0=j * cache_d_ck.shape[1], pt=page_table,
                        wk=state_d_wk[j].reshape(bs, wbuf, nkv), wv=state_d_wv[j].reshape(bs, wbuf, nkv))
            xp, rows_p = _mixer_d(xp, bp, pos_p, None, *dw)
            xs, rows_s = _mixer_d(xs, bs, pos_s, past, *dw)
            kv = lambda a, b_: a.reshape(b_, -1, N_KV, HEAD_DIM)
            new = []
            for a_p, a_s in zip(rows_p[:4], rows_s[:4]):
                new += [kv(a_p, bp), kv(a_s, bs)]
            wb_p = min(WINDOW, tp)
            for a_p, a_s, st in ((rows_p[4], rows_s[4], state_d_wk[j]), (rows_p[5], rows_s[5], state_d_wv[j])):
                new += [kv(a_p, bp)[:, tp - wb_p:], jnp.concatenate([st, kv(a_s, bs)], axis=1)[:, ts:]]
            base = 12
        for off, a in enumerate(new):
            outs[base + off].append(a)
        xp, xs = _ffn(xp, ffn2_norm[i], *ffn2_w, i), _ffn(xs, ffn2_norm[i], *ffn2_w, i)
    return (xp.reshape(bp, tp, d), xs.reshape(bs, ts, d)) + tuple(jnp.stack(o) for o in outs)
```

```python
import functools
import math

import jax
import jax.numpy as jnp
from jax import lax
from jax.experimental import pallas as pl
from jax.experimental.pallas import tpu as pltpu

F32 = jnp.float32
BF16 = jnp.bfloat16
I32 = jnp.int32

EPS = 1e-6
PAGE_SIZE = 128

RET_HEADS = 4
RET_CHUNK = 128
RET_THETA = 10000.0

LRU_BLOCKS = 4
CONV_W = 4
LRU_C = 8.0

N_HEADS = 8
HEAD_DIM = 128
N_KV = 2
GROUP = N_HEADS // N_KV
ROPE_DIM = HEAD_DIM // 4
ROPE_THETA = 500000.0

IDX_HEADS = 16
IDX_DIM = 64
IDX_ROPE = IDX_DIM // 4
TOPK_MAX = 256

CMP_BLK = 64
SLC_BLK = 64
N_SLC = 16
WINDOW = 512
BIG = 1e4

LANES = 128
VMEM_LIMIT = 56 * 1024 * 1024
ROW_TILE = 512
NEG = -1e30
ATTN_SCALE = HEAD_DIM ** -0.5
LOG2E = math.log2(math.e)
INT_MIN = -2147483648


def _params(*sem):
    return pltpu.CompilerParams(dimension_semantics=sem, vmem_limit_bytes=VMEM_LIMIT)


def _const_spec(shape):
    nd = len(shape)
    return pl.BlockSpec(shape, lambda *_: (0,) * nd, pipeline_mode=pl.Buffered(1))


def _rms(x):
    return x * lax.rsqrt(jnp.mean(x * x, axis=-1, keepdims=True) + EPS)


def _dot(a, b):
    return jnp.dot(a, b, preferred_element_type=F32)


def _dot_nt(a, b):
    return lax.dot_general(a, b, (((1,), (1,)), ((), ())), preferred_element_type=F32)


def _row_tile(n):
    return ROW_TILE if n % ROW_TILE == 0 else n


def _ffn_kernel(x_ref, g_ref, wg_ref, wu_ref, wd_ref, o_ref):
    x = x_ref[...]
    h = (_rms(x) * g_ref[...]).astype(BF16)
    gt = _dot(h, wg_ref[...])
    ut = _dot(h, wu_ref[...])
    a = (gt * jax.nn.sigmoid(gt) * ut).astype(BF16)
    o_ref[...] = x + 0.5 * _dot(a, wd_ref[...])


def _ffn(x, g, wg, wu, wd, layer):
    n, d = x.shape
    f = wg.shape[2]
    tm = _row_tile(n)
    wspec = lambda a, b_: pl.BlockSpec((None, a, b_), lambda i: (layer, 0, 0), pipeline_mode=pl.Buffered(1))
    return pl.pallas_call(
        _ffn_kernel,
        grid=(n // tm,),
        in_specs=[pl.BlockSpec((tm, d), lambda i: (i, 0)), _const_spec((1, d)),
                  wspec(d, f), wspec(d, f), wspec(f, d)],
        out_specs=pl.BlockSpec((tm, d), lambda i: (i, 0)),
        out_shape=jax.ShapeDtypeStruct((n, d), F32),
        compiler_params=_params("parallel"), name="ffn",
    )(x, g.reshape(1, d), wg, wu, wd)


def _cast_kernel(x_ref, o_ref):
    o_ref[...] = x_ref[...].astype(o_ref.dtype)


def _to_bf16(w):
    c = w.shape[-1]
    r = w.size // c
    tm = _row_tile(r)
    out = pl.pallas_call(
        _cast_kernel, grid=(r // tm,),
        in_specs=[pl.BlockSpec((tm, c), lambda i: (i, 0))],
        out_specs=pl.BlockSpec((tm, c), lambda i: (i, 0)),
        out_shape=jax.ShapeDtypeStruct((r, c), BF16),
        compiler_params=_params("parallel"), name="to_bf16",
    )(w.reshape(r, c))
    return out.reshape(w.shape)


def _proj(x, g, w, tables, consts, epilogue, outs, n_tab_rows, name):
    n, d = x.shape
    tm = _row_tile(n)
    assert n_tab_rows % tm == 0
    nt = n_tab_rows // tm
    nw = w.shape[1]

    def kern(x_ref, g_ref, w_ref, *rest):
        tabs = rest[:len(tables)]
        cs = rest[len(tables):len(tables) + len(consts)]
        o = rest[len(tables) + len(consts):]
        h = (_rms(x_ref[...]) * g_ref[...]).astype(BF16)
        epilogue(h, w_ref, tabs, cs, o)

    in_specs = [pl.BlockSpec((tm, d), lambda i: (i, 0)), _const_spec((1, d)), _const_spec((d, nw))]
    in_specs += [pl.BlockSpec((tm, t.shape[1]), lambda i: (i % nt, 0)) for t in tables]
    in_specs += [_const_spec(c.shape) for c in consts]
    out_shape, out_specs = [], []
    for fn, dt in outs:
        ashape, bshape, imap = fn(n, tm)
        out_shape.append(jax.ShapeDtypeStruct(ashape, dt))
        out_specs.append(pl.BlockSpec(bshape, imap))
    return pl.pallas_call(
        kern, grid=(n // tm,), in_specs=in_specs, out_specs=out_specs, out_shape=out_shape,
        compiler_params=_params("parallel"), name=name,
    )(x, g.reshape(1, d), w, *tables, *consts)


def _rows(c):
    return lambda n, tm: ((n, c), (tm, c), lambda i: (i, 0))


def _outproj(x, w, ins, combine):
    n, d = x.shape
    tm = _row_tile(n)

    def kern(x_ref, w_ref, *rest):
        o_ref = rest[-1]
        a = combine(*[r[...] for r in rest[:-1]]).astype(BF16)
        o_ref[...] = x_ref[...] + _dot(a, w_ref[...])

    return pl.pallas_call(
        kern, grid=(n // tm,),
        in_specs=[pl.BlockSpec((tm, d), lambda i: (i, 0)), _const_spec(w.shape)]
        + [pl.BlockSpec((tm, a.shape[1]), lambda i: (i, 0)) for a in ins],
        out_specs=pl.BlockSpec((tm, d), lambda i: (i, 0)),
        out_shape=jax.ShapeDtypeStruct((n, d), F32),
        compiler_params=_params("parallel"), name="outproj",
    )(x, w, *ins)


def _ret_epilogue(dk, dv, h, w_ref, tabs, cs, o):
    cos, sin = tabs[0][...], tabs[1][...]
    q_ref, k_ref, v_ref, g_ref = o
    half = dk // 2
    nqk = RET_HEADS * dk
    for seg, ref, scale in ((0, q_ref, 1.0), (1, k_ref, dk ** -0.5)):
        z = _dot(h, w_ref[:, seg * nqk:(seg + 1) * nqk])
        for hd in range(RET_HEADS):
            x1 = z[:, hd * dk:hd * dk + half]
            x2 = z[:, hd * dk + half:(hd + 1) * dk]
            ref[:, hd * dk:hd * dk + half] = (x1 * cos - x2 * sin) * scale
            ref[:, hd * dk + half:(hd + 1) * dk] = (x2 * cos + x1 * sin) * scale
    nv = RET_HEADS * dv
    v_ref[...] = _dot(h, w_ref[:, 2 * nqk:2 * nqk + nv]).astype(BF16)
    zg = _dot(h, w_ref[:, 2 * nqk + nv:2 * nqk + 2 * nv])
    g_ref[...] = zg * jax.nn.sigmoid(zg)


def _ret_chunk_kernel(q_ref, k_ref, v_ref, s0_ref, dm_ref, xi_ref, zt_ref, gc_ref, o_ref, s_ref):
    c = pl.program_id(2)

    @pl.when(c == 0)
    def _():
        s_ref[...] = s0_ref[...]

    nh, dk, dv = s_ref.shape
    for hd in range(nh):
        q = q_ref[:, hd * dk:(hd + 1) * dk]
        k = k_ref[:, hd * dk:(hd + 1) * dk]
        v = v_ref[:, hd * dv:(hd + 1) * dv]
        s = s_ref[hd]
        att = (_dot_nt(q.astype(BF16), k.astype(BF16)) * dm_ref[hd]).astype(BF16)
        o = _dot(att, v) + _dot((q * xi_ref[hd]).astype(BF16), s.astype(BF16))
        kz = (k * zt_ref[hd]).astype(BF16)
        s_ref[hd] = s * gc_ref[hd] + lax.dot_general(kz, v, (((0,), (0,)), ((), ())),
                                                     preferred_element_type=F32)
        o_ref[:, hd * dv:(hd + 1) * dv] = _rms(o)


def _retention(q, k, v, s0, n_valid):
    b, hh, dk, dv = s0.shape
    t = q.shape[0] // b
    c = RET_CHUNK
    nc = t // c
    cc = c if n_valid % c == 0 else n_valid
    lg = jnp.log1p(-jnp.exp2(-5.0 - jnp.arange(hh, dtype=F32)))
    idx = jnp.arange(c, dtype=F32)
    diff = idx[:, None] - idx[None, :]
    dmask = jnp.where(diff >= 0, jnp.exp(lg[:, None, None] * jnp.maximum(diff, 0.0)), 0.0)
    xi = jnp.exp(lg[:, None] * (idx + 1.0))[:, :, None]
    zeta = jnp.exp(lg[:, None] * (cc - 1.0 - idx))[:, :, None]
    g_c = jnp.exp(lg * cc)[:, None, None]
    nh = 2 if hh % 2 == 0 else 1
    return pl.pallas_call(
        _ret_chunk_kernel,
        grid=(b, hh // nh, nc),
        in_specs=[pl.BlockSpec((c, nh * dk), lambda i, j, l: (i * nc + l, j)),
                  pl.BlockSpec((c, nh * dk), lambda i, j, l: (i * nc + l, j)),
                  pl.BlockSpec((c, nh * dv), lambda i, j, l: (i * nc + l, j)),
                  pl.BlockSpec((None, nh, dk, dv), lambda i, j, l: (i, j, 0, 0)),
                  pl.BlockSpec((nh, c, c), lambda i, j, l: (j, 0, 0)),
                  pl.BlockSpec((nh, c, 1), lambda i, j, l: (j, 0, 0)),
                  pl.BlockSpec((nh, c, 1), lambda i, j, l: (j, 0, 0)),
                  pl.BlockSpec((nh, 1, 1), lambda i, j, l: (j, 0, 0))],
        out_specs=[pl.BlockSpec((c, nh * dv), lambda i, j, l: (i * nc + l, j)),
                   pl.BlockSpec((None, nh, dk, dv), lambda i, j, l: (i, j, 0, 0))],
        out_shape=[jax.ShapeDtypeStruct((b * t, hh * dv), F32),
                   jax.ShapeDtypeStruct((b, hh, dk, dv), F32)],
        compiler_params=_params("parallel", "parallel", "arbitrary"), name="retention",
    )(q, k, v, s0, dmask, xi, zeta, g_c)


def _rope_tables_full(pos, dim, theta):
    half = dim // 2
    inv = theta ** (-jnp.arange(half, dtype=F32) / half)
    ang = pos.astype(F32)[:, None] * inv[None, :]
    return jnp.cos(ang), jnp.sin(ang)


def _mixer_a(x, g, pos, bsz, s0, wcat, wo, dk, dv):
    n, d = x.shape
    t = n // bsz
    cos, sin = _rope_tables_full(pos, dk, RET_THETA)
    if t % RET_CHUNK:
        cos = jnp.tile(cos, (bsz, 1))
        sin = jnp.tile(sin, (bsz, 1))
    q, k, v, sg = _proj(
        x, g, wcat, [cos, sin], [], functools.partial(_ret_epilogue, dk, dv),
        [(_rows(RET_HEADS * dk), F32), (_rows(RET_HEADS * dk), F32),
         (_rows(RET_HEADS * dv), BF16), (_rows(RET_HEADS * dv), F32)], cos.shape[0], "proj_ret")
    if t % RET_CHUNK:
        pad = lambda a: jnp.pad(a.reshape(bsz, t, -1), ((0, 0), (0, RET_CHUNK - t), (0, 0))
                                ).reshape(bsz * RET_CHUNK, -1)
        o, s_new = _retention(pad(q), pad(k), pad(v), s0, t)
        o = o.reshape(bsz, RET_CHUNK, -1)[:, :t].reshape(n, -1)
    else:
        o, s_new = _retention(q, k, v, s0, t)
    return _outproj(x, wo, [sg, o], lambda a, b: a * b), s_new


def _gelu_tanh(x):
    return 0.5 * x * (1.0 + jnp.tanh(math.sqrt(2.0 / math.pi) * (x + 0.044715 * (x * x * x))))


def _lru_gates(xb, wa_ref, ba, wi_ref, bi, sp):
    bw = wa_ref.shape[1]
    xbb = xb.astype(BF16)
    r = jnp.concatenate([_dot(xbb[:, n * bw:(n + 1) * bw], wa_ref[n]) for n in range(LRU_BLOCKS)], axis=1)
    i = jnp.concatenate([_dot(xbb[:, n * bw:(n + 1) * bw], wi_ref[n]) for n in range(LRU_BLOCKS)], axis=1)
    r = jax.nn.sigmoid(r + ba)
    i = jax.nn.sigmoid(i + bi)
    log_a = -LRU_C * r * sp
    a = jnp.exp(log_a)
    th = jnp.tanh(log_a)
    bt = jnp.sqrt(-2.0 * th / (1.0 - th)) * (i * xb)
    return a, bt


def _lru_prompt_kernel(x_ref, g_ref, wyx_ref, byx_ref, cw_ref, cb_ref, wa_ref, ba_ref, wi_ref, bi_ref,
                       sp_ref, wo_ref, o_ref, hl_ref, cn_ref, xpad_ref, a_ref, b_ref, hs_ref, h_ref):
    j = pl.program_id(1)
    tm, dr = a_ref.shape
    pad = xpad_ref.shape[0] - tm

    @pl.when(j == 0)
    def _():
        h_ref[...] = jnp.zeros_like(h_ref)
        xpad_ref[0:pad, :] = jnp.zeros((pad, dr), F32)

    x = x_ref[...]
    h = (_rms(x) * g_ref[...]).astype(BF16)
    z = _dot(h, wyx_ref[...]) + byx_ref[...]
    gate = _gelu_tanh(z[:, :dr])
    xpad_ref[pad:, :] = z[:, dr:]
    xb = cb_ref[...] + z[:, dr:] * cw_ref[CONV_W - 1:CONV_W, :]
    for s in range(1, CONV_W):
        xb = xb + xpad_ref[pad - s:pad - s + tm, :] * cw_ref[CONV_W - 1 - s:CONV_W - s, :]
    xpad_ref[0:pad, :] = xpad_ref[tm:tm + pad, :]
    a, bt = _lru_gates(xb, wa_ref, ba_ref[...], wi_ref, bi_ref[...], sp_ref[...])
    a_ref[...] = a
    b_ref[...] = bt

    def step(t, hprev):
        hn = a_ref[pl.ds(t, 1), :] * hprev + b_ref[pl.ds(t, 1), :]
        hs_ref[pl.ds(t, 1), :] = hn
        return hn

    hlast = lax.fori_loop(0, tm, step, h_ref[...])
    h_ref[...] = hlast
    o_ref[...] = x + _dot((gate * hs_ref[...]).astype(BF16), wo_ref[...])
    hl_ref[...] = hlast
    cn_ref[...] = xpad_ref[pad - (CONV_W - 1):pad, :]


def _lru_prompt(x, bsz, g, wyx, byx, cw, cb, wa, ba, wi, bi, sp, wo):
    n, d = x.shape
    t = n // bsz
    dr = wo.shape[0]
    tm = 256 if t % 256 == 0 else t
    nt = t // tm
    pad = 8
    vec = lambda a: a.reshape(1, -1)
    out, hl, cn = pl.pallas_call(
        _lru_prompt_kernel,
        grid=(bsz, nt),
        in_specs=[pl.BlockSpec((tm, d), lambda i, j: (i * nt + j, 0)), _const_spec((1, d)),
                  _const_spec(wyx.shape), _const_spec((1, 2 * dr)), _const_spec(cw.shape),
                  _const_spec((1, dr)), _const_spec(wa.shape), _const_spec((1, dr)),
                  _const_spec(wi.shape), _const_spec((1, dr)), _const_spec((1, dr)),
                  _const_spec(wo.shape)],
        out_specs=[pl.BlockSpec((tm, d), lambda i, j: (i * nt + j, 0)),
                   pl.BlockSpec((None, 1, dr), lambda i, j: (i, 0, 0)),
                   pl.BlockSpec((None, CONV_W - 1, dr), lambda i, j: (i, 0, 0))],
        out_shape=[jax.ShapeDtypeStruct((n, d), F32), jax.ShapeDtypeStruct((bsz, 1, dr), F32),
                   jax.ShapeDtypeStruct((bsz, CONV_W - 1, dr), F32)],
        scratch_shapes=[pltpu.VMEM((tm + pad, dr), F32), pltpu.VMEM((tm, dr), F32),
                        pltpu.VMEM((tm, dr), F32), pltpu.VMEM((tm, dr), F32), pltpu.VMEM((1, dr), F32)],
        compiler_params=_params("parallel", "arbitrary"), name="lru_prompt",
    )(x, vec(g), wyx, vec(byx), cw, vec(cb), wa, vec(ba), wi, vec(bi), vec(sp), wo)
    return out, hl.reshape(bsz, dr), cn


def _lru_sample_kernel(bsz, x_ref, g_ref, buf_ref, h0_ref, wyx_ref, byx_ref, cw_ref, cb_ref, wa_ref, ba_ref,
                       wi_ref, bi_ref, sp_ref, wo_ref, o_ref, hl_ref, cn_ref, xpad_ref, hs_ref):
    n, dr = hs_ref.shape
    t = n // bsz
    nb = (CONV_W - 1) * bsz
    x = x_ref[...]
    h = (_rms(x) * g_ref[...]).astype(BF16)
    z = _dot(h, wyx_ref[...]) + byx_ref[...]
    gate = _gelu_tanh(z[:, :dr])
    xpad_ref[0:nb, :] = buf_ref[...]
    xpad_ref[nb:, :] = z[:, dr:]
    xb = cb_ref[...] + xpad_ref[0:n, :] * cw_ref[0:1, :]
    for s in range(1, CONV_W):
        xb = xb + xpad_ref[s * bsz:s * bsz + n, :] * cw_ref[s:s + 1, :]
    a, bt = _lru_gates(xb, wa_ref, ba_ref[...], wi_ref, bi_ref[...], sp_ref[...])
    hcur = h0_ref[...]
    for s in range(t):
        hcur = a[s * bsz:(s + 1) * bsz, :] * hcur + bt[s * bsz:(s + 1) * bsz, :]
        hs_ref[s * bsz:(s + 1) * bsz, :] = hcur
    o_ref[...] = x + _dot((gate * hs_ref[...]).astype(BF16), wo_ref[...])
    hl_ref[...] = hcur
    cn_ref[...] = xpad_ref[n:n + nb, :]


def _lru_sample(x, bsz, h0, conv0, g, wyx, byx, cw, cb, wa, ba, wi, bi, sp, wo):
    n, d = x.shape
    t = n // bsz
    dr = wo.shape[0]
    nb = (CONV_W - 1) * bsz
    tmaj = lambda a, tt: a.reshape(bsz, tt, -1).swapaxes(0, 1).reshape(tt * bsz, -1)
    bmaj = lambda a, tt: a.reshape(tt, bsz, -1).swapaxes(0, 1)
    vec = lambda a: a.reshape(1, -1)
    args = (tmaj(x, t), vec(g), tmaj(conv0, CONV_W - 1), h0, wyx, vec(byx), cw, vec(cb), wa, vec(ba),
            wi, vec(bi), vec(sp), wo)
    out, hl, cn = pl.pallas_call(
        functools.partial(_lru_sample_kernel, bsz),
        in_specs=[pl.BlockSpec(a.shape, lambda nd=a.ndim: (0,) * nd) for a in args],
        out_specs=[pl.BlockSpec((n, d), lambda: (0, 0)), pl.BlockSpec((bsz, dr), lambda: (0, 0)),
                   pl.BlockSpec((nb, dr), lambda: (0, 0))],
        out_shape=[jax.ShapeDtypeStruct((n, d), F32), jax.ShapeDtypeStruct((bsz, dr), F32),
                   jax.ShapeDtypeStruct((nb, dr), F32)],
        scratch_shapes=[pltpu.VMEM((n + nb, dr), F32), pltpu.VMEM((n, dr), F32)],
        compiler_params=pltpu.CompilerParams(vmem_limit_bytes=VMEM_LIMIT), name="lru_sample",
    )(*args)
    return bmaj(out, t).reshape(n, d), hl, bmaj(cn, CONV_W - 1)


def _rope_tables_partial(pos, period, rot_dim, theta):
    half = rot_dim // 2
    inv = theta ** (-jnp.arange(half, dtype=F32) / half)
    ang = pos.astype(F32)[:, None] * inv[None, :]
    cos, sin = jnp.cos(ang), jnp.sin(ang)
    t = pos.shape[0]
    zh = jnp.zeros((t, half), F32)
    rest = period - rot_dim
    c = jnp.concatenate([cos, cos, jnp.ones((t, rest), F32)], axis=1)
    s1 = jnp.concatenate([-sin, zh, jnp.zeros((t, rest), F32)], axis=1)
    s2 = jnp.concatenate([zh, sin, jnp.zeros((t, rest), F32)], axis=1)
    rep = LANES // period
    return [jnp.tile(a, (1, rep)) for a in (c, s1, s2)]


def _rot(x, c, s1, s2, half):
    n = x.shape[-1]
    return x * c + pltpu.roll(x, n - half, 1) * s1 + pltpu.roll(x, half, 1) * s2


def _sort_key(x):
    bits = pltpu.bitcast(x, I32)
    return bits ^ ((bits >> 31) & 0x7FFFFFFF)


RADIX_GROUP = 4


def _kth_largest(count_ge, k, shape):
    c0 = count_ge(jnp.zeros(shape, I32))
    t = jnp.where(c0 >= k, 0, INT_MIN).astype(I32)
    cnt = jnp.where(c0 >= k, c0, 2.0 * k)

    def decide(t, cnt, bit):
        cand = t | lax.shift_left(jnp.int32(1), bit)
        c = count_ge(cand)
        return jnp.where(c >= k, cand, t), jnp.where(c >= k, c, cnt)

    n_head = 31 % RADIX_GROUP
    for bit in range(30, 30 - n_head, -1):
        t, cnt = decide(t, cnt, bit)
    n_groups = (31 - n_head) // RADIX_GROUP

    def cond(state):
        i, _, _, pending = state
        return (i < n_groups) & (pending > 0.5)

    def body(state):
        i, t, cnt, _ = state
        for u in range(RADIX_GROUP):
            t, cnt = decide(t, cnt, 30 - n_head - (i * RADIX_GROUP + u))
        return i + 1, t, cnt, jnp.max(jnp.where(cnt == k, 0.0, 1.0))

    return lax.while_loop(cond, body, (jnp.int32(0), t, cnt, jnp.float32(1.0)))[1]


def _attn_kernel(tq, tk, lk, q_pos0, k_pos0, mode, n_cmp, q_ref, k_ref, v_ref, *rest):
    if mode == "window":
        o_ref, qs_ref, acc_ref, m_scr, l_scr = rest
        m_ref = None
    else:
        m_ref, o_ref, qs_ref, acc_ref, m_scr, l_scr = rest
    p0 = q_pos0 + pl.program_id(1) * tq
    for g in range(N_KV):
        for h in range(GROUP):
            c0 = (g * GROUP + h) * HEAD_DIM
            qs_ref[g, h * tq:(h + 1) * tq, :] = q_ref[:, c0:c0 + HEAD_DIM]
    m_scr[...] = jnp.full(m_scr.shape, NEG, F32)
    l_scr[...] = jnp.zeros(l_scr.shape, F32)
    acc_ref[...] = jnp.zeros(acc_ref.shape, F32)
    hi = jnp.minimum((p0 + tq - 1 - k_pos0) // tk + 1, lk // tk)
    lo = jnp.maximum(p0 - (WINDOW - 1) - k_pos0, 0) // tk if mode == "window" else 0
    rowpos = p0 + lax.broadcasted_iota(I32, (tq, tk), 0)

    def body(kt, carry):
        k0 = pl.multiple_of(kt * tk, tk)
        if mode == "mask":
            bias = m_ref[:, pl.ds(k0, tk)].astype(F32)
        else:
            colpos = k_pos0 + k0 + lax.broadcasted_iota(I32, (tq, tk), 1)
            valid = colpos <= rowpos
            if mode == "window":
                valid = valid & (rowpos - colpos < WINDOW) & (colpos >= 0)
                bias = jnp.where(valid, 0.0, -jnp.inf)
        for g in range(N_KV):
            if mode == "blocksel":
                blk = lax.shift_right_logical(k_pos0 + k0 + lax.broadcasted_iota(I32, (LANES, tk), 1), 6)
                e = jnp.where(blk == lax.broadcasted_iota(I32, (LANES, tk), 0), 1.0, 0.0).astype(BF16)
                sel = _dot(m_ref[:, g * LANES:(g + 1) * LANES], e) > 0.5
                bias = jnp.where(valid & (sel | (colpos >= n_cmp * SLC_BLK)), 0.0, -jnp.inf)
            kt_ = k_ref[pl.ds(k0, tk), g * HEAD_DIM:(g + 1) * HEAD_DIM]
            vt_ = v_ref[pl.ds(k0, tk), g * HEAD_DIM:(g + 1) * HEAD_DIM]
            s = _dot_nt(qs_ref[g], kt_)
            s = (s.reshape(GROUP, tq, tk) + bias[None]).reshape(GROUP * tq, tk)
            _softmax_update(g, s, vt_, acc_ref, m_scr, l_scr)
        return carry

    lax.fori_loop(lo, hi, body, 0)
    for g in range(N_KV):
        o = acc_ref[g] / jnp.maximum(l_scr[g], 1e-30)
        for h in range(GROUP):
            c0 = (g * GROUP + h) * HEAD_DIM
            o_ref[:, c0:c0 + HEAD_DIM] = o[h * tq:(h + 1) * tq, :]


def _attention(q, k, v, m, *, bsz, tq, q_pos0, k_pos0, mode, n_cmp=0):
    nq_rows, dq = q.shape
    lk = k.shape[1]
    tk = 512
    assert lk % tk == 0 and (nq_rows // bsz) % tq == 0
    nq = nq_rows // bsz // tq
    kern = functools.partial(_attn_kernel, tq, tk, lk, q_pos0, k_pos0, mode, n_cmp)
    in_specs = [pl.BlockSpec((tq, dq), lambda b, j: (b * nq + j, 0)),
                pl.BlockSpec((None, lk, k.shape[2]), lambda b, j: (b, 0, 0)),
                pl.BlockSpec((None, lk, v.shape[2]), lambda b, j: (b, 0, 0))]
    args = [q, k, v]
    if mode != "window":
        in_specs.append(pl.BlockSpec((tq, m.shape[1]), lambda b, j: (b * nq + j, 0)))
        args.append(m)
    return pl.pallas_call(
        kern, grid=(bsz, nq), in_specs=in_specs,
        out_specs=pl.BlockSpec((tq, dq), lambda b, j: (b * nq + j, 0)),
        out_shape=jax.ShapeDtypeStruct((nq_rows, dq), F32),
        scratch_shapes=[pltpu.VMEM((N_KV, GROUP * tq, HEAD_DIM), BF16),
                        pltpu.VMEM((N_KV, GROUP * tq, HEAD_DIM), F32),
                        pltpu.VMEM((N_KV, GROUP * tq, LANES), F32),
                        pltpu.VMEM((N_KV, GROUP * tq, LANES), F32)],
        compiler_params=_params("parallel", "arbitrary"), name="attn_" + mode,
    )(*args)


def _softmax_update(g, s, vt, acc_ref, m_scr, l_scr):
    m_old = m_scr[g]
    m_new = jnp.maximum(m_old, jnp.max(s, axis=-1, keepdims=True))
    alpha = jnp.exp2(m_old - m_new)
    p = jnp.exp2(s - jnp.concatenate([m_new] * (s.shape[1] // LANES), axis=1))
    l_scr[g] = alpha * l_scr[g] + jnp.sum(p, axis=-1, keepdims=True)
    acc_ref[g] = alpha * acc_ref[g] + _dot(p.astype(BF16), vt)
    m_scr[g] = m_new


def _paged_attn_kernel(tq, pp, n_steps, mode, pt_ref, q_ref, *rest):
    kp, vp = rest[:pp], rest[pp:2 * pp]
    kn_ref, vn_ref = rest[2 * pp:2 * pp + 2]
    if mode == "mask":
        mb_ref, mt_ref, o_ref, qs_ref, acc_ref, m_scr, l_scr, kbuf, vbuf = rest[2 * pp + 2:]
    else:
        mb_ref, o_ref, qs_ref, acc_ref, m_scr, l_scr, kbuf, vbuf = rest[2 * pp + 2:]
    step = pl.program_id(1)
    tk = pp * PAGE_SIZE

    @pl.when(step == 0)
    def _():
        for g in range(N_KV):
            for h in range(GROUP):
                c0 = (g * GROUP + h) * HEAD_DIM
                qs_ref[g, h * tq:(h + 1) * tq, :] = q_ref[:, c0:c0 + HEAD_DIM]
        m_scr[...] = jnp.full(m_scr.shape, NEG, F32)
        l_scr[...] = jnp.zeros(l_scr.shape, F32)
        acc_ref[...] = jnp.zeros(acc_ref.shape, F32)

    for i in range(pp):
        for g in range(N_KV):
            rows = pl.ds(g, PAGE_SIZE, stride=N_KV)
            kbuf[g, i * PAGE_SIZE:(i + 1) * PAGE_SIZE, :] = kp[i][rows, :].astype(BF16)
            vbuf[g, i * PAGE_SIZE:(i + 1) * PAGE_SIZE, :] = vp[i][rows, :].astype(BF16)

    def biased(g, s, bias):
        n = s.shape[1]
        return (s.reshape(GROUP, tq, n) + bias[None]).reshape(GROUP * tq, n)

    for g in range(N_KV):
        if mode == "mask":
            bias = mb_ref[...].astype(F32)
        else:
            blk = lax.shift_right_logical(step * tk + lax.broadcasted_iota(I32, (LANES, tk), 1), 6)
            e = jnp.where(blk == lax.broadcasted_iota(I32, (LANES, tk), 0), 1.0, 0.0).astype(BF16)
            bias = jnp.where(_dot(mb_ref[:, g * LANES:(g + 1) * LANES], e) > 0.5, 0.0, -jnp.inf)
        _softmax_update(g, biased(g, _dot_nt(qs_ref[g], kbuf[g]), bias), vbuf[g], acc_ref, m_scr, l_scr)

    @pl.when(step == n_steps - 1)
    def _():
        tn = kn_ref.shape[0]
        causal = lax.broadcasted_iota(I32, (tq, tn), 1) <= lax.broadcasted_iota(I32, (tq, tn), 0)
        for g in range(N_KV):
            sl = slice(g * HEAD_DIM, (g + 1) * HEAD_DIM)
            bias = mt_ref[...].astype(F32) if mode == "mask" else jnp.where(causal, 0.0, -jnp.inf)
            _softmax_update(g, biased(g, _dot_nt(qs_ref[g], kn_ref[:, sl]), bias), vn_ref[:, sl],
                            acc_ref, m_scr, l_scr)
            o = acc_ref[g] / jnp.maximum(l_scr[g], 1e-30)
            for h in range(GROUP):
                c0 = (g * GROUP + h) * HEAD_DIM
                o_ref[:, c0:c0 + HEAD_DIM] = o[h * tq:(h + 1) * tq, :]


PAGES_PER_STEP = 16
NEW_TILE = LANES


def _paged_attention(q, k_pool, v_pool, page0, page_table, kn, vn, m, *, bsz, tq, mode):
    dq = q.shape[1]
    n_pages = page_table.shape[1]
    pp = PAGES_PER_STEP
    assert n_pages % pp == 0
    n_steps = n_pages // pp
    tk = pp * PAGE_SIZE
    prow = PAGE_SIZE * N_KV
    page = lambda i: pl.BlockSpec((prow, HEAD_DIM), lambda b, s, pt: (page0 + pt[b, s * pp + i], 0))
    in_specs = [pl.BlockSpec((tq, dq), lambda b, s, pt: (b, 0))]
    in_specs += [page(i) for i in range(pp)] * 2
    in_specs += [pl.BlockSpec((NEW_TILE, kn.shape[1]), lambda b, s, pt: (b, 0))] * 2
    args = [q] + [k_pool] * pp + [v_pool] * pp + [kn, vn]
    if mode == "mask":
        in_specs += [pl.BlockSpec((tq, tk), lambda b, s, pt: (b, s)),
                     pl.BlockSpec((tq, NEW_TILE), lambda b, s, pt: (b, n_pages * PAGE_SIZE // NEW_TILE))]
        args += [m, m]
    else:
        in_specs.append(pl.BlockSpec((tq, m.shape[1]), lambda b, s, pt: (b, 0)))
        args.append(m)
    rows = GROUP * tq
    return pl.pallas_call(
        functools.partial(_paged_attn_kernel, tq, pp, n_steps, mode),
        grid_spec=pltpu.PrefetchScalarGridSpec(
            num_scalar_prefetch=1, grid=(bsz, n_steps), in_specs=in_specs,
            out_specs=pl.BlockSpec((tq, dq), lambda b, s, pt: (b, 0)),
            scratch_shapes=[pltpu.VMEM((N_KV, rows, HEAD_DIM), BF16), pltpu.VMEM((N_KV, rows, HEAD_DIM), F32),
                            pltpu.VMEM((N_KV, rows, LANES), F32), pltpu.VMEM((N_KV, rows, LANES), F32),
                            pltpu.VMEM((N_KV, tk, HEAD_DIM), BF16), pltpu.VMEM((N_KV, tk, HEAD_DIM), BF16)]),
        out_shape=jax.ShapeDtypeStruct((bsz * tq, dq), F32),
        compiler_params=_params("parallel", "arbitrary"), name="paged_attn_" + mode,
    )(page_table, *args)


def _heads(c):
    return lambda n, tm: ((n, c // HEAD_DIM, HEAD_DIM), (tm, c // HEAD_DIM, HEAD_DIM), lambda i: (i, 0, 0))


def _store_heads(ref, z):
    for hd in range(ref.shape[1]):
        ref[:, hd, :] = z[:, hd * HEAD_DIM:(hd + 1) * HEAD_DIM]


def _head_norm_rot(z, g, tabs, n_heads, scale, refs_f32, refs_bf16):
    c, s1, s2 = tabs
    for hd in range(n_heads):
        sl = slice(hd * HEAD_DIM, (hd + 1) * HEAD_DIM)
        r = _rot(_rms(z[:, sl]) * g, c, s1, s2, ROPE_DIM // 2)
        for ref in refs_f32:
            ref[:, hd, :] = r
        for ref in refs_bf16:
            ref[:, sl] = (r * scale).astype(BF16)


def _dsa_epilogue(h, w_ref, tabs, cs, o):
    tq_ = [t[...] for t in tabs[:3]]
    ti_ = [t[...] for t in tabs[3:]]
    gq, gk, gik = [c[...] for c in cs]
    q_ref, k_ref, kb_ref, v_ref, vb_ref, qi_ref, ki_ref, wi_ref = o
    nq, nkv = N_HEADS * HEAD_DIM, N_KV * HEAD_DIM
    ni = IDX_HEADS * IDX_DIM
    _head_norm_rot(_dot(h, w_ref[:, 0:nq]), gq, tq_, N_HEADS, ATTN_SCALE * LOG2E, [], [q_ref])
    _head_norm_rot(_dot(h, w_ref[:, nq:nq + nkv]), gk, tq_, N_KV, 1.0, [k_ref], [kb_ref])
    zv = _dot(h, w_ref[:, nq + nkv:nq + 2 * nkv])
    _store_heads(v_ref, zv)
    vb_ref[...] = zv.astype(BF16)
    c0 = nq + 2 * nkv
    zi = _dot(h, w_ref[:, c0:c0 + ni])
    for ch in range(ni // LANES):
        r = _rot(zi[:, ch * LANES:(ch + 1) * LANES], *ti_, IDX_ROPE // 2).astype(BF16)
        for u in range(LANES // IDX_DIM):
            qi_ref[ch * (LANES // IDX_DIM) + u] = r[:, u * IDX_DIM:(u + 1) * IDX_DIM]
    zl = _dot(h, w_ref[:, c0 + ni:c0 + ni + LANES])
    lane = lax.broadcasted_iota(I32, zl.shape, 1)
    ms = jnp.sum(jnp.where(lane < IDX_DIM, zl * zl, 0.0), axis=-1, keepdims=True) / IDX_DIM
    r = _rot(zl * lax.rsqrt(ms + EPS) * gik, *ti_, IDX_ROPE // 2)
    ki_ref[...] = r[:, :IDX_DIM]
    wi_ref[...] = zl[:, IDX_DIM:IDX_DIM + IDX_HEADS] * (IDX_HEADS ** -0.5 * IDX_DIM ** -0.5)


def _dsa_select_kernel(tq, nb, tk, lk, q_pos0, top, qi_ref, wi_ref, kit_ref, m_ref, key_ref, wib_ref):
    rows = nb * tq
    p0 = q_pos0 + pl.program_id(1) * tq
    nk = jnp.minimum((p0 + tq - 1) // tk + 1, lk // tk)
    wi = wi_ref[...]
    for h in range(IDX_HEADS):
        wib_ref[h] = jnp.broadcast_to(wi[:, h:h + 1], (rows, tk))
    rowpos1 = p0 + lax.broadcasted_iota(I32, (tq, tk), 0)
    col1 = lax.broadcasted_iota(I32, (tq, tk), 1)
    rowpos = jnp.concatenate([rowpos1] * nb, axis=0)
    col = lax.broadcasted_iota(I32, (rows, tk), 1)

    for i in range(nb):
        def score_pair(kp, carry, i=i):
            for u in range(2):
                k0 = pl.multiple_of((2 * kp + u) * tk, tk)
                kt_ = kit_ref[i, :, pl.ds(k0, tk)]
                acc = jnp.zeros((tq, tk), F32)
                if IDX_HEADS * tq <= 512:
                    s_all = _dot(qi_ref[:, i * tq:(i + 1) * tq, :].reshape(IDX_HEADS * tq, IDX_DIM), kt_)
                for h in range(IDX_HEADS):
                    if IDX_HEADS * tq <= 512:
                        s = s_all[h * tq:(h + 1) * tq, :]
                    else:
                        s = _dot(qi_ref[h, i * tq:(i + 1) * tq, :], kt_)
                    acc = acc + jnp.maximum(s, 0.0) * wib_ref[h, i * tq:(i + 1) * tq, :]
                acc = jnp.where(k0 + col1 <= rowpos1, acc, -jnp.inf)
                key_ref[i * tq:(i + 1) * tq, pl.ds(k0, tk)] = _sort_key(acc)
            return carry

        lax.fori_loop(0, (nk + 1) // 2, score_pair, 0)

    def count_ge(cand):
        cb = jnp.broadcast_to(cand, (rows, LANES))

        def body(kt, acc):
            for u in range(2 * tk // LANES):
                c0 = pl.multiple_of(kt * (2 * tk) + u * LANES, LANES)
                acc = acc + jnp.where(key_ref[:, pl.ds(c0, LANES)] >= cb, 1.0, 0.0)
            return acc

        acc = lax.fori_loop(0, (nk + 1) // 2, body, jnp.zeros((rows, LANES), F32))
        return jnp.sum(acc, axis=-1, keepdims=True)

    thr = jnp.broadcast_to(_kth_largest(count_ge, float(top), (rows, 1)), (rows, tk))

    def bias_tile(kt, carry):
        k0 = pl.multiple_of(kt * tk, tk)
        sel = (k0 + col <= rowpos) & (key_ref[:, pl.ds(k0, tk)] >= thr)
        m_ref[:, pl.ds(k0, tk)] = jnp.where(sel, 0.0, -jnp.inf).astype(BF16)
        return carry

    lax.fori_loop(0, nk, bias_tile, 0)

    def rest_tile(kt, carry):
        m_ref[:, pl.ds(pl.multiple_of(kt * tk, tk), tk)] = jnp.full((rows, tk), -jnp.inf, BF16)
        return carry

    lax.fori_loop(nk, lk // tk, rest_tile, 0)


SELECT_ROWS = 128


def _dsa_select(qi, wi, kit, *, bsz, tq, q_pos0, top):
    lk = kit.shape[2]
    tk = 256
    assert lk % (2 * tk) == 0 and top <= tk
    nq_rows = wi.shape[0]
    nq = nq_rows // bsz // tq
    nb = SELECT_ROWS // tq if (nq == 1 and SELECT_ROWS % tq == 0 and bsz % (SELECT_ROWS // tq) == 0) else 1
    rows = nb * tq
    return pl.pallas_call(
        functools.partial(_dsa_select_kernel, tq, nb, tk, lk, q_pos0, top),
        grid=(bsz // nb, nq),
        in_specs=[pl.BlockSpec((IDX_HEADS, rows, IDX_DIM), lambda b, j: (0, b * nq + j, 0)),
                  pl.BlockSpec((rows, IDX_HEADS), lambda b, j: (b * nq + j, 0)),
                  pl.BlockSpec((nb, IDX_DIM, lk), lambda b, j: (b, 0, 0))],
        out_specs=pl.BlockSpec((rows, lk), lambda b, j: (b * nq + j, 0)),
        out_shape=jax.ShapeDtypeStruct((nq_rows, lk), BF16),
        scratch_shapes=[pltpu.VMEM((rows, lk), I32), pltpu.VMEM((IDX_HEADS, rows, tk), F32)],
        compiler_params=_params("parallel", "arbitrary"), name="dsa_select",
    )(qi, wi, kit)


def _nsa_epilogue(h, w_ref, tabs, cs, o):
    tb = [t[...] for t in tabs]
    gq, gsk, gwk = [c[...] for c in cs]
    (qc_ref, q_ref, ck_ref, cv_ref, sk_ref, skb_ref, sv_ref, svb_ref,
     wk_ref, wkb_ref, wv_ref, wvb_ref, gate_ref) = o
    nq, nkv = N_HEADS * HEAD_DIM, N_KV * HEAD_DIM
    zq = _dot(h, w_ref[:, 0:nq])
    for hd in range(N_HEADS):
        sl = slice(hd * HEAD_DIM, (hd + 1) * HEAD_DIM)
        qn = _rms(zq[:, sl]) * gq
        qc_ref[:, sl] = (qn * ATTN_SCALE).astype(BF16)
        q_ref[:, sl] = (_rot(qn, *tb, ROPE_DIM // 2) * (ATTN_SCALE * LOG2E)).astype(BF16)
    seg = lambda i: _dot(h, w_ref[:, nq + i * nkv:nq + (i + 1) * nkv])
    ck_ref[...] = seg(0)
    cv_ref[...] = seg(1)
    _head_norm_rot(seg(2), gsk, tb, N_KV, 1.0, [sk_ref], [skb_ref])
    zsv = seg(3)
    _store_heads(sv_ref, zsv)
    svb_ref[...] = zsv.astype(BF16)
    _head_norm_rot(seg(4), gwk, tb, N_KV, 1.0, [wk_ref], [wkb_ref])
    zwv = seg(5)
    _store_heads(wv_ref, zwv)
    wvb_ref[...] = zwv.astype(BF16)
    zg = _dot(h, w_ref[:, nq + 6 * nkv:nq + 6 * nkv + LANES])
    gate_ref[...] = jax.nn.sigmoid(zg[:, :N_HEADS * 3])


def _nsa_compress_kernel(ck_ref, cv_ref, pwk_ref, pwv_ref, phik_ref, phiv_ref, g_ref, kc_ref, vc_ref):
    rows, wd = ck_ref.shape
    nb = rows // CMP_BLK
    pk = jnp.sum(ck_ref[...].reshape(nb, CMP_BLK, wd) * pwk_ref[...][None], axis=1)
    pv = jnp.sum(cv_ref[...].reshape(nb, CMP_BLK, wd) * pwv_ref[...][None], axis=1)
    for hd in range(N_KV):
        sl = slice(hd * HEAD_DIM, (hd + 1) * HEAD_DIM)
        kc_ref[:, sl] = (_rms(_dot(pk[:, sl].astype(BF16), phik_ref[hd])) * g_ref[...]).astype(BF16)
        vc_ref[:, sl] = _dot(pv[:, sl].astype(BF16), phiv_ref[hd]).astype(BF16)


def _nsa_compress(ck, cv, pwk, pwv, phik, phiv, gck):
    rows, wd = ck.shape
    step = 2048 if rows % 2048 == 0 else rows
    nb = step // CMP_BLK
    bc = lambda p: jnp.repeat(p, HEAD_DIM, axis=1)
    return pl.pallas_call(
        _nsa_compress_kernel, grid=(rows // step,),
        in_specs=[pl.BlockSpec((step, wd), lambda i: (i, 0)), pl.BlockSpec((step, wd), lambda i: (i, 0)),
                  _const_spec((CMP_BLK, wd)), _const_spec((CMP_BLK, wd)), _const_spec(phik.shape),
                  _const_spec(phiv.shape), _const_spec((1, HEAD_DIM))],
        out_specs=[pl.BlockSpec((nb, wd), lambda i: (i, 0)), pl.BlockSpec((nb, wd), lambda i: (i, 0))],
        out_shape=[jax.ShapeDtypeStruct((rows // CMP_BLK, wd), BF16)] * 2,
        compiler_params=_params("parallel"), name="nsa_compress",
    )(ck, cv, bc(pwk), bc(pwv), phik, phiv, gck.reshape(1, HEAD_DIM))


def _paged_compress_kernel(pp, pt_ref, *rest):
    ckp, cvp = rest[:pp], rest[pp:2 * pp]
    pwk_ref, pwv_ref, phik_ref, phiv_ref, g_ref, kc_ref, vc_ref, pk_scr, pv_scr = rest[2 * pp:]
    nb = PAGE_SIZE // CMP_BLK
    for hd in range(N_KV):
        sl = slice(hd * HEAD_DIM, (hd + 1) * HEAD_DIM)
        rows = pl.ds(hd, PAGE_SIZE, stride=N_KV)
        for i in range(pp):
            pool = lambda ref, pw: jnp.sum(ref[rows, :].reshape(nb, CMP_BLK, HEAD_DIM) * pw[:, sl][None], axis=1)
            pk_scr[hd, i * nb:(i + 1) * nb, :] = pool(ckp[i], pwk_ref)
            pv_scr[hd, i * nb:(i + 1) * nb, :] = pool(cvp[i], pwv_ref)
        kc_ref[:, sl] = (_rms(_dot(pk_scr[hd].astype(BF16), phik_ref[hd])) * g_ref[...]).astype(BF16)
        vc_ref[:, sl] = _dot(pv_scr[hd].astype(BF16), phiv_ref[hd]).astype(BF16)


def _paged_compress(ck_pool, cv_pool, page0, page_table, pwk, pwv, phik, phiv, gck):
    bsz, n_pages = page_table.shape
    pp = PAGES_PER_STEP
    assert n_pages % pp == 0
    n_steps = n_pages // pp
    nb = pp * PAGE_SIZE // CMP_BLK
    wd = N_KV * HEAD_DIM
    prow = PAGE_SIZE * N_KV
    bc = lambda p: jnp.repeat(p, HEAD_DIM, axis=1)
    page = lambda i: pl.BlockSpec((prow, HEAD_DIM), lambda b, s, pt: (page0 + pt[b, s * pp + i], 0))
    const = lambda shape: pl.BlockSpec(shape, lambda b, s, pt: (0,) * len(shape))
    out = pl.BlockSpec((nb, wd), lambda b, s, pt: (b * n_steps + s, 0))
    return pl.pallas_call(
        functools.partial(_paged_compress_kernel, pp),
        grid_spec=pltpu.PrefetchScalarGridSpec(
            num_scalar_prefetch=1, grid=(bsz, n_steps),
            in_specs=[page(i) for i in range(pp)] * 2
            + [const((CMP_BLK, wd)), const((CMP_BLK, wd)), const(phik.shape), const(phiv.shape),
               const((1, HEAD_DIM))],
            out_specs=[out, out],
            scratch_shapes=[pltpu.VMEM((N_KV, nb, HEAD_DIM), F32), pltpu.VMEM((N_KV, nb, HEAD_DIM), F32)]),
        out_shape=[jax.ShapeDtypeStruct((bsz * n_steps * nb, wd), BF16)] * 2,
        compiler_params=_params("parallel", "parallel"), name="paged_compress",
    )(page_table, *([ck_pool] * pp), *([cv_pool] * pp), bc(pwk), bc(pwv), phik, phiv,
      gck.reshape(1, HEAD_DIM))


def _nsa_cmp_kernel(tq, q_pos0, n_sel, qc_ref, kc_ref, vc_ref, oc_ref, bm_ref):
    p0 = q_pos0 + pl.program_id(1) * tq
    nc = kc_ref.shape[0]
    rowpos = p0 + lax.broadcasted_iota(I32, (tq, nc), 0)
    blk = lax.broadcasted_iota(I32, (tq, nc), 1)
    valid = jnp.concatenate([blk * CMP_BLK + (CMP_BLK - 1) <= rowpos] * GROUP, axis=0)
    cur = lax.shift_right_logical(rowpos, 6)
    forced = (blk == 0) | (blk == cur) | (blk == cur - 1)
    keys = []
    for g in range(N_KV):
        sl = slice(g * HEAD_DIM, (g + 1) * HEAD_DIM)
        qs = jnp.concatenate([qc_ref[:, (g * GROUP + h) * HEAD_DIM:(g * GROUP + h + 1) * HEAD_DIM]
                              for h in range(GROUP)], axis=0)
        lc = jnp.where(valid, _dot_nt(qs, kc_ref[:, sl]), NEG)
        m = jnp.max(lc, axis=-1, keepdims=True)
        m = jnp.where(m > 0.5 * NEG, m, 0.0)
        p = jnp.where(valid, jnp.exp(lc - m), 0.0)
        pc = p / jnp.maximum(jnp.sum(p, axis=-1, keepdims=True), 1e-30)
        oc = _dot(pc.astype(BF16), vc_ref[:, sl])
        for h in range(GROUP):
            c0 = (g * GROUP + h) * HEAD_DIM
            oc_ref[:, c0:c0 + HEAD_DIM] = oc[h * tq:(h + 1) * tq, :]
        imp = pc[0:tq]
        for h in range(1, GROUP):
            imp = imp + pc[h * tq:(h + 1) * tq]
        imp = jnp.where(forced, BIG, imp)
        imp = jnp.where(blk > cur, -jnp.inf, imp)
        keys.append(_sort_key(imp))
    key = jnp.concatenate(keys, axis=0)
    count_ge = lambda cand: jnp.sum(jnp.where(key >= cand, 1.0, 0.0), axis=-1, keepdims=True)
    sel = jnp.where(key >= _kth_largest(count_ge, float(n_sel), (N_KV * tq, 1)), 1.0, 0.0).astype(BF16)
    for g in range(N_KV):
        bm_ref[:, g * LANES:(g + 1) * LANES] = sel[g * tq:(g + 1) * tq, :]


def _nsa_cmp(qc, kc, vc, *, bsz, tq, q_pos0, n_sel):
    nq_rows, dq = qc.shape
    nc = kc.shape[1]
    assert nc == LANES
    nq = nq_rows // bsz // tq
    return pl.pallas_call(
        functools.partial(_nsa_cmp_kernel, tq, q_pos0, n_sel), grid=(bsz, nq),
        in_specs=[pl.BlockSpec((tq, dq), lambda b, j: (b * nq + j, 0)),
                  pl.BlockSpec((None, nc, kc.shape[2]), lambda b, j: (b, 0, 0)),
                  pl.BlockSpec((None, nc, vc.shape[2]), lambda b, j: (b, 0, 0))],
        out_specs=[pl.BlockSpec((tq, dq), lambda b, j: (b * nq + j, 0)),
                   pl.BlockSpec((tq, N_KV * LANES), lambda b, j: (b * nq + j, 0))],
        out_shape=[jax.ShapeDtypeStruct((nq_rows, dq), F32),
                   jax.ShapeDtypeStruct((nq_rows, N_KV * LANES), BF16)],
        compiler_params=_params("parallel", "parallel"), name="nsa_cmp",
    )(qc, kc, vc)


def _nsa_combine(gate, oc, os_, ow):
    parts = []
    for hd in range(N_HEADS):
        sl = slice(hd * HEAD_DIM, (hd + 1) * HEAD_DIM)
        parts.append(gate[:, 3 * hd:3 * hd + 1] * oc[:, sl] + gate[:, 3 * hd + 1:3 * hd + 2] * os_[:, sl]
                     + gate[:, 3 * hd + 2:3 * hd + 3] * ow[:, sl])
    return jnp.concatenate(parts, axis=1)


SAMPLE_TQ = 16
PROMPT_TQ = 128
PROMPT_ATTN_TQ = 256
PROMPT_CMP_TQ = 512


def _wide_tile(t, want, base):
    return want if t % want == 0 else base
KEY_ALIGN = 512


def _pad_rows(a, bsz, t, tp):
    return jnp.pad(a.reshape(bsz, t, -1), ((0, 0), (0, tp - t), (0, 0))).reshape(bsz * tp, -1)


def _unpad_rows(a, bsz, t, tp):
    return a.reshape(bsz, tp, -1)[:, :t].reshape(bsz * t, -1)


def _cat_keys(old, new, bsz):
    new = new.reshape(bsz, -1, new.shape[-1])
    n = old.shape[1] + new.shape[1]
    return jnp.pad(jnp.concatenate([old.astype(new.dtype), new], axis=1),
                   ((0, 0), (0, -n % KEY_ALIGN), (0, 0)))


def _tile_tables(tabs, bsz, t):
    return tabs if t % ROW_TILE == 0 else [jnp.tile(a, (bsz, 1)) for a in tabs]


def _mixer_c(x, bsz, pos, past, g, wcat, gq, gk, gik, wo):
    n, d = x.shape
    t = n // bsz
    tabs = _tile_tables(_rope_tables_partial(pos, HEAD_DIM, ROPE_DIM, ROPE_THETA)
                        + _rope_tables_partial(pos, IDX_DIM, IDX_ROPE, ROPE_THETA), bsz, t)
    nkv = N_KV * HEAD_DIM
    q, k, kb, v, vb, qi, ki, wi = _proj(
        x, g, wcat, tabs, [gq, gk, gik], _dsa_epilogue,
        [(_rows(N_HEADS * HEAD_DIM), BF16), (_heads(nkv), F32), (_rows(nkv), BF16), (_heads(nkv), F32),
         (_rows(nkv), BF16),
         (lambda n_, tm: ((IDX_HEADS, n_, IDX_DIM), (IDX_HEADS, tm, IDX_DIM), lambda i: (0, i, 0)), BF16),
         (_rows(IDX_DIM), F32), (_rows(IDX_HEADS), F32)], tabs[0].shape[0], "proj_dsa")
    if past is None:
        tq, q_pos0, n_keys = PROMPT_TQ, 0, t
        kit = ki.reshape(bsz, t, IDX_DIM).swapaxes(1, 2).astype(BF16)
        kall, vall = kb.reshape(bsz, t, nkv), vb.reshape(bsz, t, nkv)
    else:
        tq, q_pos0 = SAMPLE_TQ, past["kidx"].shape[1]
        n_keys = q_pos0 + t
        kit = _cat_keys(past["kidx"], ki, bsz).swapaxes(1, 2).astype(BF16)
        q, wi = _pad_rows(q, bsz, t, tq), _pad_rows(wi, bsz, t, tq)
        qi = jnp.pad(qi.reshape(IDX_HEADS, bsz, t, IDX_DIM), ((0, 0), (0, 0), (0, tq - t), (0, 0))
                     ).reshape(IDX_HEADS, bsz * tq, IDX_DIM)
    bias = _dsa_select(qi, wi, kit, bsz=bsz, tq=tq, q_pos0=q_pos0, top=min(TOPK_MAX, n_keys // 4))
    if past is None:
        o = _attention(q, kall, vall, bias, bsz=bsz, tq=_wide_tile(t, PROMPT_ATTN_TQ, tq), q_pos0=q_pos0,
                       k_pos0=0, mode="mask")
    else:
        new = lambda a: _pad_rows(a.reshape(n, nkv), bsz, t, NEW_TILE).astype(BF16)
        o = _paged_attention(q, past["k_pool"], past["v_pool"], past["page0"], past["pt"], new(k), new(v), bias,
                             bsz=bsz, tq=tq, mode="mask")
        o = _unpad_rows(o, bsz, t, tq)
    return _outproj(x, wo, [o], lambda a: a), k, v, ki


def _mixer_d(x, bsz, pos, past, g, wcat, gq, gsk, gwk, cmpw, wo):
    n, d = x.shape
    t = n // bsz
    tabs = _tile_tables(_rope_tables_partial(pos, HEAD_DIM, ROPE_DIM, ROPE_THETA), bsz, t)
    nkv = N_KV * HEAD_DIM
    kvo = [(_heads(nkv), F32), (_rows(nkv), BF16)]
    (qc, q, ck, cv, sk, skb, sv, svb, wk, wkb, wv, wvb, gate) = _proj(
        x, g, wcat, tabs, [gq, gsk, gwk], _nsa_epilogue,
        [(_rows(N_HEADS * HEAD_DIM), BF16), (_rows(N_HEADS * HEAD_DIM), BF16), (_rows(nkv), F32),
         (_rows(nkv), F32)] + kvo * 4 + [(_rows(N_HEADS * 3), F32)], tabs[0].shape[0], "proj_nsa")
    if past is None:
        tq, q_pos0, n_keys, win_pos0 = PROMPT_TQ, 0, t, 0
        kc, vc = _nsa_compress(ck, cv, *cmpw)
        three = lambda a: a.reshape(bsz, t, nkv)
        skall, svall, wkall, wvall = three(skb), three(svb), three(wkb), three(wvb)
    else:
        tq, q_pos0 = SAMPLE_TQ, past["pt"].shape[1] * PAGE_SIZE
        n_keys = q_pos0 + t
        win_pos0 = q_pos0 - past["wk"].shape[1]
        assert q_pos0 % CMP_BLK == 0 and t < CMP_BLK
        kc, vc = _paged_compress(past["ck_pool"], past["cv_pool"], past["page0"], past["pt"], *cmpw)
        wkall = _cat_keys(past["wk"], wk.reshape(n, nkv), bsz).astype(BF16)
        wvall = _cat_keys(past["wv"], wv.reshape(n, nkv), bsz).astype(BF16)
        qc, q = _pad_rows(qc, bsz, t, tq), _pad_rows(q, bsz, t, tq)
    n_cmp = kc.shape[0] // bsz
    assert n_cmp <= LANES
    lane_pad = lambda a: jnp.pad(a.reshape(bsz, n_cmp, nkv), ((0, 0), (0, LANES - n_cmp), (0, 0)))
    kc, vc = lane_pad(kc), lane_pad(vc)
    n_slc = -(-n_keys // SLC_BLK)
    n_lane = min(n_slc, LANES)
    assert n_slc == n_lane or (n_slc == n_lane + 1 and q_pos0 // SLC_BLK == n_lane)
    n_sel = min(N_SLC, n_slc) - (n_slc - n_lane)
    tq_cmp = _wide_tile(t, PROMPT_CMP_TQ, tq) if past is None else tq
    oc, bm = _nsa_cmp(qc, kc, vc, bsz=bsz, tq=tq_cmp, q_pos0=q_pos0, n_sel=n_sel)
    if past is None:
        os_ = _attention(q, skall, svall, bm, bsz=bsz, tq=_wide_tile(t, PROMPT_ATTN_TQ, tq), q_pos0=q_pos0,
                         k_pos0=0, mode="blocksel", n_cmp=n_lane)
    else:
        new = lambda a: _pad_rows(a.reshape(n, nkv), bsz, t, NEW_TILE).astype(BF16)
        os_ = _paged_attention(q, past["sk_pool"], past["sv_pool"], past["page0"], past["pt"], new(sk), new(sv),
                               bm, bsz=bsz, tq=tq, mode="blocksel")
    ow = _attention(q, wkall, wvall, None, bsz=bsz, tq=tq if past else _wide_tile(t, PROMPT_ATTN_TQ, tq),
                    q_pos0=q_pos0, k_pos0=win_pos0, mode="window")
    if past is not None:
        oc, os_, ow = (_unpad_rows(a, bsz, t, tq) for a in (oc, os_, ow))
    return _outproj(x, wo, [gate, oc, os_, ow], _nsa_combine), (ck, cv, sk, sv, wk, wv)


def _gather_pages(cache, page_table):
    rows = cache[page_table]
    return rows.reshape(page_table.shape[0], page_table.shape[1] * PAGE_SIZE, -1)


def kernel(x_prompt, x_sample, state_a_ret, state_b_h, state_b_conv, cache_c_k, cache_c_v, cache_c_kidx,
           cache_d_ck, cache_d_cv, cache_d_sk, cache_d_sv, state_d_wk, state_d_wv, page_table,
           ffn1_norm, ffn1_wg, ffn1_wu, ffn1_wd, mix_norm, ffn2_norm, ffn2_wg, ffn2_wu, ffn2_wd,
           a_wq, a_wk, a_wv, a_wg, a_wo,
           b_wy, b_by, b_wx, b_bx, b_conv_w, b_conv_b, b_wa, b_ba, b_wi, b_bi, b_lam, b_wo,
           c_wq, c_gq, c_wk, c_gk, c_wv, c_wo, c_wiq, c_wik, c_gik, c_wiw,
           d_wq, d_gq, d_wck, d_wcv, d_pwk, d_pwv, d_phik, d_phiv, d_gck, d_wsk, d_wsv, d_gsk,
           d_wwk, d_wwv, d_gwk, d_wgate, d_wo):
    bp, tp, d = x_prompt.shape
    bs, ts, _ = x_sample.shape
    depth = ffn1_norm.shape[0]
    past_len = page_table.shape[1] * PAGE_SIZE
    pos_p = jnp.arange(tp, dtype=I32)
    pos_s = past_len + jnp.arange(ts, dtype=I32)
    b16 = lambda a: a.astype(BF16)
    row = lambda a: a.reshape(1, -1)
    nkv = N_KV * HEAD_DIM
    pool = lambda c: c.reshape(-1, HEAD_DIM)
    ffn1_w = [_to_bf16(w) for w in (ffn1_wg, ffn1_wu, ffn1_wd)]
    ffn2_w = [_to_bf16(w) for w in (ffn2_wg, ffn2_wu, ffn2_wd)]

    xp = x_prompt.reshape(bp * tp, d)
    xs = x_sample.reshape(bs * ts, d)
    outs = [[] for _ in range(24)]
    for i in range(depth):
        m, j = i % 4, i // 4
        xp, xs = _ffn(xp, ffn1_norm[i], *ffn1_w, i), _ffn(xs, ffn1_norm[i], *ffn1_w, i)
        g = mix_norm[i]
        if m == 0:
            dk = a_wq.shape[2] // RET_HEADS
            dv = a_wv.shape[2] // RET_HEADS
            wcat = b16(jnp.concatenate([a_wq[j], a_wk[j], a_wv[j], a_wg[j]], axis=1))
            wo = b16(a_wo[j])
            xp, s_p = _mixer_a(xp, g, pos_p, bp, jnp.zeros((bp, RET_HEADS, dk, dv), F32), wcat, wo, dk, dv)
            xs, s_s = _mixer_a(xs, g, pos_s, bs, state_a_ret[j], wcat, wo, dk, dv)
            new = [s_p, s_s]
            base = 0
        elif m == 1:
            common = (g, b16(jnp.concatenate([b_wy[j], b_wx[j]], axis=1)), jnp.concatenate([b_by[j], b_bx[j]]),
                      b_conv_w[j], b_conv_b[j], b16(b_wa[j]), b_ba[j].reshape(-1), b16(b_wi[j]),
                      b_bi[j].reshape(-1), jax.nn.softplus(-b_lam[j]), b16(b_wo[j]))
            xp, h_p, c_p = _lru_prompt(xp, bp, *common)
            xs, h_s, c_s = _lru_sample(xs, bs, state_b_h[j], state_b_conv[j], *common)
            new = [h_p, h_s, c_p, c_s]
            base = 2
        elif m == 2:
            zpad = jnp.zeros((d, LANES - IDX_DIM - IDX_HEADS), F32)
            wcat = b16(jnp.concatenate([c_wq[j], c_wk[j], c_wv[j], c_wiq[j], c_wik[j], c_wiw[j], zpad], axis=1))
            gik = jnp.concatenate([c_gik[j], jnp.zeros((LANES - IDX_DIM,), F32)])
            cw = (g, wcat, row(c_gq[j]), row(c_gk[j]), row(gik), b16(c_wo[j]))
            past = dict(kidx=_gather_pages(cache_c_kidx[j], page_table), k_pool=pool(cache_c_k),
                        v_pool=pool(cache_c_v), page0=j * cache_c_k.shape[1], pt=page_table)
            xp, k_p, v_p, i_p = _mixer_c(xp, bp, pos_p, None, *cw)
            xs, k_s, v_s, i_s = _mixer_c(xs, bs, pos_s, past, *cw)
            kv = lambda a, b_, t_: a.reshape(b_, t_, N_KV, HEAD_DIM)
            new = [kv(k_p, bp, tp), kv(k_s, bs, ts), kv(v_p, bp, tp), kv(v_s, bs, ts),
                   i_p.reshape(bp, tp, IDX_DIM), i_s.reshape(bs, ts, IDX_DIM)]
            base = 6
        else:
            zpad = jnp.zeros((d, LANES - N_HEADS * 3), F32)
            wcat = b16(jnp.concatenate([d_wq[j], d_wck[j], d_wcv[j], d_wsk[j], d_wsv[j], d_wwk[j], d_wwv[j],
                                        d_wgate[j], zpad], axis=1))
            cmpw = (d_pwk[j], d_pwv[j], b16(d_phik[j]), b16(d_phiv[j]), d_gck[j])
            dw = (g, wcat, row(d_gq[j]), row(d_gsk[j]), row(d_gwk[j]), cmpw, b16(d_wo[j]))
            wbuf = state_d_wk.shape[2]
            past = dict(ck_pool=pool(cache_d_ck), cv_pool=pool(cache_d_cv), sk_pool=pool(cache_d_sk),
                        sv_pool=pool(cache_d_sv), page0=j * cache_d_ck.shape[1], pt=page_table,
                        wk=state_d_wk[j].reshape(bs, wbuf, nkv), wv=state_d_wv[j].reshape(bs, wbuf, nkv))
            xp, rows_p = _mixer_d(xp, bp, pos_p, None, *dw)
            xs, rows_s = _mixer_d(xs, bs, pos_s, past, *dw)
            kv = lambda a, b_: a.reshape(b_, -1, N_KV, HEAD_DIM)
            new = []
            for a_p, a_s in zip(rows_p[:4], rows_s[:4]):
                new += [kv(a_p, bp), kv(a_s, bs)]
            wb_p = min(WINDOW, tp)
            for a_p, a_s, st in ((rows_p[4], rows_s[4], state_d_wk[j]), (rows_p[5], rows_s[5], state_d_wv[j])):
                new += [kv(a_p, bp)[:, tp - wb_p:], jnp.concatenate([st, kv(a_s, bs)], axis=1)[:, ts:]]
            base = 12
        for off, a in enumerate(new):
            outs[base + off].append(a)
        xp, xs = _ffn(xp, ffn2_norm[i], *ffn2_w, i), _ffn(xs, ffn2_norm[i], *ffn2_w, i)
    return (xp.reshape(bp, tp, d), xs.reshape(bs, ts, d)) + tuple(jnp.stack(o) for o in outs)
```

```python
import functools
import math

import jax
import jax.numpy as jnp
from jax import lax
from jax.experimental import pallas as pl
from jax.experimental.pallas import tpu as pltpu

F32 = jnp.float32
BF16 = jnp.bfloat16
I32 = jnp.int32

EPS = 1e-6
PAGE_SIZE = 128

RET_HEADS = 4
RET_CHUNK = 128
RET_THETA = 10000.0

LRU_BLOCKS = 4
CONV_W = 4
LRU_C = 8.0

N_HEADS = 8
HEAD_DIM = 128
N_KV = 2
GROUP = N_HEADS // N_KV
ROPE_DIM = HEAD_DIM // 4
ROPE_THETA = 500000.0

IDX_HEADS = 16
IDX_DIM = 64
IDX_ROPE = IDX_DIM // 4
TOPK_MAX = 256

CMP_BLK = 64
SLC_BLK = 64
N_SLC = 16
WINDOW = 512
BIG = 1e4

LANES = 128
SUBLANES = 8
VMEM_LIMIT = 56 * 1024 * 1024
ROW_TILE = 512
LRU_TILE = 256
ATTN_KEY_TILE = 512
SELECT_KEY_TILE = 256
CMP_ROWS = 2048
SLC_SHIFT = 6
SUM_FLOOR = 1e-30
NEG = -1e30
ATTN_SCALE = HEAD_DIM ** -0.5
LOG2E = math.log2(math.e)
INT_MIN = -2147483648


def _params(*sem):
    return pltpu.CompilerParams(dimension_semantics=sem, vmem_limit_bytes=VMEM_LIMIT)


def _const_spec(shape):
    nd = len(shape)
    return pl.BlockSpec(shape, lambda *_: (0,) * nd, pipeline_mode=pl.Buffered(1))


def _rms(x):
    return x * lax.rsqrt(jnp.mean(x * x, axis=-1, keepdims=True) + EPS)


def _dot(a, b):
    return jnp.dot(a, b, preferred_element_type=F32)


def _dot_nt(a, b):
    return lax.dot_general(a, b, (((1,), (1,)), ((), ())), preferred_element_type=F32)


def _row_tile(n):
    return ROW_TILE if n % ROW_TILE == 0 else n


def _ffn_kernel(x_ref, g_ref, wg_ref, wu_ref, wd_ref, o_ref):
    x = x_ref[...]
    h = (_rms(x) * g_ref[...]).astype(BF16)
    gt = _dot(h, wg_ref[...])
    ut = _dot(h, wu_ref[...])
    a = (gt * jax.nn.sigmoid(gt) * ut).astype(BF16)
    o_ref[...] = x + 0.5 * _dot(a, wd_ref[...])


def _ffn(x, g, wg, wu, wd, layer):
    n, d = x.shape
    f = wg.shape[2]
    tm = _row_tile(n)
    wspec = lambda a, b_: pl.BlockSpec((None, a, b_), lambda i: (layer, 0, 0), pipeline_mode=pl.Buffered(1))
    return pl.pallas_call(
        _ffn_kernel,
        grid=(n // tm,),
        in_specs=[pl.BlockSpec((tm, d), lambda i: (i, 0)), _const_spec((1, d)),
                  wspec(d, f), wspec(d, f), wspec(f, d)],
        out_specs=pl.BlockSpec((tm, d), lambda i: (i, 0)),
        out_shape=jax.ShapeDtypeStruct((n, d), F32),
        compiler_params=_params("parallel"), name="ffn",
    )(x, g.reshape(1, d), wg, wu, wd)


def _cast_kernel(x_ref, o_ref):
    o_ref[...] = x_ref[...].astype(o_ref.dtype)


def _to_bf16(w):
    c = w.shape[-1]
    r = w.size // c
    tm = _row_tile(r)
    out = pl.pallas_call(
        _cast_kernel, grid=(r // tm,),
        in_specs=[pl.BlockSpec((tm, c), lambda i: (i, 0))],
        out_specs=pl.BlockSpec((tm, c), lambda i: (i, 0)),
        out_shape=jax.ShapeDtypeStruct((r, c), BF16),
        compiler_params=_params("parallel"), name="to_bf16",
    )(w.reshape(r, c))
    return out.reshape(w.shape)


def _proj(x, g, w, tables, consts, epilogue, outs, n_tab_rows, name):
    n, d = x.shape
    tm = _row_tile(n)
    assert n_tab_rows % tm == 0
    nt = n_tab_rows // tm
    nw = w.shape[1]

    def kern(x_ref, g_ref, w_ref, *rest):
        tabs = rest[:len(tables)]
        cs = rest[len(tables):len(tables) + len(consts)]
        o = rest[len(tables) + len(consts):]
        h = (_rms(x_ref[...]) * g_ref[...]).astype(BF16)
        epilogue(h, w_ref, tabs, cs, o)

    in_specs = [pl.BlockSpec((tm, d), lambda i: (i, 0)), _const_spec((1, d)), _const_spec((d, nw))]
    in_specs += [pl.BlockSpec((tm, t.shape[1]), lambda i: (i % nt, 0)) for t in tables]
    in_specs += [_const_spec(c.shape) for c in consts]
    out_shape, out_specs = [], []
    for fn, dt in outs:
        ashape, bshape, imap = fn(n, tm)
        out_shape.append(jax.ShapeDtypeStruct(ashape, dt))
        out_specs.append(pl.BlockSpec(bshape, imap))
    return pl.pallas_call(
        kern, grid=(n // tm,), in_specs=in_specs, out_specs=out_specs, out_shape=out_shape,
        compiler_params=_params("parallel"), name=name,
    )(x, g.reshape(1, d), w, *tables, *consts)


def _rows(c):
    return lambda n, tm: ((n, c), (tm, c), lambda i: (i, 0))


def _outproj(x, w, ins, combine):
    n, d = x.shape
    tm = _row_tile(n)

    def kern(x_ref, w_ref, *rest):
        o_ref = rest[-1]
        a = combine(*[r[...] for r in rest[:-1]]).astype(BF16)
        o_ref[...] = x_ref[...] + _dot(a, w_ref[...])

    return pl.pallas_call(
        kern, grid=(n // tm,),
        in_specs=[pl.BlockSpec((tm, d), lambda i: (i, 0)), _const_spec(w.shape)]
        + [pl.BlockSpec((tm, a.shape[1]), lambda i: (i, 0)) for a in ins],
        out_specs=pl.BlockSpec((tm, d), lambda i: (i, 0)),
        out_shape=jax.ShapeDtypeStruct((n, d), F32),
        compiler_params=_params("parallel"), name="outproj",
    )(x, w, *ins)


def _ret_epilogue(dk, dv, h, w_ref, tabs, cs, o):
    cos, sin = tabs[0][...], tabs[1][...]
    q_ref, k_ref, v_ref, g_ref = o
    half = dk // 2
    nqk = RET_HEADS * dk
    for seg, ref, scale in ((0, q_ref, 1.0), (1, k_ref, dk ** -0.5)):
        z = _dot(h, w_ref[:, seg * nqk:(seg + 1) * nqk])
        for hd in range(RET_HEADS):
            x1 = z[:, hd * dk:hd * dk + half]
            x2 = z[:, hd * dk + half:(hd + 1) * dk]
            ref[:, hd * dk:hd * dk + half] = (x1 * cos - x2 * sin) * scale
            ref[:, hd * dk + half:(hd + 1) * dk] = (x2 * cos + x1 * sin) * scale
    nv = RET_HEADS * dv
    v_ref[...] = _dot(h, w_ref[:, 2 * nqk:2 * nqk + nv]).astype(BF16)
    zg = _dot(h, w_ref[:, 2 * nqk + nv:2 * nqk + 2 * nv])
    g_ref[...] = zg * jax.nn.sigmoid(zg)


def _ret_chunk_kernel(q_ref, k_ref, v_ref, s0_ref, dm_ref, xi_ref, zt_ref, gc_ref, o_ref, s_ref):
    c = pl.program_id(2)

    @pl.when(c == 0)
    def _():
        s_ref[...] = s0_ref[...]

    nh, dk, dv = s_ref.shape
    for hd in range(nh):
        q = q_ref[:, hd * dk:(hd + 1) * dk]
        k = k_ref[:, hd * dk:(hd + 1) * dk]
        v = v_ref[:, hd * dv:(hd + 1) * dv]
        s = s_ref[hd]
        att = (_dot_nt(q.astype(BF16), k.astype(BF16)) * dm_ref[hd]).astype(BF16)
        o = _dot(att, v) + _dot((q * xi_ref[hd]).astype(BF16), s.astype(BF16))
        kz = (k * zt_ref[hd]).astype(BF16)
        s_ref[hd] = s * gc_ref[hd] + lax.dot_general(kz, v, (((0,), (0,)), ((), ())),
                                                     preferred_element_type=F32)
        o_ref[:, hd * dv:(hd + 1) * dv] = _rms(o)


def _retention(q, k, v, s0, n_valid):
    b, hh, dk, dv = s0.shape
    t = q.shape[0] // b
    c = RET_CHUNK
    nc = t // c
    cc = c if n_valid % c == 0 else n_valid
    lg = jnp.log1p(-jnp.exp2(-5.0 - jnp.arange(hh, dtype=F32)))
    idx = jnp.arange(c, dtype=F32)
    diff = idx[:, None] - idx[None, :]
    dmask = jnp.where(diff >= 0, jnp.exp(lg[:, None, None] * jnp.maximum(diff, 0.0)), 0.0)
    xi = jnp.exp(lg[:, None] * (idx + 1.0))[:, :, None]
    zeta = jnp.exp(lg[:, None] * (cc - 1.0 - idx))[:, :, None]
    g_c = jnp.exp(lg * cc)[:, None, None]
    nh = 2 if hh % 2 == 0 else 1
    return pl.pallas_call(
        _ret_chunk_kernel,
        grid=(b, hh // nh, nc),
        in_specs=[pl.BlockSpec((c, nh * dk), lambda i, j, l: (i * nc + l, j)),
                  pl.BlockSpec((c, nh * dk), lambda i, j, l: (i * nc + l, j)),
                  pl.BlockSpec((c, nh * dv), lambda i, j, l: (i * nc + l, j)),
                  pl.BlockSpec((None, nh, dk, dv), lambda i, j, l: (i, j, 0, 0)),
                  pl.BlockSpec((nh, c, c), lambda i, j, l: (j, 0, 0)),
                  pl.BlockSpec((nh, c, 1), lambda i, j, l: (j, 0, 0)),
                  pl.BlockSpec((nh, c, 1), lambda i, j, l: (j, 0, 0)),
                  pl.BlockSpec((nh, 1, 1), lambda i, j, l: (j, 0, 0))],
        out_specs=[pl.BlockSpec((c, nh * dv), lambda i, j, l: (i * nc + l, j)),
                   pl.BlockSpec((None, nh, dk, dv), lambda i, j, l: (i, j, 0, 0))],
        out_shape=[jax.ShapeDtypeStruct((b * t, hh * dv), F32),
                   jax.ShapeDtypeStruct((b, hh, dk, dv), F32)],
        compiler_params=_params("parallel", "parallel", "arbitrary"), name="retention",
    )(q, k, v, s0, dmask, xi, zeta, g_c)


def _rope_tables_full(pos, dim, theta):
    half = dim // 2
    inv = theta ** (-jnp.arange(half, dtype=F32) / half)
    ang = pos.astype(F32)[:, None] * inv[None, :]
    return jnp.cos(ang), jnp.sin(ang)


def _mixer_a(x, g, pos, bsz, s0, wcat, wo, dk, dv):
    n, d = x.shape
    t = n // bsz
    cos, sin = _rope_tables_full(pos, dk, RET_THETA)
    if t % RET_CHUNK:
        cos = jnp.tile(cos, (bsz, 1))
        sin = jnp.tile(sin, (bsz, 1))
    q, k, v, sg = _proj(
        x, g, wcat, [cos, sin], [], functools.partial(_ret_epilogue, dk, dv),
        [(_rows(RET_HEADS * dk), F32), (_rows(RET_HEADS * dk), F32),
         (_rows(RET_HEADS * dv), BF16), (_rows(RET_HEADS * dv), F32)], cos.shape[0], "proj_ret")
    if t % RET_CHUNK:
        pad = lambda a: jnp.pad(a.reshape(bsz, t, -1), ((0, 0), (0, RET_CHUNK - t), (0, 0))
                                ).reshape(bsz * RET_CHUNK, -1)
        o, s_new = _retention(pad(q), pad(k), pad(v), s0, t)
        o = o.reshape(bsz, RET_CHUNK, -1)[:, :t].reshape(n, -1)
    else:
        o, s_new = _retention(q, k, v, s0, t)
    return _outproj(x, wo, [sg, o], lambda a, b: a * b), s_new


def _gelu_tanh(x):
    return 0.5 * x * (1.0 + jnp.tanh(math.sqrt(2.0 / math.pi) * (x + 0.044715 * (x * x * x))))


def _lru_gates(xb, wa_ref, ba, wi_ref, bi, sp):
    bw = wa_ref.shape[1]
    xbb = xb.astype(BF16)
    r = jnp.concatenate([_dot(xbb[:, n * bw:(n + 1) * bw], wa_ref[n]) for n in range(LRU_BLOCKS)], axis=1)
    i = jnp.concatenate([_dot(xbb[:, n * bw:(n + 1) * bw], wi_ref[n]) for n in range(LRU_BLOCKS)], axis=1)
    r = jax.nn.sigmoid(r + ba)
    i = jax.nn.sigmoid(i + bi)
    log_a = -LRU_C * r * sp
    a = jnp.exp(log_a)
    th = jnp.tanh(log_a)
    bt = jnp.sqrt(-2.0 * th / (1.0 - th)) * (i * xb)
    return a, bt


def _lru_prompt_kernel(x_ref, g_ref, wyx_ref, byx_ref, cw_ref, cb_ref, wa_ref, ba_ref, wi_ref, bi_ref,
                       sp_ref, wo_ref, o_ref, hl_ref, cn_ref, xpad_ref, a_ref, b_ref, hs_ref, h_ref):
    j = pl.program_id(1)
    tm, dr = a_ref.shape
    pad = xpad_ref.shape[0] - tm

    @pl.when(j == 0)
    def _():
        h_ref[...] = jnp.zeros_like(h_ref)
        xpad_ref[0:pad, :] = jnp.zeros((pad, dr), F32)

    x = x_ref[...]
    h = (_rms(x) * g_ref[...]).astype(BF16)
    z = _dot(h, wyx_ref[...]) + byx_ref[...]
    gate = _gelu_tanh(z[:, :dr])
    xpad_ref[pad:, :] = z[:, dr:]
    xb = cb_ref[...] + z[:, dr:] * cw_ref[CONV_W - 1:CONV_W, :]
    for s in range(1, CONV_W):
        xb = xb + xpad_ref[pad - s:pad - s + tm, :] * cw_ref[CONV_W - 1 - s:CONV_W - s, :]
    xpad_ref[0:pad, :] = xpad_ref[tm:tm + pad, :]
    a, bt = _lru_gates(xb, wa_ref, ba_ref[...], wi_ref, bi_ref[...], sp_ref[...])
    a_ref[...] = a
    b_ref[...] = bt

    def step(t, hprev):
        hn = a_ref[pl.ds(t, 1), :] * hprev + b_ref[pl.ds(t, 1), :]
        hs_ref[pl.ds(t, 1), :] = hn
        return hn

    hlast = lax.fori_loop(0, tm, step, h_ref[...])
    h_ref[...] = hlast
    o_ref[...] = x + _dot((gate * hs_ref[...]).astype(BF16), wo_ref[...])
    hl_ref[...] = hlast
    cn_ref[...] = xpad_ref[pad - (CONV_W - 1):pad, :]


def _lru_prompt(x, bsz, g, wyx, byx, cw, cb, wa, ba, wi, bi, sp, wo):
    n, d = x.shape
    t = n // bsz
    dr = wo.shape[0]
    tm = LRU_TILE if t % LRU_TILE == 0 else t
    nt = t // tm
    pad = SUBLANES
    vec = lambda a: a.reshape(1, -1)
    out, hl, cn = pl.pallas_call(
        _lru_prompt_kernel,
        grid=(bsz, nt),
        in_specs=[pl.BlockSpec((tm, d), lambda i, j: (i * nt + j, 0)), _const_spec((1, d)),
                  _const_spec(wyx.shape), _const_spec((1, 2 * dr)), _const_spec(cw.shape),
                  _const_spec((1, dr)), _const_spec(wa.shape), _const_spec((1, dr)),
                  _const_spec(wi.shape), _const_spec((1, dr)), _const_spec((1, dr)),
                  _const_spec(wo.shape)],
        out_specs=[pl.BlockSpec((tm, d), lambda i, j: (i * nt + j, 0)),
                   pl.BlockSpec((None, 1, dr), lambda i, j: (i, 0, 0)),
                   pl.BlockSpec((None, CONV_W - 1, dr), lambda i, j: (i, 0, 0))],
        out_shape=[jax.ShapeDtypeStruct((n, d), F32), jax.ShapeDtypeStruct((bsz, 1, dr), F32),
                   jax.ShapeDtypeStruct((bsz, CONV_W - 1, dr), F32)],
        scratch_shapes=[pltpu.VMEM((tm + pad, dr), F32), pltpu.VMEM((tm, dr), F32),
                        pltpu.VMEM((tm, dr), F32), pltpu.VMEM((tm, dr), F32), pltpu.VMEM((1, dr), F32)],
        compiler_params=_params("parallel", "arbitrary"), name="lru_prompt",
    )(x, vec(g), wyx, vec(byx), cw, vec(cb), wa, vec(ba), wi, vec(bi), vec(sp), wo)
    return out, hl.reshape(bsz, dr), cn


def _lru_sample_kernel(bsz, x_ref, g_ref, buf_ref, h0_ref, wyx_ref, byx_ref, cw_ref, cb_ref, wa_ref, ba_ref,
                       wi_ref, bi_ref, sp_ref, wo_ref, o_ref, hl_ref, cn_ref, xpad_ref, hs_ref):
    n, dr = hs_ref.shape
    t = n // bsz
    nb = (CONV_W - 1) * bsz
    x = x_ref[...]
    h = (_rms(x) * g_ref[...]).astype(BF16)
    z = _dot(h, wyx_ref[...]) + byx_ref[...]
    gate = _gelu_tanh(z[:, :dr])
    xpad_ref[0:nb, :] = buf_ref[...]
    xpad_ref[nb:, :] = z[:, dr:]
    xb = cb_ref[...] + xpad_ref[0:n, :] * cw_ref[0:1, :]
    for s in range(1, CONV_W):
        xb = xb + xpad_ref[s * bsz:s * bsz + n, :] * cw_ref[s:s + 1, :]
    a, bt = _lru_gates(xb, wa_ref, ba_ref[...], wi_ref, bi_ref[...], sp_ref[...])
    hcur = h0_ref[...]
    for s in range(t):
        hcur = a[s * bsz:(s + 1) * bsz, :] * hcur + bt[s * bsz:(s + 1) * bsz, :]
        hs_ref[s * bsz:(s + 1) * bsz, :] = hcur
    o_ref[...] = x + _dot((gate * hs_ref[...]).astype(BF16), wo_ref[...])
    hl_ref[...] = hcur
    cn_ref[...] = xpad_ref[n:n + nb, :]


def _lru_sample(x, bsz, h0, conv0, g, wyx, byx, cw, cb, wa, ba, wi, bi, sp, wo):
    n, d = x.shape
    t = n // bsz
    dr = wo.shape[0]
    nb = (CONV_W - 1) * bsz
    tmaj = lambda a, tt: a.reshape(bsz, tt, -1).swapaxes(0, 1).reshape(tt * bsz, -1)
    bmaj = lambda a, tt: a.reshape(tt, bsz, -1).swapaxes(0, 1)
    vec = lambda a: a.reshape(1, -1)
    args = (tmaj(x, t), vec(g), tmaj(conv0, CONV_W - 1), h0, wyx, vec(byx), cw, vec(cb), wa, vec(ba),
            wi, vec(bi), vec(sp), wo)
    out, hl, cn = pl.pallas_call(
        functools.partial(_lru_sample_kernel, bsz),
        in_specs=[pl.BlockSpec(a.shape, lambda nd=a.ndim: (0,) * nd) for a in args],
        out_specs=[pl.BlockSpec((n, d), lambda: (0, 0)), pl.BlockSpec((bsz, dr), lambda: (0, 0)),
                   pl.BlockSpec((nb, dr), lambda: (0, 0))],
        out_shape=[jax.ShapeDtypeStruct((n, d), F32), jax.ShapeDtypeStruct((bsz, dr), F32),
                   jax.ShapeDtypeStruct((nb, dr), F32)],
        scratch_shapes=[pltpu.VMEM((n + nb, dr), F32), pltpu.VMEM((n, dr), F32)],
        compiler_params=pltpu.CompilerParams(vmem_limit_bytes=VMEM_LIMIT), name="lru_sample",
    )(*args)
    return bmaj(out, t).reshape(n, d), hl, bmaj(cn, CONV_W - 1)


def _rope_tables_partial(pos, period, rot_dim, theta):
    half = rot_dim // 2
    inv = theta ** (-jnp.arange(half, dtype=F32) / half)
    ang = pos.astype(F32)[:, None] * inv[None, :]
    cos, sin = jnp.cos(ang), jnp.sin(ang)
    t = pos.shape[0]
    zh = jnp.zeros((t, half), F32)
    rest = period - rot_dim
    c = jnp.concatenate([cos, cos, jnp.ones((t, rest), F32)], axis=1)
    s1 = jnp.concatenate([-sin, zh, jnp.zeros((t, rest), F32)], axis=1)
    s2 = jnp.concatenate([zh, sin, jnp.zeros((t, rest), F32)], axis=1)
    rep = LANES // period
    return [jnp.tile(a, (1, rep)) for a in (c, s1, s2)]


def _rot(x, c, s1, s2, half):
    n = x.shape[-1]
    return x * c + pltpu.roll(x, n - half, 1) * s1 + pltpu.roll(x, half, 1) * s2


def _sort_key(x):
    bits = pltpu.bitcast(x, I32)
    return bits ^ ((bits >> 31) & 0x7FFFFFFF)


RADIX_GROUP = 4


def _kth_largest(count_ge, k, shape):
    c0 = count_ge(jnp.zeros(shape, I32))
    t = jnp.where(c0 >= k, 0, INT_MIN).astype(I32)
    cnt = jnp.where(c0 >= k, c0, 2.0 * k)

    def decide(t, cnt, bit):
        cand = t | lax.shift_left(jnp.int32(1), bit)
        c = count_ge(cand)
        return jnp.where(c >= k, cand, t), jnp.where(c >= k, c, cnt)

    n_head = 31 % RADIX_GROUP
    for bit in range(30, 30 - n_head, -1):
        t, cnt = decide(t, cnt, bit)
    n_groups = (31 - n_head) // RADIX_GROUP

    def cond(state):
        i, _, _, pending = state
        return (i < n_groups) & (pending > 0.5)

    def body(state):
        i, t, cnt, _ = state
        for u in range(RADIX_GROUP):
            t, cnt = decide(t, cnt, 30 - n_head - (i * RADIX_GROUP + u))
        return i + 1, t, cnt, jnp.max(jnp.where(cnt == k, 0.0, 1.0))

    return lax.while_loop(cond, body, (jnp.int32(0), t, cnt, jnp.float32(1.0)))[1]


def _attn_kernel(tq, tk, lk, q_pos0, k_pos0, mode, n_cmp, q_ref, k_ref, v_ref, *rest):
    if mode == "window":
        o_ref, qs_ref, acc_ref, m_scr, l_scr = rest
        m_ref = None
    else:
        m_ref, o_ref, qs_ref, acc_ref, m_scr, l_scr = rest
    p0 = q_pos0 + pl.program_id(1) * tq
    for g in range(N_KV):
        for h in range(GROUP):
            c0 = (g * GROUP + h) * HEAD_DIM
            qs_ref[g, h * tq:(h + 1) * tq, :] = q_ref[:, c0:c0 + HEAD_DIM]
    m_scr[...] = jnp.full(m_scr.shape, NEG, F32)
    l_scr[...] = jnp.zeros(l_scr.shape, F32)
    acc_ref[...] = jnp.zeros(acc_ref.shape, F32)
    hi = jnp.minimum((p0 + tq - 1 - k_pos0) // tk + 1, lk // tk)
    lo = jnp.maximum(p0 - (WINDOW - 1) - k_pos0, 0) // tk if mode == "window" else 0
    rowpos = p0 + lax.broadcasted_iota(I32, (tq, tk), 0)

    def body(kt, carry):
        k0 = pl.multiple_of(kt * tk, tk)
        if mode == "mask":
            bias = m_ref[:, pl.ds(k0, tk)].astype(F32)
        else:
            colpos = k_pos0 + k0 + lax.broadcasted_iota(I32, (tq, tk), 1)
            valid = colpos <= rowpos
            if mode == "window":
                valid = valid & (rowpos - colpos < WINDOW) & (colpos >= 0)
                bias = jnp.where(valid, 0.0, -jnp.inf)
        for g in range(N_KV):
            if mode == "blocksel":
                blk = lax.shift_right_logical(k_pos0 + k0 + lax.broadcasted_iota(I32, (LANES, tk), 1), SLC_SHIFT)
                e = jnp.where(blk == lax.broadcasted_iota(I32, (LANES, tk), 0), 1.0, 0.0).astype(BF16)
                sel = _dot(m_ref[:, g * LANES:(g + 1) * LANES], e) > 0.5
                bias = jnp.where(valid & (sel | (colpos >= n_cmp * SLC_BLK)), 0.0, -jnp.inf)
            kt_ = k_ref[pl.ds(k0, tk), g * HEAD_DIM:(g + 1) * HEAD_DIM]
            vt_ = v_ref[pl.ds(k0, tk), g * HEAD_DIM:(g + 1) * HEAD_DIM]
            s = _dot_nt(qs_ref[g], kt_)
            s = (s.reshape(GROUP, tq, tk) + bias[None]).reshape(GROUP * tq, tk)
            _softmax_update(g, s, vt_, acc_ref, m_scr, l_scr)
        return carry

    lax.fori_loop(lo, hi, body, 0)
    for g in range(N_KV):
        o = acc_ref[g] / jnp.maximum(l_scr[g], SUM_FLOOR)
        for h in range(GROUP):
            c0 = (g * GROUP + h) * HEAD_DIM
            o_ref[:, c0:c0 + HEAD_DIM] = o[h * tq:(h + 1) * tq, :]


def _attention(q, k, v, m, *, bsz, tq, q_pos0, k_pos0, mode, n_cmp=0):
    nq_rows, dq = q.shape
    lk = k.shape[1]
    tk = ATTN_KEY_TILE
    assert lk % tk == 0 and (nq_rows // bsz) % tq == 0
    nq = nq_rows // bsz // tq
    kern = functools.partial(_attn_kernel, tq, tk, lk, q_pos0, k_pos0, mode, n_cmp)
    in_specs = [pl.BlockSpec((tq, dq), lambda b, j: (b * nq + j, 0)),
                pl.BlockSpec((None, lk, k.shape[2]), lambda b, j: (b, 0, 0)),
                pl.BlockSpec((None, lk, v.shape[2]), lambda b, j: (b, 0, 0))]
    args = [q, k, v]
    if mode != "window":
        in_specs.append(pl.BlockSpec((tq, m.shape[1]), lambda b, j: (b * nq + j, 0)))
        args.append(m)
    return pl.pallas_call(
        kern, grid=(bsz, nq), in_specs=in_specs,
        out_specs=pl.BlockSpec((tq, dq), lambda b, j: (b * nq + j, 0)),
        out_shape=jax.ShapeDtypeStruct((nq_rows, dq), F32),
        scratch_shapes=[pltpu.VMEM((N_KV, GROUP * tq, HEAD_DIM), BF16),
                        pltpu.VMEM((N_KV, GROUP * tq, HEAD_DIM), F32),
                        pltpu.VMEM((N_KV, GROUP * tq, LANES), F32),
                        pltpu.VMEM((N_KV, GROUP * tq, LANES), F32)],
        compiler_params=_params("parallel", "arbitrary"), name="attn_" + mode,
    )(*args)


def _softmax_update(g, s, vt, acc_ref, m_scr, l_scr):
    m_old = m_scr[g]
    m_new = jnp.maximum(m_old, jnp.max(s, axis=-1, keepdims=True))
    alpha = jnp.exp2(m_old - m_new)
    p = jnp.exp2(s - jnp.concatenate([m_new] * (s.shape[1] // LANES), axis=1))
    l_scr[g] = alpha * l_scr[g] + jnp.sum(p, axis=-1, keepdims=True)
    acc_ref[g] = alpha * acc_ref[g] + _dot(p.astype(BF16), vt)
    m_scr[g] = m_new


def _paged_attn_kernel(tq, pp, n_steps, mode, pt_ref, q_ref, *rest):
    kp, vp = rest[:pp], rest[pp:2 * pp]
    kn_ref, vn_ref = rest[2 * pp:2 * pp + 2]
    if mode == "mask":
        mb_ref, mt_ref, o_ref, qs_ref, acc_ref, m_scr, l_scr, kbuf, vbuf = rest[2 * pp + 2:]
    else:
        mb_ref, o_ref, qs_ref, acc_ref, m_scr, l_scr, kbuf, vbuf = rest[2 * pp + 2:]
    step = pl.program_id(1)
    tk = pp * PAGE_SIZE

    @pl.when(step == 0)
    def _():
        for g in range(N_KV):
            for h in range(GROUP):
                c0 = (g * GROUP + h) * HEAD_DIM
                qs_ref[g, h * tq:(h + 1) * tq, :] = q_ref[:, c0:c0 + HEAD_DIM]
        m_scr[...] = jnp.full(m_scr.shape, NEG, F32)
        l_scr[...] = jnp.zeros(l_scr.shape, F32)
        acc_ref[...] = jnp.zeros(acc_ref.shape, F32)

    for i in range(pp):
        for g in range(N_KV):
            rows = pl.ds(g, PAGE_SIZE, stride=N_KV)
            kbuf[g, i * PAGE_SIZE:(i + 1) * PAGE_SIZE, :] = kp[i][rows, :].astype(BF16)
            vbuf[g, i * PAGE_SIZE:(i + 1) * PAGE_SIZE, :] = vp[i][rows, :].astype(BF16)

    def biased(g, s, bias):
        n = s.shape[1]
        return (s.reshape(GROUP, tq, n) + bias[None]).reshape(GROUP * tq, n)

    for g in range(N_KV):
        if mode == "mask":
            bias = mb_ref[...].astype(F32)
        else:
            blk = lax.shift_right_logical(step * tk + lax.broadcasted_iota(I32, (LANES, tk), 1), SLC_SHIFT)
            e = jnp.where(blk == lax.broadcasted_iota(I32, (LANES, tk), 0), 1.0, 0.0).astype(BF16)
            bias = jnp.where(_dot(mb_ref[:, g * LANES:(g + 1) * LANES], e) > 0.5, 0.0, -jnp.inf)
        _softmax_update(g, biased(g, _dot_nt(qs_ref[g], kbuf[g]), bias), vbuf[g], acc_ref, m_scr, l_scr)

    @pl.when(step == n_steps - 1)
    def _():
        tn = kn_ref.shape[0]
        causal = lax.broadcasted_iota(I32, (tq, tn), 1) <= lax.broadcasted_iota(I32, (tq, tn), 0)
        for g in range(N_KV):
            sl = slice(g * HEAD_DIM, (g + 1) * HEAD_DIM)
            bias = mt_ref[...].astype(F32) if mode == "mask" else jnp.where(causal, 0.0, -jnp.inf)
            _softmax_update(g, biased(g, _dot_nt(qs_ref[g], kn_ref[:, sl]), bias), vn_ref[:, sl],
                            acc_ref, m_scr, l_scr)
            o = acc_ref[g] / jnp.maximum(l_scr[g], SUM_FLOOR)
            for h in range(GROUP):
                c0 = (g * GROUP + h) * HEAD_DIM
                o_ref[:, c0:c0 + HEAD_DIM] = o[h * tq:(h + 1) * tq, :]


PAGES_PER_STEP = 16
NEW_TILE = LANES


def _paged_attention(q, k_pool, v_pool, page0, page_table, kn, vn, m, *, bsz, tq, mode):
    dq = q.shape[1]
    n_pages = page_table.shape[1]
    pp = PAGES_PER_STEP
    assert n_pages % pp == 0
    n_steps = n_pages // pp
    tk = pp * PAGE_SIZE
    prow = PAGE_SIZE * N_KV
    page = lambda i: pl.BlockSpec((prow, HEAD_DIM), lambda b, s, pt: (page0 + pt[b, s * pp + i], 0))
    in_specs = [pl.BlockSpec((tq, dq), lambda b, s, pt: (b, 0))]
    in_specs += [page(i) for i in range(pp)] * 2
    in_specs += [pl.BlockSpec((NEW_TILE, kn.shape[1]), lambda b, s, pt: (b, 0))] * 2
    args = [q] + [k_pool] * pp + [v_pool] * pp + [kn, vn]
    if mode == "mask":
        in_specs += [pl.BlockSpec((tq, tk), lambda b, s, pt: (b, s)),
                     pl.BlockSpec((tq, NEW_TILE), lambda b, s, pt: (b, n_pages * PAGE_SIZE // NEW_TILE))]
        args += [m, m]
    else:
        in_specs.append(pl.BlockSpec((tq, m.shape[1]), lambda b, s, pt: (b, 0)))
        args.append(m)
    rows = GROUP * tq
    return pl.pallas_call(
        functools.partial(_paged_attn_kernel, tq, pp, n_steps, mode),
        grid_spec=pltpu.PrefetchScalarGridSpec(
            num_scalar_prefetch=1, grid=(bsz, n_steps), in_specs=in_specs,
            out_specs=pl.BlockSpec((tq, dq), lambda b, s, pt: (b, 0)),
            scratch_shapes=[pltpu.VMEM((N_KV, rows, HEAD_DIM), BF16), pltpu.VMEM((N_KV, rows, HEAD_DIM), F32),
                            pltpu.VMEM((N_KV, rows, LANES), F32), pltpu.VMEM((N_KV, rows, LANES), F32),
                            pltpu.VMEM((N_KV, tk, HEAD_DIM), BF16), pltpu.VMEM((N_KV, tk, HEAD_DIM), BF16)]),
        out_shape=jax.ShapeDtypeStruct((bsz * tq, dq), F32),
        compiler_params=_params("parallel", "arbitrary"), name="paged_attn_" + mode,
    )(page_table, *args)


def _heads(c):
    return lambda n, tm: ((n, c // HEAD_DIM, HEAD_DIM), (tm, c // HEAD_DIM, HEAD_DIM), lambda i: (i, 0, 0))


def _store_heads(ref, z):
    for hd in range(ref.shape[1]):
        ref[:, hd, :] = z[:, hd * HEAD_DIM:(hd + 1) * HEAD_DIM]


def _head_norm_rot(z, g, tabs, n_heads, scale, refs_f32, refs_bf16):
    c, s1, s2 = tabs
    for hd in range(n_heads):
        sl = slice(hd * HEAD_DIM, (hd + 1) * HEAD_DIM)
        r = _rot(_rms(z[:, sl]) * g, c, s1, s2, ROPE_DIM // 2)
        for ref in refs_f32:
            ref[:, hd, :] = r
        for ref in refs_bf16:
            ref[:, sl] = (r * scale).astype(BF16)


def _dsa_epilogue(h, w_ref, tabs, cs, o):
    tq_ = [t[...] for t in tabs[:3]]
    ti_ = [t[...] for t in tabs[3:]]
    gq, gk, gik = [c[...] for c in cs]
    q_ref, k_ref, kb_ref, v_ref, vb_ref, qi_ref, ki_ref, wi_ref = o
    nq, nkv = N_HEADS * HEAD_DIM, N_KV * HEAD_DIM
    ni = IDX_HEADS * IDX_DIM
    _head_norm_rot(_dot(h, w_ref[:, 0:nq]), gq, tq_, N_HEADS, ATTN_SCALE * LOG2E, [], [q_ref])
    _head_norm_rot(_dot(h, w_ref[:, nq:nq + nkv]), gk, tq_, N_KV, 1.0, [k_ref], [kb_ref])
    zv = _dot(h, w_ref[:, nq + nkv:nq + 2 * nkv])
    _store_heads(v_ref, zv)
    vb_ref[...] = zv.astype(BF16)
    c0 = nq + 2 * nkv
    zi = _dot(h, w_ref[:, c0:c0 + ni])
    for ch in range(ni // LANES):
        r = _rot(zi[:, ch * LANES:(ch + 1) * LANES], *ti_, IDX_ROPE // 2).astype(BF16)
        for u in range(LANES // IDX_DIM):
            qi_ref[ch * (LANES // IDX_DIM) + u] = r[:, u * IDX_DIM:(u + 1) * IDX_DIM]
    zl = _dot(h, w_ref[:, c0 + ni:c0 + ni + LANES])
    lane = lax.broadcasted_iota(I32, zl.shape, 1)
    ms = jnp.sum(jnp.where(lane < IDX_DIM, zl * zl, 0.0), axis=-1, keepdims=True) / IDX_DIM
    r = _rot(zl * lax.rsqrt(ms + EPS) * gik, *ti_, IDX_ROPE // 2)
    ki_ref[...] = r[:, :IDX_DIM]
    wi_ref[...] = zl[:, IDX_DIM:IDX_DIM + IDX_HEADS] * (IDX_HEADS ** -0.5 * IDX_DIM ** -0.5)


def _dsa_select_kernel(tq, nb, tk, lk, q_pos0, top, qi_ref, wi_ref, kit_ref, m_ref, key_ref, wib_ref):
    rows = nb * tq
    p0 = q_pos0 + pl.program_id(1) * tq
    nk = jnp.minimum((p0 + tq - 1) // tk + 1, lk // tk)
    wi = wi_ref[...]
    for h in range(IDX_HEADS):
        wib_ref[h] = jnp.broadcast_to(wi[:, h:h + 1], (rows, tk))
    rowpos1 = p0 + lax.broadcasted_iota(I32, (tq, tk), 0)
    col1 = lax.broadcasted_iota(I32, (tq, tk), 1)
    rowpos = jnp.concatenate([rowpos1] * nb, axis=0)
    col = lax.broadcasted_iota(I32, (rows, tk), 1)

    for i in range(nb):
        def score_pair(kp, carry, i=i):
            for u in range(2):
                k0 = pl.multiple_of((2 * kp + u) * tk, tk)
                kt_ = kit_ref[i, :, pl.ds(k0, tk)]
                acc = jnp.zeros((tq, tk), F32)
                if IDX_HEADS * tq <= 512:
                    s_all = _dot(qi_ref[:, i * tq:(i + 1) * tq, :].reshape(IDX_HEADS * tq, IDX_DIM), kt_)
                for h in range(IDX_HEADS):
                    if IDX_HEADS * tq <= 512:
                        s = s_all[h * tq:(h + 1) * tq, :]
                    else:
                        s = _dot(qi_ref[h, i * tq:(i + 1) * tq, :], kt_)
                    acc = acc + jnp.maximum(s, 0.0) * wib_ref[h, i * tq:(i + 1) * tq, :]
                acc = jnp.where(k0 + col1 <= rowpos1, acc, -jnp.inf)
                key_ref[i * tq:(i + 1) * tq, pl.ds(k0, tk)] = _sort_key(acc)
            return carry

        lax.fori_loop(0, (nk + 1) // 2, score_pair, 0)

    def count_ge(cand):
        cb = jnp.broadcast_to(cand, (rows, LANES))

        def body(kt, acc):
            for u in range(2 * tk // LANES):
                c0 = pl.multiple_of(kt * (2 * tk) + u * LANES, LANES)
                acc = acc + jnp.where(key_ref[:, pl.ds(c0, LANES)] >= cb, 1.0, 0.0)
            return acc

        acc = lax.fori_loop(0, (nk + 1) // 2, body, jnp.zeros((rows, LANES), F32))
        return jnp.sum(acc, axis=-1, keepdims=True)

    thr = jnp.broadcast_to(_kth_largest(count_ge, float(top), (rows, 1)), (rows, tk))

    def bias_tile(kt, carry):
        k0 = pl.multiple_of(kt * tk, tk)
        sel = (k0 + col <= rowpos) & (key_ref[:, pl.ds(k0, tk)] >= thr)
        m_ref[:, pl.ds(k0, tk)] = jnp.where(sel, 0.0, -jnp.inf).astype(BF16)
        return carry

    lax.fori_loop(0, nk, bias_tile, 0)

    def rest_tile(kt, carry):
        m_ref[:, pl.ds(pl.multiple_of(kt * tk, tk), tk)] = jnp.full((rows, tk), -jnp.inf, BF16)
        return carry

    lax.fori_loop(nk, lk // tk, rest_tile, 0)


SELECT_ROWS = 128


def _dsa_select(qi, wi, kit, *, bsz, tq, q_pos0, top):
    lk = kit.shape[2]
    tk = SELECT_KEY_TILE
    assert lk % (2 * tk) == 0 and top <= tk
    nq_rows = wi.shape[0]
    nq = nq_rows // bsz // tq
    nb = SELECT_ROWS // tq if (nq == 1 and SELECT_ROWS % tq == 0 and bsz % (SELECT_ROWS // tq) == 0) else 1
    rows = nb * tq
    return pl.pallas_call(
        functools.partial(_dsa_select_kernel, tq, nb, tk, lk, q_pos0, top),
        grid=(bsz // nb, nq),
        in_specs=[pl.BlockSpec((IDX_HEADS, rows, IDX_DIM), lambda b, j: (0, b * nq + j, 0)),
                  pl.BlockSpec((rows, IDX_HEADS), lambda b, j: (b * nq + j, 0)),
                  pl.BlockSpec((nb, IDX_DIM, lk), lambda b, j: (b, 0, 0))],
        out_specs=pl.BlockSpec((rows, lk), lambda b, j: (b * nq + j, 0)),
        out_shape=jax.ShapeDtypeStruct((nq_rows, lk), BF16),
        scratch_shapes=[pltpu.VMEM((rows, lk), I32), pltpu.VMEM((IDX_HEADS, rows, tk), F32)],
        compiler_params=_params("parallel", "arbitrary"), name="dsa_select",
    )(qi, wi, kit)


def _nsa_epilogue(h, w_ref, tabs, cs, o):
    tb = [t[...] for t in tabs]
    gq, gsk, gwk = [c[...] for c in cs]
    (qc_ref, q_ref, ck_ref, cv_ref, sk_ref, skb_ref, sv_ref, svb_ref,
     wk_ref, wkb_ref, wv_ref, wvb_ref, gate_ref) = o
    nq, nkv = N_HEADS * HEAD_DIM, N_KV * HEAD_DIM
    zq = _dot(h, w_ref[:, 0:nq])
    for hd in range(N_HEADS):
        sl = slice(hd * HEAD_DIM, (hd + 1) * HEAD_DIM)
        qn = _rms(zq[:, sl]) * gq
        qc_ref[:, sl] = (qn * ATTN_SCALE).astype(BF16)
        q_ref[:, sl] = (_rot(qn, *tb, ROPE_DIM // 2) * (ATTN_SCALE * LOG2E)).astype(BF16)
    seg = lambda i: _dot(h, w_ref[:, nq + i * nkv:nq + (i + 1) * nkv])
    ck_ref[...] = seg(0)
    cv_ref[...] = seg(1)
    _head_norm_rot(seg(2), gsk, tb, N_KV, 1.0, [sk_ref], [skb_ref])
    zsv = seg(3)
    _store_heads(sv_ref, zsv)
    svb_ref[...] = zsv.astype(BF16)
    _head_norm_rot(seg(4), gwk, tb, N_KV, 1.0, [wk_ref], [wkb_ref])
    zwv = seg(5)
    _store_heads(wv_ref, zwv)
    wvb_ref[...] = zwv.astype(BF16)
    zg = _dot(h, w_ref[:, nq + 6 * nkv:nq + 6 * nkv + LANES])
    gate_ref[...] = jax.nn.sigmoid(zg[:, :N_HEADS * 3])


def _nsa_compress_kernel(ck_ref, cv_ref, pwk_ref, pwv_ref, phik_ref, phiv_ref, g_ref, kc_ref, vc_ref):
    rows, wd = ck_ref.shape
    nb = rows // CMP_BLK
    pk = jnp.sum(ck_ref[...].reshape(nb, CMP_BLK, wd) * pwk_ref[...][None], axis=1)
    pv = jnp.sum(cv_ref[...].reshape(nb, CMP_BLK, wd) * pwv_ref[...][None], axis=1)
    for hd in range(N_KV):
        sl = slice(hd * HEAD_DIM, (hd + 1) * HEAD_DIM)
        kc_ref[:, sl] = (_rms(_dot(pk[:, sl].astype(BF16), phik_ref[hd])) * g_ref[...]).astype(BF16)
        vc_ref[:, sl] = _dot(pv[:, sl].astype(BF16), phiv_ref[hd]).astype(BF16)


def _nsa_compress(ck, cv, pwk, pwv, phik, phiv, gck):
    rows, wd = ck.shape
    step = CMP_ROWS if rows % CMP_ROWS == 0 else rows
    nb = step // CMP_BLK
    bc = lambda p: jnp.repeat(p, HEAD_DIM, axis=1)
    return pl.pallas_call(
        _nsa_compress_kernel, grid=(rows // step,),
        in_specs=[pl.BlockSpec((step, wd), lambda i: (i, 0)), pl.BlockSpec((step, wd), lambda i: (i, 0)),
                  _const_spec((CMP_BLK, wd)), _const_spec((CMP_BLK, wd)), _const_spec(phik.shape),
                  _const_spec(phiv.shape), _const_spec((1, HEAD_DIM))],
        out_specs=[pl.BlockSpec((nb, wd), lambda i: (i, 0)), pl.BlockSpec((nb, wd), lambda i: (i, 0))],
        out_shape=[jax.ShapeDtypeStruct((rows // CMP_BLK, wd), BF16)] * 2,
        compiler_params=_params("parallel"), name="nsa_compress",
    )(ck, cv, bc(pwk), bc(pwv), phik, phiv, gck.reshape(1, HEAD_DIM))


def _paged_compress_kernel(pp, pt_ref, *rest):
    ckp, cvp = rest[:pp], rest[pp:2 * pp]
    pwk_ref, pwv_ref, phik_ref, phiv_ref, g_ref, kc_ref, vc_ref, pk_scr, pv_scr = rest[2 * pp:]
    nb = PAGE_SIZE // CMP_BLK
    for hd in range(N_KV):
        sl = slice(hd * HEAD_DIM, (hd + 1) * HEAD_DIM)
        rows = pl.ds(hd, PAGE_SIZE, stride=N_KV)
        for i in range(pp):
            pool = lambda ref, pw: jnp.sum(ref[rows, :].reshape(nb, CMP_BLK, HEAD_DIM) * pw[:, sl][None], axis=1)
            pk_scr[hd, i * nb:(i + 1) * nb, :] = pool(ckp[i], pwk_ref)
            pv_scr[hd, i * nb:(i + 1) * nb, :] = pool(cvp[i], pwv_ref)
        kc_ref[:, sl] = (_rms(_dot(pk_scr[hd].astype(BF16), phik_ref[hd])) * g_ref[...]).astype(BF16)
        vc_ref[:, sl] = _dot(pv_scr[hd].astype(BF16), phiv_ref[hd]).astype(BF16)


def _paged_compress(ck_pool, cv_pool, page0, page_table, pwk, pwv, phik, phiv, gck):
    bsz, n_pages = page_table.shape
    pp = PAGES_PER_STEP
    assert n_pages % pp == 0
    n_steps = n_pages // pp
    nb = pp * PAGE_SIZE // CMP_BLK
    wd = N_KV * HEAD_DIM
    prow = PAGE_SIZE * N_KV
    bc = lambda p: jnp.repeat(p, HEAD_DIM, axis=1)
    page = lambda i: pl.BlockSpec((prow, HEAD_DIM), lambda b, s, pt: (page0 + pt[b, s * pp + i], 0))
    const = lambda shape: pl.BlockSpec(shape, lambda b, s, pt: (0,) * len(shape))
    out = pl.BlockSpec((nb, wd), lambda b, s, pt: (b * n_steps + s, 0))
    return pl.pallas_call(
        functools.partial(_paged_compress_kernel, pp),
        grid_spec=pltpu.PrefetchScalarGridSpec(
            num_scalar_prefetch=1, grid=(bsz, n_steps),
            in_specs=[page(i) for i in range(pp)] * 2
            + [const((CMP_BLK, wd)), const((CMP_BLK, wd)), const(phik.shape), const(phiv.shape),
               const((1, HEAD_DIM))],
            out_specs=[out, out],
            scratch_shapes=[pltpu.VMEM((N_KV, nb, HEAD_DIM), F32), pltpu.VMEM((N_KV, nb, HEAD_DIM), F32)]),
        out_shape=[jax.ShapeDtypeStruct((bsz * n_steps * nb, wd), BF16)] * 2,
        compiler_params=_params("parallel", "parallel"), name="paged_compress",
    )(page_table, *([ck_pool] * pp), *([cv_pool] * pp), bc(pwk), bc(pwv), phik, phiv,
      gck.reshape(1, HEAD_DIM))


def _nsa_cmp_kernel(tq, q_pos0, n_sel, qc_ref, kc_ref, vc_ref, oc_ref, bm_ref):
    p0 = q_pos0 + pl.program_id(1) * tq
    nc = kc_ref.shape[0]
    rowpos = p0 + lax.broadcasted_iota(I32, (tq, nc), 0)
    blk = lax.broadcasted_iota(I32, (tq, nc), 1)
    valid = jnp.concatenate([blk * CMP_BLK + (CMP_BLK - 1) <= rowpos] * GROUP, axis=0)
    cur = lax.shift_right_logical(rowpos, SLC_SHIFT)
    forced = (blk == 0) | (blk == cur) | (blk == cur - 1)
    keys = []
    for g in range(N_KV):
        sl = slice(g * HEAD_DIM, (g + 1) * HEAD_DIM)
        qs = jnp.concatenate([qc_ref[:, (g * GROUP + h) * HEAD_DIM:(g * GROUP + h + 1) * HEAD_DIM]
                              for h in range(GROUP)], axis=0)
        lc = jnp.where(valid, _dot_nt(qs, kc_ref[:, sl]), NEG)
        m = jnp.max(lc, axis=-1, keepdims=True)
        m = jnp.where(m > 0.5 * NEG, m, 0.0)
        p = jnp.where(valid, jnp.exp(lc - m), 0.0)
        pc = p / jnp.maximum(jnp.sum(p, axis=-1, keepdims=True), SUM_FLOOR)
        oc = _dot(pc.astype(BF16), vc_ref[:, sl])
        for h in range(GROUP):
            c0 = (g * GROUP + h) * HEAD_DIM
            oc_ref[:, c0:c0 + HEAD_DIM] = oc[h * tq:(h + 1) * tq, :]
        imp = pc[0:tq]
        for h in range(1, GROUP):
            imp = imp + pc[h * tq:(h + 1) * tq]
        imp = jnp.where(forced, BIG, imp)
        imp = jnp.where(blk > cur, -jnp.inf, imp)
        keys.append(_sort_key(imp))
    key = jnp.concatenate(keys, axis=0)
    count_ge = lambda cand: jnp.sum(jnp.where(key >= cand, 1.0, 0.0), axis=-1, keepdims=True)
    sel = jnp.where(key >= _kth_largest(count_ge, float(n_sel), (N_KV * tq, 1)), 1.0, 0.0).astype(BF16)
    for g in range(N_KV):
        bm_ref[:, g * LANES:(g + 1) * LANES] = sel[g * tq:(g + 1) * tq, :]


def _nsa_cmp(qc, kc, vc, *, bsz, tq, q_pos0, n_sel):
    nq_rows, dq = qc.shape
    nc = kc.shape[1]
    assert nc == LANES
    nq = nq_rows // bsz // tq
    return pl.pallas_call(
        functools.partial(_nsa_cmp_kernel, tq, q_pos0, n_sel), grid=(bsz, nq),
        in_specs=[pl.BlockSpec((tq, dq), lambda b, j: (b * nq + j, 0)),
                  pl.BlockSpec((None, nc, kc.shape[2]), lambda b, j: (b, 0, 0)),
                  pl.BlockSpec((None, nc, vc.shape[2]), lambda b, j: (b, 0, 0))],
        out_specs=[pl.BlockSpec((tq, dq), lambda b, j: (b * nq + j, 0)),
                   pl.BlockSpec((tq, N_KV * LANES), lambda b, j: (b * nq + j, 0))],
        out_shape=[jax.ShapeDtypeStruct((nq_rows, dq), F32),
                   jax.ShapeDtypeStruct((nq_rows, N_KV * LANES), BF16)],
        compiler_params=_params("parallel", "parallel"), name="nsa_cmp",
    )(qc, kc, vc)


def _nsa_combine(gate, oc, os_, ow):
    parts = []
    for hd in range(N_HEADS):
        sl = slice(hd * HEAD_DIM, (hd + 1) * HEAD_DIM)
        parts.append(gate[:, 3 * hd:3 * hd + 1] * oc[:, sl] + gate[:, 3 * hd + 1:3 * hd + 2] * os_[:, sl]
                     + gate[:, 3 * hd + 2:3 * hd + 3] * ow[:, sl])
    return jnp.concatenate(parts, axis=1)


SAMPLE_TQ = 16
PROMPT_TQ = 128
PROMPT_ATTN_TQ = 512
PROMPT_CMP_TQ = 512


def _wide_tile(t, want, base):
    return want if t % want == 0 else base
KEY_ALIGN = ATTN_KEY_TILE


def _pad_rows(a, bsz, t, tp):
    return jnp.pad(a.reshape(bsz, t, -1), ((0, 0), (0, tp - t), (0, 0))).reshape(bsz * tp, -1)


def _unpad_rows(a, bsz, t, tp):
    return a.reshape(bsz, tp, -1)[:, :t].reshape(bsz * t, -1)


def _cat_keys(old, new, bsz):
    new = new.reshape(bsz, -1, new.shape[-1])
    n = old.shape[1] + new.shape[1]
    return jnp.pad(jnp.concatenate([old.astype(new.dtype), new], axis=1),
                   ((0, 0), (0, -n % KEY_ALIGN), (0, 0)))


def _tile_tables(tabs, bsz, t):
    return tabs if t % ROW_TILE == 0 else [jnp.tile(a, (bsz, 1)) for a in tabs]


def _mixer_c(x, bsz, pos, past, g, wcat, gq, gk, gik, wo):
    n, d = x.shape
    t = n // bsz
    tabs = _tile_tables(_rope_tables_partial(pos, HEAD_DIM, ROPE_DIM, ROPE_THETA)
                        + _rope_tables_partial(pos, IDX_DIM, IDX_ROPE, ROPE_THETA), bsz, t)
    nkv = N_KV * HEAD_DIM
    q, k, kb, v, vb, qi, ki, wi = _proj(
        x, g, wcat, tabs, [gq, gk, gik], _dsa_epilogue,
        [(_rows(N_HEADS * HEAD_DIM), BF16), (_heads(nkv), F32), (_rows(nkv), BF16), (_heads(nkv), F32),
         (_rows(nkv), BF16),
         (lambda n_, tm: ((IDX_HEADS, n_, IDX_DIM), (IDX_HEADS, tm, IDX_DIM), lambda i: (0, i, 0)), BF16),
         (_rows(IDX_DIM), F32), (_rows(IDX_HEADS), F32)], tabs[0].shape[0], "proj_dsa")
    if past is None:
        tq, q_pos0, n_keys = PROMPT_TQ, 0, t
        kit = ki.reshape(bsz, t, IDX_DIM).swapaxes(1, 2).astype(BF16)
        kall, vall = kb.reshape(bsz, t, nkv), vb.reshape(bsz, t, nkv)
    else:
        tq, q_pos0 = SAMPLE_TQ, past["kidx"].shape[1]
        n_keys = q_pos0 + t
        kit = _cat_keys(past["kidx"], ki, bsz).swapaxes(1, 2).astype(BF16)
        q, wi = _pad_rows(q, bsz, t, tq), _pad_rows(wi, bsz, t, tq)
        qi = jnp.pad(qi.reshape(IDX_HEADS, bsz, t, IDX_DIM), ((0, 0), (0, 0), (0, tq - t), (0, 0))
                     ).reshape(IDX_HEADS, bsz * tq, IDX_DIM)
    bias = _dsa_select(qi, wi, kit, bsz=bsz, tq=tq, q_pos0=q_pos0, top=min(TOPK_MAX, n_keys // 4))
    if past is None:
        o = _attention(q, kall, vall, bias, bsz=bsz, tq=_wide_tile(t, PROMPT_ATTN_TQ, tq), q_pos0=q_pos0,
                       k_pos0=0, mode="mask")
    else:
        new = lambda a: _pad_rows(a.reshape(n, nkv), bsz, t, NEW_TILE).astype(BF16)
        o = _paged_attention(q, past["k_pool"], past["v_pool"], past["page0"], past["pt"], new(k), new(v), bias,
                             bsz=bsz, tq=tq, mode="mask")
        o = _unpad_rows(o, bsz, t, tq)
    return _outproj(x, wo, [o], lambda a: a), k, v, ki


def _mixer_d(x, bsz, pos, past, g, wcat, gq, gsk, gwk, cmpw, wo):
    n, d = x.shape
    t = n // bsz
    tabs = _tile_tables(_rope_tables_partial(pos, HEAD_DIM, ROPE_DIM, ROPE_THETA), bsz, t)
    nkv = N_KV * HEAD_DIM
    kvo = [(_heads(nkv), F32), (_rows(nkv), BF16)]
    (qc, q, ck, cv, sk, skb, sv, svb, wk, wkb, wv, wvb, gate) = _proj(
        x, g, wcat, tabs, [gq, gsk, gwk], _nsa_epilogue,
        [(_rows(N_HEADS * HEAD_DIM), BF16), (_rows(N_HEADS * HEAD_DIM), BF16), (_rows(nkv), F32),
         (_rows(nkv), F32)] + kvo * 4 + [(_rows(N_HEADS * 3), F32)], tabs[0].shape[0], "proj_nsa")
    if past is None:
        tq, q_pos0, n_keys, win_pos0 = PROMPT_TQ, 0, t, 0
        kc, vc = _nsa_compress(ck, cv, *cmpw)
        three = lambda a: a.reshape(bsz, t, nkv)
        skall, svall, wkall, wvall = three(skb), three(svb), three(wkb), three(wvb)
    else:
        tq, q_pos0 = SAMPLE_TQ, past["pt"].shape[1] * PAGE_SIZE
        n_keys = q_pos0 + t
        win_pos0 = q_pos0 - past["wk"].shape[1]
        assert q_pos0 % CMP_BLK == 0 and t < CMP_BLK
        kc, vc = _paged_compress(past["ck_pool"], past["cv_pool"], past["page0"], past["pt"], *cmpw)
        wkall = _cat_keys(past["wk"], wk.reshape(n, nkv), bsz).astype(BF16)
        wvall = _cat_keys(past["wv"], wv.reshape(n, nkv), bsz).astype(BF16)
        qc, q = _pad_rows(qc, bsz, t, tq), _pad_rows(q, bsz, t, tq)
    n_cmp = kc.shape[0] // bsz
    assert n_cmp <= LANES
    lane_pad = lambda a: jnp.pad(a.reshape(bsz, n_cmp, nkv), ((0, 0), (0, LANES - n_cmp), (0, 0)))
    kc, vc = lane_pad(kc), lane_pad(vc)
    n_slc = -(-n_keys // SLC_BLK)
    n_lane = min(n_slc, LANES)
    assert n_slc == n_lane or (n_slc == n_lane + 1 and q_pos0 // SLC_BLK == n_lane)
    n_sel = min(N_SLC, n_slc) - (n_slc - n_lane)
    tq_cmp = _wide_tile(t, PROMPT_CMP_TQ, tq) if past is None else tq
    oc, bm = _nsa_cmp(qc, kc, vc, bsz=bsz, tq=tq_cmp, q_pos0=q_pos0, n_sel=n_sel)
    if past is None:
        os_ = _attention(q, skall, svall, bm, bsz=bsz, tq=_wide_tile(t, PROMPT_ATTN_TQ, tq), q_pos0=q_pos0,
                         k_pos0=0, mode="blocksel", n_cmp=n_lane)
    else:
        new = lambda a: _pad_rows(a.reshape(n, nkv), bsz, t, NEW_TILE).astype(BF16)
        os_ = _paged_attention(q, past["sk_pool"], past["sv_pool"], past["page0"], past["pt"], new(sk), new(sv),
                               bm, bsz=bsz, tq=tq, mode="blocksel")
    ow = _attention(q, wkall, wvall, None, bsz=bsz, tq=tq if past else _wide_tile(t, PROMPT_ATTN_TQ, tq),
                    q_pos0=q_pos0, k_pos0=win_pos0, mode="window")
    if past is not None:
        oc, os_, ow = (_unpad_rows(a, bsz, t, tq) for a in (oc, os_, ow))
    return _outproj(x, wo, [gate, oc, os_, ow], _nsa_combine), (ck, cv, sk, sv, wk, wv)


def _gather_pages(cache, page_table):
    rows = cache[page_table]
    return rows.reshape(page_table.shape[0], page_table.shape[1] * PAGE_SIZE, -1)


def kernel(x_prompt, x_sample, state_a_ret, state_b_h, state_b_conv, cache_c_k, cache_c_v, cache_c_kidx,
           cache_d_ck, cache_d_cv, cache_d_sk, cache_d_sv, state_d_wk, state_d_wv, page_table,
           ffn1_norm, ffn1_wg, ffn1_wu, ffn1_wd, mix_norm, ffn2_norm, ffn2_wg, ffn2_wu, ffn2_wd,
           a_wq, a_wk, a_wv, a_wg, a_wo,
           b_wy, b_by, b_wx, b_bx, b_conv_w, b_conv_b, b_wa, b_ba, b_wi, b_bi, b_lam, b_wo,
           c_wq, c_gq, c_wk, c_gk, c_wv, c_wo, c_wiq, c_wik, c_gik, c_wiw,
           d_wq, d_gq, d_wck, d_wcv, d_pwk, d_pwv, d_phik, d_phiv, d_gck, d_wsk, d_wsv, d_gsk,
           d_wwk, d_wwv, d_gwk, d_wgate, d_wo):
    bp, tp, d = x_prompt.shape
    bs, ts, _ = x_sample.shape
    depth = ffn1_norm.shape[0]
    past_len = page_table.shape[1] * PAGE_SIZE
    pos_p = jnp.arange(tp, dtype=I32)
    pos_s = past_len + jnp.arange(ts, dtype=I32)
    b16 = lambda a: a.astype(BF16)
    row = lambda a: a.reshape(1, -1)
    nkv = N_KV * HEAD_DIM
    pool = lambda c: c.reshape(-1, HEAD_DIM)
    ffn1_w = [_to_bf16(w) for w in (ffn1_wg, ffn1_wu, ffn1_wd)]
    ffn2_w = [_to_bf16(w) for w in (ffn2_wg, ffn2_wu, ffn2_wd)]

    xp = x_prompt.reshape(bp * tp, d)
    xs = x_sample.reshape(bs * ts, d)
    outs = [[] for _ in range(24)]
    for i in range(depth):
        m, j = i % 4, i // 4
        xp, xs = _ffn(xp, ffn1_norm[i], *ffn1_w, i), _ffn(xs, ffn1_norm[i], *ffn1_w, i)
        g = mix_norm[i]
        if m == 0:
            dk = a_wq.shape[2] // RET_HEADS
            dv = a_wv.shape[2] // RET_HEADS
            wcat = b16(jnp.concatenate([a_wq[j], a_wk[j], a_wv[j], a_wg[j]], axis=1))
            wo = b16(a_wo[j])
            xp, s_p = _mixer_a(xp, g, pos_p, bp, jnp.zeros((bp, RET_HEADS, dk, dv), F32), wcat, wo, dk, dv)
            xs, s_s = _mixer_a(xs, g, pos_s, bs, state_a_ret[j], wcat, wo, dk, dv)
            new = [s_p, s_s]
            base = 0
        elif m == 1:
            common = (g, b16(jnp.concatenate([b_wy[j], b_wx[j]], axis=1)), jnp.concatenate([b_by[j], b_bx[j]]),
                      b_conv_w[j], b_conv_b[j], b16(b_wa[j]), b_ba[j].reshape(-1), b16(b_wi[j]),
                      b_bi[j].reshape(-1), jax.nn.softplus(-b_lam[j]), b16(b_wo[j]))
            xp, h_p, c_p = _lru_prompt(xp, bp, *common)
            xs, h_s, c_s = _lru_sample(xs, bs, state_b_h[j], state_b_conv[j], *common)
            new = [h_p, h_s, c_p, c_s]
            base = 2
        elif m == 2:
            zpad = jnp.zeros((d, LANES - IDX_DIM - IDX_HEADS), F32)
            wcat = b16(jnp.concatenate([c_wq[j], c_wk[j], c_wv[j], c_wiq[j], c_wik[j], c_wiw[j], zpad], axis=1))
            gik = jnp.concatenate([c_gik[j], jnp.zeros((LANES - IDX_DIM,), F32)])
            cw = (g, wcat, row(c_gq[j]), row(c_gk[j]), row(gik), b16(c_wo[j]))
            past = dict(kidx=_gather_pages(cache_c_kidx[j], page_table), k_pool=pool(cache_c_k),
                        v_pool=pool(cache_c_v), page0=j * cache_c_k.shape[1], pt=page_table)
            xp, k_p, v_p, i_p = _mixer_c(xp, bp, pos_p, None, *cw)
            xs, k_s, v_s, i_s = _mixer_c(xs, bs, pos_s, past, *cw)
            kv = lambda a, b_, t_: a.reshape(b_, t_, N_KV, HEAD_DIM)
            new = [kv(k_p, bp, tp), kv(k_s, bs, ts), kv(v_p, bp, tp), kv(v_s, bs, ts),
                   i_p.reshape(bp, tp, IDX_DIM), i_s.reshape(bs, ts, IDX_DIM)]
            base = 6
        else:
            zpad = jnp.zeros((d, LANES - N_HEADS * 3), F32)
            wcat = b16(jnp.concatenate([d_wq[j], d_wck[j], d_wcv[j], d_wsk[j], d_wsv[j], d_wwk[j], d_wwv[j],
                                        d_wgate[j], zpad], axis=1))
            cmpw = (d_pwk[j], d_pwv[j], b16(d_phik[j]), b16(d_phiv[j]), d_gck[j])
            dw = (g, wcat, row(d_gq[j]), row(d_gsk[j]), row(d_gwk[j]), cmpw, b16(d_wo[j]))
            wbuf = state_d_wk.shape[2]
            past = dict(ck_pool=pool(cache_d_ck), cv_pool=pool(cache_d_cv), sk_pool=pool(cache_d_sk),
                        sv_pool=pool(cache_d_sv), page0=j * cache_d_ck.shape[1], pt=page_table,
                        wk=state_d_wk[j].reshape(bs, wbuf, nkv), wv=state_d_wv[j].reshape(bs, wbuf, nkv))
            xp, rows_p = _mixer_d(xp, bp, pos_p, None, *dw)
            xs, rows_s = _mixer_d(xs, bs, pos_s, past, *dw)
            kv = lambda a, b_: a.reshape(b_, -1, N_KV, HEAD_DIM)
            new = []
            for a_p, a_s in zip(rows_p[:4], rows_s[:4]):
                new += [kv(a_p, bp), kv(a_s, bs)]
            wb_p = min(WINDOW, tp)
            for a_p, a_s, st in ((rows_p[4], rows_s[4], state_d_wk[j]), (rows_p[5], rows_s[5], state_d_wv[j])):
                new += [kv(a_p, bp)[:, tp - wb_p:], jnp.concatenate([st, kv(a_s, bs)], axis=1)[:, ts:]]
            base = 12
        for off, a in enumerate(new):
            outs[base + off].append(a)
        xp, xs = _ffn(xp, ffn2_norm[i], *ffn2_w, i), _ffn(xs, ffn2_norm[i], *ffn2_w, i)
    return (xp.reshape(bp, tp, d), xs.reshape(bs, ts, d)) + tuple(jnp.stack(o) for o in outs)
```

```python
import functools
import math

import jax
import jax.numpy as jnp
from jax import lax
from jax.experimental import pallas as pl
from jax.experimental.pallas import tpu as pltpu

F32 = jnp.float32
BF16 = jnp.bfloat16
I32 = jnp.int32

EPS = 1e-6
PAGE_SIZE = 128

RET_HEADS = 4
RET_CHUNK = 128
RET_THETA = 10000.0

LRU_BLOCKS = 4
CONV_W = 4
LRU_C = 8.0

N_HEADS = 8
HEAD_DIM = 128
N_KV = 2
GROUP = N_HEADS // N_KV
ROPE_DIM = HEAD_DIM // 4
ROPE_THETA = 500000.0

IDX_HEADS = 16
IDX_DIM = 64
IDX_ROPE = IDX_DIM // 4
TOPK_MAX = 256

CMP_BLK = 64
SLC_BLK = 64
N_SLC = 16
WINDOW = 512
BIG = 1e4

LANES = 128
SUBLANES = 8
VMEM_LIMIT = 56 * 1024 * 1024
ROW_TILE = 512
LRU_TILE = 256
ATTN_KEY_TILE = 512
SELECT_KEY_TILE = 256
CMP_ROWS = 2048
SLC_SHIFT = 6
SUM_FLOOR = 1e-30
NEG = -1e30
ATTN_SCALE = HEAD_DIM ** -0.5
LOG2E = math.log2(math.e)
INT_MIN = -2147483648


def _params(*sem):
    return pltpu.CompilerParams(dimension_semantics=sem, vmem_limit_bytes=VMEM_LIMIT)


def _const_spec(shape):
    nd = len(shape)
    return pl.BlockSpec(shape, lambda *_: (0,) * nd, pipeline_mode=pl.Buffered(1))


def _rms(x):
    return x * lax.rsqrt(jnp.mean(x * x, axis=-1, keepdims=True) + EPS)


def _dot(a, b):
    return jnp.dot(a, b, preferred_element_type=F32)


def _dot_nt(a, b):
    return lax.dot_general(a, b, (((1,), (1,)), ((), ())), preferred_element_type=F32)


def _row_tile(n):
    return ROW_TILE if n % ROW_TILE == 0 else n


def _ffn_kernel(x_ref, g_ref, wg_ref, wu_ref, wd_ref, o_ref):
    x = x_ref[...]
    h = (_rms(x) * g_ref[...]).astype(BF16)
    gt = _dot(h, wg_ref[...])
    ut = _dot(h, wu_ref[...])
    a = (gt * jax.nn.sigmoid(gt) * ut).astype(BF16)
    o_ref[...] = x + 0.5 * _dot(a, wd_ref[...])


def _ffn(x, g, wg, wu, wd, layer):
    n, d = x.shape
    f = wg.shape[2]
    tm = _row_tile(n)
    wspec = lambda a, b_: pl.BlockSpec((None, a, b_), lambda i: (layer, 0, 0), pipeline_mode=pl.Buffered(1))
    return pl.pallas_call(
        _ffn_kernel,
        grid=(n // tm,),
        in_specs=[pl.BlockSpec((tm, d), lambda i: (i, 0)), _const_spec((1, d)),
                  wspec(d, f), wspec(d, f), wspec(f, d)],
        out_specs=pl.BlockSpec((tm, d), lambda i: (i, 0)),
        out_shape=jax.ShapeDtypeStruct((n, d), F32),
        compiler_params=_params("parallel"), name="ffn",
    )(x, g.reshape(1, d), wg, wu, wd)


def _cast_kernel(x_ref, o_ref):
    o_ref[...] = x_ref[...].astype(o_ref.dtype)


def _to_bf16(w):
    c = w.shape[-1]
    r = w.size // c
    tm = _row_tile(r)
    out = pl.pallas_call(
        _cast_kernel, grid=(r // tm,),
        in_specs=[pl.BlockSpec((tm, c), lambda i: (i, 0))],
        out_specs=pl.BlockSpec((tm, c), lambda i: (i, 0)),
        out_shape=jax.ShapeDtypeStruct((r, c), BF16),
        compiler_params=_params("parallel"), name="to_bf16",
    )(w.reshape(r, c))
    return out.reshape(w.shape)


def _proj(x, g, w, tables, consts, epilogue, outs, n_tab_rows, name):
    n, d = x.shape
    tm = _row_tile(n)
    assert n_tab_rows % tm == 0
    nt = n_tab_rows // tm
    nw = w.shape[1]

    def kern(x_ref, g_ref, w_ref, *rest):
        tabs = rest[:len(tables)]
        cs = rest[len(tables):len(tables) + len(consts)]
        o = rest[len(tables) + len(consts):]
        h = (_rms(x_ref[...]) * g_ref[...]).astype(BF16)
        epilogue(h, w_ref, tabs, cs, o)

    in_specs = [pl.BlockSpec((tm, d), lambda i: (i, 0)), _const_spec((1, d)), _const_spec((d, nw))]
    in_specs += [pl.BlockSpec((tm, t.shape[1]), lambda i: (i % nt, 0)) for t in tables]
    in_specs += [_const_spec(c.shape) for c in consts]
    out_shape, out_specs = [], []
    for fn, dt in outs:
        ashape, bshape, imap = fn(n, tm)
        out_shape.append(jax.ShapeDtypeStruct(ashape, dt))
        out_specs.append(pl.BlockSpec(bshape, imap))
    return pl.pallas_call(
        kern, grid=(n // tm,), in_specs=in_specs, out_specs=out_specs, out_shape=out_shape,
        compiler_params=_params("parallel"), name=name,
    )(x, g.reshape(1, d), w, *tables, *consts)


def _rows(c):
    return lambda n, tm: ((n, c), (tm, c), lambda i: (i, 0))


def _outproj(x, w, ins, combine):
    n, d = x.shape
    tm = _row_tile(n)

    def kern(x_ref, w_ref, *rest):
        o_ref = rest[-1]
        a = combine(*[r[...] for r in rest[:-1]]).astype(BF16)
        o_ref[...] = x_ref[...] + _dot(a, w_ref[...])

    return pl.pallas_call(
        kern, grid=(n // tm,),
        in_specs=[pl.BlockSpec((tm, d), lambda i: (i, 0)), _const_spec(w.shape)]
        + [pl.BlockSpec((tm, a.shape[1]), lambda i: (i, 0)) for a in ins],
        out_specs=pl.BlockSpec((tm, d), lambda i: (i, 0)),
        out_shape=jax.ShapeDtypeStruct((n, d), F32),
        compiler_params=_params("parallel"), name="outproj",
    )(x, w, *ins)


def _ret_epilogue(dk, dv, h, w_ref, tabs, cs, o):
    cos, sin = tabs[0][...], tabs[1][...]
    q_ref, k_ref, v_ref, g_ref = o
    half = dk // 2
    nqk = RET_HEADS * dk
    for seg, ref, scale in ((0, q_ref, 1.0), (1, k_ref, dk ** -0.5)):
        z = _dot(h, w_ref[:, seg * nqk:(seg + 1) * nqk])
        for hd in range(RET_HEADS):
            x1 = z[:, hd * dk:hd * dk + half]
            x2 = z[:, hd * dk + half:(hd + 1) * dk]
            ref[:, hd * dk:hd * dk + half] = (x1 * cos - x2 * sin) * scale
            ref[:, hd * dk + half:(hd + 1) * dk] = (x2 * cos + x1 * sin) * scale
    nv = RET_HEADS * dv
    v_ref[...] = _dot(h, w_ref[:, 2 * nqk:2 * nqk + nv]).astype(BF16)
    zg = _dot(h, w_ref[:, 2 * nqk + nv:2 * nqk + 2 * nv])
    g_ref[...] = zg * jax.nn.sigmoid(zg)


def _ret_chunk_kernel(q_ref, k_ref, v_ref, s0_ref, dm_ref, xi_ref, zt_ref, gc_ref, o_ref, s_ref):
    c = pl.program_id(2)

    @pl.when(c == 0)
    def _():
        s_ref[...] = s0_ref[...]

    nh, dk, dv = s_ref.shape
    for hd in range(nh):
        q = q_ref[:, hd * dk:(hd + 1) * dk]
        k = k_ref[:, hd * dk:(hd + 1) * dk]
        v = v_ref[:, hd * dv:(hd + 1) * dv]
        s = s_ref[hd]
        att = (_dot_nt(q.astype(BF16), k.astype(BF16)) * dm_ref[hd]).astype(BF16)
        o = _dot(att, v) + _dot((q * xi_ref[hd]).astype(BF16), s.astype(BF16))
        kz = (k * zt_ref[hd]).astype(BF16)
        s_ref[hd] = s * gc_ref[hd] + lax.dot_general(kz, v, (((0,), (0,)), ((), ())),
                                                     preferred_element_type=F32)
        o_ref[:, hd * dv:(hd + 1) * dv] = _rms(o)


def _retention(q, k, v, s0, n_valid):
    b, hh, dk, dv = s0.shape
    t = q.shape[0] // b
    c = RET_CHUNK
    nc = t // c
    cc = c if n_valid % c == 0 else n_valid
    lg = jnp.log1p(-jnp.exp2(-5.0 - jnp.arange(hh, dtype=F32)))
    idx = jnp.arange(c, dtype=F32)
    diff = idx[:, None] - idx[None, :]
    dmask = jnp.where(diff >= 0, jnp.exp(lg[:, None, None] * jnp.maximum(diff, 0.0)), 0.0)
    xi = jnp.exp(lg[:, None] * (idx + 1.0))[:, :, None]
    zeta = jnp.exp(lg[:, None] * (cc - 1.0 - idx))[:, :, None]
    g_c = jnp.exp(lg * cc)[:, None, None]
    nh = 2 if hh % 2 == 0 else 1
    return pl.pallas_call(
        _ret_chunk_kernel,
        grid=(b, hh // nh, nc),
        in_specs=[pl.BlockSpec((c, nh * dk), lambda i, j, l: (i * nc + l, j)),
                  pl.BlockSpec((c, nh * dk), lambda i, j, l: (i * nc + l, j)),
                  pl.BlockSpec((c, nh * dv), lambda i, j, l: (i * nc + l, j)),
                  pl.BlockSpec((None, nh, dk, dv), lambda i, j, l: (i, j, 0, 0)),
                  pl.BlockSpec((nh, c, c), lambda i, j, l: (j, 0, 0)),
                  pl.BlockSpec((nh, c, 1), lambda i, j, l: (j, 0, 0)),
                  pl.BlockSpec((nh, c, 1), lambda i, j, l: (j, 0, 0)),
                  pl.BlockSpec((nh, 1, 1), lambda i, j, l: (j, 0, 0))],
        out_specs=[pl.BlockSpec((c, nh * dv), lambda i, j, l: (i * nc + l, j)),
                   pl.BlockSpec((None, nh, dk, dv), lambda i, j, l: (i, j, 0, 0))],
        out_shape=[jax.ShapeDtypeStruct((b * t, hh * dv), F32),
                   jax.ShapeDtypeStruct((b, hh, dk, dv), F32)],
        compiler_params=_params("parallel", "parallel", "arbitrary"), name="retention",
    )(q, k, v, s0, dmask, xi, zeta, g_c)


def _rope_tables_full(pos, dim, theta):
    half = dim // 2
    inv = theta ** (-jnp.arange(half, dtype=F32) / half)
    ang = pos.astype(F32)[:, None] * inv[None, :]
    return jnp.cos(ang), jnp.sin(ang)


def _mixer_a(x, g, pos, bsz, s0, wcat, wo, dk, dv):
    n, d = x.shape
    t = n // bsz
    cos, sin = _rope_tables_full(pos, dk, RET_THETA)
    if t % RET_CHUNK:
        cos = jnp.tile(cos, (bsz, 1))
        sin = jnp.tile(sin, (bsz, 1))
    q, k, v, sg = _proj(
        x, g, wcat, [cos, sin], [], functools.partial(_ret_epilogue, dk, dv),
        [(_rows(RET_HEADS * dk), F32), (_rows(RET_HEADS * dk), F32),
         (_rows(RET_HEADS * dv), BF16), (_rows(RET_HEADS * dv), F32)], cos.shape[0], "proj_ret")
    if t % RET_CHUNK:
        pad = lambda a: jnp.pad(a.reshape(bsz, t, -1), ((0, 0), (0, RET_CHUNK - t), (0, 0))
                                ).reshape(bsz * RET_CHUNK, -1)
        o, s_new = _retention(pad(q), pad(k), pad(v), s0, t)
        o = o.reshape(bsz, RET_CHUNK, -1)[:, :t].reshape(n, -1)
    else:
        o, s_new = _retention(q, k, v, s0, t)
    return _outproj(x, wo, [sg, o], lambda a, b: a * b), s_new


def _gelu_tanh(x):
    return 0.5 * x * (1.0 + jnp.tanh(math.sqrt(2.0 / math.pi) * (x + 0.044715 * (x * x * x))))


def _lru_gates(xb, wa_ref, ba, wi_ref, bi, sp):
    bw = wa_ref.shape[1]
    xbb = xb.astype(BF16)
    r = jnp.concatenate([_dot(xbb[:, n * bw:(n + 1) * bw], wa_ref[n]) for n in range(LRU_BLOCKS)], axis=1)
    i = jnp.concatenate([_dot(xbb[:, n * bw:(n + 1) * bw], wi_ref[n]) for n in range(LRU_BLOCKS)], axis=1)
    r = jax.nn.sigmoid(r + ba)
    i = jax.nn.sigmoid(i + bi)
    log_a = -LRU_C * r * sp
    a = jnp.exp(log_a)
    th = jnp.tanh(log_a)
    bt = jnp.sqrt(-2.0 * th / (1.0 - th)) * (i * xb)
    return a, bt


def _lru_prompt_kernel(x_ref, g_ref, wyx_ref, byx_ref, cw_ref, cb_ref, wa_ref, ba_ref, wi_ref, bi_ref,
                       sp_ref, wo_ref, o_ref, hl_ref, cn_ref, xpad_ref, a_ref, b_ref, hs_ref, h_ref):
    j = pl.program_id(1)
    tm, dr = a_ref.shape
    pad = xpad_ref.shape[0] - tm

    @pl.when(j == 0)
    def _():
        h_ref[...] = jnp.zeros_like(h_ref)
        xpad_ref[0:pad, :] = jnp.zeros((pad, dr), F32)

    x = x_ref[...]
    h = (_rms(x) * g_ref[...]).astype(BF16)
    z = _dot(h, wyx_ref[...]) + byx_ref[...]
    gate = _gelu_tanh(z[:, :dr])
    xpad_ref[pad:, :] = z[:, dr:]
    xb = cb_ref[...] + z[:, dr:] * cw_ref[CONV_W - 1:CONV_W, :]
    for s in range(1, CONV_W):
        xb = xb + xpad_ref[pad - s:pad - s + tm, :] * cw_ref[CONV_W - 1 - s:CONV_W - s, :]
    xpad_ref[0:pad, :] = xpad_ref[tm:tm + pad, :]
    a, bt = _lru_gates(xb, wa_ref, ba_ref[...], wi_ref, bi_ref[...], sp_ref[...])
    a_ref[...] = a
    b_ref[...] = bt

    def step(t, hprev):
        hn = a_ref[pl.ds(t, 1), :] * hprev + b_ref[pl.ds(t, 1), :]
        hs_ref[pl.ds(t, 1), :] = hn
        return hn

    hlast = lax.fori_loop(0, tm, step, h_ref[...])
    h_ref[...] = hlast
    o_ref[...] = x + _dot((gate * hs_ref[...]).astype(BF16), wo_ref[...])
    hl_ref[...] = hlast
    cn_ref[...] = xpad_ref[pad - (CONV_W - 1):pad, :]


def _lru_prompt(x, bsz, g, wyx, byx, cw, cb, wa, ba, wi, bi, sp, wo):
    n, d = x.shape
    t = n // bsz
    dr = wo.shape[0]
    tm = LRU_TILE if t % LRU_TILE == 0 else t
    nt = t // tm
    pad = SUBLANES
    vec = lambda a: a.reshape(1, -1)
    out, hl, cn = pl.pallas_call(
        _lru_prompt_kernel,
        grid=(bsz, nt),
        in_specs=[pl.BlockSpec((tm, d), lambda i, j: (i * nt + j, 0)), _const_spec((1, d)),
                  _const_spec(wyx.shape), _const_spec((1, 2 * dr)), _const_spec(cw.shape),
                  _const_spec((1, dr)), _const_spec(wa.shape), _const_spec((1, dr)),
                  _const_spec(wi.shape), _const_spec((1, dr)), _const_spec((1, dr)),
                  _const_spec(wo.shape)],
        out_specs=[pl.BlockSpec((tm, d), lambda i, j: (i * nt + j, 0)),
                   pl.BlockSpec((None, 1, dr), lambda i, j: (i, 0, 0)),
                   pl.BlockSpec((None, CONV_W - 1, dr), lambda i, j: (i, 0, 0))],
        out_shape=[jax.ShapeDtypeStruct((n, d), F32), jax.ShapeDtypeStruct((bsz, 1, dr), F32),
                   jax.ShapeDtypeStruct((bsz, CONV_W - 1, dr), F32)],
        scratch_shapes=[pltpu.VMEM((tm + pad, dr), F32), pltpu.VMEM((tm, dr), F32),
                        pltpu.VMEM((tm, dr), F32), pltpu.VMEM((tm, dr), F32), pltpu.VMEM((1, dr), F32)],
        compiler_params=_params("parallel", "arbitrary"), name="lru_prompt",
    )(x, vec(g), wyx, vec(byx), cw, vec(cb), wa, vec(ba), wi, vec(bi), vec(sp), wo)
    return out, hl.reshape(bsz, dr), cn


def _lru_sample_kernel(bsz, x_ref, g_ref, buf_ref, h0_ref, wyx_ref, byx_ref, cw_ref, cb_ref, wa_ref, ba_ref,
                       wi_ref, bi_ref, sp_ref, wo_ref, o_ref, hl_ref, cn_ref, xpad_ref, hs_ref):
    n, dr = hs_ref.shape
    t = n // bsz
    nb = (CONV_W - 1) * bsz
    x = x_ref[...]
    h = (_rms(x) * g_ref[...]).astype(BF16)
    z = _dot(h, wyx_ref[...]) + byx_ref[...]
    gate = _gelu_tanh(z[:, :dr])
    xpad_ref[0:nb, :] = buf_ref[...]
    xpad_ref[nb:, :] = z[:, dr:]
    xb = cb_ref[...] + xpad_ref[0:n, :] * cw_ref[0:1, :]
    for s in range(1, CONV_W):
        xb = xb + xpad_ref[s * bsz:s * bsz + n, :] * cw_ref[s:s + 1, :]
    a, bt = _lru_gates(xb, wa_ref, ba_ref[...], wi_ref, bi_ref[...], sp_ref[...])
    hcur = h0_ref[...]
    for s in range(t):
        hcur = a[s * bsz:(s + 1) * bsz, :] * hcur + bt[s * bsz:(s + 1) * bsz, :]
        hs_ref[s * bsz:(s + 1) * bsz, :] = hcur
    o_ref[...] = x + _dot((gate * hs_ref[...]).astype(BF16), wo_ref[...])
    hl_ref[...] = hcur
    cn_ref[...] = xpad_ref[n:n + nb, :]


def _lru_sample(x, bsz, h0, conv0, g, wyx, byx, cw, cb, wa, ba, wi, bi, sp, wo):
    n, d = x.shape
    t = n // bsz
    dr = wo.shape[0]
    nb = (CONV_W - 1) * bsz
    tmaj = lambda a, tt: a.reshape(bsz, tt, -1).swapaxes(0, 1).reshape(tt * bsz, -1)
    bmaj = lambda a, tt: a.reshape(tt, bsz, -1).swapaxes(0, 1)
    vec = lambda a: a.reshape(1, -1)
    args = (tmaj(x, t), vec(g), tmaj(conv0, CONV_W - 1), h0, wyx, vec(byx), cw, vec(cb), wa, vec(ba),
            wi, vec(bi), vec(sp), wo)
    out, hl, cn = pl.pallas_call(
        functools.partial(_lru_sample_kernel, bsz),
        in_specs=[pl.BlockSpec(a.shape, lambda nd=a.ndim: (0,) * nd) for a in args],
        out_specs=[pl.BlockSpec((n, d), lambda: (0, 0)), pl.BlockSpec((bsz, dr), lambda: (0, 0)),
                   pl.BlockSpec((nb, dr), lambda: (0, 0))],
        out_shape=[jax.ShapeDtypeStruct((n, d), F32), jax.ShapeDtypeStruct((bsz, dr), F32),
                   jax.ShapeDtypeStruct((nb, dr), F32)],
        scratch_shapes=[pltpu.VMEM((n + nb, dr), F32), pltpu.VMEM((n, dr), F32)],
        compiler_params=pltpu.CompilerParams(vmem_limit_bytes=VMEM_LIMIT), name="lru_sample",
    )(*args)
    return bmaj(out, t).reshape(n, d), hl, bmaj(cn, CONV_W - 1)


def _rope_tables_partial(pos, period, rot_dim, theta):
    half = rot_dim // 2
    inv = theta ** (-jnp.arange(half, dtype=F32) / half)
    ang = pos.astype(F32)[:, None] * inv[None, :]
    cos, sin = jnp.cos(ang), jnp.sin(ang)
    t = pos.shape[0]
    zh = jnp.zeros((t, half), F32)
    rest = period - rot_dim
    c = jnp.concatenate([cos, cos, jnp.ones((t, rest), F32)], axis=1)
    s1 = jnp.concatenate([-sin, zh, jnp.zeros((t, rest), F32)], axis=1)
    s2 = jnp.concatenate([zh, sin, jnp.zeros((t, rest), F32)], axis=1)
    rep = LANES // period
    return [jnp.tile(a, (1, rep)) for a in (c, s1, s2)]


def _rot(x, c, s1, s2, half):
    n = x.shape[-1]
    return x * c + pltpu.roll(x, n - half, 1) * s1 + pltpu.roll(x, half, 1) * s2


def _sort_key(x):
    bits = pltpu.bitcast(x, I32)
    return bits ^ ((bits >> 31) & 0x7FFFFFFF)


RADIX_GROUP = 4


def _kth_largest(count_ge, k, shape):
    c0 = count_ge(jnp.zeros(shape, I32))
    t = jnp.where(c0 >= k, 0, INT_MIN).astype(I32)
    cnt = jnp.where(c0 >= k, c0, 2.0 * k)

    def decide(t, cnt, bit):
        cand = t | lax.shift_left(jnp.int32(1), bit)
        c = count_ge(cand)
        return jnp.where(c >= k, cand, t), jnp.where(c >= k, c, cnt)

    n_head = 31 % RADIX_GROUP
    for bit in range(30, 30 - n_head, -1):
        t, cnt = decide(t, cnt, bit)
    n_groups = (31 - n_head) // RADIX_GROUP

    def cond(state):
        i, _, _, pending = state
        return (i < n_groups) & (pending > 0.5)

    def body(state):
        i, t, cnt, _ = state
        for u in range(RADIX_GROUP):
            t, cnt = decide(t, cnt, 30 - n_head - (i * RADIX_GROUP + u))
        return i + 1, t, cnt, jnp.max(jnp.where(cnt == k, 0.0, 1.0))

    return lax.while_loop(cond, body, (jnp.int32(0), t, cnt, jnp.float32(1.0)))[1]


def _attn_kernel(tq, tk, lk, q_pos0, k_pos0, mode, n_cmp, q_ref, k_ref, v_ref, *rest):
    if mode == "window":
        o_ref, qs_ref, acc_ref, m_scr, l_scr = rest
        m_ref = None
    else:
        m_ref, o_ref, qs_ref, acc_ref, m_scr, l_scr = rest
    p0 = q_pos0 + pl.program_id(1) * tq
    for g in range(N_KV):
        for h in range(GROUP):
            c0 = (g * GROUP + h) * HEAD_DIM
            qs_ref[g, h * tq:(h + 1) * tq, :] = q_ref[:, c0:c0 + HEAD_DIM]
    m_scr[...] = jnp.full(m_scr.shape, NEG, F32)
    l_scr[...] = jnp.zeros(l_scr.shape, F32)
    acc_ref[...] = jnp.zeros(acc_ref.shape, F32)
    hi = jnp.minimum((p0 + tq - 1 - k_pos0) // tk + 1, lk // tk)
    lo = jnp.maximum(p0 - (WINDOW - 1) - k_pos0, 0) // tk if mode == "window" else 0
    rowpos = p0 + lax.broadcasted_iota(I32, (tq, tk), 0)

    def body(kt, carry):
        k0 = pl.multiple_of(kt * tk, tk)
        if mode == "mask":
            bias = m_ref[:, pl.ds(k0, tk)].astype(F32)
        else:
            colpos = k_pos0 + k0 + lax.broadcasted_iota(I32, (tq, tk), 1)
            valid = colpos <= rowpos
            if mode == "window":
                valid = valid & (rowpos - colpos < WINDOW) & (colpos >= 0)
                bias = jnp.where(valid, 0.0, -jnp.inf)
        for g in range(N_KV):
            if mode == "blocksel":
                blk = lax.shift_right_logical(k_pos0 + k0 + lax.broadcasted_iota(I32, (LANES, tk), 1), SLC_SHIFT)
                e = jnp.where(blk == lax.broadcasted_iota(I32, (LANES, tk), 0), 1.0, 0.0).astype(BF16)
                sel = _dot(m_ref[:, g * LANES:(g + 1) * LANES], e) > 0.5
                bias = jnp.where(valid & (sel | (colpos >= n_cmp * SLC_BLK)), 0.0, -jnp.inf)
            kt_ = k_ref[pl.ds(k0, tk), g * HEAD_DIM:(g + 1) * HEAD_DIM]
            vt_ = v_ref[pl.ds(k0, tk), g * HEAD_DIM:(g + 1) * HEAD_DIM]
            s = _dot_nt(qs_ref[g], kt_)
            s = (s.reshape(GROUP, tq, tk) + bias[None]).reshape(GROUP * tq, tk)
            _softmax_update(g, s, vt_, acc_ref, m_scr, l_scr)
        return carry

    lax.fori_loop(lo, hi, body, 0)
    for g in range(N_KV):
        o = acc_ref[g] / jnp.maximum(l_scr[g], SUM_FLOOR)
        for h in range(GROUP):
            c0 = (g * GROUP + h) * HEAD_DIM
            o_ref[:, c0:c0 + HEAD_DIM] = o[h * tq:(h + 1) * tq, :]


def _attention(q, k, v, m, *, bsz, tq, q_pos0, k_pos0, mode, n_cmp=0):
    nq_rows, dq = q.shape
    lk = k.shape[1]
    tk = ATTN_KEY_TILE
    assert lk % tk == 0 and (nq_rows // bsz) % tq == 0
    nq = nq_rows // bsz // tq
    kern = functools.partial(_attn_kernel, tq, tk, lk, q_pos0, k_pos0, mode, n_cmp)
    in_specs = [pl.BlockSpec((tq, dq), lambda b, j: (b * nq + j, 0)),
                pl.BlockSpec((None, lk, k.shape[2]), lambda b, j: (b, 0, 0)),
                pl.BlockSpec((None, lk, v.shape[2]), lambda b, j: (b, 0, 0))]
    args = [q, k, v]
    if mode != "window":
        in_specs.append(pl.BlockSpec((tq, m.shape[1]), lambda b, j: (b * nq + j, 0)))
        args.append(m)
    return pl.pallas_call(
        kern, grid=(bsz, nq), in_specs=in_specs,
        out_specs=pl.BlockSpec((tq, dq), lambda b, j: (b * nq + j, 0)),
        out_shape=jax.ShapeDtypeStruct((nq_rows, dq), F32),
        scratch_shapes=[pltpu.VMEM((N_KV, GROUP * tq, HEAD_DIM), BF16),
                        pltpu.VMEM((N_KV, GROUP * tq, HEAD_DIM), F32),
                        pltpu.VMEM((N_KV, GROUP * tq, LANES), F32),
                        pltpu.VMEM((N_KV, GROUP * tq, LANES), F32)],
        compiler_params=_params("parallel", "arbitrary"), name="attn_" + mode,
    )(*args)


def _softmax_update(g, s, vt, acc_ref, m_scr, l_scr):
    m_old = m_scr[g]
    m_new = jnp.maximum(m_old, jnp.max(s, axis=-1, keepdims=True))
    alpha = jnp.exp2(m_old - m_new)
    p = jnp.exp2(s - jnp.concatenate([m_new] * (s.shape[1] // LANES), axis=1))
    l_scr[g] = alpha * l_scr[g] + jnp.sum(p, axis=-1, keepdims=True)
    acc_ref[g] = alpha * acc_ref[g] + _dot(p.astype(BF16), vt)
    m_scr[g] = m_new


def _paged_attn_kernel(tq, pp, n_steps, mode, pt_ref, q_ref, *rest):
    kp, vp = rest[:pp], rest[pp:2 * pp]
    kn_ref, vn_ref = rest[2 * pp:2 * pp + 2]
    if mode == "mask":
        mb_ref, mt_ref, o_ref, qs_ref, acc_ref, m_scr, l_scr, kbuf, vbuf = rest[2 * pp + 2:]
    else:
        mb_ref, o_ref, qs_ref, acc_ref, m_scr, l_scr, kbuf, vbuf = rest[2 * pp + 2:]
    step = pl.program_id(1)
    tk = pp * PAGE_SIZE

    @pl.when(step == 0)
    def _():
        for g in range(N_KV):
            for h in range(GROUP):
                c0 = (g * GROUP + h) * HEAD_DIM
                qs_ref[g, h * tq:(h + 1) * tq, :] = q_ref[:, c0:c0 + HEAD_DIM]
        m_scr[...] = jnp.full(m_scr.shape, NEG, F32)
        l_scr[...] = jnp.zeros(l_scr.shape, F32)
        acc_ref[...] = jnp.zeros(acc_ref.shape, F32)

    for i in range(pp):
        for g in range(N_KV):
            rows = pl.ds(g, PAGE_SIZE, stride=N_KV)
            kbuf[g, i * PAGE_SIZE:(i + 1) * PAGE_SIZE, :] = kp[i][rows, :].astype(BF16)
            vbuf[g, i * PAGE_SIZE:(i + 1) * PAGE_SIZE, :] = vp[i][rows, :].astype(BF16)

    def biased(g, s, bias):
        n = s.shape[1]
        return (s.reshape(GROUP, tq, n) + bias[None]).reshape(GROUP * tq, n)

    for g in range(N_KV):
        if mode == "mask":
            bias = mb_ref[...].astype(F32)
        else:
            blk = lax.shift_right_logical(step * tk + lax.broadcasted_iota(I32, (LANES, tk), 1), SLC_SHIFT)
            e = jnp.where(blk == lax.broadcasted_iota(I32, (LANES, tk), 0), 1.0, 0.0).astype(BF16)
            bias = jnp.where(_dot(mb_ref[:, g * LANES:(g + 1) * LANES], e) > 0.5, 0.0, -jnp.inf)
        _softmax_update(g, biased(g, _dot_nt(qs_ref[g], kbuf[g]), bias), vbuf[g], acc_ref, m_scr, l_scr)

    @pl.when(step == n_steps - 1)
    def _():
        tn = kn_ref.shape[0]
        causal = lax.broadcasted_iota(I32, (tq, tn), 1) <= lax.broadcasted_iota(I32, (tq, tn), 0)
        for g in range(N_KV):
            sl = slice(g * HEAD_DIM, (g + 1) * HEAD_DIM)
            bias = mt_ref[...].astype(F32) if mode == "mask" else jnp.where(causal, 0.0, -jnp.inf)
            _softmax_update(g, biased(g, _dot_nt(qs_ref[g], kn_ref[:, sl]), bias), vn_ref[:, sl],
                            acc_ref, m_scr, l_scr)
            o = acc_ref[g] / jnp.maximum(l_scr[g], SUM_FLOOR)
            for h in range(GROUP):
                c0 = (g * GROUP + h) * HEAD_DIM
                o_ref[:, c0:c0 + HEAD_DIM] = o[h * tq:(h + 1) * tq, :]


PAGES_PER_STEP = 16
NEW_TILE = LANES


def _paged_attention(q, k_pool, v_pool, page0, page_table, kn, vn, m, *, bsz, tq, mode):
    dq = q.shape[1]
    n_pages = page_table.shape[1]
    pp = PAGES_PER_STEP
    assert n_pages % pp == 0
    n_steps = n_pages // pp
    tk = pp * PAGE_SIZE
    prow = PAGE_SIZE * N_KV
    page = lambda i: pl.BlockSpec((prow, HEAD_DIM), lambda b, s, pt: (page0 + pt[b, s * pp + i], 0))
    in_specs = [pl.BlockSpec((tq, dq), lambda b, s, pt: (b, 0))]
    in_specs += [page(i) for i in range(pp)] * 2
    in_specs += [pl.BlockSpec((NEW_TILE, kn.shape[1]), lambda b, s, pt: (b, 0))] * 2
    args = [q] + [k_pool] * pp + [v_pool] * pp + [kn, vn]
    if mode == "mask":
        in_specs += [pl.BlockSpec((tq, tk), lambda b, s, pt: (b, s)),
                     pl.BlockSpec((tq, NEW_TILE), lambda b, s, pt: (b, n_pages * PAGE_SIZE // NEW_TILE))]
        args += [m, m]
    else:
        in_specs.append(pl.BlockSpec((tq, m.shape[1]), lambda b, s, pt: (b, 0)))
        args.append(m)
    rows = GROUP * tq
    return pl.pallas_call(
        functools.partial(_paged_attn_kernel, tq, pp, n_steps, mode),
        grid_spec=pltpu.PrefetchScalarGridSpec(
            num_scalar_prefetch=1, grid=(bsz, n_steps), in_specs=in_specs,
            out_specs=pl.BlockSpec((tq, dq), lambda b, s, pt: (b, 0)),
            scratch_shapes=[pltpu.VMEM((N_KV, rows, HEAD_DIM), BF16), pltpu.VMEM((N_KV, rows, HEAD_DIM), F32),
                            pltpu.VMEM((N_KV, rows, LANES), F32), pltpu.VMEM((N_KV, rows, LANES), F32),
                            pltpu.VMEM((N_KV, tk, HEAD_DIM), BF16), pltpu.VMEM((N_KV, tk, HEAD_DIM), BF16)]),
        out_shape=jax.ShapeDtypeStruct((bsz * tq, dq), F32),
        compiler_params=_params("parallel", "arbitrary"), name="paged_attn_" + mode,
    )(page_table, *args)


def _heads(c):
    return lambda n, tm: ((n, c // HEAD_DIM, HEAD_DIM), (tm, c // HEAD_DIM, HEAD_DIM), lambda i: (i, 0, 0))


def _store_heads(ref, z):
    for hd in range(ref.shape[1]):
        ref[:, hd, :] = z[:, hd * HEAD_DIM:(hd + 1) * HEAD_DIM]


def _head_norm_rot(z, g, tabs, n_heads, scale, refs_f32, refs_bf16):
    c, s1, s2 = tabs
    for hd in range(n_heads):
        sl = slice(hd * HEAD_DIM, (hd + 1) * HEAD_DIM)
        r = _rot(_rms(z[:, sl]) * g, c, s1, s2, ROPE_DIM // 2)
        for ref in refs_f32:
            ref[:, hd, :] = r
        for ref in refs_bf16:
            ref[:, sl] = (r * scale).astype(BF16)


def _dsa_epilogue(h, w_ref, tabs, cs, o):
    tq_ = [t[...] for t in tabs[:3]]
    ti_ = [t[...] for t in tabs[3:]]
    gq, gk, gik = [c[...] for c in cs]
    q_ref, k_ref, kb_ref, v_ref, vb_ref, qi_ref, ki_ref, wi_ref = o
    nq, nkv = N_HEADS * HEAD_DIM, N_KV * HEAD_DIM
    ni = IDX_HEADS * IDX_DIM
    _head_norm_rot(_dot(h, w_ref[:, 0:nq]), gq, tq_, N_HEADS, ATTN_SCALE * LOG2E, [], [q_ref])
    _head_norm_rot(_dot(h, w_ref[:, nq:nq + nkv]), gk, tq_, N_KV, 1.0, [k_ref], [kb_ref])
    zv = _dot(h, w_ref[:, nq + nkv:nq + 2 * nkv])
    _store_heads(v_ref, zv)
    vb_ref[...] = zv.astype(BF16)
    c0 = nq + 2 * nkv
    zi = _dot(h, w_ref[:, c0:c0 + ni])
    for ch in range(ni // LANES):
        r = _rot(zi[:, ch * LANES:(ch + 1) * LANES], *ti_, IDX_ROPE // 2).astype(BF16)
        for u in range(LANES // IDX_DIM):
            qi_ref[ch * (LANES // IDX_DIM) + u] = r[:, u * IDX_DIM:(u + 1) * IDX_DIM]
    zl = _dot(h, w_ref[:, c0 + ni:c0 + ni + LANES])
    lane = lax.broadcasted_iota(I32, zl.shape, 1)
    ms = jnp.sum(jnp.where(lane < IDX_DIM, zl * zl, 0.0), axis=-1, keepdims=True) / IDX_DIM
    r = _rot(zl * lax.rsqrt(ms + EPS) * gik, *ti_, IDX_ROPE // 2)
    ki_ref[...] = r[:, :IDX_DIM]
    wi_ref[...] = zl[:, IDX_DIM:IDX_DIM + IDX_HEADS] * (IDX_HEADS ** -0.5 * IDX_DIM ** -0.5)


def _dsa_select_kernel(tq, nb, same_batch, tk, lk, q_pos0, top, qi_ref, wi_ref, kit_ref, m_ref, key_ref,
                       wib_ref):
    rows = nb * tq
    p0 = q_pos0 + pl.program_id(1) * (rows if same_batch else tq)
    p0s = [p0 + i * tq if same_batch else p0 for i in range(nb)]
    nk = jnp.minimum((p0s[-1] + tq - 1) // tk + 1, lk // tk)
    wi = wi_ref[...]
    for h in range(IDX_HEADS):
        wib_ref[h] = jnp.broadcast_to(wi[:, h:h + 1], (rows, tk))
    row1 = lax.broadcasted_iota(I32, (tq, tk), 0)
    col1 = lax.broadcasted_iota(I32, (tq, tk), 1)
    rowpos = jnp.concatenate([p + row1 for p in p0s], axis=0)
    col = lax.broadcasted_iota(I32, (rows, tk), 1)

    for i in range(nb):
        def score_pair(kp, carry, i=i):
            for u in range(2):
                k0 = pl.multiple_of((2 * kp + u) * tk, tk)
                kt_ = kit_ref[0 if same_batch else i, :, pl.ds(k0, tk)]
                acc = jnp.zeros((tq, tk), F32)
                if IDX_HEADS * tq <= 512:
                    s_all = _dot(qi_ref[:, i * tq:(i + 1) * tq, :].reshape(IDX_HEADS * tq, IDX_DIM), kt_)
                for h in range(IDX_HEADS):
                    if IDX_HEADS * tq <= 512:
                        s = s_all[h * tq:(h + 1) * tq, :]
                    else:
                        s = _dot(qi_ref[h, i * tq:(i + 1) * tq, :], kt_)
                    acc = acc + jnp.maximum(s, 0.0) * wib_ref[h, i * tq:(i + 1) * tq, :]
                acc = jnp.where(k0 + col1 <= p0s[i] + row1, acc, -jnp.inf)
                key_ref[i * tq:(i + 1) * tq, pl.ds(k0, tk)] = _sort_key(acc)
            return carry

        lax.fori_loop(0, (nk + 1) // 2, score_pair, 0)

    grp = min(rows, SELECT_ROWS)

    def count_ge(cand):
        parts = []
        for r0 in range(0, rows, grp):
            cb = jnp.broadcast_to(cand[r0:r0 + grp], (grp, LANES))

            def body(kt, acc, r0=r0, cb=cb):
                for u in range(2 * tk // LANES):
                    c0 = pl.multiple_of(kt * (2 * tk) + u * LANES, LANES)
                    acc = acc + jnp.where(key_ref[r0:r0 + grp, pl.ds(c0, LANES)] >= cb, 1.0, 0.0)
                return acc

            parts.append(lax.fori_loop(0, (nk + 1) // 2, body, jnp.zeros((grp, LANES), F32)))
        return jnp.sum(jnp.concatenate(parts, axis=0), axis=-1, keepdims=True)

    thr = jnp.broadcast_to(_kth_largest(count_ge, float(top), (rows, 1)), (rows, tk))

    def bias_tile(kt, carry):
        k0 = pl.multiple_of(kt * tk, tk)
        sel = (k0 + col <= rowpos) & (key_ref[:, pl.ds(k0, tk)] >= thr)
        m_ref[:, pl.ds(k0, tk)] = jnp.where(sel, 0.0, -jnp.inf).astype(BF16)
        return carry

    lax.fori_loop(0, nk, bias_tile, 0)

    def rest_tile(kt, carry):
        m_ref[:, pl.ds(pl.multiple_of(kt * tk, tk), tk)] = jnp.full((rows, tk), -jnp.inf, BF16)
        return carry

    lax.fori_loop(nk, lk // tk, rest_tile, 0)


SELECT_ROWS = 128
SELECT_TILES = 2


def _dsa_select(qi, wi, kit, *, bsz, tq, q_pos0, top):
    lk = kit.shape[2]
    tk = SELECT_KEY_TILE
    assert lk % (2 * tk) == 0 and top <= tk
    nq_rows = wi.shape[0]
    nq = nq_rows // bsz // tq
    if nq == 1 and SELECT_ROWS % tq == 0 and bsz % (SELECT_ROWS // tq) == 0:
        nb, same_batch, kb, grid = SELECT_ROWS // tq, False, SELECT_ROWS // tq, (bsz * tq // SELECT_ROWS, 1)
    elif nq % SELECT_TILES == 0:
        nb, same_batch, kb, grid = SELECT_TILES, True, 1, (bsz, nq // SELECT_TILES)
    else:
        nb, same_batch, kb, grid = 1, True, 1, (bsz, nq)
    rows = nb * tq
    nj = grid[1]
    return pl.pallas_call(
        functools.partial(_dsa_select_kernel, tq, nb, same_batch, tk, lk, q_pos0, top),
        grid=grid,
        in_specs=[pl.BlockSpec((IDX_HEADS, rows, IDX_DIM), lambda b, j: (0, b * nj + j, 0)),
                  pl.BlockSpec((rows, IDX_HEADS), lambda b, j: (b * nj + j, 0)),
                  pl.BlockSpec((kb, IDX_DIM, lk), lambda b, j: (b, 0, 0))],
        out_specs=pl.BlockSpec((rows, lk), lambda b, j: (b * nj + j, 0)),
        out_shape=jax.ShapeDtypeStruct((nq_rows, lk), BF16),
        scratch_shapes=[pltpu.VMEM((rows, lk), I32), pltpu.VMEM((IDX_HEADS, rows, tk), F32)],
        compiler_params=_params("parallel", "arbitrary"), name="dsa_select",
    )(qi, wi, kit)


def _nsa_epilogue(h, w_ref, tabs, cs, o):
    tb = [t[...] for t in tabs]
    gq, gsk, gwk = [c[...] for c in cs]
    (qc_ref, q_ref, ck_ref, cv_ref, sk_ref, skb_ref, sv_ref, svb_ref,
     wk_ref, wkb_ref, wv_ref, wvb_ref, gate_ref) = o
    nq, nkv = N_HEADS * HEAD_DIM, N_KV * HEAD_DIM
    zq = _dot(h, w_ref[:, 0:nq])
    for hd in range(N_HEADS):
        sl = slice(hd * HEAD_DIM, (hd + 1) * HEAD_DIM)
        qn = _rms(zq[:, sl]) * gq
        qc_ref[:, sl] = (qn * ATTN_SCALE).astype(BF16)
        q_ref[:, sl] = (_rot(qn, *tb, ROPE_DIM // 2) * (ATTN_SCALE * LOG2E)).astype(BF16)
    seg = lambda i: _dot(h, w_ref[:, nq + i * nkv:nq + (i + 1) * nkv])
    ck_ref[...] = seg(0)
    cv_ref[...] = seg(1)
    _head_norm_rot(seg(2), gsk, tb, N_KV, 1.0, [sk_ref], [skb_ref])
    zsv = seg(3)
    _store_heads(sv_ref, zsv)
    svb_ref[...] = zsv.astype(BF16)
    _head_norm_rot(seg(4), gwk, tb, N_KV, 1.0, [wk_ref], [wkb_ref])
    zwv = seg(5)
    _store_heads(wv_ref, zwv)
    wvb_ref[...] = zwv.astype(BF16)
    zg = _dot(h, w_ref[:, nq + 6 * nkv:nq + 6 * nkv + LANES])
    gate_ref[...] = jax.nn.sigmoid(zg[:, :N_HEADS * 3])


def _nsa_compress_kernel(ck_ref, cv_ref, pwk_ref, pwv_ref, phik_ref, phiv_ref, g_ref, kc_ref, vc_ref):
    rows, wd = ck_ref.shape
    nb = rows // CMP_BLK
    pk = jnp.sum(ck_ref[...].reshape(nb, CMP_BLK, wd) * pwk_ref[...][None], axis=1)
    pv = jnp.sum(cv_ref[...].reshape(nb, CMP_BLK, wd) * pwv_ref[...][None], axis=1)
    for hd in range(N_KV):
        sl = slice(hd * HEAD_DIM, (hd + 1) * HEAD_DIM)
        kc_ref[:, sl] = (_rms(_dot(pk[:, sl].astype(BF16), phik_ref[hd])) * g_ref[...]).astype(BF16)
        vc_ref[:, sl] = _dot(pv[:, sl].astype(BF16), phiv_ref[hd]).astype(BF16)


def _nsa_compress(ck, cv, pwk, pwv, phik, phiv, gck):
    rows, wd = ck.shape
    step = CMP_ROWS if rows % CMP_ROWS == 0 else rows
    nb = step // CMP_BLK
    bc = lambda p: jnp.repeat(p, HEAD_DIM, axis=1)
    return pl.pallas_call(
        _nsa_compress_kernel, grid=(rows // step,),
        in_specs=[pl.BlockSpec((step, wd), lambda i: (i, 0)), pl.BlockSpec((step, wd), lambda i: (i, 0)),
                  _const_spec((CMP_BLK, wd)), _const_spec((CMP_BLK, wd)), _const_spec(phik.shape),
                  _const_spec(phiv.shape), _const_spec((1, HEAD_DIM))],
        out_specs=[pl.BlockSpec((nb, wd), lambda i: (i, 0)), pl.BlockSpec((nb, wd), lambda i: (i, 0))],
        out_shape=[jax.ShapeDtypeStruct((rows // CMP_BLK, wd), BF16)] * 2,
        compiler_params=_params("parallel"), name="nsa_compress",
    )(ck, cv, bc(pwk), bc(pwv), phik, phiv, gck.reshape(1, HEAD_DIM))


def _paged_compress_kernel(pp, pt_ref, *rest):
    ckp, cvp = rest[:pp], rest[pp:2 * pp]
    pwk_ref, pwv_ref, phik_ref, phiv_ref, g_ref, kc_ref, vc_ref, pk_scr, pv_scr = rest[2 * pp:]
    nb = PAGE_SIZE // CMP_BLK
    for hd in range(N_KV):
        sl = slice(hd * HEAD_DIM, (hd + 1) * HEAD_DIM)
        rows = pl.ds(hd, PAGE_SIZE, stride=N_KV)
        for i in range(pp):
            pool = lambda ref, pw: jnp.sum(ref[rows, :].reshape(nb, CMP_BLK, HEAD_DIM) * pw[:, sl][None], axis=1)
            pk_scr[hd, i * nb:(i + 1) * nb, :] = pool(ckp[i], pwk_ref)
            pv_scr[hd, i * nb:(i + 1) * nb, :] = pool(cvp[i], pwv_ref)
        kc_ref[:, sl] = (_rms(_dot(pk_scr[hd].astype(BF16), phik_ref[hd])) * g_ref[...]).astype(BF16)
        vc_ref[:, sl] = _dot(pv_scr[hd].astype(BF16), phiv_ref[hd]).astype(BF16)


def _paged_compress(ck_pool, cv_pool, page0, page_table, pwk, pwv, phik, phiv, gck):
    bsz, n_pages = page_table.shape
    pp = PAGES_PER_STEP
    assert n_pages % pp == 0
    n_steps = n_pages // pp
    nb = pp * PAGE_SIZE // CMP_BLK
    wd = N_KV * HEAD_DIM
    prow = PAGE_SIZE * N_KV
    bc = lambda p: jnp.repeat(p, HEAD_DIM, axis=1)
    page = lambda i: pl.BlockSpec((prow, HEAD_DIM), lambda b, s, pt: (page0 + pt[b, s * pp + i], 0))
    const = lambda shape: pl.BlockSpec(shape, lambda b, s, pt: (0,) * len(shape))
    out = pl.BlockSpec((nb, wd), lambda b, s, pt: (b * n_steps + s, 0))
    return pl.pallas_call(
        functools.partial(_paged_compress_kernel, pp),
        grid_spec=pltpu.PrefetchScalarGridSpec(
            num_scalar_prefetch=1, grid=(bsz, n_steps),
            in_specs=[page(i) for i in range(pp)] * 2
            + [const((CMP_BLK, wd)), const((CMP_BLK, wd)), const(phik.shape), const(phiv.shape),
               const((1, HEAD_DIM))],
            out_specs=[out, out],
            scratch_shapes=[pltpu.VMEM((N_KV, nb, HEAD_DIM), F32), pltpu.VMEM((N_KV, nb, HEAD_DIM), F32)]),
        out_shape=[jax.ShapeDtypeStruct((bsz * n_steps * nb, wd), BF16)] * 2,
        compiler_params=_params("parallel", "parallel"), name="paged_compress",
    )(page_table, *([ck_pool] * pp), *([cv_pool] * pp), bc(pwk), bc(pwv), phik, phiv,
      gck.reshape(1, HEAD_DIM))


def _nsa_cmp_kernel(tq, q_pos0, n_sel, qc_ref, kc_ref, vc_ref, oc_ref, bm_ref):
    p0 = q_pos0 + pl.program_id(1) * tq
    nc = kc_ref.shape[0]
    rowpos = p0 + lax.broadcasted_iota(I32, (tq, nc), 0)
    blk = lax.broadcasted_iota(I32, (tq, nc), 1)
    valid = jnp.concatenate([blk * CMP_BLK + (CMP_BLK - 1) <= rowpos] * GROUP, axis=0)
    cur = lax.shift_right_logical(rowpos, SLC_SHIFT)
    forced = (blk == 0) | (blk == cur) | (blk == cur - 1)
    keys = []
    for g in range(N_KV):
        sl = slice(g * HEAD_DIM, (g + 1) * HEAD_DIM)
        qs = jnp.concatenate([qc_ref[:, (g * GROUP + h) * HEAD_DIM:(g * GROUP + h + 1) * HEAD_DIM]
                              for h in range(GROUP)], axis=0)
        lc = jnp.where(valid, _dot_nt(qs, kc_ref[:, sl]), NEG)
        m = jnp.max(lc, axis=-1, keepdims=True)
        m = jnp.where(m > 0.5 * NEG, m, 0.0)
        p = jnp.where(valid, jnp.exp(lc - m), 0.0)
        pc = p / jnp.maximum(jnp.sum(p, axis=-1, keepdims=True), SUM_FLOOR)
        oc = _dot(pc.astype(BF16), vc_ref[:, sl])
        for h in range(GROUP):
            c0 = (g * GROUP + h) * HEAD_DIM
            oc_ref[:, c0:c0 + HEAD_DIM] = oc[h * tq:(h + 1) * tq, :]
        imp = pc[0:tq]
        for h in range(1, GROUP):
            imp = imp + pc[h * tq:(h + 1) * tq]
        imp = jnp.where(forced, BIG, imp)
        imp = jnp.where(blk > cur, -jnp.inf, imp)
        keys.append(_sort_key(imp))
    key = jnp.concatenate(keys, axis=0)
    count_ge = lambda cand: jnp.sum(jnp.where(key >= cand, 1.0, 0.0), axis=-1, keepdims=True)
    sel = jnp.where(key >= _kth_largest(count_ge, float(n_sel), (N_KV * tq, 1)), 1.0, 0.0).astype(BF16)
    for g in range(N_KV):
        bm_ref[:, g * LANES:(g + 1) * LANES] = sel[g * tq:(g + 1) * tq, :]


def _nsa_cmp(qc, kc, vc, *, bsz, tq, q_pos0, n_sel):
    nq_rows, dq = qc.shape
    nc = kc.shape[1]
    assert nc == LANES
    nq = nq_rows // bsz // tq
    return pl.pallas_call(
        functools.partial(_nsa_cmp_kernel, tq, q_pos0, n_sel), grid=(bsz, nq),
        in_specs=[pl.BlockSpec((tq, dq), lambda b, j: (b * nq + j, 0)),
                  pl.BlockSpec((None, nc, kc.shape[2]), lambda b, j: (b, 0, 0)),
                  pl.BlockSpec((None, nc, vc.shape[2]), lambda b, j: (b, 0, 0))],
        out_specs=[pl.BlockSpec((tq, dq), lambda b, j: (b * nq + j, 0)),
                   pl.BlockSpec((tq, N_KV * LANES), lambda b, j: (b * nq + j, 0))],
        out_shape=[jax.ShapeDtypeStruct((nq_rows, dq), F32),
                   jax.ShapeDtypeStruct((nq_rows, N_KV * LANES), BF16)],
        compiler_params=_params("parallel", "parallel"), name="nsa_cmp",
    )(qc, kc, vc)


def _nsa_combine(gate, oc, os_, ow):
    parts = []
    for hd in range(N_HEADS):
        sl = slice(hd * HEAD_DIM, (hd + 1) * HEAD_DIM)
        parts.append(gate[:, 3 * hd:3 * hd + 1] * oc[:, sl] + gate[:, 3 * hd + 1:3 * hd + 2] * os_[:, sl]
                     + gate[:, 3 * hd + 2:3 * hd + 3] * ow[:, sl])
    return jnp.concatenate(parts, axis=1)


SAMPLE_TQ = 16
PROMPT_TQ = 128
PROMPT_ATTN_TQ = 512
PROMPT_CMP_TQ = 512


def _wide_tile(t, want, base):
    return want if t % want == 0 else base
KEY_ALIGN = ATTN_KEY_TILE


def _pad_rows(a, bsz, t, tp):
    return jnp.pad(a.reshape(bsz, t, -1), ((0, 0), (0, tp - t), (0, 0))).reshape(bsz * tp, -1)


def _unpad_rows(a, bsz, t, tp):
    return a.reshape(bsz, tp, -1)[:, :t].reshape(bsz * t, -1)


def _cat_keys(old, new, bsz):
    new = new.reshape(bsz, -1, new.shape[-1])
    n = old.shape[1] + new.shape[1]
    return jnp.pad(jnp.concatenate([old.astype(new.dtype), new], axis=1),
                   ((0, 0), (0, -n % KEY_ALIGN), (0, 0)))


def _tile_tables(tabs, bsz, t):
    return tabs if t % ROW_TILE == 0 else [jnp.tile(a, (bsz, 1)) for a in tabs]


def _mixer_c(x, bsz, pos, past, g, wcat, gq, gk, gik, wo):
    n, d = x.shape
    t = n // bsz
    tabs = _tile_tables(_rope_tables_partial(pos, HEAD_DIM, ROPE_DIM, ROPE_THETA)
                        + _rope_tables_partial(pos, IDX_DIM, IDX_ROPE, ROPE_THETA), bsz, t)
    nkv = N_KV * HEAD_DIM
    q, k, kb, v, vb, qi, ki, wi = _proj(
        x, g, wcat, tabs, [gq, gk, gik], _dsa_epilogue,
        [(_rows(N_HEADS * HEAD_DIM), BF16), (_heads(nkv), F32), (_rows(nkv), BF16), (_heads(nkv), F32),
         (_rows(nkv), BF16),
         (lambda n_, tm: ((IDX_HEADS, n_, IDX_DIM), (IDX_HEADS, tm, IDX_DIM), lambda i: (0, i, 0)), BF16),
         (_rows(IDX_DIM), F32), (_rows(IDX_HEADS), F32)], tabs[0].shape[0], "proj_dsa")
    if past is None:
        tq, q_pos0, n_keys = PROMPT_TQ, 0, t
        kit = ki.reshape(bsz, t, IDX_DIM).swapaxes(1, 2).astype(BF16)
        kall, vall = kb.reshape(bsz, t, nkv), vb.reshape(bsz, t, nkv)
    else:
        tq, q_pos0 = SAMPLE_TQ, past["kidx"].shape[1]
        n_keys = q_pos0 + t
        kit = _cat_keys(past["kidx"], ki, bsz).swapaxes(1, 2).astype(BF16)
        q, wi = _pad_rows(q, bsz, t, tq), _pad_rows(wi, bsz, t, tq)
        qi = jnp.pad(qi.reshape(IDX_HEADS, bsz, t, IDX_DIM), ((0, 0), (0, 0), (0, tq - t), (0, 0))
                     ).reshape(IDX_HEADS, bsz * tq, IDX_DIM)
    bias = _dsa_select(qi, wi, kit, bsz=bsz, tq=tq, q_pos0=q_pos0, top=min(TOPK_MAX, n_keys // 4))
    if past is None:
        o = _attention(q, kall, vall, bias, bsz=bsz, tq=_wide_tile(t, PROMPT_ATTN_TQ, tq), q_pos0=q_pos0,
                       k_pos0=0, mode="mask")
    else:
        new = lambda a: _pad_rows(a.reshape(n, nkv), bsz, t, NEW_TILE).astype(BF16)
        o = _paged_attention(q, past["k_pool"], past["v_pool"], past["page0"], past["pt"], new(k), new(v), bias,
                             bsz=bsz, tq=tq, mode="mask")
        o = _unpad_rows(o, bsz, t, tq)
    return _outproj(x, wo, [o], lambda a: a), k, v, ki


def _mixer_d(x, bsz, pos, past, g, wcat, gq, gsk, gwk, cmpw, wo):
    n, d = x.shape
    t = n // bsz
    tabs = _tile_tables(_rope_tables_partial(pos, HEAD_DIM, ROPE_DIM, ROPE_THETA), bsz, t)
    nkv = N_KV * HEAD_DIM
    kvo = [(_heads(nkv), F32), (_rows(nkv), BF16)]
    (qc, q, ck, cv, sk, skb, sv, svb, wk, wkb, wv, wvb, gate) = _proj(
        x, g, wcat, tabs, [gq, gsk, gwk], _nsa_epilogue,
        [(_rows(N_HEADS * HEAD_DIM), BF16), (_rows(N_HEADS * HEAD_DIM), BF16), (_rows(nkv), F32),
         (_rows(nkv), F32)] + kvo * 4 + [(_rows(N_HEADS * 3), F32)], tabs[0].shape[0], "proj_nsa")
    if past is None:
        tq, q_pos0, n_keys, win_pos0 = PROMPT_TQ, 0, t, 0
        kc, vc = _nsa_compress(ck, cv, *cmpw)
        three = lambda a: a.reshape(bsz, t, nkv)
        skall, svall, wkall, wvall = three(skb), three(svb), three(wkb), three(wvb)
    else:
        tq, q_pos0 = SAMPLE_TQ, past["pt"].shape[1] * PAGE_SIZE
        n_keys = q_pos0 + t
        win_pos0 = q_pos0 - past["wk"].shape[1]
        assert q_pos0 % CMP_BLK == 0 and t < CMP_BLK
        kc, vc = _paged_compress(past["ck_pool"], past["cv_pool"], past["page0"], past["pt"], *cmpw)
        wkall = _cat_keys(past["wk"], wk.reshape(n, nkv), bsz).astype(BF16)
        wvall = _cat_keys(past["wv"], wv.reshape(n, nkv), bsz).astype(BF16)
        qc, q = _pad_rows(qc, bsz, t, tq), _pad_rows(q, bsz, t, tq)
    n_cmp = kc.shape[0] // bsz
    assert n_cmp <= LANES
    lane_pad = lambda a: jnp.pad(a.reshape(bsz, n_cmp, nkv), ((0, 0), (0, LANES - n_cmp), (0, 0)))
    kc, vc = lane_pad(kc), lane_pad(vc)
    n_slc = -(-n_keys // SLC_BLK)
    n_lane = min(n_slc, LANES)
    assert n_slc == n_lane or (n_slc == n_lane + 1 and q_pos0 // SLC_BLK == n_lane)
    n_sel = min(N_SLC, n_slc) - (n_slc - n_lane)
    tq_cmp = _wide_tile(t, PROMPT_CMP_TQ, tq) if past is None else tq
    oc, bm = _nsa_cmp(qc, kc, vc, bsz=bsz, tq=tq_cmp, q_pos0=q_pos0, n_sel=n_sel)
    if past is None:
        os_ = _attention(q, skall, svall, bm, bsz=bsz, tq=_wide_tile(t, PROMPT_ATTN_TQ, tq), q_pos0=q_pos0,
                         k_pos0=0, mode="blocksel", n_cmp=n_lane)
    else:
        new = lambda a: _pad_rows(a.reshape(n, nkv), bsz, t, NEW_TILE).astype(BF16)
        os_ = _paged_attention(q, past["sk_pool"], past["sv_pool"], past["page0"], past["pt"], new(sk), new(sv),
                               bm, bsz=bsz, tq=tq, mode="blocksel")
    ow = _attention(q, wkall, wvall, None, bsz=bsz, tq=tq if past else _wide_tile(t, PROMPT_ATTN_TQ, tq),
                    q_pos0=q_pos0, k_pos0=win_pos0, mode="window")
    if past is not None:
        oc, os_, ow = (_unpad_rows(a, bsz, t, tq) for a in (oc, os_, ow))
    return _outproj(x, wo, [gate, oc, os_, ow], _nsa_combine), (ck, cv, sk, sv, wk, wv)


def _gather_pages(cache, page_table):
    rows = cache[page_table]
    return rows.reshape(page_table.shape[0], page_table.shape[1] * PAGE_SIZE, -1)


def kernel(x_prompt, x_sample, state_a_ret, state_b_h, state_b_conv, cache_c_k, cache_c_v, cache_c_kidx,
           cache_d_ck, cache_d_cv, cache_d_sk, cache_d_sv, state_d_wk, state_d_wv, page_table,
           ffn1_norm, ffn1_wg, ffn1_wu, ffn1_wd, mix_norm, ffn2_norm, ffn2_wg, ffn2_wu, ffn2_wd,
           a_wq, a_wk, a_wv, a_wg, a_wo,
           b_wy, b_by, b_wx, b_bx, b_conv_w, b_conv_b, b_wa, b_ba, b_wi, b_bi, b_lam, b_wo,
           c_wq, c_gq, c_wk, c_gk, c_wv, c_wo, c_wiq, c_wik, c_gik, c_wiw,
           d_wq, d_gq, d_wck, d_wcv, d_pwk, d_pwv, d_phik, d_phiv, d_gck, d_wsk, d_wsv, d_gsk,
           d_wwk, d_wwv, d_gwk, d_wgate, d_wo):
    bp, tp, d = x_prompt.shape
    bs, ts, _ = x_sample.shape
    depth = ffn1_norm.shape[0]
    past_len = page_table.shape[1] * PAGE_SIZE
    pos_p = jnp.arange(tp, dtype=I32)
    pos_s = past_len + jnp.arange(ts, dtype=I32)
    b16 = lambda a: a.astype(BF16)
    row = lambda a: a.reshape(1, -1)
    nkv = N_KV * HEAD_DIM
    pool = lambda c: c.reshape(-1, HEAD_DIM)
    ffn1_w = [_to_bf16(w) for w in (ffn1_wg, ffn1_wu, ffn1_wd)]
    ffn2_w = [_to_bf16(w) for w in (ffn2_wg, ffn2_wu, ffn2_wd)]

    xp = x_prompt.reshape(bp * tp, d)
    xs = x_sample.reshape(bs * ts, d)
    outs = [[] for _ in range(24)]
    for i in range(depth):
        m, j = i % 4, i // 4
        xp, xs = _ffn(xp, ffn1_norm[i], *ffn1_w, i), _ffn(xs, ffn1_norm[i], *ffn1_w, i)
        g = mix_norm[i]
        if m == 0:
            dk = a_wq.shape[2] // RET_HEADS
            dv = a_wv.shape[2] // RET_HEADS
            wcat = b16(jnp.concatenate([a_wq[j], a_wk[j], a_wv[j], a_wg[j]], axis=1))
            wo = b16(a_wo[j])
            xp, s_p = _mixer_a(xp, g, pos_p, bp, jnp.zeros((bp, RET_HEADS, dk, dv), F32), wcat, wo, dk, dv)
            xs, s_s = _mixer_a(xs, g, pos_s, bs, state_a_ret[j], wcat, wo, dk, dv)
            new = [s_p, s_s]
            base = 0
        elif m == 1:
            common = (g, b16(jnp.concatenate([b_wy[j], b_wx[j]], axis=1)), jnp.concatenate([b_by[j], b_bx[j]]),
                      b_conv_w[j], b_conv_b[j], b16(b_wa[j]), b_ba[j].reshape(-1), b16(b_wi[j]),
                      b_bi[j].reshape(-1), jax.nn.softplus(-b_lam[j]), b16(b_wo[j]))
            xp, h_p, c_p = _lru_prompt(xp, bp, *common)
            xs, h_s, c_s = _lru_sample(xs, bs, state_b_h[j], state_b_conv[j], *common)
            new = [h_p, h_s, c_p, c_s]
            base = 2
        elif m == 2:
            zpad = jnp.zeros((d, LANES - IDX_DIM - IDX_HEADS), F32)
            wcat = b16(jnp.concatenate([c_wq[j], c_wk[j], c_wv[j], c_wiq[j], c_wik[j], c_wiw[j], zpad], axis=1))
            gik = jnp.concatenate([c_gik[j], jnp.zeros((LANES - IDX_DIM,), F32)])
            cw = (g, wcat, row(c_gq[j]), row(c_gk[j]), row(gik), b16(c_wo[j]))
            past = dict(kidx=_gather_pages(cache_c_kidx[j], page_table), k_pool=pool(cache_c_k),
                        v_pool=pool(cache_c_v), page0=j * cache_c_k.shape[1], pt=page_table)
            xp, k_p, v_p, i_p = _mixer_c(xp, bp, pos_p, None, *cw)
            xs, k_s, v_s, i_s = _mixer_c(xs, bs, pos_s, past, *cw)
            kv = lambda a, b_, t_: a.reshape(b_, t_, N_KV, HEAD_DIM)
            new = [kv(k_p, bp, tp), kv(k_s, bs, ts), kv(v_p, bp, tp), kv(v_s, bs, ts),
                   i_p.reshape(bp, tp, IDX_DIM), i_s.reshape(bs, ts, IDX_DIM)]
            base = 6
        else:
            zpad = jnp.zeros((d, LANES - N_HEADS * 3), F32)
            wcat = b16(jnp.concatenate([d_wq[j], d_wck[j], d_wcv[j], d_wsk[j], d_wsv[j], d_wwk[j], d_wwv[j],
                                        d_wgate[j], zpad], axis=1))
            cmpw = (d_pwk[j], d_pwv[j], b16(d_phik[j]), b16(d_phiv[j]), d_gck[j])
            dw = (g, wcat, row(d_gq[j]), row(d_gsk[j]), row(d_gwk[j]), cmpw, b16(d_wo[j]))
            wbuf = state_d_wk.shape[2]
            past = dict(ck_pool=pool(cache_d_ck), cv_pool=pool(cache_d_cv), sk_pool=pool(cache_d_sk),
                        sv_pool=pool(cache_d_sv), page0=j * cache_d_ck.shape[1], pt=page_table,
                        wk=state_d_wk[j].reshape(bs, wbuf, nkv), wv=state_d_wv[j].reshape(bs, wbuf, nkv))
            xp, rows_p = _mixer_d(xp, bp, pos_p, None, *dw)
            xs, rows_s = _mixer_d(xs, bs, pos_s, past, *dw)
            kv = lambda a, b_: a.reshape(b_, -1, N_KV, HEAD_DIM)
            new = []
            for a_p, a_s in zip(rows_p[:4], rows_s[:4]):
                new += [kv(a_p, bp), kv(a_s, bs)]
            wb_p = min(WINDOW, tp)
            for a_p, a_s, st in ((rows_p[4], rows_s[4], state_d_wk[j]), (rows_p[5], rows_s[5], state_d_wv[j])):
                new += [kv(a_p, bp)[:, tp - wb_p:], jnp.concatenate([st, kv(a_s, bs)], axis=1)[:, ts:]]
            base = 12
        for off, a in enumerate(new):
            outs[base + off].append(a)
        xp, xs = _ffn(xp, ffn2_norm[i], *ffn2_w, i), _ffn(xs, ffn2_norm[i], *ffn2_w, i)
    return (xp.reshape(bp, tp, d), xs.reshape(bs, ts, d)) + tuple(jnp.stack(o) for o in outs)
```

```python
import functools
import math

import jax
import jax.numpy as jnp
from jax import lax
from jax.experimental import pallas as pl
from jax.experimental.pallas import tpu as pltpu

F32 = jnp.float32
BF16 = jnp.bfloat16
I32 = jnp.int32

EPS = 1e-6
PAGE_SIZE = 128

RET_HEADS = 4
RET_CHUNK = 128
RET_WIDE_CHUNK = 512
RET_THETA = 10000.0

LRU_BLOCKS = 4
CONV_W = 4
LRU_C = 8.0

N_HEADS = 8
HEAD_DIM = 128
N_KV = 2
GROUP = N_HEADS // N_KV
ROPE_DIM = HEAD_DIM // 4
ROPE_THETA = 500000.0

IDX_HEADS = 16
IDX_DIM = 64
IDX_ROPE = IDX_DIM // 4
TOPK_MAX = 256

CMP_BLK = 64
SLC_BLK = 64
N_SLC = 16
WINDOW = 512
BIG = 1e4

LANES = 128
SUBLANES = 8
VMEM_LIMIT = 56 * 1024 * 1024
ROW_TILE = 512
LRU_TILE = 256
ATTN_KEY_TILE = 512
SELECT_KEY_TILE = 256
CMP_ROWS = 2048
SLC_SHIFT = 6
SUM_FLOOR = 1e-30
NEG = -1e30
ATTN_SCALE = HEAD_DIM ** -0.5
LOG2E = math.log2(math.e)
INT_MIN = -2147483648


def _params(*sem):
    return pltpu.CompilerParams(dimension_semantics=sem, vmem_limit_bytes=VMEM_LIMIT)


def _const_spec(shape):
    nd = len(shape)
    return pl.BlockSpec(shape, lambda *_: (0,) * nd, pipeline_mode=pl.Buffered(1))


def _rms(x):
    return x * lax.rsqrt(jnp.mean(x * x, axis=-1, keepdims=True) + EPS)


def _dot(a, b):
    return jnp.dot(a, b, preferred_element_type=F32)


def _dot_nt(a, b):
    return lax.dot_general(a, b, (((1,), (1,)), ((), ())), preferred_element_type=F32)


def _row_tile(n):
    return ROW_TILE if n % ROW_TILE == 0 else n


def _ffn_kernel(x_ref, g_ref, wg_ref, wu_ref, wd_ref, o_ref):
    x = x_ref[...]
    h = (_rms(x) * g_ref[...]).astype(BF16)
    gt = _dot(h, wg_ref[...])
    ut = _dot(h, wu_ref[...])
    a = (gt * jax.nn.sigmoid(gt) * ut).astype(BF16)
    o_ref[...] = x + 0.5 * _dot(a, wd_ref[...])


def _ffn(x, g, wg, wu, wd, layer):
    n, d = x.shape
    f = wg.shape[2]
    tm = _row_tile(n)
    wspec = lambda a, b_: pl.BlockSpec((None, a, b_), lambda i: (layer, 0, 0), pipeline_mode=pl.Buffered(1))
    return pl.pallas_call(
        _ffn_kernel,
        grid=(n // tm,),
        in_specs=[pl.BlockSpec((tm, d), lambda i: (i, 0)), _const_spec((1, d)),
                  wspec(d, f), wspec(d, f), wspec(f, d)],
        out_specs=pl.BlockSpec((tm, d), lambda i: (i, 0)),
        out_shape=jax.ShapeDtypeStruct((n, d), F32),
        compiler_params=_params("parallel"), name="ffn",
    )(x, g.reshape(1, d), wg, wu, wd)


def _cast_kernel(x_ref, o_ref):
    o_ref[...] = x_ref[...].astype(o_ref.dtype)


def _to_bf16(w):
    c = w.shape[-1]
    r = w.size // c
    tm = _row_tile(r)
    out = pl.pallas_call(
        _cast_kernel, grid=(r // tm,),
        in_specs=[pl.BlockSpec((tm, c), lambda i: (i, 0))],
        out_specs=pl.BlockSpec((tm, c), lambda i: (i, 0)),
        out_shape=jax.ShapeDtypeStruct((r, c), BF16),
        compiler_params=_params("parallel"), name="to_bf16",
    )(w.reshape(r, c))
    return out.reshape(w.shape)


def _proj(x, g, w, tables, consts, epilogue, outs, n_tab_rows, name):
    n, d = x.shape
    tm = _row_tile(n)
    assert n_tab_rows % tm == 0
    nt = n_tab_rows // tm
    nw = w.shape[1]

    def kern(x_ref, g_ref, w_ref, *rest):
        tabs = rest[:len(tables)]
        cs = rest[len(tables):len(tables) + len(consts)]
        o = rest[len(tables) + len(consts):]
        h = (_rms(x_ref[...]) * g_ref[...]).astype(BF16)
        epilogue(h, w_ref, tabs, cs, o)

    in_specs = [pl.BlockSpec((tm, d), lambda i: (i, 0)), _const_spec((1, d)), _const_spec((d, nw))]
    in_specs += [pl.BlockSpec((tm, t.shape[1]), lambda i: (i % nt, 0)) for t in tables]
    in_specs += [_const_spec(c.shape) for c in consts]
    out_shape, out_specs = [], []
    for fn, dt in outs:
        ashape, bshape, imap = fn(n, tm)
        out_shape.append(jax.ShapeDtypeStruct(ashape, dt))
        out_specs.append(pl.BlockSpec(bshape, imap))
    return pl.pallas_call(
        kern, grid=(n // tm,), in_specs=in_specs, out_specs=out_specs, out_shape=out_shape,
        compiler_params=_params("parallel"), name=name,
    )(x, g.reshape(1, d), w, *tables, *consts)


def _rows(c):
    return lambda n, tm: ((n, c), (tm, c), lambda i: (i, 0))


def _outproj(x, w, ins, combine):
    n, d = x.shape
    tm = _row_tile(n)

    def kern(x_ref, w_ref, *rest):
        o_ref = rest[-1]
        a = combine(*[r[...] for r in rest[:-1]]).astype(BF16)
        o_ref[...] = x_ref[...] + _dot(a, w_ref[...])

    return pl.pallas_call(
        kern, grid=(n // tm,),
        in_specs=[pl.BlockSpec((tm, d), lambda i: (i, 0)), _const_spec(w.shape)]
        + [pl.BlockSpec((tm, a.shape[1]), lambda i: (i, 0)) for a in ins],
        out_specs=pl.BlockSpec((tm, d), lambda i: (i, 0)),
        out_shape=jax.ShapeDtypeStruct((n, d), F32),
        compiler_params=_params("parallel"), name="outproj",
    )(x, w, *ins)


def _ret_epilogue(dk, dv, h, w_ref, tabs, cs, o):
    cos, sin = tabs[0][...], tabs[1][...]
    q_ref, k_ref, v_ref, g_ref = o
    half = dk // 2
    nqk = RET_HEADS * dk
    for seg, ref, scale in ((0, q_ref, 1.0), (1, k_ref, dk ** -0.5)):
        z = _dot(h, w_ref[:, seg * nqk:(seg + 1) * nqk])
        for hd in range(RET_HEADS):
            x1 = z[:, hd * dk:hd * dk + half]
            x2 = z[:, hd * dk + half:(hd + 1) * dk]
            ref[:, hd * dk:hd * dk + half] = (x1 * cos - x2 * sin) * scale
            ref[:, hd * dk + half:(hd + 1) * dk] = (x2 * cos + x1 * sin) * scale
    nv = RET_HEADS * dv
    v_ref[...] = _dot(h, w_ref[:, 2 * nqk:2 * nqk + nv]).astype(BF16)
    zg = _dot(h, w_ref[:, 2 * nqk + nv:2 * nqk + 2 * nv])
    g_ref[...] = zg * jax.nn.sigmoid(zg)


def _ret_chunk_kernel(q_ref, k_ref, v_ref, s0_ref, dm_ref, xi_ref, zt_ref, gc_ref, o_ref, s_ref):
    c = pl.program_id(2)

    @pl.when(c == 0)
    def _():
        s_ref[...] = s0_ref[...]

    nh, dk, dv = s_ref.shape
    for hd in range(nh):
        q = q_ref[:, hd * dk:(hd + 1) * dk]
        k = k_ref[:, hd * dk:(hd + 1) * dk]
        v = v_ref[:, hd * dv:(hd + 1) * dv]
        s = s_ref[hd]
        att = (_dot_nt(q.astype(BF16), k.astype(BF16)) * dm_ref[hd]).astype(BF16)
        o = _dot(att, v) + _dot((q * xi_ref[hd]).astype(BF16), s.astype(BF16))
        kz = (k * zt_ref[hd]).astype(BF16)
        s_ref[hd] = s * gc_ref[hd] + lax.dot_general(kz, v, (((0,), (0,)), ((), ())),
                                                     preferred_element_type=F32)
        o_ref[:, hd * dv:(hd + 1) * dv] = _rms(o)


def _retention(q, k, v, s0, n_valid):
    b, hh, dk, dv = s0.shape
    t = q.shape[0] // b
    c = RET_WIDE_CHUNK if (n_valid % RET_CHUNK == 0 and t % RET_WIDE_CHUNK == 0) else RET_CHUNK
    nc = t // c
    cc = c if n_valid % c == 0 else n_valid
    lg = jnp.log1p(-jnp.exp2(-5.0 - jnp.arange(hh, dtype=F32)))
    idx = jnp.arange(c, dtype=F32)
    diff = idx[:, None] - idx[None, :]
    dmask = jnp.where(diff >= 0, jnp.exp(lg[:, None, None] * jnp.maximum(diff, 0.0)), 0.0)
    xi = jnp.exp(lg[:, None] * (idx + 1.0))[:, :, None]
    zeta = jnp.exp(lg[:, None] * (cc - 1.0 - idx))[:, :, None]
    g_c = jnp.exp(lg * cc)[:, None, None]
    nh = 2 if hh % 2 == 0 else 1
    return pl.pallas_call(
        _ret_chunk_kernel,
        grid=(b, hh // nh, nc),
        in_specs=[pl.BlockSpec((c, nh * dk), lambda i, j, l: (i * nc + l, j)),
                  pl.BlockSpec((c, nh * dk), lambda i, j, l: (i * nc + l, j)),
                  pl.BlockSpec((c, nh * dv), lambda i, j, l: (i * nc + l, j)),
                  pl.BlockSpec((None, nh, dk, dv), lambda i, j, l: (i, j, 0, 0)),
                  pl.BlockSpec((nh, c, c), lambda i, j, l: (j, 0, 0)),
                  pl.BlockSpec((nh, c, 1), lambda i, j, l: (j, 0, 0)),
                  pl.BlockSpec((nh, c, 1), lambda i, j, l: (j, 0, 0)),
                  pl.BlockSpec((nh, 1, 1), lambda i, j, l: (j, 0, 0))],
        out_specs=[pl.BlockSpec((c, nh * dv), lambda i, j, l: (i * nc + l, j)),
                   pl.BlockSpec((None, nh, dk, dv), lambda i, j, l: (i, j, 0, 0))],
        out_shape=[jax.ShapeDtypeStruct((b * t, hh * dv), F32),
                   jax.ShapeDtypeStruct((b, hh, dk, dv), F32)],
        compiler_params=_params("parallel", "parallel", "arbitrary"), name="retention",
    )(q, k, v, s0, dmask, xi, zeta, g_c)


def _rope_tables_full(pos, dim, theta):
    half = dim // 2
    inv = theta ** (-jnp.arange(half, dtype=F32) / half)
    ang = pos.astype(F32)[:, None] * inv[None, :]
    return jnp.cos(ang), jnp.sin(ang)


def _mixer_a(x, g, pos, bsz, s0, wcat, wo, dk, dv):
    n, d = x.shape
    t = n // bsz
    cos, sin = _rope_tables_full(pos, dk, RET_THETA)
    if t % RET_CHUNK:
        cos = jnp.tile(cos, (bsz, 1))
        sin = jnp.tile(sin, (bsz, 1))
    q, k, v, sg = _proj(
        x, g, wcat, [cos, sin], [], functools.partial(_ret_epilogue, dk, dv),
        [(_rows(RET_HEADS * dk), F32), (_rows(RET_HEADS * dk), F32),
         (_rows(RET_HEADS * dv), BF16), (_rows(RET_HEADS * dv), F32)], cos.shape[0], "proj_ret")
    if t % RET_CHUNK:
        pad = lambda a: jnp.pad(a.reshape(bsz, t, -1), ((0, 0), (0, RET_CHUNK - t), (0, 0))
                                ).reshape(bsz * RET_CHUNK, -1)
        o, s_new = _retention(pad(q), pad(k), pad(v), s0, t)
        o = o.reshape(bsz, RET_CHUNK, -1)[:, :t].reshape(n, -1)
    else:
        o, s_new = _retention(q, k, v, s0, t)
    return _outproj(x, wo, [sg, o], lambda a, b: a * b), s_new


def _gelu_tanh(x):
    return 0.5 * x * (1.0 + jnp.tanh(math.sqrt(2.0 / math.pi) * (x + 0.044715 * (x * x * x))))


def _lru_gates(xb, wa_ref, ba, wi_ref, bi, sp):
    bw = wa_ref.shape[1]
    xbb = xb.astype(BF16)
    r = jnp.concatenate([_dot(xbb[:, n * bw:(n + 1) * bw], wa_ref[n]) for n in range(LRU_BLOCKS)], axis=1)
    i = jnp.concatenate([_dot(xbb[:, n * bw:(n + 1) * bw], wi_ref[n]) for n in range(LRU_BLOCKS)], axis=1)
    r = jax.nn.sigmoid(r + ba)
    i = jax.nn.sigmoid(i + bi)
    log_a = -LRU_C * r * sp
    a = jnp.exp(log_a)
    th = jnp.tanh(log_a)
    bt = jnp.sqrt(-2.0 * th / (1.0 - th)) * (i * xb)
    return a, bt


def _lru_prompt_kernel(x_ref, g_ref, wyx_ref, byx_ref, cw_ref, cb_ref, wa_ref, ba_ref, wi_ref, bi_ref,
                       sp_ref, wo_ref, o_ref, hl_ref, cn_ref, xpad_ref, a_ref, b_ref, hs_ref, h_ref):
    j = pl.program_id(1)
    tm, dr = a_ref.shape
    pad = xpad_ref.shape[0] - tm

    @pl.when(j == 0)
    def _():
        h_ref[...] = jnp.zeros_like(h_ref)
        xpad_ref[0:pad, :] = jnp.zeros((pad, dr), F32)

    x = x_ref[...]
    h = (_rms(x) * g_ref[...]).astype(BF16)
    z = _dot(h, wyx_ref[...]) + byx_ref[...]
    gate = _gelu_tanh(z[:, :dr])
    xpad_ref[pad:, :] = z[:, dr:]
    xb = cb_ref[...] + z[:, dr:] * cw_ref[CONV_W - 1:CONV_W, :]
    for s in range(1, CONV_W):
        xb = xb + xpad_ref[pad - s:pad - s + tm, :] * cw_ref[CONV_W - 1 - s:CONV_W - s, :]
    xpad_ref[0:pad, :] = xpad_ref[tm:tm + pad, :]
    a, bt = _lru_gates(xb, wa_ref, ba_ref[...], wi_ref, bi_ref[...], sp_ref[...])
    a_ref[...] = a
    b_ref[...] = bt

    def step(t, hprev):
        hn = a_ref[pl.ds(t, 1), :] * hprev + b_ref[pl.ds(t, 1), :]
        hs_ref[pl.ds(t, 1), :] = hn
        return hn

    hlast = lax.fori_loop(0, tm, step, h_ref[...])
    h_ref[...] = hlast
    o_ref[...] = x + _dot((gate * hs_ref[...]).astype(BF16), wo_ref[...])
    hl_ref[...] = hlast
    cn_ref[...] = xpad_ref[pad - (CONV_W - 1):pad, :]


def _lru_prompt(x, bsz, g, wyx, byx, cw, cb, wa, ba, wi, bi, sp, wo):
    n, d = x.shape
    t = n // bsz
    dr = wo.shape[0]
    tm = LRU_TILE if t % LRU_TILE == 0 else t
    nt = t // tm
    pad = SUBLANES
    vec = lambda a: a.reshape(1, -1)
    out, hl, cn = pl.pallas_call(
        _lru_prompt_kernel,
        grid=(bsz, nt),
        in_specs=[pl.BlockSpec((tm, d), lambda i, j: (i * nt + j, 0)), _const_spec((1, d)),
                  _const_spec(wyx.shape), _const_spec((1, 2 * dr)), _const_spec(cw.shape),
                  _const_spec((1, dr)), _const_spec(wa.shape), _const_spec((1, dr)),
                  _const_spec(wi.shape), _const_spec((1, dr)), _const_spec((1, dr)),
                  _const_spec(wo.shape)],
        out_specs=[pl.BlockSpec((tm, d), lambda i, j: (i * nt + j, 0)),
                   pl.BlockSpec((None, 1, dr), lambda i, j: (i, 0, 0)),
                   pl.BlockSpec((None, CONV_W - 1, dr), lambda i, j: (i, 0, 0))],
        out_shape=[jax.ShapeDtypeStruct((n, d), F32), jax.ShapeDtypeStruct((bsz, 1, dr), F32),
                   jax.ShapeDtypeStruct((bsz, CONV_W - 1, dr), F32)],
        scratch_shapes=[pltpu.VMEM((tm + pad, dr), F32), pltpu.VMEM((tm, dr), F32),
                        pltpu.VMEM((tm, dr), F32), pltpu.VMEM((tm, dr), F32), pltpu.VMEM((1, dr), F32)],
        compiler_params=_params("parallel", "arbitrary"), name="lru_prompt",
    )(x, vec(g), wyx, vec(byx), cw, vec(cb), wa, vec(ba), wi, vec(bi), vec(sp), wo)
    return out, hl.reshape(bsz, dr), cn


def _lru_sample_kernel(bsz, x_ref, g_ref, buf_ref, h0_ref, wyx_ref, byx_ref, cw_ref, cb_ref, wa_ref, ba_ref,
                       wi_ref, bi_ref, sp_ref, wo_ref, o_ref, hl_ref, cn_ref, xpad_ref, hs_ref):
    n, dr = hs_ref.shape
    t = n // bsz
    nb = (CONV_W - 1) * bsz
    x = x_ref[...]
    h = (_rms(x) * g_ref[...]).astype(BF16)
    z = _dot(h, wyx_ref[...]) + byx_ref[...]
    gate = _gelu_tanh(z[:, :dr])
    xpad_ref[0:nb, :] = buf_ref[...]
    xpad_ref[nb:, :] = z[:, dr:]
    xb = cb_ref[...] + xpad_ref[0:n, :] * cw_ref[0:1, :]
    for s in range(1, CONV_W):
        xb = xb + xpad_ref[s * bsz:s * bsz + n, :] * cw_ref[s:s + 1, :]
    a, bt = _lru_gates(xb, wa_ref, ba_ref[...], wi_ref, bi_ref[...], sp_ref[...])
    hcur = h0_ref[...]
    for s in range(t):
        hcur = a[s * bsz:(s + 1) * bsz, :] * hcur + bt[s * bsz:(s + 1) * bsz, :]
        hs_ref[s * bsz:(s + 1) * bsz, :] = hcur
    o_ref[...] = x + _dot((gate * hs_ref[...]).astype(BF16), wo_ref[...])
    hl_ref[...] = hcur
    cn_ref[...] = xpad_ref[n:n + nb, :]


def _lru_sample(x, bsz, h0, conv0, g, wyx, byx, cw, cb, wa, ba, wi, bi, sp, wo):
    n, d = x.shape
    t = n // bsz
    dr = wo.shape[0]
    nb = (CONV_W - 1) * bsz
    tmaj = lambda a, tt: a.reshape(bsz, tt, -1).swapaxes(0, 1).reshape(tt * bsz, -1)
    bmaj = lambda a, tt: a.reshape(tt, bsz, -1).swapaxes(0, 1)
    vec = lambda a: a.reshape(1, -1)
    args = (tmaj(x, t), vec(g), tmaj(conv0, CONV_W - 1), h0, wyx, vec(byx), cw, vec(cb), wa, vec(ba),
            wi, vec(bi), vec(sp), wo)
    out, hl, cn = pl.pallas_call(
        functools.partial(_lru_sample_kernel, bsz),
        in_specs=[pl.BlockSpec(a.shape, lambda nd=a.ndim: (0,) * nd) for a in args],
        out_specs=[pl.BlockSpec((n, d), lambda: (0, 0)), pl.BlockSpec((bsz, dr), lambda: (0, 0)),
                   pl.BlockSpec((nb, dr), lambda: (0, 0))],
        out_shape=[jax.ShapeDtypeStruct((n, d), F32), jax.ShapeDtypeStruct((bsz, dr), F32),
                   jax.ShapeDtypeStruct((nb, dr), F32)],
        scratch_shapes=[pltpu.VMEM((n + nb, dr), F32), pltpu.VMEM((n, dr), F32)],
        compiler_params=pltpu.CompilerParams(vmem_limit_bytes=VMEM_LIMIT), name="lru_sample",
    )(*args)
    return bmaj(out, t).reshape(n, d), hl, bmaj(cn, CONV_W - 1)


def _rope_tables_partial(pos, period, rot_dim, theta):
    half = rot_dim // 2
    inv = theta ** (-jnp.arange(half, dtype=F32) / half)
    ang = pos.astype(F32)[:, None] * inv[None, :]
    cos, sin = jnp.cos(ang), jnp.sin(ang)
    t = pos.shape[0]
    zh = jnp.zeros((t, half), F32)
    rest = period - rot_dim
    c = jnp.concatenate([cos, cos, jnp.ones((t, rest), F32)], axis=1)
    s1 = jnp.concatenate([-sin, zh, jnp.zeros((t, rest), F32)], axis=1)
    s2 = jnp.concatenate([zh, sin, jnp.zeros((t, rest), F32)], axis=1)
    rep = LANES // period
    return [jnp.tile(a, (1, rep)) for a in (c, s1, s2)]


def _rot(x, c, s1, s2, half):
    n = x.shape[-1]
    return x * c + pltpu.roll(x, n - half, 1) * s1 + pltpu.roll(x, half, 1) * s2


def _sort_key(x):
    bits = pltpu.bitcast(x, I32)
    return bits ^ ((bits >> 31) & 0x7FFFFFFF)


RADIX_GROUP = 4


def _kth_largest(count_ge, k, shape):
    c0 = count_ge(jnp.zeros(shape, I32))
    t = jnp.where(c0 >= k, 0, INT_MIN).astype(I32)
    cnt = jnp.where(c0 >= k, c0, 2.0 * k)

    def decide(t, cnt, bit):
        cand = t | lax.shift_left(jnp.int32(1), bit)
        c = count_ge(cand)
        return jnp.where(c >= k, cand, t), jnp.where(c >= k, c, cnt)

    n_head = 31 % RADIX_GROUP
    for bit in range(30, 30 - n_head, -1):
        t, cnt = decide(t, cnt, bit)
    n_groups = (31 - n_head) // RADIX_GROUP

    def cond(state):
        i, _, _, pending = state
        return (i < n_groups) & (pending > 0.5)

    def body(state):
        i, t, cnt, _ = state
        for u in range(RADIX_GROUP):
            t, cnt = decide(t, cnt, 30 - n_head - (i * RADIX_GROUP + u))
        return i + 1, t, cnt, jnp.max(jnp.where(cnt == k, 0.0, 1.0))

    return lax.while_loop(cond, body, (jnp.int32(0), t, cnt, jnp.float32(1.0)))[1]


def _attn_kernel(tq, tk, lk, q_pos0, k_pos0, mode, n_cmp, q_ref, k_ref, v_ref, *rest):
    if mode == "window":
        o_ref, qs_ref, acc_ref, m_scr, l_scr = rest
        m_ref = None
    else:
        m_ref, o_ref, qs_ref, acc_ref, m_scr, l_scr = rest
    p0 = q_pos0 + pl.program_id(1) * tq
    for g in range(N_KV):
        for h in range(GROUP):
            c0 = (g * GROUP + h) * HEAD_DIM
            qs_ref[g, h * tq:(h + 1) * tq, :] = q_ref[:, c0:c0 + HEAD_DIM]
    m_scr[...] = jnp.full(m_scr.shape, NEG, F32)
    l_scr[...] = jnp.zeros(l_scr.shape, F32)
    acc_ref[...] = jnp.zeros(acc_ref.shape, F32)
    hi = jnp.minimum((p0 + tq - 1 - k_pos0) // tk + 1, lk // tk)
    lo = jnp.maximum(p0 - (WINDOW - 1) - k_pos0, 0) // tk if mode == "window" else 0
    rowpos = p0 + lax.broadcasted_iota(I32, (tq, tk), 0)

    def body(kt, carry):
        k0 = pl.multiple_of(kt * tk, tk)
        if mode == "mask":
            bias = m_ref[:, pl.ds(k0, tk)].astype(F32)
        else:
            colpos = k_pos0 + k0 + lax.broadcasted_iota(I32, (tq, tk), 1)
            valid = colpos <= rowpos
            if mode == "window":
                valid = valid & (rowpos - colpos < WINDOW) & (colpos >= 0)
                bias = jnp.where(valid, 0.0, -jnp.inf)
        for g in range(N_KV):
            if mode == "blocksel":
                blk = lax.shift_right_logical(k_pos0 + k0 + lax.broadcasted_iota(I32, (LANES, tk), 1), SLC_SHIFT)
                e = jnp.where(blk == lax.broadcasted_iota(I32, (LANES, tk), 0), 1.0, 0.0).astype(BF16)
                sel = _dot(m_ref[:, g * LANES:(g + 1) * LANES], e) > 0.5
                bias = jnp.where(valid & (sel | (colpos >= n_cmp * SLC_BLK)), 0.0, -jnp.inf)
            kt_ = k_ref[pl.ds(k0, tk), g * HEAD_DIM:(g + 1) * HEAD_DIM]
            vt_ = v_ref[pl.ds(k0, tk), g * HEAD_DIM:(g + 1) * HEAD_DIM]
            s = _dot_nt(qs_ref[g], kt_)
            s = (s.reshape(GROUP, tq, tk) + bias[None]).reshape(GROUP * tq, tk)
            _softmax_update(g, s, vt_, acc_ref, m_scr, l_scr)
        return carry

    lax.fori_loop(lo, hi, body, 0)
    for g in range(N_KV):
        o = acc_ref[g] / jnp.maximum(l_scr[g], SUM_FLOOR)
        for h in range(GROUP):
            c0 = (g * GROUP + h) * HEAD_DIM
            o_ref[:, c0:c0 + HEAD_DIM] = o[h * tq:(h + 1) * tq, :]


def _attention(q, k, v, m, *, bsz, tq, q_pos0, k_pos0, mode, n_cmp=0):
    nq_rows, dq = q.shape
    lk = k.shape[1]
    tk = ATTN_KEY_TILE
    assert lk % tk == 0 and (nq_rows // bsz) % tq == 0
    nq = nq_rows // bsz // tq
    kern = functools.partial(_attn_kernel, tq, tk, lk, q_pos0, k_pos0, mode, n_cmp)
    in_specs = [pl.BlockSpec((tq, dq), lambda b, j: (b * nq + j, 0)),
                pl.BlockSpec((None, lk, k.shape[2]), lambda b, j: (b, 0, 0)),
                pl.BlockSpec((None, lk, v.shape[2]), lambda b, j: (b, 0, 0))]
    args = [q, k, v]
    if mode != "window":
        in_specs.append(pl.BlockSpec((tq, m.shape[1]), lambda b, j: (b * nq + j, 0)))
        args.append(m)
    return pl.pallas_call(
        kern, grid=(bsz, nq), in_specs=in_specs,
        out_specs=pl.BlockSpec((tq, dq), lambda b, j: (b * nq + j, 0)),
        out_shape=jax.ShapeDtypeStruct((nq_rows, dq), F32),
        scratch_shapes=[pltpu.VMEM((N_KV, GROUP * tq, HEAD_DIM), BF16),
                        pltpu.VMEM((N_KV, GROUP * tq, HEAD_DIM), F32),
                        pltpu.VMEM((N_KV, GROUP * tq, LANES), F32),
                        pltpu.VMEM((N_KV, GROUP * tq, LANES), F32)],
        compiler_params=_params("parallel", "arbitrary"), name="attn_" + mode,
    )(*args)


def _softmax_update(g, s, vt, acc_ref, m_scr, l_scr):
    m_old = m_scr[g]
    m_new = jnp.maximum(m_old, jnp.max(s, axis=-1, keepdims=True))
    alpha = jnp.exp2(m_old - m_new)
    p = jnp.exp2(s - jnp.concatenate([m_new] * (s.shape[1] // LANES), axis=1))
    l_scr[g] = alpha * l_scr[g] + jnp.sum(p, axis=-1, keepdims=True)
    acc_ref[g] = alpha * acc_ref[g] + _dot(p.astype(BF16), vt)
    m_scr[g] = m_new


def _paged_attn_kernel(tq, pp, n_steps, mode, pt_ref, q_ref, *rest):
    kp, vp = rest[:pp], rest[pp:2 * pp]
    kn_ref, vn_ref = rest[2 * pp:2 * pp + 2]
    if mode == "mask":
        mb_ref, mt_ref, o_ref, qs_ref, acc_ref, m_scr, l_scr, kbuf, vbuf = rest[2 * pp + 2:]
    else:
        mb_ref, o_ref, qs_ref, acc_ref, m_scr, l_scr, kbuf, vbuf = rest[2 * pp + 2:]
    step = pl.program_id(1)
    tk = pp * PAGE_SIZE

    @pl.when(step == 0)
    def _():
        for g in range(N_KV):
            for h in range(GROUP):
                c0 = (g * GROUP + h) * HEAD_DIM
                qs_ref[g, h * tq:(h + 1) * tq, :] = q_ref[:, c0:c0 + HEAD_DIM]
        m_scr[...] = jnp.full(m_scr.shape, NEG, F32)
        l_scr[...] = jnp.zeros(l_scr.shape, F32)
        acc_ref[...] = jnp.zeros(acc_ref.shape, F32)

    for i in range(pp):
        for g in range(N_KV):
            rows = pl.ds(g, PAGE_SIZE, stride=N_KV)
            kbuf[g, i * PAGE_SIZE:(i + 1) * PAGE_SIZE, :] = kp[i][rows, :].astype(BF16)
            vbuf[g, i * PAGE_SIZE:(i + 1) * PAGE_SIZE, :] = vp[i][rows, :].astype(BF16)

    def biased(g, s, bias):
        n = s.shape[1]
        return (s.reshape(GROUP, tq, n) + bias[None]).reshape(GROUP * tq, n)

    for g in range(N_KV):
        if mode == "mask":
            bias = mb_ref[...].astype(F32)
        else:
            blk = lax.shift_right_logical(step * tk + lax.broadcasted_iota(I32, (LANES, tk), 1), SLC_SHIFT)
            e = jnp.where(blk == lax.broadcasted_iota(I32, (LANES, tk), 0), 1.0, 0.0).astype(BF16)
            bias = jnp.where(_dot(mb_ref[:, g * LANES:(g + 1) * LANES], e) > 0.5, 0.0, -jnp.inf)
        _softmax_update(g, biased(g, _dot_nt(qs_ref[g], kbuf[g]), bias), vbuf[g], acc_ref, m_scr, l_scr)

    @pl.when(step == n_steps - 1)
    def _():
        tn = kn_ref.shape[0]
        causal = lax.broadcasted_iota(I32, (tq, tn), 1) <= lax.broadcasted_iota(I32, (tq, tn), 0)
        for g in range(N_KV):
            sl = slice(g * HEAD_DIM, (g + 1) * HEAD_DIM)
            bias = mt_ref[...].astype(F32) if mode == "mask" else jnp.where(causal, 0.0, -jnp.inf)
            _softmax_update(g, biased(g, _dot_nt(qs_ref[g], kn_ref[:, sl]), bias), vn_ref[:, sl],
                            acc_ref, m_scr, l_scr)
            o = acc_ref[g] / jnp.maximum(l_scr[g], SUM_FLOOR)
            for h in range(GROUP):
                c0 = (g * GROUP + h) * HEAD_DIM
                o_ref[:, c0:c0 + HEAD_DIM] = o[h * tq:(h + 1) * tq, :]


PAGES_PER_STEP = 16
NEW_TILE = LANES


def _paged_attention(q, k_pool, v_pool, page0, page_table, kn, vn, m, *, bsz, tq, mode):
    dq = q.shape[1]
    n_pages = page_table.shape[1]
    pp = PAGES_PER_STEP
    assert n_pages % pp == 0
    n_steps = n_pages // pp
    tk = pp * PAGE_SIZE
    prow = PAGE_SIZE * N_KV
    page = lambda i: pl.BlockSpec((prow, HEAD_DIM), lambda b, s, pt: (page0 + pt[b, s * pp + i], 0))
    in_specs = [pl.BlockSpec((tq, dq), lambda b, s, pt: (b, 0))]
    in_specs += [page(i) for i in range(pp)] * 2
    in_specs += [pl.BlockSpec((NEW_TILE, kn.shape[1]), lambda b, s, pt: (b, 0))] * 2
    args = [q] + [k_pool] * pp + [v_pool] * pp + [kn, vn]
    if mode == "mask":
        in_specs += [pl.BlockSpec((tq, tk), lambda b, s, pt: (b, s)),
                     pl.BlockSpec((tq, NEW_TILE), lambda b, s, pt: (b, n_pages * PAGE_SIZE // NEW_TILE))]
        args += [m, m]
    else:
        in_specs.append(pl.BlockSpec((tq, m.shape[1]), lambda b, s, pt: (b, 0)))
        args.append(m)
    rows = GROUP * tq
    return pl.pallas_call(
        functools.partial(_paged_attn_kernel, tq, pp, n_steps, mode),
        grid_spec=pltpu.PrefetchScalarGridSpec(
            num_scalar_prefetch=1, grid=(bsz, n_steps), in_specs=in_specs,
            out_specs=pl.BlockSpec((tq, dq), lambda b, s, pt: (b, 0)),
            scratch_shapes=[pltpu.VMEM((N_KV, rows, HEAD_DIM), BF16), pltpu.VMEM((N_KV, rows, HEAD_DIM), F32),
                            pltpu.VMEM((N_KV, rows, LANES), F32), pltpu.VMEM((N_KV, rows, LANES), F32),
                            pltpu.VMEM((N_KV, tk, HEAD_DIM), BF16), pltpu.VMEM((N_KV, tk, HEAD_DIM), BF16)]),
        out_shape=jax.ShapeDtypeStruct((bsz * tq, dq), F32),
        compiler_params=_params("parallel", "arbitrary"), name="paged_attn_" + mode,
    )(page_table, *args)


def _heads(c):
    return lambda n, tm: ((n, c // HEAD_DIM, HEAD_DIM), (tm, c // HEAD_DIM, HEAD_DIM), lambda i: (i, 0, 0))


def _store_heads(ref, z):
    for hd in range(ref.shape[1]):
        ref[:, hd, :] = z[:, hd * HEAD_DIM:(hd + 1) * HEAD_DIM]


def _head_norm_rot(z, g, tabs, n_heads, scale, refs_f32, refs_bf16):
    c, s1, s2 = tabs
    for hd in range(n_heads):
        sl = slice(hd * HEAD_DIM, (hd + 1) * HEAD_DIM)
        r = _rot(_rms(z[:, sl]) * g, c, s1, s2, ROPE_DIM // 2)
        for ref in refs_f32:
            ref[:, hd, :] = r
        for ref in refs_bf16:
            ref[:, sl] = (r * scale).astype(BF16)


def _dsa_epilogue(h, w_ref, tabs, cs, o):
    tq_ = [t[...] for t in tabs[:3]]
    ti_ = [t[...] for t in tabs[3:]]
    gq, gk, gik = [c[...] for c in cs]
    q_ref, k_ref, kb_ref, v_ref, vb_ref, qi_ref, ki_ref, wi_ref = o
    nq, nkv = N_HEADS * HEAD_DIM, N_KV * HEAD_DIM
    ni = IDX_HEADS * IDX_DIM
    _head_norm_rot(_dot(h, w_ref[:, 0:nq]), gq, tq_, N_HEADS, ATTN_SCALE * LOG2E, [], [q_ref])
    _head_norm_rot(_dot(h, w_ref[:, nq:nq + nkv]), gk, tq_, N_KV, 1.0, [k_ref], [kb_ref])
    zv = _dot(h, w_ref[:, nq + nkv:nq + 2 * nkv])
    _store_heads(v_ref, zv)
    vb_ref[...] = zv.astype(BF16)
    c0 = nq + 2 * nkv
    zi = _dot(h, w_ref[:, c0:c0 + ni])
    for ch in range(ni // LANES):
        r = _rot(zi[:, ch * LANES:(ch + 1) * LANES], *ti_, IDX_ROPE // 2).astype(BF16)
        for u in range(LANES // IDX_DIM):
            qi_ref[ch * (LANES // IDX_DIM) + u] = r[:, u * IDX_DIM:(u + 1) * IDX_DIM]
    zl = _dot(h, w_ref[:, c0 + ni:c0 + ni + LANES])
    lane = lax.broadcasted_iota(I32, zl.shape, 1)
    ms = jnp.sum(jnp.where(lane < IDX_DIM, zl * zl, 0.0), axis=-1, keepdims=True) / IDX_DIM
    r = _rot(zl * lax.rsqrt(ms + EPS) * gik, *ti_, IDX_ROPE // 2)
    ki_ref[...] = r[:, :IDX_DIM]
    wi_ref[...] = zl[:, IDX_DIM:IDX_DIM + IDX_HEADS] * (IDX_HEADS ** -0.5 * IDX_DIM ** -0.5)


def _dsa_select_kernel(tq, nb, same_batch, tk, lk, q_pos0, top, qi_ref, wi_ref, kit_ref, m_ref, key_ref,
                       wib_ref):
    rows = nb * tq
    p0 = q_pos0 + pl.program_id(1) * (rows if same_batch else tq)
    p0s = [p0 + i * tq if same_batch else p0 for i in range(nb)]
    nk = jnp.minimum((p0s[-1] + tq - 1) // tk + 1, lk // tk)
    wi = wi_ref[...]
    for h in range(IDX_HEADS):
        wib_ref[h] = jnp.broadcast_to(wi[:, h:h + 1], (rows, tk))
    row1 = lax.broadcasted_iota(I32, (tq, tk), 0)
    col1 = lax.broadcasted_iota(I32, (tq, tk), 1)
    rowpos = jnp.concatenate([p + row1 for p in p0s], axis=0)
    col = lax.broadcasted_iota(I32, (rows, tk), 1)

    for i in range(nb):
        def score_pair(kp, carry, i=i):
            for u in range(2):
                k0 = pl.multiple_of((2 * kp + u) * tk, tk)
                kt_ = kit_ref[0 if same_batch else i, :, pl.ds(k0, tk)]
                acc = jnp.zeros((tq, tk), F32)
                if IDX_HEADS * tq <= 512:
                    s_all = _dot(qi_ref[:, i * tq:(i + 1) * tq, :].reshape(IDX_HEADS * tq, IDX_DIM), kt_)
                for h in range(IDX_HEADS):
                    if IDX_HEADS * tq <= 512:
                        s = s_all[h * tq:(h + 1) * tq, :]
                    else:
                        s = _dot(qi_ref[h, i * tq:(i + 1) * tq, :], kt_)
                    acc = acc + jnp.maximum(s, 0.0) * wib_ref[h, i * tq:(i + 1) * tq, :]
                acc = jnp.where(k0 + col1 <= p0s[i] + row1, acc, -jnp.inf)
                key_ref[i * tq:(i + 1) * tq, pl.ds(k0, tk)] = _sort_key(acc)
            return carry

        lax.fori_loop(0, (nk + 1) // 2, score_pair, 0)

    grp = min(rows, SELECT_ROWS)

    def count_ge(cand):
        parts = []
        for r0 in range(0, rows, grp):
            cb = jnp.broadcast_to(cand[r0:r0 + grp], (grp, LANES))

            def body(kt, acc, r0=r0, cb=cb):
                for u in range(2 * tk // LANES):
                    c0 = pl.multiple_of(kt * (2 * tk) + u * LANES, LANES)
                    acc = acc + jnp.where(key_ref[r0:r0 + grp, pl.ds(c0, LANES)] >= cb, 1.0, 0.0)
                return acc

            parts.append(lax.fori_loop(0, (nk + 1) // 2, body, jnp.zeros((grp, LANES), F32)))
        return jnp.sum(jnp.concatenate(parts, axis=0), axis=-1, keepdims=True)

    thr = jnp.broadcast_to(_kth_largest(count_ge, float(top), (rows, 1)), (rows, tk))

    def bias_tile(kt, carry):
        k0 = pl.multiple_of(kt * tk, tk)
        sel = (k0 + col <= rowpos) & (key_ref[:, pl.ds(k0, tk)] >= thr)
        m_ref[:, pl.ds(k0, tk)] = jnp.where(sel, 0.0, -jnp.inf).astype(BF16)
        return carry

    lax.fori_loop(0, nk, bias_tile, 0)

    def rest_tile(kt, carry):
        m_ref[:, pl.ds(pl.multiple_of(kt * tk, tk), tk)] = jnp.full((rows, tk), -jnp.inf, BF16)
        return carry

    lax.fori_loop(nk, lk // tk, rest_tile, 0)


SELECT_ROWS = 128
SELECT_TILES = 2


def _dsa_select(qi, wi, kit, *, bsz, tq, q_pos0, top):
    lk = kit.shape[2]
    tk = SELECT_KEY_TILE
    assert lk % (2 * tk) == 0 and top <= tk
    nq_rows = wi.shape[0]
    nq = nq_rows // bsz // tq
    if nq == 1 and SELECT_ROWS % tq == 0 and bsz % (SELECT_ROWS // tq) == 0:
        nb, same_batch, kb, grid = SELECT_ROWS // tq, False, SELECT_ROWS // tq, (bsz * tq // SELECT_ROWS, 1)
    elif nq % SELECT_TILES == 0:
        nb, same_batch, kb, grid = SELECT_TILES, True, 1, (bsz, nq // SELECT_TILES)
    else:
        nb, same_batch, kb, grid = 1, True, 1, (bsz, nq)
    rows = nb * tq
    nj = grid[1]
    return pl.pallas_call(
        functools.partial(_dsa_select_kernel, tq, nb, same_batch, tk, lk, q_pos0, top),
        grid=grid,
        in_specs=[pl.BlockSpec((IDX_HEADS, rows, IDX_DIM), lambda b, j: (0, b * nj + j, 0)),
                  pl.BlockSpec((rows, IDX_HEADS), lambda b, j: (b * nj + j, 0)),
                  pl.BlockSpec((kb, IDX_DIM, lk), lambda b, j: (b, 0, 0))],
        out_specs=pl.BlockSpec((rows, lk), lambda b, j: (b * nj + j, 0)),
        out_shape=jax.ShapeDtypeStruct((nq_rows, lk), BF16),
        scratch_shapes=[pltpu.VMEM((rows, lk), I32), pltpu.VMEM((IDX_HEADS, rows, tk), F32)],
        compiler_params=_params("parallel", "arbitrary"), name="dsa_select",
    )(qi, wi, kit)


def _nsa_epilogue(h, w_ref, tabs, cs, o):
    tb = [t[...] for t in tabs]
    gq, gsk, gwk = [c[...] for c in cs]
    (qc_ref, q_ref, ck_ref, cv_ref, sk_ref, skb_ref, sv_ref, svb_ref,
     wk_ref, wkb_ref, wv_ref, wvb_ref, gate_ref) = o
    nq, nkv = N_HEADS * HEAD_DIM, N_KV * HEAD_DIM
    zq = _dot(h, w_ref[:, 0:nq])
    for hd in range(N_HEADS):
        sl = slice(hd * HEAD_DIM, (hd + 1) * HEAD_DIM)
        qn = _rms(zq[:, sl]) * gq
        qc_ref[:, sl] = (qn * ATTN_SCALE).astype(BF16)
        q_ref[:, sl] = (_rot(qn, *tb, ROPE_DIM // 2) * (ATTN_SCALE * LOG2E)).astype(BF16)
    seg = lambda i: _dot(h, w_ref[:, nq + i * nkv:nq + (i + 1) * nkv])
    ck_ref[...] = seg(0)
    cv_ref[...] = seg(1)
    _head_norm_rot(seg(2), gsk, tb, N_KV, 1.0, [sk_ref], [skb_ref])
    zsv = seg(3)
    _store_heads(sv_ref, zsv)
    svb_ref[...] = zsv.astype(BF16)
    _head_norm_rot(seg(4), gwk, tb, N_KV, 1.0, [wk_ref], [wkb_ref])
    zwv = seg(5)
    _store_heads(wv_ref, zwv)
    wvb_ref[...] = zwv.astype(BF16)
    zg = _dot(h, w_ref[:, nq + 6 * nkv:nq + 6 * nkv + LANES])
    gate_ref[...] = jax.nn.sigmoid(zg[:, :N_HEADS * 3])


def _nsa_compress_kernel(ck_ref, cv_ref, pwk_ref, pwv_ref, phik_ref, phiv_ref, g_ref, kc_ref, vc_ref):
    rows, wd = ck_ref.shape
    nb = rows // CMP_BLK
    pk = jnp.sum(ck_ref[...].reshape(nb, CMP_BLK, wd) * pwk_ref[...][None], axis=1)
    pv = jnp.sum(cv_ref[...].reshape(nb, CMP_BLK, wd) * pwv_ref[...][None], axis=1)
    for hd in range(N_KV):
        sl = slice(hd * HEAD_DIM, (hd + 1) * HEAD_DIM)
        kc_ref[:, sl] = (_rms(_dot(pk[:, sl].astype(BF16), phik_ref[hd])) * g_ref[...]).astype(BF16)
        vc_ref[:, sl] = _dot(pv[:, sl].astype(BF16), phiv_ref[hd]).astype(BF16)


def _nsa_compress(ck, cv, pwk, pwv, phik, phiv, gck):
    rows, wd = ck.shape
    step = CMP_ROWS if rows % CMP_ROWS == 0 else rows
    nb = step // CMP_BLK
    bc = lambda p: jnp.repeat(p, HEAD_DIM, axis=1)
    return pl.pallas_call(
        _nsa_compress_kernel, grid=(rows // step,),
        in_specs=[pl.BlockSpec((step, wd), lambda i: (i, 0)), pl.BlockSpec((step, wd), lambda i: (i, 0)),
                  _const_spec((CMP_BLK, wd)), _const_spec((CMP_BLK, wd)), _const_spec(phik.shape),
                  _const_spec(phiv.shape), _const_spec((1, HEAD_DIM))],
        out_specs=[pl.BlockSpec((nb, wd), lambda i: (i, 0)), pl.BlockSpec((nb, wd), lambda i: (i, 0))],
        out_shape=[jax.ShapeDtypeStruct((rows // CMP_BLK, wd), BF16)] * 2,
        compiler_params=_params("parallel"), name="nsa_compress",
    )(ck, cv, bc(pwk), bc(pwv), phik, phiv, gck.reshape(1, HEAD_DIM))


def _paged_compress_kernel(pp, pt_ref, *rest):
    ckp, cvp = rest[:pp], rest[pp:2 * pp]
    pwk_ref, pwv_ref, phik_ref, phiv_ref, g_ref, kc_ref, vc_ref, pk_scr, pv_scr = rest[2 * pp:]
    nb = PAGE_SIZE // CMP_BLK
    for hd in range(N_KV):
        sl = slice(hd * HEAD_DIM, (hd + 1) * HEAD_DIM)
        rows = pl.ds(hd, PAGE_SIZE, stride=N_KV)
        for i in range(pp):
            pool = lambda ref, pw: jnp.sum(ref[rows, :].reshape(nb, CMP_BLK, HEAD_DIM) * pw[:, sl][None], axis=1)
            pk_scr[hd, i * nb:(i + 1) * nb, :] = pool(ckp[i], pwk_ref)
            pv_scr[hd, i * nb:(i + 1) * nb, :] = pool(cvp[i], pwv_ref)
        kc_ref[:, sl] = (_rms(_dot(pk_scr[hd].astype(BF16), phik_ref[hd])) * g_ref[...]).astype(BF16)
        vc_ref[:, sl] = _dot(pv_scr[hd].astype(BF16), phiv_ref[hd]).astype(BF16)


def _paged_compress(ck_pool, cv_pool, page0, page_table, pwk, pwv, phik, phiv, gck):
    bsz, n_pages = page_table.shape
    pp = PAGES_PER_STEP
    assert n_pages % pp == 0
    n_steps = n_pages // pp
    nb = pp * PAGE_SIZE // CMP_BLK
    wd = N_KV * HEAD_DIM
    prow = PAGE_SIZE * N_KV
    bc = lambda p: jnp.repeat(p, HEAD_DIM, axis=1)
    page = lambda i: pl.BlockSpec((prow, HEAD_DIM), lambda b, s, pt: (page0 + pt[b, s * pp + i], 0))
    const = lambda shape: pl.BlockSpec(shape, lambda b, s, pt: (0,) * len(shape))
    out = pl.BlockSpec((nb, wd), lambda b, s, pt: (b * n_steps + s, 0))
    return pl.pallas_call(
        functools.partial(_paged_compress_kernel, pp),
        grid_spec=pltpu.PrefetchScalarGridSpec(
            num_scalar_prefetch=1, grid=(bsz, n_steps),
            in_specs=[page(i) for i in range(pp)] * 2
            + [const((CMP_BLK, wd)), const((CMP_BLK, wd)), const(phik.shape), const(phiv.shape),
               const((1, HEAD_DIM))],
            out_specs=[out, out],
            scratch_shapes=[pltpu.VMEM((N_KV, nb, HEAD_DIM), F32), pltpu.VMEM((N_KV, nb, HEAD_DIM), F32)]),
        out_shape=[jax.ShapeDtypeStruct((bsz * n_steps * nb, wd), BF16)] * 2,
        compiler_params=_params("parallel", "parallel"), name="paged_compress",
    )(page_table, *([ck_pool] * pp), *([cv_pool] * pp), bc(pwk), bc(pwv), phik, phiv,
      gck.reshape(1, HEAD_DIM))


def _nsa_cmp_kernel(tq, q_pos0, n_sel, qc_ref, kc_ref, vc_ref, oc_ref, bm_ref):
    p0 = q_pos0 + pl.program_id(1) * tq
    nc = kc_ref.shape[0]
    rowpos = p0 + lax.broadcasted_iota(I32, (tq, nc), 0)
    blk = lax.broadcasted_iota(I32, (tq, nc), 1)
    valid = jnp.concatenate([blk * CMP_BLK + (CMP_BLK - 1) <= rowpos] * GROUP, axis=0)
    cur = lax.shift_right_logical(rowpos, SLC_SHIFT)
    forced = (blk == 0) | (blk == cur) | (blk == cur - 1)
    keys = []
    for g in range(N_KV):
        sl = slice(g * HEAD_DIM, (g + 1) * HEAD_DIM)
        qs = jnp.concatenate([qc_ref[:, (g * GROUP + h) * HEAD_DIM:(g * GROUP + h + 1) * HEAD_DIM]
                              for h in range(GROUP)], axis=0)
        lc = jnp.where(valid, _dot_nt(qs, kc_ref[:, sl]), NEG)
        m = jnp.max(lc, axis=-1, keepdims=True)
        m = jnp.where(m > 0.5 * NEG, m, 0.0)
        p = jnp.where(valid, jnp.exp(lc - m), 0.0)
        pc = p / jnp.maximum(jnp.sum(p, axis=-1, keepdims=True), SUM_FLOOR)
        oc = _dot(pc.astype(BF16), vc_ref[:, sl])
        for h in range(GROUP):
            c0 = (g * GROUP + h) * HEAD_DIM
            oc_ref[:, c0:c0 + HEAD_DIM] = oc[h * tq:(h + 1) * tq, :]
        imp = pc[0:tq]
        for h in range(1, GROUP):
            imp = imp + pc[h * tq:(h + 1) * tq]
        imp = jnp.where(forced, BIG, imp)
        imp = jnp.where(blk > cur, -jnp.inf, imp)
        keys.append(_sort_key(imp))
    key = jnp.concatenate(keys, axis=0)
    count_ge = lambda cand: jnp.sum(jnp.where(key >= cand, 1.0, 0.0), axis=-1, keepdims=True)
    sel = jnp.where(key >= _kth_largest(count_ge, float(n_sel), (N_KV * tq, 1)), 1.0, 0.0).astype(BF16)
    for g in range(N_KV):
        bm_ref[:, g * LANES:(g + 1) * LANES] = sel[g * tq:(g + 1) * tq, :]


def _nsa_cmp(qc, kc, vc, *, bsz, tq, q_pos0, n_sel):
    nq_rows, dq = qc.shape
    nc = kc.shape[1]
    assert nc == LANES
    nq = nq_rows // bsz // tq
    return pl.pallas_call(
        functools.partial(_nsa_cmp_kernel, tq, q_pos0, n_sel), grid=(bsz, nq),
        in_specs=[pl.BlockSpec((tq, dq), lambda b, j: (b * nq + j, 0)),
                  pl.BlockSpec((None, nc, kc.shape[2]), lambda b, j: (b, 0, 0)),
                  pl.BlockSpec((None, nc, vc.shape[2]), lambda b, j: (b, 0, 0))],
        out_specs=[pl.BlockSpec((tq, dq), lambda b, j: (b * nq + j, 0)),
                   pl.BlockSpec((tq, N_KV * LANES), lambda b, j: (b * nq + j, 0))],
        out_shape=[jax.ShapeDtypeStruct((nq_rows, dq), F32),
                   jax.ShapeDtypeStruct((nq_rows, N_KV * LANES), BF16)],
        compiler_params=_params("parallel", "parallel"), name="nsa_cmp",
    )(qc, kc, vc)


def _nsa_combine(gate, oc, os_, ow):
    parts = []
    for hd in range(N_HEADS):
        sl = slice(hd * HEAD_DIM, (hd + 1) * HEAD_DIM)
        parts.append(gate[:, 3 * hd:3 * hd + 1] * oc[:, sl] + gate[:, 3 * hd + 1:3 * hd + 2] * os_[:, sl]
                     + gate[:, 3 * hd + 2:3 * hd + 3] * ow[:, sl])
    return jnp.concatenate(parts, axis=1)


SAMPLE_TQ = 16
PROMPT_TQ = 128
PROMPT_ATTN_TQ = 512
PROMPT_CMP_TQ = 512


def _wide_tile(t, want, base):
    return want if t % want == 0 else base
KEY_ALIGN = ATTN_KEY_TILE


def _pad_rows(a, bsz, t, tp):
    return jnp.pad(a.reshape(bsz, t, -1), ((0, 0), (0, tp - t), (0, 0))).reshape(bsz * tp, -1)


def _unpad_rows(a, bsz, t, tp):
    return a.reshape(bsz, tp, -1)[:, :t].reshape(bsz * t, -1)


def _cat_keys(old, new, bsz):
    new = new.reshape(bsz, -1, new.shape[-1])
    n = old.shape[1] + new.shape[1]
    return jnp.pad(jnp.concatenate([old.astype(new.dtype), new], axis=1),
                   ((0, 0), (0, -n % KEY_ALIGN), (0, 0)))


def _tile_tables(tabs, bsz, t):
    return tabs if t % ROW_TILE == 0 else [jnp.tile(a, (bsz, 1)) for a in tabs]


def _mixer_c(x, bsz, pos, past, g, wcat, gq, gk, gik, wo):
    n, d = x.shape
    t = n // bsz
    tabs = _tile_tables(_rope_tables_partial(pos, HEAD_DIM, ROPE_DIM, ROPE_THETA)
                        + _rope_tables_partial(pos, IDX_DIM, IDX_ROPE, ROPE_THETA), bsz, t)
    nkv = N_KV * HEAD_DIM
    q, k, kb, v, vb, qi, ki, wi = _proj(
        x, g, wcat, tabs, [gq, gk, gik], _dsa_epilogue,
        [(_rows(N_HEADS * HEAD_DIM), BF16), (_heads(nkv), F32), (_rows(nkv), BF16), (_heads(nkv), F32),
         (_rows(nkv), BF16),
         (lambda n_, tm: ((IDX_HEADS, n_, IDX_DIM), (IDX_HEADS, tm, IDX_DIM), lambda i: (0, i, 0)), BF16),
         (_rows(IDX_DIM), F32), (_rows(IDX_HEADS), F32)], tabs[0].shape[0], "proj_dsa")
    if past is None:
        tq, q_pos0, n_keys = PROMPT_TQ, 0, t
        kit = ki.reshape(bsz, t, IDX_DIM).swapaxes(1, 2).astype(BF16)
        kall, vall = kb.reshape(bsz, t, nkv), vb.reshape(bsz, t, nkv)
    else:
        tq, q_pos0 = SAMPLE_TQ, past["kidx"].shape[1]
        n_keys = q_pos0 + t
        kit = _cat_keys(past["kidx"], ki, bsz).swapaxes(1, 2).astype(BF16)
        q, wi = _pad_rows(q, bsz, t, tq), _pad_rows(wi, bsz, t, tq)
        qi = jnp.pad(qi.reshape(IDX_HEADS, bsz, t, IDX_DIM), ((0, 0), (0, 0), (0, tq - t), (0, 0))
                     ).reshape(IDX_HEADS, bsz * tq, IDX_DIM)
    bias = _dsa_select(qi, wi, kit, bsz=bsz, tq=tq, q_pos0=q_pos0, top=min(TOPK_MAX, n_keys // 4))
    if past is None:
        o = _attention(q, kall, vall, bias, bsz=bsz, tq=_wide_tile(t, PROMPT_ATTN_TQ, tq), q_pos0=q_pos0,
                       k_pos0=0, mode="mask")
    else:
        new = lambda a: _pad_rows(a.reshape(n, nkv), bsz, t, NEW_TILE).astype(BF16)
        o = _paged_attention(q, past["k_pool"], past["v_pool"], past["page0"], past["pt"], new(k), new(v), bias,
                             bsz=bsz, tq=tq, mode="mask")
        o = _unpad_rows(o, bsz, t, tq)
    return _outproj(x, wo, [o], lambda a: a), k, v, ki


def _mixer_d(x, bsz, pos, past, g, wcat, gq, gsk, gwk, cmpw, wo):
    n, d = x.shape
    t = n // bsz
    tabs = _tile_tables(_rope_tables_partial(pos, HEAD_DIM, ROPE_DIM, ROPE_THETA), bsz, t)
    nkv = N_KV * HEAD_DIM
    kvo = [(_heads(nkv), F32), (_rows(nkv), BF16)]
    (qc, q, ck, cv, sk, skb, sv, svb, wk, wkb, wv, wvb, gate) = _proj(
        x, g, wcat, tabs, [gq, gsk, gwk], _nsa_epilogue,
        [(_rows(N_HEADS * HEAD_DIM), BF16), (_rows(N_HEADS * HEAD_DIM), BF16), (_rows(nkv), F32),
         (_rows(nkv), F32)] + kvo * 4 + [(_rows(N_HEADS * 3), F32)], tabs[0].shape[0], "proj_nsa")
    if past is None:
        tq, q_pos0, n_keys, win_pos0 = PROMPT_TQ, 0, t, 0
        kc, vc = _nsa_compress(ck, cv, *cmpw)
        three = lambda a: a.reshape(bsz, t, nkv)
        skall, svall, wkall, wvall = three(skb), three(svb), three(wkb), three(wvb)
    else:
        tq, q_pos0 = SAMPLE_TQ, past["pt"].shape[1] * PAGE_SIZE
        n_keys = q_pos0 + t
        win_pos0 = q_pos0 - past["wk"].shape[1]
        assert q_pos0 % CMP_BLK == 0 and t < CMP_BLK
        kc, vc = _paged_compress(past["ck_pool"], past["cv_pool"], past["page0"], past["pt"], *cmpw)
        wkall = _cat_keys(past["wk"], wk.reshape(n, nkv), bsz).astype(BF16)
        wvall = _cat_keys(past["wv"], wv.reshape(n, nkv), bsz).astype(BF16)
        qc, q = _pad_rows(qc, bsz, t, tq), _pad_rows(q, bsz, t, tq)
    n_cmp = kc.shape[0] // bsz
    assert n_cmp <= LANES
    lane_pad = lambda a: jnp.pad(a.reshape(bsz, n_cmp, nkv), ((0, 0), (0, LANES - n_cmp), (0, 0)))
    kc, vc = lane_pad(kc), lane_pad(vc)
    n_slc = -(-n_keys // SLC_BLK)
    n_lane = min(n_slc, LANES)
    assert n_slc == n_lane or (n_slc == n_lane + 1 and q_pos0 // SLC_BLK == n_lane)
    n_sel = min(N_SLC, n_slc) - (n_slc - n_lane)
    tq_cmp = _wide_tile(t, PROMPT_CMP_TQ, tq) if past is None else tq
    oc, bm = _nsa_cmp(qc, kc, vc, bsz=bsz, tq=tq_cmp, q_pos0=q_pos0, n_sel=n_sel)
    if past is None:
        os_ = _attention(q, skall, svall, bm, bsz=bsz, tq=_wide_tile(t, PROMPT_ATTN_TQ, tq), q_pos0=q_pos0,
                         k_pos0=0, mode="blocksel", n_cmp=n_lane)
    else:
        new = lambda a: _pad_rows(a.reshape(n, nkv), bsz, t, NEW_TILE).astype(BF16)
        os_ = _paged_attention(q, past["sk_pool"], past["sv_pool"], past["page0"], past["pt"], new(sk), new(sv),
                               bm, bsz=bsz, tq=tq, mode="blocksel")
    ow = _attention(q, wkall, wvall, None, bsz=bsz, tq=tq if past else _wide_tile(t, PROMPT_ATTN_TQ, tq),
                    q_pos0=q_pos0, k_pos0=win_pos0, mode="window")
    if past is not None:
        oc, os_, ow = (_unpad_rows(a, bsz, t, tq) for a in (oc, os_, ow))
    return _outproj(x, wo, [gate, oc, os_, ow], _nsa_combine), (ck, cv, sk, sv, wk, wv)


def _gather_pages(cache, page_table):
    rows = cache[page_table]
    return rows.reshape(page_table.shape[0], page_table.shape[1] * PAGE_SIZE, -1)


def kernel(x_prompt, x_sample, state_a_ret, state_b_h, state_b_conv, cache_c_k, cache_c_v, cache_c_kidx,
           cache_d_ck, cache_d_cv, cache_d_sk, cache_d_sv, state_d_wk, state_d_wv, page_table,
           ffn1_norm, ffn1_wg, ffn1_wu, ffn1_wd, mix_norm, ffn2_norm, ffn2_wg, ffn2_wu, ffn2_wd,
           a_wq, a_wk, a_wv, a_wg, a_wo,
           b_wy, b_by, b_wx, b_bx, b_conv_w, b_conv_b, b_wa, b_ba, b_wi, b_bi, b_lam, b_wo,
           c_wq, c_gq, c_wk, c_gk, c_wv, c_wo, c_wiq, c_wik, c_gik, c_wiw,
           d_wq, d_gq, d_wck, d_wcv, d_pwk, d_pwv, d_phik, d_phiv, d_gck, d_wsk, d_wsv, d_gsk,
           d_wwk, d_wwv, d_gwk, d_wgate, d_wo):
    bp, tp, d = x_prompt.shape
    bs, ts, _ = x_sample.shape
    depth = ffn1_norm.shape[0]
    past_len = page_table.shape[1] * PAGE_SIZE
    pos_p = jnp.arange(tp, dtype=I32)
    pos_s = past_len + jnp.arange(ts, dtype=I32)
    b16 = lambda a: a.astype(BF16)
    row = lambda a: a.reshape(1, -1)
    nkv = N_KV * HEAD_DIM
    pool = lambda c: c.reshape(-1, HEAD_DIM)
    ffn1_w = [_to_bf16(w) for w in (ffn1_wg, ffn1_wu, ffn1_wd)]
    ffn2_w = [_to_bf16(w) for w in (ffn2_wg, ffn2_wu, ffn2_wd)]

    xp = x_prompt.reshape(bp * tp, d)
    xs = x_sample.reshape(bs * ts, d)
    outs = [[] for _ in range(24)]
    for i in range(depth):
        m, j = i % 4, i // 4
        xp, xs = _ffn(xp, ffn1_norm[i], *ffn1_w, i), _ffn(xs, ffn1_norm[i], *ffn1_w, i)
        g = mix_norm[i]
        if m == 0:
            dk = a_wq.shape[2] // RET_HEADS
            dv = a_wv.shape[2] // RET_HEADS
            wcat = b16(jnp.concatenate([a_wq[j], a_wk[j], a_wv[j], a_wg[j]], axis=1))
            wo = b16(a_wo[j])
            xp, s_p = _mixer_a(xp, g, pos_p, bp, jnp.zeros((bp, RET_HEADS, dk, dv), F32), wcat, wo, dk, dv)
            xs, s_s = _mixer_a(xs, g, pos_s, bs, state_a_ret[j], wcat, wo, dk, dv)
            new = [s_p, s_s]
            base = 0
        elif m == 1:
            common = (g, b16(jnp.concatenate([b_wy[j], b_wx[j]], axis=1)), jnp.concatenate([b_by[j], b_bx[j]]),
                      b_conv_w[j], b_conv_b[j], b16(b_wa[j]), b_ba[j].reshape(-1), b16(b_wi[j]),
                      b_bi[j].reshape(-1), jax.nn.softplus(-b_lam[j]), b16(b_wo[j]))
            xp, h_p, c_p = _lru_prompt(xp, bp, *common)
            xs, h_s, c_s = _lru_sample(xs, bs, state_b_h[j], state_b_conv[j], *common)
            new = [h_p, h_s, c_p, c_s]
            base = 2
        elif m == 2:
            zpad = jnp.zeros((d, LANES - IDX_DIM - IDX_HEADS), F32)
            wcat = b16(jnp.concatenate([c_wq[j], c_wk[j], c_wv[j], c_wiq[j], c_wik[j], c_wiw[j], zpad], axis=1))
            gik = jnp.concatenate([c_gik[j], jnp.zeros((LANES - IDX_DIM,), F32)])
            cw = (g, wcat, row(c_gq[j]), row(c_gk[j]), row(gik), b16(c_wo[j]))
            past = dict(kidx=_gather_pages(cache_c_kidx[j], page_table), k_pool=pool(cache_c_k),
                        v_pool=pool(cache_c_v), page0=j * cache_c_k.shape[1], pt=page_table)
            xp, k_p, v_p, i_p = _mixer_c(xp, bp, pos_p, None, *cw)
            xs, k_s, v_s, i_s = _mixer_c(xs, bs, pos_s, past, *cw)
            kv = lambda a, b_, t_: a.reshape(b_, t_, N_KV, HEAD_DIM)
            new = [kv(k_p, bp, tp), kv(k_s, bs, ts), kv(v_p, bp, tp), kv(v_s, bs, ts),
                   i_p.reshape(bp, tp, IDX_DIM), i_s.reshape(bs, ts, IDX_DIM)]
            base = 6
        else:
            zpad = jnp.zeros((d, LANES - N_HEADS * 3), F32)
            wcat = b16(jnp.concatenate([d_wq[j], d_wck[j], d_wcv[j], d_wsk[j], d_wsv[j], d_wwk[j], d_wwv[j],
                                        d_wgate[j], zpad], axis=1))
            cmpw = (d_pwk[j], d_pwv[j], b16(d_phik[j]), b16(d_phiv[j]), d_gck[j])
            dw = (g, wcat, row(d_gq[j]), row(d_gsk[j]), row(d_gwk[j]), cmpw, b16(d_wo[j]))
            wbuf = state_d_wk.shape[2]
            past = dict(ck_pool=pool(cache_d_ck), cv_pool=pool(cache_d_cv), sk_pool=pool(cache_d_sk),
                        sv_pool=pool(cache_d_sv), page0=j * cache_d_ck.shape[1], pt=page_table,
                        wk=state_d_wk[j].reshape(bs, wbuf, nkv), wv=state_d_wv[j].reshape(bs, wbuf, nkv))
            xp, rows_p = _mixer_d(xp, bp, pos_p, None, *dw)
            xs, rows_s = _mixer_d(xs, bs, pos_s, past, *dw)
            kv = lambda a, b_: a.reshape(b_, -1, N_KV, HEAD_DIM)
            new = []
            for a_p, a_s in zip(rows_p[:4], rows_s[:4]):
                new += [kv(a_p, bp), kv(a_s, bs)]
            wb_p = min(WINDOW, tp)
            for a_p, a_s, st in ((rows_p[4], rows_s[4], state_d_wk[j]), (rows_p[5], rows_s[5], state_d_wv[j])):
                new += [kv(a_p, bp)[:, tp - wb_p:], jnp.concatenate([st, kv(a_s, bs)], axis=1)[:, ts:]]
            base = 12
        for off, a in enumerate(new):
            outs[base + off].append(a)
        xp, xs = _ffn(xp, ffn2_norm[i], *ffn2_w, i), _ffn(xs, ffn2_norm[i], *ffn2_w, i)
    return (xp.reshape(bp, tp, d), xs.reshape(bs, ts, d)) + tuple(jnp.stack(o) for o in outs)
```

```python
import functools
import math

import jax
import jax.numpy as jnp
from jax import lax
from jax.experimental import pallas as pl
from jax.experimental.pallas import tpu as pltpu

F32 = jnp.float32
BF16 = jnp.bfloat16
I32 = jnp.int32

EPS = 1e-6
PAGE_SIZE = 128

RET_HEADS = 4
RET_CHUNK = 128
RET_WIDE_CHUNK = 512
RET_THETA = 10000.0

LRU_BLOCKS = 4
CONV_W = 4
LRU_C = 8.0

N_HEADS = 8
HEAD_DIM = 128
N_KV = 2
GROUP = N_HEADS // N_KV
ROPE_DIM = HEAD_DIM // 4
ROPE_THETA = 500000.0

IDX_HEADS = 16
IDX_DIM = 64
IDX_ROPE = IDX_DIM // 4
TOPK_MAX = 256

CMP_BLK = 64
SLC_BLK = 64
N_SLC = 16
WINDOW = 512
BIG = 1e4

LANES = 128
SUBLANES = 8
VMEM_LIMIT = 56 * 1024 * 1024
ROW_TILE = 512
LRU_TILE = 256
ATTN_KEY_TILE = 512
SELECT_KEY_TILE = 256
CMP_ROWS = 2048
SLC_SHIFT = 6
SUM_FLOOR = 1e-30
NEG = -1e30
ATTN_SCALE = HEAD_DIM ** -0.5
LOG2E = math.log2(math.e)
INT_MIN = -2147483648


def _params(*sem):
    return pltpu.CompilerParams(dimension_semantics=sem, vmem_limit_bytes=VMEM_LIMIT)


def _const_spec(shape):
    nd = len(shape)
    return pl.BlockSpec(shape, lambda *_: (0,) * nd, pipeline_mode=pl.Buffered(1))


def _rms(x):
    return x * lax.rsqrt(jnp.mean(x * x, axis=-1, keepdims=True) + EPS)


def _dot(a, b):
    return jnp.dot(a, b, preferred_element_type=F32)


def _dot_nt(a, b):
    return lax.dot_general(a, b, (((1,), (1,)), ((), ())), preferred_element_type=F32)


def _row_tile(n):
    return ROW_TILE if n % ROW_TILE == 0 else n


def _ffn_kernel(x_ref, g_ref, wg_ref, wu_ref, wd_ref, o_ref):
    x = x_ref[...]
    h = (_rms(x) * g_ref[...]).astype(BF16)
    gt = _dot(h, wg_ref[...])
    ut = _dot(h, wu_ref[...])
    a = (gt * jax.nn.sigmoid(gt) * ut).astype(BF16)
    o_ref[...] = x + 0.5 * _dot(a, wd_ref[...])


def _ffn(x, g, wg, wu, wd, layer):
    n, d = x.shape
    f = wg.shape[2]
    tm = _row_tile(n)
    wspec = lambda a, b_: pl.BlockSpec((None, a, b_), lambda i: (layer, 0, 0), pipeline_mode=pl.Buffered(1))
    return pl.pallas_call(
        _ffn_kernel,
        grid=(n // tm,),
        in_specs=[pl.BlockSpec((tm, d), lambda i: (i, 0)), _const_spec((1, d)),
                  wspec(d, f), wspec(d, f), wspec(f, d)],
        out_specs=pl.BlockSpec((tm, d), lambda i: (i, 0)),
        out_shape=jax.ShapeDtypeStruct((n, d), F32),
        compiler_params=_params("parallel"), name="ffn",
    )(x, g.reshape(1, d), wg, wu, wd)


def _cast_kernel(x_ref, o_ref):
    o_ref[...] = x_ref[...].astype(o_ref.dtype)


def _to_bf16(w):
    c = w.shape[-1]
    r = w.size // c
    tm = _row_tile(r)
    out = pl.pallas_call(
        _cast_kernel, grid=(r // tm,),
        in_specs=[pl.BlockSpec((tm, c), lambda i: (i, 0))],
        out_specs=pl.BlockSpec((tm, c), lambda i: (i, 0)),
        out_shape=jax.ShapeDtypeStruct((r, c), BF16),
        compiler_params=_params("parallel"), name="to_bf16",
    )(w.reshape(r, c))
    return out.reshape(w.shape)


def _proj(x, g, w, tables, consts, epilogue, outs, n_tab_rows, name):
    n, d = x.shape
    tm = _row_tile(n)
    assert n_tab_rows % tm == 0
    nt = n_tab_rows // tm
    nw = w.shape[1]

    def kern(x_ref, g_ref, w_ref, *rest):
        tabs = rest[:len(tables)]
        cs = rest[len(tables):len(tables) + len(consts)]
        o = rest[len(tables) + len(consts):]
        h = (_rms(x_ref[...]) * g_ref[...]).astype(BF16)
        epilogue(h, w_ref, tabs, cs, o)

    in_specs = [pl.BlockSpec((tm, d), lambda i: (i, 0)), _const_spec((1, d)), _const_spec((d, nw))]
    in_specs += [pl.BlockSpec((tm, t.shape[1]), lambda i: (i % nt, 0)) for t in tables]
    in_specs += [_const_spec(c.shape) for c in consts]
    out_shape, out_specs = [], []
    for fn, dt in outs:
        ashape, bshape, imap = fn(n, tm)
        out_shape.append(jax.ShapeDtypeStruct(ashape, dt))
        out_specs.append(pl.BlockSpec(bshape, imap))
    return pl.pallas_call(
        kern, grid=(n // tm,), in_specs=in_specs, out_specs=out_specs, out_shape=out_shape,
        compiler_params=_params("parallel"), name=name,
    )(x, g.reshape(1, d), w, *tables, *consts)


def _rows(c):
    return lambda n, tm: ((n, c), (tm, c), lambda i: (i, 0))


def _outproj(x, w, ins, combine):
    n, d = x.shape
    tm = _row_tile(n)

    def kern(x_ref, w_ref, *rest):
        o_ref = rest[-1]
        a = combine(*[r[...] for r in rest[:-1]]).astype(BF16)
        o_ref[...] = x_ref[...] + _dot(a, w_ref[...])

    return pl.pallas_call(
        kern, grid=(n // tm,),
        in_specs=[pl.BlockSpec((tm, d), lambda i: (i, 0)), _const_spec(w.shape)]
        + [pl.BlockSpec((tm, a.shape[1]), lambda i: (i, 0)) for a in ins],
        out_specs=pl.BlockSpec((tm, d), lambda i: (i, 0)),
        out_shape=jax.ShapeDtypeStruct((n, d), F32),
        compiler_params=_params("parallel"), name="outproj",
    )(x, w, *ins)


def _ret_epilogue(dk, dv, h, w_ref, tabs, cs, o):
    cos, sin = tabs[0][...], tabs[1][...]
    q_ref, k_ref, v_ref, g_ref = o
    half = dk // 2
    nqk = RET_HEADS * dk
    for seg, ref, scale in ((0, q_ref, 1.0), (1, k_ref, dk ** -0.5)):
        z = _dot(h, w_ref[:, seg * nqk:(seg + 1) * nqk])
        for hd in range(RET_HEADS):
            x1 = z[:, hd * dk:hd * dk + half]
            x2 = z[:, hd * dk + half:(hd + 1) * dk]
            ref[:, hd * dk:hd * dk + half] = (x1 * cos - x2 * sin) * scale
            ref[:, hd * dk + half:(hd + 1) * dk] = (x2 * cos + x1 * sin) * scale
    nv = RET_HEADS * dv
    v_ref[...] = _dot(h, w_ref[:, 2 * nqk:2 * nqk + nv]).astype(BF16)
    zg = _dot(h, w_ref[:, 2 * nqk + nv:2 * nqk + 2 * nv])
    g_ref[...] = zg * jax.nn.sigmoid(zg)


def _ret_chunk_kernel(q_ref, k_ref, v_ref, s0_ref, dm_ref, xi_ref, zt_ref, gc_ref, o_ref, s_ref):
    c = pl.program_id(2)

    @pl.when(c == 0)
    def _():
        s_ref[...] = s0_ref[...]

    nh, dk, dv = s_ref.shape
    for hd in range(nh):
        q = q_ref[:, hd * dk:(hd + 1) * dk]
        k = k_ref[:, hd * dk:(hd + 1) * dk]
        v = v_ref[:, hd * dv:(hd + 1) * dv]
        s = s_ref[hd]
        att = (_dot_nt(q.astype(BF16), k.astype(BF16)) * dm_ref[hd]).astype(BF16)
        o = _dot(att, v) + _dot((q * xi_ref[hd]).astype(BF16), s.astype(BF16))
        kz = (k * zt_ref[hd]).astype(BF16)
        s_ref[hd] = s * gc_ref[hd] + lax.dot_general(kz, v, (((0,), (0,)), ((), ())),
                                                     preferred_element_type=F32)
        o_ref[:, hd * dv:(hd + 1) * dv] = _rms(o)


def _retention(q, k, v, s0, n_valid):
    b, hh, dk, dv = s0.shape
    t = q.shape[0] // b
    c = RET_WIDE_CHUNK if (n_valid % RET_CHUNK == 0 and t % RET_WIDE_CHUNK == 0) else RET_CHUNK
    nc = t // c
    cc = c if n_valid % c == 0 else n_valid
    lg = jnp.log1p(-jnp.exp2(-5.0 - jnp.arange(hh, dtype=F32)))
    idx = jnp.arange(c, dtype=F32)
    diff = idx[:, None] - idx[None, :]
    dmask = jnp.where(diff >= 0, jnp.exp(lg[:, None, None] * jnp.maximum(diff, 0.0)), 0.0)
    xi = jnp.exp(lg[:, None] * (idx + 1.0))[:, :, None]
    zeta = jnp.exp(lg[:, None] * (cc - 1.0 - idx))[:, :, None]
    g_c = jnp.exp(lg * cc)[:, None, None]
    nh = 2 if hh % 2 == 0 else 1
    return pl.pallas_call(
        _ret_chunk_kernel,
        grid=(b, hh // nh, nc),
        in_specs=[pl.BlockSpec((c, nh * dk), lambda i, j, l: (i * nc + l, j)),
                  pl.BlockSpec((c, nh * dk), lambda i, j, l: (i * nc + l, j)),
                  pl.BlockSpec((c, nh * dv), lambda i, j, l: (i * nc + l, j)),
                  pl.BlockSpec((None, nh, dk, dv), lambda i, j, l: (i, j, 0, 0)),
                  pl.BlockSpec((nh, c, c), lambda i, j, l: (j, 0, 0)),
                  pl.BlockSpec((nh, c, 1), lambda i, j, l: (j, 0, 0)),
                  pl.BlockSpec((nh, c, 1), lambda i, j, l: (j, 0, 0)),
                  pl.BlockSpec((nh, 1, 1), lambda i, j, l: (j, 0, 0))],
        out_specs=[pl.BlockSpec((c, nh * dv), lambda i, j, l: (i * nc + l, j)),
                   pl.BlockSpec((None, nh, dk, dv), lambda i, j, l: (i, j, 0, 0))],
        out_shape=[jax.ShapeDtypeStruct((b * t, hh * dv), F32),
                   jax.ShapeDtypeStruct((b, hh, dk, dv), F32)],
        compiler_params=_params("parallel", "parallel", "arbitrary"), name="retention",
    )(q, k, v, s0, dmask, xi, zeta, g_c)


def _rope_tables_full(pos, dim, theta):
    half = dim // 2
    inv = theta ** (-jnp.arange(half, dtype=F32) / half)
    ang = pos.astype(F32)[:, None] * inv[None, :]
    return jnp.cos(ang), jnp.sin(ang)


def _mixer_a(x, g, pos, bsz, s0, wcat, wo, dk, dv):
    n, d = x.shape
    t = n // bsz
    cos, sin = _rope_tables_full(pos, dk, RET_THETA)
    if t % RET_CHUNK:
        cos = jnp.tile(cos, (bsz, 1))
        sin = jnp.tile(sin, (bsz, 1))
    q, k, v, sg = _proj(
        x, g, wcat, [cos, sin], [], functools.partial(_ret_epilogue, dk, dv),
        [(_rows(RET_HEADS * dk), F32), (_rows(RET_HEADS * dk), F32),
         (_rows(RET_HEADS * dv), BF16), (_rows(RET_HEADS * dv), F32)], cos.shape[0], "proj_ret")
    if t % RET_CHUNK:
        pad = lambda a: jnp.pad(a.reshape(bsz, t, -1), ((0, 0), (0, RET_CHUNK - t), (0, 0))
                                ).reshape(bsz * RET_CHUNK, -1)
        o, s_new = _retention(pad(q), pad(k), pad(v), s0, t)
        o = o.reshape(bsz, RET_CHUNK, -1)[:, :t].reshape(n, -1)
    else:
        o, s_new = _retention(q, k, v, s0, t)
    return _outproj(x, wo, [sg, o], lambda a, b: a * b), s_new


def _gelu_tanh(x):
    return 0.5 * x * (1.0 + jnp.tanh(math.sqrt(2.0 / math.pi) * (x + 0.044715 * (x * x * x))))


def _lru_gates(xb, wa_ref, ba, wi_ref, bi, sp):
    bw = wa_ref.shape[1]
    xbb = xb.astype(BF16)
    r = jnp.concatenate([_dot(xbb[:, n * bw:(n + 1) * bw], wa_ref[n]) for n in range(LRU_BLOCKS)], axis=1)
    i = jnp.concatenate([_dot(xbb[:, n * bw:(n + 1) * bw], wi_ref[n]) for n in range(LRU_BLOCKS)], axis=1)
    r = jax.nn.sigmoid(r + ba)
    i = jax.nn.sigmoid(i + bi)
    log_a = -LRU_C * r * sp
    a = jnp.exp(log_a)
    th = jnp.tanh(log_a)
    bt = jnp.sqrt(-2.0 * th / (1.0 - th)) * (i * xb)
    return a, bt


def _lru_prompt_kernel(x_ref, g_ref, wyx_ref, byx_ref, cw_ref, cb_ref, wa_ref, ba_ref, wi_ref, bi_ref,
                       sp_ref, wo_ref, o_ref, hl_ref, cn_ref, xpad_ref, a_ref, b_ref, hs_ref, h_ref):
    j = pl.program_id(1)
    tm, dr = a_ref.shape
    pad = xpad_ref.shape[0] - tm

    @pl.when(j == 0)
    def _():
        h_ref[...] = jnp.zeros_like(h_ref)
        xpad_ref[0:pad, :] = jnp.zeros((pad, dr), F32)

    x = x_ref[...]
    h = (_rms(x) * g_ref[...]).astype(BF16)
    z = _dot(h, wyx_ref[...]) + byx_ref[...]
    gate = _gelu_tanh(z[:, :dr])
    xpad_ref[pad:, :] = z[:, dr:]
    xb = cb_ref[...] + z[:, dr:] * cw_ref[CONV_W - 1:CONV_W, :]
    for s in range(1, CONV_W):
        xb = xb + xpad_ref[pad - s:pad - s + tm, :] * cw_ref[CONV_W - 1 - s:CONV_W - s, :]
    xpad_ref[0:pad, :] = xpad_ref[tm:tm + pad, :]
    a, bt = _lru_gates(xb, wa_ref, ba_ref[...], wi_ref, bi_ref[...], sp_ref[...])
    a_ref[...] = a
    b_ref[...] = bt

    def step(t, hprev):
        hn = a_ref[pl.ds(t, 1), :] * hprev + b_ref[pl.ds(t, 1), :]
        hs_ref[pl.ds(t, 1), :] = hn
        return hn

    hlast = lax.fori_loop(0, tm, step, h_ref[...])
    h_ref[...] = hlast
    o_ref[...] = x + _dot((gate * hs_ref[...]).astype(BF16), wo_ref[...])
    hl_ref[...] = hlast
    cn_ref[...] = xpad_ref[pad - (CONV_W - 1):pad, :]


def _lru_prompt(x, bsz, g, wyx, byx, cw, cb, wa, ba, wi, bi, sp, wo):
    n, d = x.shape
    t = n // bsz
    dr = wo.shape[0]
    tm = LRU_TILE if t % LRU_TILE == 0 else t
    nt = t // tm
    pad = SUBLANES
    vec = lambda a: a.reshape(1, -1)
    out, hl, cn = pl.pallas_call(
        _lru_prompt_kernel,
        grid=(bsz, nt),
        in_specs=[pl.BlockSpec((tm, d), lambda i, j: (i * nt + j, 0)), _const_spec((1, d)),
                  _const_spec(wyx.shape), _const_spec((1, 2 * dr)), _const_spec(cw.shape),
                  _const_spec((1, dr)), _const_spec(wa.shape), _const_spec((1, dr)),
                  _const_spec(wi.shape), _const_spec((1, dr)), _const_spec((1, dr)),
                  _const_spec(wo.shape)],
        out_specs=[pl.BlockSpec((tm, d), lambda i, j: (i * nt + j, 0)),
                   pl.BlockSpec((None, 1, dr), lambda i, j: (i, 0, 0)),
                   pl.BlockSpec((None, CONV_W - 1, dr), lambda i, j: (i, 0, 0))],
        out_shape=[jax.ShapeDtypeStruct((n, d), F32), jax.ShapeDtypeStruct((bsz, 1, dr), F32),
                   jax.ShapeDtypeStruct((bsz, CONV_W - 1, dr), F32)],
        scratch_shapes=[pltpu.VMEM((tm + pad, dr), F32), pltpu.VMEM((tm, dr), F32),
                        pltpu.VMEM((tm, dr), F32), pltpu.VMEM((tm, dr), F32), pltpu.VMEM((1, dr), F32)],
        compiler_params=_params("parallel", "arbitrary"), name="lru_prompt",
    )(x, vec(g), wyx, vec(byx), cw, vec(cb), wa, vec(ba), wi, vec(bi), vec(sp), wo)
    return out, hl.reshape(bsz, dr), cn


def _lru_sample_kernel(bsz, x_ref, g_ref, buf_ref, h0_ref, wyx_ref, byx_ref, cw_ref, cb_ref, wa_ref, ba_ref,
                       wi_ref, bi_ref, sp_ref, wo_ref, o_ref, hl_ref, cn_ref, xpad_ref, hs_ref):
    n, dr = hs_ref.shape
    t = n // bsz
    nb = (CONV_W - 1) * bsz
    x = x_ref[...]
    h = (_rms(x) * g_ref[...]).astype(BF16)
    z = _dot(h, wyx_ref[...]) + byx_ref[...]
    gate = _gelu_tanh(z[:, :dr])
    xpad_ref[0:nb, :] = buf_ref[...]
    xpad_ref[nb:, :] = z[:, dr:]
    xb = cb_ref[...] + xpad_ref[0:n, :] * cw_ref[0:1, :]
    for s in range(1, CONV_W):
        xb = xb + xpad_ref[s * bsz:s * bsz + n, :] * cw_ref[s:s + 1, :]
    a, bt = _lru_gates(xb, wa_ref, ba_ref[...], wi_ref, bi_ref[...], sp_ref[...])
    hcur = h0_ref[...]
    for s in range(t):
        hcur = a[s * bsz:(s + 1) * bsz, :] * hcur + bt[s * bsz:(s + 1) * bsz, :]
        hs_ref[s * bsz:(s + 1) * bsz, :] = hcur
    o_ref[...] = x + _dot((gate * hs_ref[...]).astype(BF16), wo_ref[...])
    hl_ref[...] = hcur
    cn_ref[...] = xpad_ref[n:n + nb, :]


def _lru_sample(x, bsz, h0, conv0, g, wyx, byx, cw, cb, wa, ba, wi, bi, sp, wo):
    n, d = x.shape
    t = n // bsz
    dr = wo.shape[0]
    nb = (CONV_W - 1) * bsz
    tmaj = lambda a, tt: a.reshape(bsz, tt, -1).swapaxes(0, 1).reshape(tt * bsz, -1)
    bmaj = lambda a, tt: a.reshape(tt, bsz, -1).swapaxes(0, 1)
    vec = lambda a: a.reshape(1, -1)
    args = (tmaj(x, t), vec(g), tmaj(conv0, CONV_W - 1), h0, wyx, vec(byx), cw, vec(cb), wa, vec(ba),
            wi, vec(bi), vec(sp), wo)
    out, hl, cn = pl.pallas_call(
        functools.partial(_lru_sample_kernel, bsz),
        in_specs=[pl.BlockSpec(a.shape, lambda nd=a.ndim: (0,) * nd) for a in args],
        out_specs=[pl.BlockSpec((n, d), lambda: (0, 0)), pl.BlockSpec((bsz, dr), lambda: (0, 0)),
                   pl.BlockSpec((nb, dr), lambda: (0, 0))],
        out_shape=[jax.ShapeDtypeStruct((n, d), F32), jax.ShapeDtypeStruct((bsz, dr), F32),
                   jax.ShapeDtypeStruct((nb, dr), F32)],
        scratch_shapes=[pltpu.VMEM((n + nb, dr), F32), pltpu.VMEM((n, dr), F32)],
        compiler_params=pltpu.CompilerParams(vmem_limit_bytes=VMEM_LIMIT), name="lru_sample",
    )(*args)
    return bmaj(out, t).reshape(n, d), hl, bmaj(cn, CONV_W - 1)


def _rope_tables_partial(pos, period, rot_dim, theta):
    half = rot_dim // 2
    inv = theta ** (-jnp.arange(half, dtype=F32) / half)
    ang = pos.astype(F32)[:, None] * inv[None, :]
    cos, sin = jnp.cos(ang), jnp.sin(ang)
    t = pos.shape[0]
    zh = jnp.zeros((t, half), F32)
    rest = period - rot_dim
    c = jnp.concatenate([cos, cos, jnp.ones((t, rest), F32)], axis=1)
    s1 = jnp.concatenate([-sin, zh, jnp.zeros((t, rest), F32)], axis=1)
    s2 = jnp.concatenate([zh, sin, jnp.zeros((t, rest), F32)], axis=1)
    rep = LANES // period
    return [jnp.tile(a, (1, rep)) for a in (c, s1, s2)]


def _rot(x, c, s1, s2, half):
    n = x.shape[-1]
    return x * c + pltpu.roll(x, n - half, 1) * s1 + pltpu.roll(x, half, 1) * s2


def _sort_key(x):
    bits = pltpu.bitcast(x, I32)
    return bits ^ ((bits >> 31) & 0x7FFFFFFF)


RADIX_GROUP = 4


def _kth_largest(count_ge, k, shape):
    c0 = count_ge(jnp.zeros(shape, I32))
    t = jnp.where(c0 >= k, 0, INT_MIN).astype(I32)
    cnt = jnp.where(c0 >= k, c0, 2.0 * k)

    def decide(t, cnt, bit):
        cand = t | lax.shift_left(jnp.int32(1), bit)
        c = count_ge(cand)
        return jnp.where(c >= k, cand, t), jnp.where(c >= k, c, cnt)

    n_head = 31 % RADIX_GROUP
    for bit in range(30, 30 - n_head, -1):
        t, cnt = decide(t, cnt, bit)
    n_groups = (31 - n_head) // RADIX_GROUP

    def cond(state):
        i, _, _, pending = state
        return (i < n_groups) & (pending > 0.5)

    def body(state):
        i, t, cnt, _ = state
        for u in range(RADIX_GROUP):
            t, cnt = decide(t, cnt, 30 - n_head - (i * RADIX_GROUP + u))
        return i + 1, t, cnt, jnp.max(jnp.where(cnt == k, 0.0, 1.0))

    return lax.while_loop(cond, body, (jnp.int32(0), t, cnt, jnp.float32(1.0)))[1]


def _attn_kernel(tq, tk, lk, q_pos0, k_pos0, mode, n_cmp, q_ref, k_ref, v_ref, *rest):
    if mode == "window":
        o_ref, qs_ref, acc_ref, m_scr, l_scr = rest
        m_ref = None
    else:
        m_ref, o_ref, qs_ref, acc_ref, m_scr, l_scr = rest
    p0 = q_pos0 + pl.program_id(1) * tq
    for g in range(N_KV):
        for h in range(GROUP):
            c0 = (g * GROUP + h) * HEAD_DIM
            qs_ref[g, h * tq:(h + 1) * tq, :] = q_ref[:, c0:c0 + HEAD_DIM]
    m_scr[...] = jnp.full(m_scr.shape, NEG, F32)
    l_scr[...] = jnp.zeros(l_scr.shape, F32)
    acc_ref[...] = jnp.zeros(acc_ref.shape, F32)
    hi = jnp.minimum((p0 + tq - 1 - k_pos0) // tk + 1, lk // tk)
    lo = jnp.maximum(p0 - (WINDOW - 1) - k_pos0, 0) // tk if mode == "window" else 0
    rowpos = p0 + lax.broadcasted_iota(I32, (tq, tk), 0)

    def body(kt, carry):
        k0 = pl.multiple_of(kt * tk, tk)
        if mode == "mask":
            bias = m_ref[:, pl.ds(k0, tk)].astype(F32)
        else:
            colpos = k_pos0 + k0 + lax.broadcasted_iota(I32, (tq, tk), 1)
            valid = colpos <= rowpos
            if mode == "window":
                valid = valid & (rowpos - colpos < WINDOW) & (colpos >= 0)
                bias = jnp.where(valid, 0.0, -jnp.inf)
        for g in range(N_KV):
            if mode == "blocksel":
                blk = lax.shift_right_logical(k_pos0 + k0 + lax.broadcasted_iota(I32, (LANES, tk), 1), SLC_SHIFT)
                e = jnp.where(blk == lax.broadcasted_iota(I32, (LANES, tk), 0), 1.0, 0.0).astype(BF16)
                sel = _dot(m_ref[:, g * LANES:(g + 1) * LANES], e) > 0.5
                bias = jnp.where(valid & (sel | (colpos >= n_cmp * SLC_BLK)), 0.0, -jnp.inf)
            kt_ = k_ref[pl.ds(k0, tk), g * HEAD_DIM:(g + 1) * HEAD_DIM]
            vt_ = v_ref[pl.ds(k0, tk), g * HEAD_DIM:(g + 1) * HEAD_DIM]
            s = _dot_nt(qs_ref[g], kt_)
            s = (s.reshape(GROUP, tq, tk) + bias[None]).reshape(GROUP * tq, tk)
            _softmax_update(g, s, vt_, acc_ref, m_scr, l_scr)
        return carry

    lax.fori_loop(lo, hi, body, 0)
    for g in range(N_KV):
        o = acc_ref[g] / jnp.maximum(l_scr[g], SUM_FLOOR)
        for h in range(GROUP):
            c0 = (g * GROUP + h) * HEAD_DIM
            o_ref[:, c0:c0 + HEAD_DIM] = o[h * tq:(h + 1) * tq, :]


def _attention(q, k, v, m, *, bsz, tq, q_pos0, k_pos0, mode, n_cmp=0):
    nq_rows, dq = q.shape
    lk = k.shape[1]
    tk = ATTN_KEY_TILE
    assert lk % tk == 0 and (nq_rows // bsz) % tq == 0
    nq = nq_rows // bsz // tq
    kern = functools.partial(_attn_kernel, tq, tk, lk, q_pos0, k_pos0, mode, n_cmp)
    in_specs = [pl.BlockSpec((tq, dq), lambda b, j: (b * nq + j, 0)),
                pl.BlockSpec((None, lk, k.shape[2]), lambda b, j: (b, 0, 0)),
                pl.BlockSpec((None, lk, v.shape[2]), lambda b, j: (b, 0, 0))]
    args = [q, k, v]
    if mode != "window":
        in_specs.append(pl.BlockSpec((tq, m.shape[1]), lambda b, j: (b * nq + j, 0)))
        args.append(m)
    return pl.pallas_call(
        kern, grid=(bsz, nq), in_specs=in_specs,
        out_specs=pl.BlockSpec((tq, dq), lambda b, j: (b * nq + j, 0)),
        out_shape=jax.ShapeDtypeStruct((nq_rows, dq), F32),
        scratch_shapes=[pltpu.VMEM((N_KV, GROUP * tq, HEAD_DIM), BF16),
                        pltpu.VMEM((N_KV, GROUP * tq, HEAD_DIM), F32),
                        pltpu.VMEM((N_KV, GROUP * tq, LANES), F32),
                        pltpu.VMEM((N_KV, GROUP * tq, LANES), F32)],
        compiler_params=_params("parallel", "arbitrary"), name="attn_" + mode,
    )(*args)


def _softmax_update(g, s, vt, acc_ref, m_scr, l_scr):
    m_old = m_scr[g]
    m_new = jnp.maximum(m_old, jnp.max(s, axis=-1, keepdims=True))
    alpha = jnp.exp2(m_old - m_new)
    p = jnp.exp2(s - jnp.concatenate([m_new] * (s.shape[1] // LANES), axis=1))
    l_scr[g] = alpha * l_scr[g] + jnp.sum(p, axis=-1, keepdims=True)
    acc_ref[g] = alpha * acc_ref[g] + _dot(p.astype(BF16), vt)
    m_scr[g] = m_new


def _paged_attn_kernel(tq, pp, n_steps, mode, pt_ref, q_ref, *rest):
    kp, vp = rest[:pp], rest[pp:2 * pp]
    kn_ref, vn_ref = rest[2 * pp:2 * pp + 2]
    if mode == "mask":
        mb_ref, mt_ref, o_ref, qs_ref, acc_ref, m_scr, l_scr, kbuf, vbuf = rest[2 * pp + 2:]
    else:
        mb_ref, o_ref, qs_ref, acc_ref, m_scr, l_scr, kbuf, vbuf = rest[2 * pp + 2:]
    step = pl.program_id(1)
    tk = pp * PAGE_SIZE

    @pl.when(step == 0)
    def _():
        for g in range(N_KV):
            for h in range(GROUP):
                c0 = (g * GROUP + h) * HEAD_DIM
                qs_ref[g, h * tq:(h + 1) * tq, :] = q_ref[:, c0:c0 + HEAD_DIM]
        m_scr[...] = jnp.full(m_scr.shape, NEG, F32)
        l_scr[...] = jnp.zeros(l_scr.shape, F32)
        acc_ref[...] = jnp.zeros(acc_ref.shape, F32)

    for i in range(pp):
        for g in range(N_KV):
            rows = pl.ds(g, PAGE_SIZE, stride=N_KV)
            kbuf[g, i * PAGE_SIZE:(i + 1) * PAGE_SIZE, :] = kp[i][rows, :].astype(BF16)
            vbuf[g, i * PAGE_SIZE:(i + 1) * PAGE_SIZE, :] = vp[i][rows, :].astype(BF16)

    def biased(g, s, bias):
        n = s.shape[1]
        return (s.reshape(GROUP, tq, n) + bias[None]).reshape(GROUP * tq, n)

    for g in range(N_KV):
        if mode == "mask":
            bias = mb_ref[...].astype(F32)
        else:
            blk = lax.shift_right_logical(step * tk + lax.broadcasted_iota(I32, (LANES, tk), 1), SLC_SHIFT)
            e = jnp.where(blk == lax.broadcasted_iota(I32, (LANES, tk), 0), 1.0, 0.0).astype(BF16)
            bias = jnp.where(_dot(mb_ref[:, g * LANES:(g + 1) * LANES], e) > 0.5, 0.0, -jnp.inf)
        _softmax_update(g, biased(g, _dot_nt(qs_ref[g], kbuf[g]), bias), vbuf[g], acc_ref, m_scr, l_scr)

    @pl.when(step == n_steps - 1)
    def _():
        tn = kn_ref.shape[0]
        causal = lax.broadcasted_iota(I32, (tq, tn), 1) <= lax.broadcasted_iota(I32, (tq, tn), 0)
        for g in range(N_KV):
            sl = slice(g * HEAD_DIM, (g + 1) * HEAD_DIM)
            bias = mt_ref[...].astype(F32) if mode == "mask" else jnp.where(causal, 0.0, -jnp.inf)
            _softmax_update(g, biased(g, _dot_nt(qs_ref[g], kn_ref[:, sl]), bias), vn_ref[:, sl],
                            acc_ref, m_scr, l_scr)
            o = acc_ref[g] / jnp.maximum(l_scr[g], SUM_FLOOR)
            for h in range(GROUP):
                c0 = (g * GROUP + h) * HEAD_DIM
                o_ref[:, c0:c0 + HEAD_DIM] = o[h * tq:(h + 1) * tq, :]


PAGES_PER_STEP = 32
NEW_TILE = LANES


def _paged_attention(q, k_pool, v_pool, page0, page_table, kn, vn, m, *, bsz, tq, mode):
    dq = q.shape[1]
    n_pages = page_table.shape[1]
    pp = math.gcd(n_pages, PAGES_PER_STEP)
    n_steps = n_pages // pp
    tk = pp * PAGE_SIZE
    prow = PAGE_SIZE * N_KV
    page = lambda i: pl.BlockSpec((prow, HEAD_DIM), lambda b, s, pt: (page0 + pt[b, s * pp + i], 0))
    in_specs = [pl.BlockSpec((tq, dq), lambda b, s, pt: (b, 0))]
    in_specs += [page(i) for i in range(pp)] * 2
    in_specs += [pl.BlockSpec((NEW_TILE, kn.shape[1]), lambda b, s, pt: (b, 0))] * 2
    args = [q] + [k_pool] * pp + [v_pool] * pp + [kn, vn]
    if mode == "mask":
        in_specs += [pl.BlockSpec((tq, tk), lambda b, s, pt: (b, s)),
                     pl.BlockSpec((tq, NEW_TILE), lambda b, s, pt: (b, n_pages * PAGE_SIZE // NEW_TILE))]
        args += [m, m]
    else:
        in_specs.append(pl.BlockSpec((tq, m.shape[1]), lambda b, s, pt: (b, 0)))
        args.append(m)
    rows = GROUP * tq
    return pl.pallas_call(
        functools.partial(_paged_attn_kernel, tq, pp, n_steps, mode),
        grid_spec=pltpu.PrefetchScalarGridSpec(
            num_scalar_prefetch=1, grid=(bsz, n_steps), in_specs=in_specs,
            out_specs=pl.BlockSpec((tq, dq), lambda b, s, pt: (b, 0)),
            scratch_shapes=[pltpu.VMEM((N_KV, rows, HEAD_DIM), BF16), pltpu.VMEM((N_KV, rows, HEAD_DIM), F32),
                            pltpu.VMEM((N_KV, rows, LANES), F32), pltpu.VMEM((N_KV, rows, LANES), F32),
                            pltpu.VMEM((N_KV, tk, HEAD_DIM), BF16), pltpu.VMEM((N_KV, tk, HEAD_DIM), BF16)]),
        out_shape=jax.ShapeDtypeStruct((bsz * tq, dq), F32),
        compiler_params=_params("parallel", "arbitrary"), name="paged_attn_" + mode,
    )(page_table, *args)


def _heads(c):
    return lambda n, tm: ((n, c // HEAD_DIM, HEAD_DIM), (tm, c // HEAD_DIM, HEAD_DIM), lambda i: (i, 0, 0))


def _store_heads(ref, z):
    for hd in range(ref.shape[1]):
        ref[:, hd, :] = z[:, hd * HEAD_DIM:(hd + 1) * HEAD_DIM]


def _head_norm_rot(z, g, tabs, n_heads, scale, refs_f32, refs_bf16):
    c, s1, s2 = tabs
    for hd in range(n_heads):
        sl = slice(hd * HEAD_DIM, (hd + 1) * HEAD_DIM)
        r = _rot(_rms(z[:, sl]) * g, c, s1, s2, ROPE_DIM // 2)
        for ref in refs_f32:
            ref[:, hd, :] = r
        for ref in refs_bf16:
            ref[:, sl] = (r * scale).astype(BF16)


def _dsa_epilogue(h, w_ref, tabs, cs, o):
    tq_ = [t[...] for t in tabs[:3]]
    ti_ = [t[...] for t in tabs[3:]]
    gq, gk, gik = [c[...] for c in cs]
    q_ref, k_ref, kb_ref, v_ref, vb_ref, qi_ref, ki_ref, wi_ref = o
    nq, nkv = N_HEADS * HEAD_DIM, N_KV * HEAD_DIM
    ni = IDX_HEADS * IDX_DIM
    _head_norm_rot(_dot(h, w_ref[:, 0:nq]), gq, tq_, N_HEADS, ATTN_SCALE * LOG2E, [], [q_ref])
    _head_norm_rot(_dot(h, w_ref[:, nq:nq + nkv]), gk, tq_, N_KV, 1.0, [k_ref], [kb_ref])
    zv = _dot(h, w_ref[:, nq + nkv:nq + 2 * nkv])
    _store_heads(v_ref, zv)
    vb_ref[...] = zv.astype(BF16)
    c0 = nq + 2 * nkv
    zi = _dot(h, w_ref[:, c0:c0 + ni])
    for ch in range(ni // LANES):
        r = _rot(zi[:, ch * LANES:(ch + 1) * LANES], *ti_, IDX_ROPE // 2).astype(BF16)
        for u in range(LANES // IDX_DIM):
            qi_ref[ch * (LANES // IDX_DIM) + u] = r[:, u * IDX_DIM:(u + 1) * IDX_DIM]
    zl = _dot(h, w_ref[:, c0 + ni:c0 + ni + LANES])
    lane = lax.broadcasted_iota(I32, zl.shape, 1)
    ms = jnp.sum(jnp.where(lane < IDX_DIM, zl * zl, 0.0), axis=-1, keepdims=True) / IDX_DIM
    r = _rot(zl * lax.rsqrt(ms + EPS) * gik, *ti_, IDX_ROPE // 2)
    ki_ref[...] = r[:, :IDX_DIM]
    wi_ref[...] = zl[:, IDX_DIM:IDX_DIM + IDX_HEADS] * (IDX_HEADS ** -0.5 * IDX_DIM ** -0.5)


def _dsa_select_kernel(tq, nb, same_batch, tk, lk, q_pos0, top, qi_ref, wi_ref, kit_ref, m_ref, key_ref,
                       wib_ref):
    rows = nb * tq
    p0 = q_pos0 + pl.program_id(1) * (rows if same_batch else tq)
    p0s = [p0 + i * tq if same_batch else p0 for i in range(nb)]
    nk = jnp.minimum((p0s[-1] + tq - 1) // tk + 1, lk // tk)
    wi = wi_ref[...]
    for h in range(IDX_HEADS):
        wib_ref[h] = jnp.broadcast_to(wi[:, h:h + 1], (rows, tk))
    row1 = lax.broadcasted_iota(I32, (tq, tk), 0)
    col1 = lax.broadcasted_iota(I32, (tq, tk), 1)
    rowpos = jnp.concatenate([p + row1 for p in p0s], axis=0)
    col = lax.broadcasted_iota(I32, (rows, tk), 1)

    for i in range(nb):
        def score_pair(kp, carry, i=i):
            for u in range(2):
                k0 = pl.multiple_of((2 * kp + u) * tk, tk)
                kt_ = kit_ref[0 if same_batch else i, :, pl.ds(k0, tk)]
                acc = jnp.zeros((tq, tk), F32)
                if IDX_HEADS * tq <= 512:
                    s_all = _dot(qi_ref[:, i * tq:(i + 1) * tq, :].reshape(IDX_HEADS * tq, IDX_DIM), kt_)
                for h in range(IDX_HEADS):
                    if IDX_HEADS * tq <= 512:
                        s = s_all[h * tq:(h + 1) * tq, :]
                    else:
                        s = _dot(qi_ref[h, i * tq:(i + 1) * tq, :], kt_)
                    acc = acc + jnp.maximum(s, 0.0) * wib_ref[h, i * tq:(i + 1) * tq, :]
                acc = jnp.where(k0 + col1 <= p0s[i] + row1, acc, -jnp.inf)
                key_ref[i * tq:(i + 1) * tq, pl.ds(k0, tk)] = _sort_key(acc)
            return carry

        lax.fori_loop(0, (nk + 1) // 2, score_pair, 0)

    grp = min(rows, SELECT_ROWS)

    def count_ge(cand):
        parts = []
        for r0 in range(0, rows, grp):
            cb = jnp.broadcast_to(cand[r0:r0 + grp], (grp, LANES))

            def body(kt, acc, r0=r0, cb=cb):
                for u in range(2 * tk // LANES):
                    c0 = pl.multiple_of(kt * (2 * tk) + u * LANES, LANES)
                    acc = acc + jnp.where(key_ref[r0:r0 + grp, pl.ds(c0, LANES)] >= cb, 1.0, 0.0)
                return acc

            parts.append(lax.fori_loop(0, (nk + 1) // 2, body, jnp.zeros((grp, LANES), F32)))
        return jnp.sum(jnp.concatenate(parts, axis=0), axis=-1, keepdims=True)

    thr = jnp.broadcast_to(_kth_largest(count_ge, float(top), (rows, 1)), (rows, tk))

    def bias_tile(kt, carry):
        k0 = pl.multiple_of(kt * tk, tk)
        sel = (k0 + col <= rowpos) & (key_ref[:, pl.ds(k0, tk)] >= thr)
        m_ref[:, pl.ds(k0, tk)] = jnp.where(sel, 0.0, -jnp.inf).astype(BF16)
        return carry

    lax.fori_loop(0, nk, bias_tile, 0)

    def rest_tile(kt, carry):
        m_ref[:, pl.ds(pl.multiple_of(kt * tk, tk), tk)] = jnp.full((rows, tk), -jnp.inf, BF16)
        return carry

    lax.fori_loop(nk, lk // tk, rest_tile, 0)


SELECT_ROWS = 128
SELECT_TILES = 2


def _dsa_select(qi, wi, kit, *, bsz, tq, q_pos0, top):
    lk = kit.shape[2]
    tk = SELECT_KEY_TILE
    assert lk % (2 * tk) == 0 and top <= tk
    nq_rows = wi.shape[0]
    nq = nq_rows // bsz // tq
    if nq == 1 and SELECT_ROWS % tq == 0 and bsz % (SELECT_ROWS // tq) == 0:
        nb, same_batch, kb, grid = SELECT_ROWS // tq, False, SELECT_ROWS // tq, (bsz * tq // SELECT_ROWS, 1)
    elif nq % SELECT_TILES == 0:
        nb, same_batch, kb, grid = SELECT_TILES, True, 1, (bsz, nq // SELECT_TILES)
    else:
        nb, same_batch, kb, grid = 1, True, 1, (bsz, nq)
    rows = nb * tq
    nj = grid[1]
    return pl.pallas_call(
        functools.partial(_dsa_select_kernel, tq, nb, same_batch, tk, lk, q_pos0, top),
        grid=grid,
        in_specs=[pl.BlockSpec((IDX_HEADS, rows, IDX_DIM), lambda b, j: (0, b * nj + j, 0)),
                  pl.BlockSpec((rows, IDX_HEADS), lambda b, j: (b * nj + j, 0)),
                  pl.BlockSpec((kb, IDX_DIM, lk), lambda b, j: (b, 0, 0))],
        out_specs=pl.BlockSpec((rows, lk), lambda b, j: (b * nj + j, 0)),
        out_shape=jax.ShapeDtypeStruct((nq_rows, lk), BF16),
        scratch_shapes=[pltpu.VMEM((rows, lk), I32), pltpu.VMEM((IDX_HEADS, rows, tk), F32)],
        compiler_params=_params("parallel", "arbitrary"), name="dsa_select",
    )(qi, wi, kit)


def _nsa_epilogue(h, w_ref, tabs, cs, o):
    tb = [t[...] for t in tabs]
    gq, gsk, gwk = [c[...] for c in cs]
    (qc_ref, q_ref, ck_ref, cv_ref, sk_ref, skb_ref, sv_ref, svb_ref,
     wk_ref, wkb_ref, wv_ref, wvb_ref, gate_ref) = o
    nq, nkv = N_HEADS * HEAD_DIM, N_KV * HEAD_DIM
    zq = _dot(h, w_ref[:, 0:nq])
    for hd in range(N_HEADS):
        sl = slice(hd * HEAD_DIM, (hd + 1) * HEAD_DIM)
        qn = _rms(zq[:, sl]) * gq
        qc_ref[:, sl] = (qn * ATTN_SCALE).astype(BF16)
        q_ref[:, sl] = (_rot(qn, *tb, ROPE_DIM // 2) * (ATTN_SCALE * LOG2E)).astype(BF16)
    seg = lambda i: _dot(h, w_ref[:, nq + i * nkv:nq + (i + 1) * nkv])
    ck_ref[...] = seg(0)
    cv_ref[...] = seg(1)
    _head_norm_rot(seg(2), gsk, tb, N_KV, 1.0, [sk_ref], [skb_ref])
    zsv = seg(3)
    _store_heads(sv_ref, zsv)
    svb_ref[...] = zsv.astype(BF16)
    _head_norm_rot(seg(4), gwk, tb, N_KV, 1.0, [wk_ref], [wkb_ref])
    zwv = seg(5)
    _store_heads(wv_ref, zwv)
    wvb_ref[...] = zwv.astype(BF16)
    zg = _dot(h, w_ref[:, nq + 6 * nkv:nq + 6 * nkv + LANES])
    gate_ref[...] = jax.nn.sigmoid(zg[:, :N_HEADS * 3])


def _nsa_compress_kernel(ck_ref, cv_ref, pwk_ref, pwv_ref, phik_ref, phiv_ref, g_ref, kc_ref, vc_ref):
    rows, wd = ck_ref.shape
    nb = rows // CMP_BLK
    pk = jnp.sum(ck_ref[...].reshape(nb, CMP_BLK, wd) * pwk_ref[...][None], axis=1)
    pv = jnp.sum(cv_ref[...].reshape(nb, CMP_BLK, wd) * pwv_ref[...][None], axis=1)
    for hd in range(N_KV):
        sl = slice(hd * HEAD_DIM, (hd + 1) * HEAD_DIM)
        kc_ref[:, sl] = (_rms(_dot(pk[:, sl].astype(BF16), phik_ref[hd])) * g_ref[...]).astype(BF16)
        vc_ref[:, sl] = _dot(pv[:, sl].astype(BF16), phiv_ref[hd]).astype(BF16)


def _nsa_compress(ck, cv, pwk, pwv, phik, phiv, gck):
    rows, wd = ck.shape
    step = CMP_ROWS if rows % CMP_ROWS == 0 else rows
    nb = step // CMP_BLK
    bc = lambda p: jnp.repeat(p, HEAD_DIM, axis=1)
    return pl.pallas_call(
        _nsa_compress_kernel, grid=(rows // step,),
        in_specs=[pl.BlockSpec((step, wd), lambda i: (i, 0)), pl.BlockSpec((step, wd), lambda i: (i, 0)),
                  _const_spec((CMP_BLK, wd)), _const_spec((CMP_BLK, wd)), _const_spec(phik.shape),
                  _const_spec(phiv.shape), _const_spec((1, HEAD_DIM))],
        out_specs=[pl.BlockSpec((nb, wd), lambda i: (i, 0)), pl.BlockSpec((nb, wd), lambda i: (i, 0))],
        out_shape=[jax.ShapeDtypeStruct((rows // CMP_BLK, wd), BF16)] * 2,
        compiler_params=_params("parallel"), name="nsa_compress",
    )(ck, cv, bc(pwk), bc(pwv), phik, phiv, gck.reshape(1, HEAD_DIM))


def _paged_compress_kernel(pp, pt_ref, *rest):
    ckp, cvp = rest[:pp], rest[pp:2 * pp]
    pwk_ref, pwv_ref, phik_ref, phiv_ref, g_ref, kc_ref, vc_ref, pk_scr, pv_scr = rest[2 * pp:]
    nb = PAGE_SIZE // CMP_BLK
    for hd in range(N_KV):
        sl = slice(hd * HEAD_DIM, (hd + 1) * HEAD_DIM)
        rows = pl.ds(hd, PAGE_SIZE, stride=N_KV)
        for i in range(pp):
            pool = lambda ref, pw: jnp.sum(ref[rows, :].reshape(nb, CMP_BLK, HEAD_DIM) * pw[:, sl][None], axis=1)
            pk_scr[hd, i * nb:(i + 1) * nb, :] = pool(ckp[i], pwk_ref)
            pv_scr[hd, i * nb:(i + 1) * nb, :] = pool(cvp[i], pwv_ref)
        kc_ref[:, sl] = (_rms(_dot(pk_scr[hd].astype(BF16), phik_ref[hd])) * g_ref[...]).astype(BF16)
        vc_ref[:, sl] = _dot(pv_scr[hd].astype(BF16), phiv_ref[hd]).astype(BF16)


def _paged_compress(ck_pool, cv_pool, page0, page_table, pwk, pwv, phik, phiv, gck):
    bsz, n_pages = page_table.shape
    pp = math.gcd(n_pages, PAGES_PER_STEP)
    n_steps = n_pages // pp
    nb = pp * PAGE_SIZE // CMP_BLK
    wd = N_KV * HEAD_DIM
    prow = PAGE_SIZE * N_KV
    bc = lambda p: jnp.repeat(p, HEAD_DIM, axis=1)
    page = lambda i: pl.BlockSpec((prow, HEAD_DIM), lambda b, s, pt: (page0 + pt[b, s * pp + i], 0))
    const = lambda shape: pl.BlockSpec(shape, lambda b, s, pt: (0,) * len(shape))
    out = pl.BlockSpec((nb, wd), lambda b, s, pt: (b * n_steps + s, 0))
    return pl.pallas_call(
        functools.partial(_paged_compress_kernel, pp),
        grid_spec=pltpu.PrefetchScalarGridSpec(
            num_scalar_prefetch=1, grid=(bsz, n_steps),
            in_specs=[page(i) for i in range(pp)] * 2
            + [const((CMP_BLK, wd)), const((CMP_BLK, wd)), const(phik.shape), const(phiv.shape),
               const((1, HEAD_DIM))],
            out_specs=[out, out],
            scratch_shapes=[pltpu.VMEM((N_KV, nb, HEAD_DIM), F32), pltpu.VMEM((N_KV, nb, HEAD_DIM), F32)]),
        out_shape=[jax.ShapeDtypeStruct((bsz * n_steps * nb, wd), BF16)] * 2,
        compiler_params=_params("parallel", "parallel"), name="paged_compress",
    )(page_table, *([ck_pool] * pp), *([cv_pool] * pp), bc(pwk), bc(pwv), phik, phiv,
      gck.reshape(1, HEAD_DIM))


def _nsa_cmp_kernel(tq, q_pos0, n_sel, qc_ref, kc_ref, vc_ref, oc_ref, bm_ref):
    p0 = q_pos0 + pl.program_id(1) * tq
    nc = kc_ref.shape[0]
    rowpos = p0 + lax.broadcasted_iota(I32, (tq, nc), 0)
    blk = lax.broadcasted_iota(I32, (tq, nc), 1)
    valid = jnp.concatenate([blk * CMP_BLK + (CMP_BLK - 1) <= rowpos] * GROUP, axis=0)
    cur = lax.shift_right_logical(rowpos, SLC_SHIFT)
    forced = (blk == 0) | (blk == cur) | (blk == cur - 1)
    keys = []
    for g in range(N_KV):
        sl = slice(g * HEAD_DIM, (g + 1) * HEAD_DIM)
        qs = jnp.concatenate([qc_ref[:, (g * GROUP + h) * HEAD_DIM:(g * GROUP + h + 1) * HEAD_DIM]
                              for h in range(GROUP)], axis=0)
        lc = jnp.where(valid, _dot_nt(qs, kc_ref[:, sl]), NEG)
        m = jnp.max(lc, axis=-1, keepdims=True)
        m = jnp.where(m > 0.5 * NEG, m, 0.0)
        p = jnp.where(valid, jnp.exp(lc - m), 0.0)
        pc = p / jnp.maximum(jnp.sum(p, axis=-1, keepdims=True), SUM_FLOOR)
        oc = _dot(pc.astype(BF16), vc_ref[:, sl])
        for h in range(GROUP):
            c0 = (g * GROUP + h) * HEAD_DIM
            oc_ref[:, c0:c0 + HEAD_DIM] = oc[h * tq:(h + 1) * tq, :]
        imp = pc[0:tq]
        for h in range(1, GROUP):
            imp = imp + pc[h * tq:(h + 1) * tq]
        imp = jnp.where(forced, BIG, imp)
        imp = jnp.where(blk > cur, -jnp.inf, imp)
        keys.append(_sort_key(imp))
    key = jnp.concatenate(keys, axis=0)
    count_ge = lambda cand: jnp.sum(jnp.where(key >= cand, 1.0, 0.0), axis=-1, keepdims=True)
    sel = jnp.where(key >= _kth_largest(count_ge, float(n_sel), (N_KV * tq, 1)), 1.0, 0.0).astype(BF16)
    for g in range(N_KV):
        bm_ref[:, g * LANES:(g + 1) * LANES] = sel[g * tq:(g + 1) * tq, :]


def _nsa_cmp(qc, kc, vc, *, bsz, tq, q_pos0, n_sel):
    nq_rows, dq = qc.shape
    nc = kc.shape[1]
    assert nc == LANES
    nq = nq_rows // bsz // tq
    return pl.pallas_call(
        functools.partial(_nsa_cmp_kernel, tq, q_pos0, n_sel), grid=(bsz, nq),
        in_specs=[pl.BlockSpec((tq, dq), lambda b, j: (b * nq + j, 0)),
                  pl.BlockSpec((None, nc, kc.shape[2]), lambda b, j: (b, 0, 0)),
                  pl.BlockSpec((None, nc, vc.shape[2]), lambda b, j: (b, 0, 0))],
        out_specs=[pl.BlockSpec((tq, dq), lambda b, j: (b * nq + j, 0)),
                   pl.BlockSpec((tq, N_KV * LANES), lambda b, j: (b * nq + j, 0))],
        out_shape=[jax.ShapeDtypeStruct((nq_rows, dq), F32),
                   jax.ShapeDtypeStruct((nq_rows, N_KV * LANES), BF16)],
        compiler_params=_params("parallel", "parallel"), name="nsa_cmp",
    )(qc, kc, vc)


def _nsa_combine(gate, oc, os_, ow):
    parts = []
    for hd in range(N_HEADS):
        sl = slice(hd * HEAD_DIM, (hd + 1) * HEAD_DIM)
        parts.append(gate[:, 3 * hd:3 * hd + 1] * oc[:, sl] + gate[:, 3 * hd + 1:3 * hd + 2] * os_[:, sl]
                     + gate[:, 3 * hd + 2:3 * hd + 3] * ow[:, sl])
    return jnp.concatenate(parts, axis=1)


SAMPLE_TQ = 16
PROMPT_TQ = 128
PROMPT_ATTN_TQ = 512
PROMPT_CMP_TQ = 512


def _wide_tile(t, want, base):
    return want if t % want == 0 else base
KEY_ALIGN = ATTN_KEY_TILE


def _pad_rows(a, bsz, t, tp):
    return jnp.pad(a.reshape(bsz, t, -1), ((0, 0), (0, tp - t), (0, 0))).reshape(bsz * tp, -1)


def _unpad_rows(a, bsz, t, tp):
    return a.reshape(bsz, tp, -1)[:, :t].reshape(bsz * t, -1)


def _cat_keys(old, new, bsz):
    new = new.reshape(bsz, -1, new.shape[-1])
    n = old.shape[1] + new.shape[1]
    return jnp.pad(jnp.concatenate([old.astype(new.dtype), new], axis=1),
                   ((0, 0), (0, -n % KEY_ALIGN), (0, 0)))


def _tile_tables(tabs, bsz, t):
    return tabs if t % ROW_TILE == 0 else [jnp.tile(a, (bsz, 1)) for a in tabs]


def _mixer_c(x, bsz, pos, past, g, wcat, gq, gk, gik, wo):
    n, d = x.shape
    t = n // bsz
    tabs = _tile_tables(_rope_tables_partial(pos, HEAD_DIM, ROPE_DIM, ROPE_THETA)
                        + _rope_tables_partial(pos, IDX_DIM, IDX_ROPE, ROPE_THETA), bsz, t)
    nkv = N_KV * HEAD_DIM
    q, k, kb, v, vb, qi, ki, wi = _proj(
        x, g, wcat, tabs, [gq, gk, gik], _dsa_epilogue,
        [(_rows(N_HEADS * HEAD_DIM), BF16), (_heads(nkv), F32), (_rows(nkv), BF16), (_heads(nkv), F32),
         (_rows(nkv), BF16),
         (lambda n_, tm: ((IDX_HEADS, n_, IDX_DIM), (IDX_HEADS, tm, IDX_DIM), lambda i: (0, i, 0)), BF16),
         (_rows(IDX_DIM), F32), (_rows(IDX_HEADS), F32)], tabs[0].shape[0], "proj_dsa")
    if past is None:
        tq, q_pos0, n_keys = PROMPT_TQ, 0, t
        kit = ki.reshape(bsz, t, IDX_DIM).swapaxes(1, 2).astype(BF16)
        kall, vall = kb.reshape(bsz, t, nkv), vb.reshape(bsz, t, nkv)
    else:
        tq, q_pos0 = SAMPLE_TQ, past["kidx"].shape[1]
        n_keys = q_pos0 + t
        kit = _cat_keys(past["kidx"], ki, bsz).swapaxes(1, 2).astype(BF16)
        q, wi = _pad_rows(q, bsz, t, tq), _pad_rows(wi, bsz, t, tq)
        qi = jnp.pad(qi.reshape(IDX_HEADS, bsz, t, IDX_DIM), ((0, 0), (0, 0), (0, tq - t), (0, 0))
                     ).reshape(IDX_HEADS, bsz * tq, IDX_DIM)
    bias = _dsa_select(qi, wi, kit, bsz=bsz, tq=tq, q_pos0=q_pos0, top=min(TOPK_MAX, n_keys // 4))
    if past is None:
        o = _attention(q, kall, vall, bias, bsz=bsz, tq=_wide_tile(t, PROMPT_ATTN_TQ, tq), q_pos0=q_pos0,
                       k_pos0=0, mode="mask")
    else:
        new = lambda a: _pad_rows(a.reshape(n, nkv), bsz, t, NEW_TILE).astype(BF16)
        o = _paged_attention(q, past["k_pool"], past["v_pool"], past["page0"], past["pt"], new(k), new(v), bias,
                             bsz=bsz, tq=tq, mode="mask")
        o = _unpad_rows(o, bsz, t, tq)
    return _outproj(x, wo, [o], lambda a: a), k, v, ki


def _mixer_d(x, bsz, pos, past, g, wcat, gq, gsk, gwk, cmpw, wo):
    n, d = x.shape
    t = n // bsz
    tabs = _tile_tables(_rope_tables_partial(pos, HEAD_DIM, ROPE_DIM, ROPE_THETA), bsz, t)
    nkv = N_KV * HEAD_DIM
    kvo = [(_heads(nkv), F32), (_rows(nkv), BF16)]
    (qc, q, ck, cv, sk, skb, sv, svb, wk, wkb, wv, wvb, gate) = _proj(
        x, g, wcat, tabs, [gq, gsk, gwk], _nsa_epilogue,
        [(_rows(N_HEADS * HEAD_DIM), BF16), (_rows(N_HEADS * HEAD_DIM), BF16), (_rows(nkv), F32),
         (_rows(nkv), F32)] + kvo * 4 + [(_rows(N_HEADS * 3), F32)], tabs[0].shape[0], "proj_nsa")
    if past is None:
        tq, q_pos0, n_keys, win_pos0 = PROMPT_TQ, 0, t, 0
        kc, vc = _nsa_compress(ck, cv, *cmpw)
        three = lambda a: a.reshape(bsz, t, nkv)
        skall, svall, wkall, wvall = three(skb), three(svb), three(wkb), three(wvb)
    else:
        tq, q_pos0 = SAMPLE_TQ, past["pt"].shape[1] * PAGE_SIZE
        n_keys = q_pos0 + t
        win_pos0 = q_pos0 - past["wk"].shape[1]
        assert q_pos0 % CMP_BLK == 0 and t < CMP_BLK
        kc, vc = _paged_compress(past["ck_pool"], past["cv_pool"], past["page0"], past["pt"], *cmpw)
        wkall = _cat_keys(past["wk"], wk.reshape(n, nkv), bsz).astype(BF16)
        wvall = _cat_keys(past["wv"], wv.reshape(n, nkv), bsz).astype(BF16)
        qc, q = _pad_rows(qc, bsz, t, tq), _pad_rows(q, bsz, t, tq)
    n_cmp = kc.shape[0] // bsz
    assert n_cmp <= LANES
    lane_pad = lambda a: jnp.pad(a.reshape(bsz, n_cmp, nkv), ((0, 0), (0, LANES - n_cmp), (0, 0)))
    kc, vc = lane_pad(kc), lane_pad(vc)
    n_slc = -(-n_keys // SLC_BLK)
    n_lane = min(n_slc, LANES)
    assert n_slc == n_lane or (n_slc == n_lane + 1 and q_pos0 // SLC_BLK == n_lane)
    n_sel = min(N_SLC, n_slc) - (n_slc - n_lane)
    tq_cmp = _wide_tile(t, PROMPT_CMP_TQ, tq) if past is None else tq
    oc, bm = _nsa_cmp(qc, kc, vc, bsz=bsz, tq=tq_cmp, q_pos0=q_pos0, n_sel=n_sel)
    if past is None:
        os_ = _attention(q, skall, svall, bm, bsz=bsz, tq=_wide_tile(t, PROMPT_ATTN_TQ, tq), q_pos0=q_pos0,
                         k_pos0=0, mode="blocksel", n_cmp=n_lane)
    else:
        new = lambda a: _pad_rows(a.reshape(n, nkv), bsz, t, NEW_TILE).astype(BF16)
        os_ = _paged_attention(q, past["sk_pool"], past["sv_pool"], past["page0"], past["pt"], new(sk), new(sv),
                               bm, bsz=bsz, tq=tq, mode="blocksel")
    ow = _attention(q, wkall, wvall, None, bsz=bsz, tq=tq if past else _wide_tile(t, PROMPT_ATTN_TQ, tq),
                    q_pos0=q_pos0, k_pos0=win_pos0, mode="window")
    if past is not None:
        oc, os_, ow = (_unpad_rows(a, bsz, t, tq) for a in (oc, os_, ow))
    return _outproj(x, wo, [gate, oc, os_, ow], _nsa_combine), (ck, cv, sk, sv, wk, wv)


def _gather_pages(cache, page_table):
    rows = cache[page_table]
    return rows.reshape(page_table.shape[0], page_table.shape[1] * PAGE_SIZE, -1)


def kernel(x_prompt, x_sample, state_a_ret, state_b_h, state_b_conv, cache_c_k, cache_c_v, cache_c_kidx,
           cache_d_ck, cache_d_cv, cache_d_sk, cache_d_sv, state_d_wk, state_d_wv, page_table,
           ffn1_norm, ffn1_wg, ffn1_wu, ffn1_wd, mix_norm, ffn2_norm, ffn2_wg, ffn2_wu, ffn2_wd,
           a_wq, a_wk, a_wv, a_wg, a_wo,
           b_wy, b_by, b_wx, b_bx, b_conv_w, b_conv_b, b_wa, b_ba, b_wi, b_bi, b_lam, b_wo,
           c_wq, c_gq, c_wk, c_gk, c_wv, c_wo, c_wiq, c_wik, c_gik, c_wiw,
           d_wq, d_gq, d_wck, d_wcv, d_pwk, d_pwv, d_phik, d_phiv, d_gck, d_wsk, d_wsv, d_gsk,
           d_wwk, d_wwv, d_gwk, d_wgate, d_wo):
    bp, tp, d = x_prompt.shape
    bs, ts, _ = x_sample.shape
    depth = ffn1_norm.shape[0]
    past_len = page_table.shape[1] * PAGE_SIZE
    pos_p = jnp.arange(tp, dtype=I32)
    pos_s = past_len + jnp.arange(ts, dtype=I32)
    b16 = lambda a: a.astype(BF16)
    row = lambda a: a.reshape(1, -1)
    nkv = N_KV * HEAD_DIM
    pool = lambda c: c.reshape(-1, HEAD_DIM)
    ffn1_w = [_to_bf16(w) for w in (ffn1_wg, ffn1_wu, ffn1_wd)]
    ffn2_w = [_to_bf16(w) for w in (ffn2_wg, ffn2_wu, ffn2_wd)]

    xp = x_prompt.reshape(bp * tp, d)
    xs = x_sample.reshape(bs * ts, d)
    outs = [[] for _ in range(24)]
    for i in range(depth):
        m, j = i % 4, i // 4
        xp, xs = _ffn(xp, ffn1_norm[i], *ffn1_w, i), _ffn(xs, ffn1_norm[i], *ffn1_w, i)
        g = mix_norm[i]
        if m == 0:
            dk = a_wq.shape[2] // RET_HEADS
            dv = a_wv.shape[2] // RET_HEADS
            wcat = b16(jnp.concatenate([a_wq[j], a_wk[j], a_wv[j], a_wg[j]], axis=1))
            wo = b16(a_wo[j])
            xp, s_p = _mixer_a(xp, g, pos_p, bp, jnp.zeros((bp, RET_HEADS, dk, dv), F32), wcat, wo, dk, dv)
            xs, s_s = _mixer_a(xs, g, pos_s, bs, state_a_ret[j], wcat, wo, dk, dv)
            new = [s_p, s_s]
            base = 0
        elif m == 1:
            common = (g, b16(jnp.concatenate([b_wy[j], b_wx[j]], axis=1)), jnp.concatenate([b_by[j], b_bx[j]]),
                      b_conv_w[j], b_conv_b[j], b16(b_wa[j]), b_ba[j].reshape(-1), b16(b_wi[j]),
                      b_bi[j].reshape(-1), jax.nn.softplus(-b_lam[j]), b16(b_wo[j]))
            xp, h_p, c_p = _lru_prompt(xp, bp, *common)
            xs, h_s, c_s = _lru_sample(xs, bs, state_b_h[j], state_b_conv[j], *common)
            new = [h_p, h_s, c_p, c_s]
            base = 2
        elif m == 2:
            zpad = jnp.zeros((d, LANES - IDX_DIM - IDX_HEADS), F32)
            wcat = b16(jnp.concatenate([c_wq[j], c_wk[j], c_wv[j], c_wiq[j], c_wik[j], c_wiw[j], zpad], axis=1))
            gik = jnp.concatenate([c_gik[j], jnp.zeros((LANES - IDX_DIM,), F32)])
            cw = (g, wcat, row(c_gq[j]), row(c_gk[j]), row(gik), b16(c_wo[j]))
            past = dict(kidx=_gather_pages(cache_c_kidx[j], page_table), k_pool=pool(cache_c_k),
                        v_pool=pool(cache_c_v), page0=j * cache_c_k.shape[1], pt=page_table)
            xp, k_p, v_p, i_p = _mixer_c(xp, bp, pos_p, None, *cw)
            xs, k_s, v_s, i_s = _mixer_c(xs, bs, pos_s, past, *cw)
            kv = lambda a, b_, t_: a.reshape(b_, t_, N_KV, HEAD_DIM)
            new = [kv(k_p, bp, tp), kv(k_s, bs, ts), kv(v_p, bp, tp), kv(v_s, bs, ts),
                   i_p.reshape(bp, tp, IDX_DIM), i_s.reshape(bs, ts, IDX_DIM)]
            base = 6
        else:
            zpad = jnp.zeros((d, LANES - N_HEADS * 3), F32)
            wcat = b16(jnp.concatenate([d_wq[j], d_wck[j], d_wcv[j], d_wsk[j], d_wsv[j], d_wwk[j], d_wwv[j],
                                        d_wgate[j], zpad], axis=1))
            cmpw = (d_pwk[j], d_pwv[j], b16(d_phik[j]), b16(d_phiv[j]), d_gck[j])
            dw = (g, wcat, row(d_gq[j]), row(d_gsk[j]), row(d_gwk[j]), cmpw, b16(d_wo[j]))
            wbuf = state_d_wk.shape[2]
            past = dict(ck_pool=pool(cache_d_ck), cv_pool=pool(cache_d_cv), sk_pool=pool(cache_d_sk),
                        sv_pool=pool(cache_d_sv), page0=j * cache_d_ck.shape[1], pt=page_table,
                        wk=state_d_wk[j].reshape(bs, wbuf, nkv), wv=state_d_wv[j].reshape(bs, wbuf, nkv))
            xp, rows_p = _mixer_d(xp, bp, pos_p, None, *dw)
            xs, rows_s = _mixer_d(xs, bs, pos_s, past, *dw)
            kv = lambda a, b_: a.reshape(b_, -1, N_KV, HEAD_DIM)
            new = []
            for a_p, a_s in zip(rows_p[:4], rows_s[:4]):
                new += [kv(a_p, bp), kv(a_s, bs)]
            wb_p = min(WINDOW, tp)
            for a_p, a_s, st in ((rows_p[4], rows_s[4], state_d_wk[j]), (rows_p[5], rows_s[5], state_d_wv[j])):
                new += [kv(a_p, bp)[:, tp - wb_p:], jnp.concatenate([st, kv(a_s, bs)], axis=1)[:, ts:]]
            base = 12
        for off, a in enumerate(new):
            outs[base + off].append(a)
        xp, xs = _ffn(xp, ffn2_norm[i], *ffn2_w, i), _ffn(xs, ffn2_norm[i], *ffn2_w, i)
    return (xp.reshape(bp, tp, d), xs.reshape(bs, ts, d)) + tuple(jnp.stack(o) for o in outs)
```
